```python
import math
import jax, jax.numpy as jnp
from jax import lax
import numpy as np

D_MODEL = 1024
BATCH = 8
SEQ = 16384
DEPTH = 1

N_MEM = 256
D_FF = 2816
D_CONV = D_MODEL
CONV_WIDTH = 31
D_SGU = D_MODEL
SGU_GROUPS = 4
CHUNK = 128
X_HEADS = 4
X_HEAD_DIM = D_MODEL // X_HEADS
D_IN = 2 * D_CONV + 2 * D_SGU + 2 * D_MODEL
EPS_RMS = 1e-6
EPS_LN = 1e-5

kernel_name = "hybrid_conformer_gmlp_memxattn_block"


def rms_norm(x, g):
    xf = x.astype(jnp.float32)
    y = xf * lax.rsqrt(jnp.mean(xf * xf, axis=-1, keepdims=True) + EPS_RMS)
    return (y * g.astype(jnp.float32)).astype(x.dtype)


def layer_norm(x, g, b):
    xf = x.astype(jnp.float32)
    mu = jnp.mean(xf, axis=-1, keepdims=True)
    xc = xf - mu
    var = jnp.mean(xc * xc, axis=-1, keepdims=True)
    y = xc * lax.rsqrt(var + EPS_LN)
    return (y * g.astype(jnp.float32) + b.astype(jnp.float32)).astype(x.dtype)


def swiglu(x, w_gu, w_down):
    gu = x @ w_gu
    g, u = jnp.split(gu, 2, axis=-1)
    return (jax.nn.silu(g) * u) @ w_down


def causal_depthwise_conv(a, w, b):
    c = a.shape[-1]
    y = lax.conv_general_dilated(
        a, w.astype(a.dtype)[:, None, :],
        window_strides=(1,), padding=[(CONV_WIDTH - 1, 0)],
        dimension_numbers=("NWC", "WIO", "NWC"),
        feature_group_count=c)
    return y + b.astype(a.dtype)


def conformer_conv_branch(a_val, a_gate, conv_w, conv_b, ln_g, ln_b, w_a_out):
    a = a_val * jax.nn.sigmoid(a_gate)
    a = causal_depthwise_conv(a, conv_w, conv_b)
    a = jax.nn.silu(layer_norm(a, ln_g, ln_b))
    return a @ w_a_out


def spatial_gating_branch(u, v, ln_g, ln_b, sgu_w, sgu_b, w_b_out):
    bsz, s, _ = u.shape
    u = jax.nn.gelu(u)
    v = layer_norm(jax.nn.gelu(v), ln_g, ln_b)
    n_chunks = s // CHUNK
    gd = D_SGU // SGU_GROUPS
    vc = v.reshape(bsz, n_chunks, CHUNK, SGU_GROUPS, gd)
    mask = jnp.tril(jnp.ones((CHUNK, CHUNK), dtype=bool))
    w_s = jnp.where(mask[None], sgu_w, 0.0).astype(v.dtype)
    mixed = jnp.einsum("gts,bcsgd->bctgd", w_s, vc)
    mixed = mixed + jnp.transpose(sgu_b)[None, None, :, :, None].astype(v.dtype)
    out = u * mixed.reshape(bsz, s, D_SGU)
    return out @ w_b_out


def memory_cross_attention(xn, memn, w_q, w_kv, w_o):
    bsz, s, _ = xn.shape
    q = (xn @ w_q).reshape(bsz, s, X_HEADS, X_HEAD_DIM)
    kv = memn @ w_kv
    k, v = jnp.split(kv, 2, axis=-1)
    k = k.reshape(bsz, N_MEM, X_HEADS, X_HEAD_DIM)
    v = v.reshape(bsz, N_MEM, X_HEADS, X_HEAD_DIM)
    scores = jnp.einsum("bshd,bmhd->bhsm", q.astype(jnp.float32), k.astype(jnp.float32))
    p = jax.nn.softmax(scores * (1.0 / math.sqrt(X_HEAD_DIM)), axis=-1).astype(v.dtype)
    o = jnp.einsum("bhsm,bmhd->bshd", p, v).reshape(bsz, s, D_MODEL)
    return o @ w_o


def _fwd_setup_inputs(seed: int = 0) -> dict:
    key = jax.random.key(seed)
    ks = jax.random.split(key, 32)

    def dense(k, shape, fan_in):
        return jax.random.normal(k, shape, jnp.float32) * (fan_in ** -0.5)

    def gain(k, shape):
        return 1.0 + 0.02 * jax.random.normal(k, shape, jnp.float32)

    def small(k, shape):
        return 0.02 * jax.random.normal(k, shape, jnp.float32)

    L = DEPTH
    return {
        "x": jax.random.normal(ks[0], (BATCH, SEQ, D_MODEL), jnp.float32),
        "mem": jax.random.normal(ks[1], (BATCH, N_MEM, D_MODEL), jnp.float32),
        "ffn1_norm": gain(ks[2], (L, D_MODEL)),
        "ffn1_w_gu": dense(ks[3], (L, D_MODEL, 2 * D_FF), D_MODEL),
        "ffn1_w_down": dense(ks[4], (L, D_FF, D_MODEL), D_FF),
        "mix_norm": gain(ks[5], (L, D_MODEL)),
        "w_in": dense(ks[6], (L, D_MODEL, D_IN), D_MODEL),
        "b_in": small(ks[7], (L, D_IN)),
        "conv_w": dense(ks[8], (L, CONV_WIDTH, D_CONV), CONV_WIDTH),
        "conv_b": small(ks[9], (L, D_CONV)),
        "conv_ln_g": gain(ks[10], (L, D_CONV)),
        "conv_ln_b": small(ks[11], (L, D_CONV)),
        "w_a_out": dense(ks[12], (L, D_CONV, D_MODEL), D_CONV),
        "sgu_ln_g": gain(ks[13], (L, D_SGU)),
        "sgu_ln_b": small(ks[14], (L, D_SGU)),
        "sgu_w": dense(ks[15], (L, SGU_GROUPS, CHUNK, CHUNK), CHUNK),
        "sgu_b": gain(ks[16], (L, SGU_GROUPS, CHUNK)),
        "w_b_out": dense(ks[17], (L, D_SGU, D_MODEL), D_SGU),
        "w_out": dense(ks[18], (L, D_MODEL, D_MODEL), D_MODEL),
        "xattn_norm": gain(ks[19], (L, D_MODEL)),
        "mem_norm": gain(ks[20], (L, D_MODEL)),
        "w_q": dense(ks[21], (L, D_MODEL, D_MODEL), D_MODEL),
        "w_kv": dense(ks[22], (L, D_MODEL, 2 * D_MODEL), D_MODEL),
        "w_o": dense(ks[23], (L, D_MODEL, D_MODEL), D_MODEL),
        "ffn2_norm": gain(ks[24], (L, D_MODEL)),
        "ffn2_w_gu": dense(ks[25], (L, D_MODEL, 2 * D_FF), D_MODEL),
        "ffn2_w_down": dense(ks[26], (L, D_FF, D_MODEL), D_FF),
        "final_norm": gain(ks[27], (D_MODEL,)),
    }


def _fwd_reference(x, mem, ffn1_norm, ffn1_w_gu, ffn1_w_down, mix_norm, w_in, b_in,
              conv_w, conv_b, conv_ln_g, conv_ln_b, w_a_out,
              sgu_ln_g, sgu_ln_b, sgu_w, sgu_b, w_b_out, w_out,
              xattn_norm, mem_norm, w_q, w_kv, w_o,
              ffn2_norm, ffn2_w_gu, ffn2_w_down, final_norm):
    split_at = [D_CONV, 2 * D_CONV, 2 * D_CONV + D_SGU, 2 * D_CONV + 2 * D_SGU,
                2 * D_CONV + 2 * D_SGU + D_MODEL]
    h = x
    for l in range(DEPTH):
        h = h + 0.5 * swiglu(rms_norm(h, ffn1_norm[l]), ffn1_w_gu[l], ffn1_w_down[l])

        n = rms_norm(h, mix_norm[l])
        p = n @ w_in[l] + b_in[l]
        a_val, a_gate, b_u, b_v, g_a, g_b = jnp.split(p, split_at, axis=-1)
        y_a = conformer_conv_branch(a_val, a_gate, conv_w[l], conv_b[l],
                                    conv_ln_g[l], conv_ln_b[l], w_a_out[l])
        y_b = spatial_gating_branch(b_u, b_v, sgu_ln_g[l], sgu_ln_b[l],
                                    sgu_w[l], sgu_b[l], w_b_out[l])
        merged = jax.nn.sigmoid(g_a) * y_a + jax.nn.sigmoid(g_b) * y_b
        h = h + merged @ w_out[l]

        h = h + memory_cross_attention(rms_norm(h, xattn_norm[l]), rms_norm(mem, mem_norm[l]),
                                       w_q[l], w_kv[l], w_o[l])

        h = h + 0.5 * swiglu(rms_norm(h, ffn2_norm[l]), ffn2_w_gu[l], ffn2_w_down[l])
    return rms_norm(h, final_norm)


import jax as _jax
import jax.numpy as _jnp

TWIN_FORMAT = 'train_step'
FWD_PARAMS = ['x', 'mem', 'ffn1_norm', 'ffn1_w_gu', 'ffn1_w_down', 'mix_norm', 'w_in', 'b_in', 'conv_w', 'conv_b', 'conv_ln_g', 'conv_ln_b', 'w_a_out', 'sgu_ln_g', 'sgu_ln_b', 'sgu_w', 'sgu_b', 'w_b_out', 'w_out', 'xattn_norm', 'mem_norm', 'w_q', 'w_kv', 'w_o', 'ffn2_norm', 'ffn2_w_gu', 'ffn2_w_down', 'final_norm']
TWIN_WEIGHTS = ['ffn1_norm', 'ffn1_w_gu', 'ffn1_w_down', 'mix_norm', 'w_in', 'b_in', 'conv_w', 'conv_b', 'conv_ln_g', 'conv_ln_b', 'w_a_out', 'sgu_ln_g', 'sgu_ln_b', 'sgu_w', 'sgu_b', 'w_b_out', 'w_out', 'xattn_norm', 'mem_norm', 'w_q', 'w_kv', 'w_o', 'ffn2_norm', 'ffn2_w_gu', 'ffn2_w_down', 'final_norm']
TWIN_DIFF_INPUT = 'x'
TWIN_INPUTS = ['x', 'mem', 'ffn1_norm', 'ffn1_w_gu', 'ffn1_w_down', 'mix_norm', 'w_in', 'b_in', 'conv_w', 'conv_b', 'conv_ln_g', 'conv_ln_b', 'w_a_out', 'sgu_ln_g', 'sgu_ln_b', 'sgu_w', 'sgu_b', 'w_b_out', 'w_out', 'xattn_norm', 'mem_norm', 'w_q', 'w_kv', 'w_o', 'ffn2_norm', 'ffn2_w_gu', 'ffn2_w_down', 'final_norm', 'loss_target', 'm_ffn1_norm', 'm_ffn1_w_gu', 'm_ffn1_w_down', 'm_mix_norm', 'm_w_in', 'm_b_in', 'm_conv_w', 'm_conv_b', 'm_conv_ln_g', 'm_conv_ln_b', 'm_w_a_out', 'm_sgu_ln_g', 'm_sgu_ln_b', 'm_sgu_w', 'm_sgu_b', 'm_w_b_out', 'm_w_out', 'm_xattn_norm', 'm_mem_norm', 'm_w_q', 'm_w_kv', 'm_w_o', 'm_ffn2_norm', 'm_ffn2_w_gu', 'm_ffn2_w_down', 'm_final_norm', 'v_ffn1_norm', 'v_ffn1_w_gu', 'v_ffn1_w_down', 'v_mix_norm', 'v_w_in', 'v_b_in', 'v_conv_w', 'v_conv_b', 'v_conv_ln_g', 'v_conv_ln_b', 'v_w_a_out', 'v_sgu_ln_g', 'v_sgu_ln_b', 'v_sgu_w', 'v_sgu_b', 'v_w_b_out', 'v_w_out', 'v_xattn_norm', 'v_mem_norm', 'v_w_q', 'v_w_kv', 'v_w_o', 'v_ffn2_norm', 'v_ffn2_w_gu', 'v_ffn2_w_down', 'v_final_norm']
TWIN_OUTPUTS = ['loss', 'grad_x', 'grad_ffn1_norm', 'grad_ffn1_w_gu', 'grad_ffn1_w_down', 'grad_mix_norm', 'grad_w_in', 'grad_b_in', 'grad_conv_w', 'grad_conv_b', 'grad_conv_ln_g', 'grad_conv_ln_b', 'grad_w_a_out', 'grad_sgu_ln_g', 'grad_sgu_ln_b', 'grad_sgu_w', 'grad_sgu_b', 'grad_w_b_out', 'grad_w_out', 'grad_xattn_norm', 'grad_mem_norm', 'grad_w_q', 'grad_w_kv', 'grad_w_o', 'grad_ffn2_norm', 'grad_ffn2_w_gu', 'grad_ffn2_w_down', 'grad_final_norm', 'delta_ffn1_norm', 'delta_ffn1_w_gu', 'delta_ffn1_w_down', 'delta_mix_norm', 'delta_w_in', 'delta_b_in', 'delta_conv_w', 'delta_conv_b', 'delta_conv_ln_g', 'delta_conv_ln_b', 'delta_w_a_out', 'delta_sgu_ln_g', 'delta_sgu_ln_b', 'delta_sgu_w', 'delta_sgu_b', 'delta_w_b_out', 'delta_w_out', 'delta_xattn_norm', 'delta_mem_norm', 'delta_w_q', 'delta_w_kv', 'delta_w_o', 'delta_ffn2_norm', 'delta_ffn2_w_gu', 'delta_ffn2_w_down', 'delta_final_norm', 'new_m_ffn1_norm', 'new_m_ffn1_w_gu', 'new_m_ffn1_w_down', 'new_m_mix_norm', 'new_m_w_in', 'new_m_b_in', 'new_m_conv_w', 'new_m_conv_b', 'new_m_conv_ln_g', 'new_m_conv_ln_b', 'new_m_w_a_out', 'new_m_sgu_ln_g', 'new_m_sgu_ln_b', 'new_m_sgu_w', 'new_m_sgu_b', 'new_m_w_b_out', 'new_m_w_out', 'new_m_xattn_norm', 'new_m_mem_norm', 'new_m_w_q', 'new_m_w_kv', 'new_m_w_o', 'new_m_ffn2_norm', 'new_m_ffn2_w_gu', 'new_m_ffn2_w_down', 'new_m_final_norm', 'new_v_ffn1_norm', 'new_v_ffn1_w_gu', 'new_v_ffn1_w_down', 'new_v_mix_norm', 'new_v_w_in', 'new_v_b_in', 'new_v_conv_w', 'new_v_conv_b', 'new_v_conv_ln_g', 'new_v_conv_ln_b', 'new_v_w_a_out', 'new_v_sgu_ln_g', 'new_v_sgu_ln_b', 'new_v_sgu_w', 'new_v_sgu_b', 'new_v_w_b_out', 'new_v_w_out', 'new_v_xattn_norm', 'new_v_mem_norm', 'new_v_w_q', 'new_v_w_kv', 'new_v_w_o', 'new_v_ffn2_norm', 'new_v_ffn2_w_gu', 'new_v_ffn2_w_down', 'new_v_final_norm']
TWIN_LEAF_KINDS = {'loss': 'loss', 'grad_x': 'grad_x', 'grad_ffn1_norm': 'grad_w', 'grad_ffn1_w_gu': 'grad_w', 'grad_ffn1_w_down': 'grad_w', 'grad_mix_norm': 'grad_w', 'grad_w_in': 'grad_w', 'grad_b_in': 'grad_w', 'grad_conv_w': 'grad_w', 'grad_conv_b': 'grad_w', 'grad_conv_ln_g': 'grad_w', 'grad_conv_ln_b': 'grad_w', 'grad_w_a_out': 'grad_w', 'grad_sgu_ln_g': 'grad_w', 'grad_sgu_ln_b': 'grad_w', 'grad_sgu_w': 'grad_w', 'grad_sgu_b': 'grad_w', 'grad_w_b_out': 'grad_w', 'grad_w_out': 'grad_w', 'grad_xattn_norm': 'grad_w', 'grad_mem_norm': 'grad_w', 'grad_w_q': 'grad_w', 'grad_w_kv': 'grad_w', 'grad_w_o': 'grad_w', 'grad_ffn2_norm': 'grad_w', 'grad_ffn2_w_gu': 'grad_w', 'grad_ffn2_w_down': 'grad_w', 'grad_final_norm': 'grad_w', 'delta_ffn1_norm': 'delta_w', 'delta_ffn1_w_gu': 'delta_w', 'delta_ffn1_w_down': 'delta_w', 'delta_mix_norm': 'delta_w', 'delta_w_in': 'delta_w', 'delta_b_in': 'delta_w', 'delta_conv_w': 'delta_w', 'delta_conv_b': 'delta_w', 'delta_conv_ln_g': 'delta_w', 'delta_conv_ln_b': 'delta_w', 'delta_w_a_out': 'delta_w', 'delta_sgu_ln_g': 'delta_w', 'delta_sgu_ln_b': 'delta_w', 'delta_sgu_w': 'delta_w', 'delta_sgu_b': 'delta_w', 'delta_w_b_out': 'delta_w', 'delta_w_out': 'delta_w', 'delta_xattn_norm': 'delta_w', 'delta_mem_norm': 'delta_w', 'delta_w_q': 'delta_w', 'delta_w_kv': 'delta_w', 'delta_w_o': 'delta_w', 'delta_ffn2_norm': 'delta_w', 'delta_ffn2_w_gu': 'delta_w', 'delta_ffn2_w_down': 'delta_w', 'delta_final_norm': 'delta_w', 'new_m_ffn1_norm': 'new_m', 'new_m_ffn1_w_gu': 'new_m', 'new_m_ffn1_w_down': 'new_m', 'new_m_mix_norm': 'new_m', 'new_m_w_in': 'new_m', 'new_m_b_in': 'new_m', 'new_m_conv_w': 'new_m', 'new_m_conv_b': 'new_m', 'new_m_conv_ln_g': 'new_m', 'new_m_conv_ln_b': 'new_m', 'new_m_w_a_out': 'new_m', 'new_m_sgu_ln_g': 'new_m', 'new_m_sgu_ln_b': 'new_m', 'new_m_sgu_w': 'new_m', 'new_m_sgu_b': 'new_m', 'new_m_w_b_out': 'new_m', 'new_m_w_out': 'new_m', 'new_m_xattn_norm': 'new_m', 'new_m_mem_norm': 'new_m', 'new_m_w_q': 'new_m', 'new_m_w_kv': 'new_m', 'new_m_w_o': 'new_m', 'new_m_ffn2_norm': 'new_m', 'new_m_ffn2_w_gu': 'new_m', 'new_m_ffn2_w_down': 'new_m', 'new_m_final_norm': 'new_m', 'new_v_ffn1_norm': 'new_v', 'new_v_ffn1_w_gu': 'new_v', 'new_v_ffn1_w_down': 'new_v', 'new_v_mix_norm': 'new_v', 'new_v_w_in': 'new_v', 'new_v_b_in': 'new_v', 'new_v_conv_w': 'new_v', 'new_v_conv_b': 'new_v', 'new_v_conv_ln_g': 'new_v', 'new_v_conv_ln_b': 'new_v', 'new_v_w_a_out': 'new_v', 'new_v_sgu_ln_g': 'new_v', 'new_v_sgu_ln_b': 'new_v', 'new_v_sgu_w': 'new_v', 'new_v_sgu_b': 'new_v', 'new_v_w_b_out': 'new_v', 'new_v_w_out': 'new_v', 'new_v_xattn_norm': 'new_v', 'new_v_mem_norm': 'new_v', 'new_v_w_q': 'new_v', 'new_v_w_kv': 'new_v', 'new_v_w_o': 'new_v', 'new_v_ffn2_norm': 'new_v', 'new_v_ffn2_w_gu': 'new_v', 'new_v_ffn2_w_down': 'new_v', 'new_v_final_norm': 'new_v'}


def _forward(args):
    return _fwd_reference(*[args[k] for k in FWD_PARAMS])


def _output_shape():
    def fwd():
        inp = _fwd_setup_inputs(0)
        return _fwd_reference(*[inp[k] for k in FWD_PARAMS])
    out = _jax.eval_shape(fwd)
    return out.shape, out.dtype

N_MICROBATCH = 1
ADAM_LR = 0.001
ADAM_B1 = 0.9
ADAM_B2 = 0.999
ADAM_EPS = 1e-08
ADAM_WD = 0.01
ADAM_STEP = 10
PER_EXAMPLE_BATCH_AXIS = {'x': 0, 'mem': 0, 'loss_target': 0}
SHARED_INPUTS = []
_WEIGHT_DTYPES = {'ffn1_norm': _jnp.float32, 'ffn1_w_gu': _jnp.float32, 'ffn1_w_down': _jnp.float32, 'mix_norm': _jnp.float32, 'w_in': _jnp.float32, 'b_in': _jnp.float32, 'conv_w': _jnp.float32, 'conv_b': _jnp.float32, 'conv_ln_g': _jnp.float32, 'conv_ln_b': _jnp.float32, 'w_a_out': _jnp.float32, 'sgu_ln_g': _jnp.float32, 'sgu_ln_b': _jnp.float32, 'sgu_w': _jnp.float32, 'sgu_b': _jnp.float32, 'w_b_out': _jnp.float32, 'w_out': _jnp.float32, 'xattn_norm': _jnp.float32, 'mem_norm': _jnp.float32, 'w_q': _jnp.float32, 'w_kv': _jnp.float32, 'w_o': _jnp.float32, 'ffn2_norm': _jnp.float32, 'ffn2_w_gu': _jnp.float32, 'ffn2_w_down': _jnp.float32, 'final_norm': _jnp.float32}
MOMENT_SCALE = {'ffn1_norm': 1.622077e-01, 'ffn1_w_gu': 6.769036e-02, 'ffn1_w_down': 1.107214e-01, 'mix_norm': 2.099770e-01, 'w_in': 8.573338e-02, 'b_in': 9.236341e-02, 'conv_w': 1.046721e-01, 'conv_b': 2.115981e-01, 'conv_ln_g': 1.319874e-01, 'conv_ln_b': 1.279823e-01, 'w_a_out': 1.037460e-01, 'sgu_ln_g': 8.086344e-02, 'sgu_ln_b': 7.516210e-02, 'sgu_w': 1.096625e-01, 'sgu_b': 1.699250e-01, 'w_b_out': 1.400922e-01, 'w_out': 1.748514e-01, 'xattn_norm': 3.290355e-02, 'mem_norm': 4.811616e-02, 'w_q': 3.246667e-02, 'w_kv': 3.261725e-02, 'w_o': 3.298825e-02, 'ffn2_norm': 1.374704e-01, 'ffn2_w_gu': 5.348158e-02, 'ffn2_w_down': 8.707098e-02, 'final_norm': 1.280407e+02}


def _to_microbatches(a, axis):
    t = _jnp.moveaxis(a, axis, 0)
    t = t.reshape((N_MICROBATCH, t.shape[0] // N_MICROBATCH) + t.shape[1:])
    return _jnp.moveaxis(t, 1, axis + 1)


def setup_inputs(seed: int = 0) -> dict:
    inp = _fwd_setup_inputs(seed)
    key = _jax.random.fold_in(_jax.random.key(seed), 7919)
    shape, _ = _output_shape()
    out = dict(inp)
    out["loss_target"] = _jax.random.normal(_jax.random.fold_in(key, 0), shape, _jnp.float32)
    for i, name in enumerate(TWIN_WEIGHTS):
        w = inp[name].astype(_jnp.float32)
        if MOMENT_SCALE is None:
            s = _jnp.sqrt(_jnp.mean(_jnp.square(w)) + 1e-30)
        else:
            s = MOMENT_SCALE[name]
        km, kv = _jax.random.split(_jax.random.fold_in(key, i + 1))
        out[name] = w
        out["m_" + name] = s * _jax.random.normal(km, w.shape, _jnp.float32)
        out["v_" + name] = (s * s) * _jax.random.uniform(kv, w.shape, _jnp.float32, 0.5, 1.5)
    if N_MICROBATCH > 1:
        for name, axis in PER_EXAMPLE_BATCH_AXIS.items():
            out[name] = _to_microbatches(out[name], axis)
    return {'x': out['x'], 'mem': out['mem'], 'ffn1_norm': out['ffn1_norm'], 'ffn1_w_gu': out['ffn1_w_gu'], 'ffn1_w_down': out['ffn1_w_down'], 'mix_norm': out['mix_norm'], 'w_in': out['w_in'], 'b_in': out['b_in'], 'conv_w': out['conv_w'], 'conv_b': out['conv_b'], 'conv_ln_g': out['conv_ln_g'], 'conv_ln_b': out['conv_ln_b'], 'w_a_out': out['w_a_out'], 'sgu_ln_g': out['sgu_ln_g'], 'sgu_ln_b': out['sgu_ln_b'], 'sgu_w': out['sgu_w'], 'sgu_b': out['sgu_b'], 'w_b_out': out['w_b_out'], 'w_out': out['w_out'], 'xattn_norm': out['xattn_norm'], 'mem_norm': out['mem_norm'], 'w_q': out['w_q'], 'w_kv': out['w_kv'], 'w_o': out['w_o'], 'ffn2_norm': out['ffn2_norm'], 'ffn2_w_gu': out['ffn2_w_gu'], 'ffn2_w_down': out['ffn2_w_down'], 'final_norm': out['final_norm'], 'loss_target': out['loss_target'], 'm_ffn1_norm': out['m_ffn1_norm'], 'm_ffn1_w_gu': out['m_ffn1_w_gu'], 'm_ffn1_w_down': out['m_ffn1_w_down'], 'm_mix_norm': out['m_mix_norm'], 'm_w_in': out['m_w_in'], 'm_b_in': out['m_b_in'], 'm_conv_w': out['m_conv_w'], 'm_conv_b': out['m_conv_b'], 'm_conv_ln_g': out['m_conv_ln_g'], 'm_conv_ln_b': out['m_conv_ln_b'], 'm_w_a_out': out['m_w_a_out'], 'm_sgu_ln_g': out['m_sgu_ln_g'], 'm_sgu_ln_b': out['m_sgu_ln_b'], 'm_sgu_w': out['m_sgu_w'], 'm_sgu_b': out['m_sgu_b'], 'm_w_b_out': out['m_w_b_out'], 'm_w_out': out['m_w_out'], 'm_xattn_norm': out['m_xattn_norm'], 'm_mem_norm': out['m_mem_norm'], 'm_w_q': out['m_w_q'], 'm_w_kv': out['m_w_kv'], 'm_w_o': out['m_w_o'], 'm_ffn2_norm': out['m_ffn2_norm'], 'm_ffn2_w_gu': out['m_ffn2_w_gu'], 'm_ffn2_w_down': out['m_ffn2_w_down'], 'm_final_norm': out['m_final_norm'], 'v_ffn1_norm': out['v_ffn1_norm'], 'v_ffn1_w_gu': out['v_ffn1_w_gu'], 'v_ffn1_w_down': out['v_ffn1_w_down'], 'v_mix_norm': out['v_mix_norm'], 'v_w_in': out['v_w_in'], 'v_b_in': out['v_b_in'], 'v_conv_w': out['v_conv_w'], 'v_conv_b': out['v_conv_b'], 'v_conv_ln_g': out['v_conv_ln_g'], 'v_conv_ln_b': out['v_conv_ln_b'], 'v_w_a_out': out['v_w_a_out'], 'v_sgu_ln_g': out['v_sgu_ln_g'], 'v_sgu_ln_b': out['v_sgu_ln_b'], 'v_sgu_w': out['v_sgu_w'], 'v_sgu_b': out['v_sgu_b'], 'v_w_b_out': out['v_w_b_out'], 'v_w_out': out['v_w_out'], 'v_xattn_norm': out['v_xattn_norm'], 'v_mem_norm': out['v_mem_norm'], 'v_w_q': out['v_w_q'], 'v_w_kv': out['v_w_kv'], 'v_w_o': out['v_w_o'], 'v_ffn2_norm': out['v_ffn2_norm'], 'v_ffn2_w_gu': out['v_ffn2_w_gu'], 'v_ffn2_w_down': out['v_ffn2_w_down'], 'v_final_norm': out['v_final_norm']}


def _loss(weights, diff, rest, loss_target):
    with _jax.named_scope("forward"):
        args = {**rest, TWIN_DIFF_INPUT: diff, **{k: w.astype(_WEIGHT_DTYPES[k]) for k, w in weights.items()}}
        y = _forward(args)
    with _jax.named_scope("loss_head"):
        err = _jnp.square(y.astype(_jnp.float32) - loss_target)
        return 0.5 * _jnp.sum(_jnp.mean(err, axis=-1)) if err.ndim else 0.5 * err


def _adamw(w, g, m, v):
    m = ADAM_B1 * m + (1.0 - ADAM_B1) * g
    v = ADAM_B2 * v + (1.0 - ADAM_B2) * _jnp.square(g)
    m_hat = m / (1.0 - ADAM_B1 ** ADAM_STEP)
    v_hat = v / (1.0 - ADAM_B2 ** ADAM_STEP)
    delta = -ADAM_LR * (m_hat / (_jnp.sqrt(v_hat) + ADAM_EPS) + ADAM_WD * w)
    return delta, m, v


def reference(x, mem, ffn1_norm, ffn1_w_gu, ffn1_w_down, mix_norm, w_in, b_in, conv_w, conv_b, conv_ln_g, conv_ln_b, w_a_out, sgu_ln_g, sgu_ln_b, sgu_w, sgu_b, w_b_out, w_out, xattn_norm, mem_norm, w_q, w_kv, w_o, ffn2_norm, ffn2_w_gu, ffn2_w_down, final_norm, loss_target, m_ffn1_norm, m_ffn1_w_gu, m_ffn1_w_down, m_mix_norm, m_w_in, m_b_in, m_conv_w, m_conv_b, m_conv_ln_g, m_conv_ln_b, m_w_a_out, m_sgu_ln_g, m_sgu_ln_b, m_sgu_w, m_sgu_b, m_w_b_out, m_w_out, m_xattn_norm, m_mem_norm, m_w_q, m_w_kv, m_w_o, m_ffn2_norm, m_ffn2_w_gu, m_ffn2_w_down, m_final_norm, v_ffn1_norm, v_ffn1_w_gu, v_ffn1_w_down, v_mix_norm, v_w_in, v_b_in, v_conv_w, v_conv_b, v_conv_ln_g, v_conv_ln_b, v_w_a_out, v_sgu_ln_g, v_sgu_ln_b, v_sgu_w, v_sgu_b, v_w_b_out, v_w_out, v_xattn_norm, v_mem_norm, v_w_q, v_w_kv, v_w_o, v_ffn2_norm, v_ffn2_w_gu, v_ffn2_w_down, v_final_norm):
    given = dict(x=x, mem=mem, ffn1_norm=ffn1_norm, ffn1_w_gu=ffn1_w_gu, ffn1_w_down=ffn1_w_down, mix_norm=mix_norm, w_in=w_in, b_in=b_in, conv_w=conv_w, conv_b=conv_b, conv_ln_g=conv_ln_g, conv_ln_b=conv_ln_b, w_a_out=w_a_out, sgu_ln_g=sgu_ln_g, sgu_ln_b=sgu_ln_b, sgu_w=sgu_w, sgu_b=sgu_b, w_b_out=w_b_out, w_out=w_out, xattn_norm=xattn_norm, mem_norm=mem_norm, w_q=w_q, w_kv=w_kv, w_o=w_o, ffn2_norm=ffn2_norm, ffn2_w_gu=ffn2_w_gu, ffn2_w_down=ffn2_w_down, final_norm=final_norm, loss_target=loss_target, m_ffn1_norm=m_ffn1_norm, m_ffn1_w_gu=m_ffn1_w_gu, m_ffn1_w_down=m_ffn1_w_down, m_mix_norm=m_mix_norm, m_w_in=m_w_in, m_b_in=m_b_in, m_conv_w=m_conv_w, m_conv_b=m_conv_b, m_conv_ln_g=m_conv_ln_g, m_conv_ln_b=m_conv_ln_b, m_w_a_out=m_w_a_out, m_sgu_ln_g=m_sgu_ln_g, m_sgu_ln_b=m_sgu_ln_b, m_sgu_w=m_sgu_w, m_sgu_b=m_sgu_b, m_w_b_out=m_w_b_out, m_w_out=m_w_out, m_xattn_norm=m_xattn_norm, m_mem_norm=m_mem_norm, m_w_q=m_w_q, m_w_kv=m_w_kv, m_w_o=m_w_o, m_ffn2_norm=m_ffn2_norm, m_ffn2_w_gu=m_ffn2_w_gu, m_ffn2_w_down=m_ffn2_w_down, m_final_norm=m_final_norm, v_ffn1_norm=v_ffn1_norm, v_ffn1_w_gu=v_ffn1_w_gu, v_ffn1_w_down=v_ffn1_w_down, v_mix_norm=v_mix_norm, v_w_in=v_w_in, v_b_in=v_b_in, v_conv_w=v_conv_w, v_conv_b=v_conv_b, v_conv_ln_g=v_conv_ln_g, v_conv_ln_b=v_conv_ln_b, v_w_a_out=v_w_a_out, v_sgu_ln_g=v_sgu_ln_g, v_sgu_ln_b=v_sgu_ln_b, v_sgu_w=v_sgu_w, v_sgu_b=v_sgu_b, v_w_b_out=v_w_b_out, v_w_out=v_w_out, v_xattn_norm=v_xattn_norm, v_mem_norm=v_mem_norm, v_w_q=v_w_q, v_w_kv=v_w_kv, v_w_o=v_w_o, v_ffn2_norm=v_ffn2_norm, v_ffn2_w_gu=v_ffn2_w_gu, v_ffn2_w_down=v_ffn2_w_down, v_final_norm=v_final_norm)
    weights = {n: given[n] for n in TWIN_WEIGHTS}
    shared = {n: given[n] for n in SHARED_INPUTS}
    per_example = {n: given[n] for n in ['x', 'mem']}
    grad_fn = _jax.value_and_grad(_loss, argnums=(0, 1))

    def one_microbatch(ex, loss_target):
        ex = dict(ex)
        diff = ex.pop(TWIN_DIFF_INPUT)
        return grad_fn(weights, diff, {**shared, **ex}, loss_target)

    if N_MICROBATCH == 1:
        loss, (grad_w, grad_x) = one_microbatch(per_example, given["loss_target"])
    else:
        def body(carry, xs):
            loss_sum, grad_sum = carry
            l_k, (gw_k, gx_k) = one_microbatch(xs[0], xs[1])
            with _jax.named_scope("update"):
                return (loss_sum + l_k, _jax.tree.map(_jnp.add, grad_sum, gw_k)), gx_k

        init = (_jnp.zeros((), _jnp.float32), _jax.tree.map(_jnp.zeros_like, weights))
        (loss, grad_w), grad_x = _jax.lax.scan(body, init, (per_example, given["loss_target"]))
    with _jax.named_scope("update"):
        delta_w, new_m, new_v = {}, {}, {}
        for n in TWIN_WEIGHTS:
            delta_w[n], new_m[n], new_v[n] = _adamw(weights[n], grad_w[n], given["m_" + n], given["v_" + n])
    return (loss, grad_x, *[grad_w[n] for n in TWIN_WEIGHTS], *[delta_w[n] for n in TWIN_WEIGHTS],
            *[new_m[n] for n in TWIN_WEIGHTS], *[new_v[n] for n in TWIN_WEIGHTS])
```

```python
import functools
import math

import jax
import jax.numpy as jnp
from jax import lax
from jax.experimental import pallas as pl
from jax.experimental.pallas import tpu as pltpu

F32 = jnp.float32
BF = jnp.bfloat16
MESH = pl.DeviceIdType.MESH

D = 1024
FF = 2816
HC = FF // 2
NSH = 4
DIN = 6 * D
INB = DIN // NSH
CW = 31
HALO = 32
CHUNK = 128
NG = 4
GD = D // NG
NH = 4
HD = D // NH
NMEM = 256
EPS_RMS = 1e-6
EPS_LN = 1e-5
GELU_C0 = math.sqrt(2.0 / math.pi)
GELU_C1 = 0.044715
ATT_SCALE = 1.0 / math.sqrt(HD)

ADAM_LR = 0.001
ADAM_B1 = 0.9
ADAM_B2 = 0.999
ADAM_EPS = 1e-08
ADAM_WD = 0.01
ADAM_STEP = 10

VMEM_LIMIT = 56 * 1024 * 1024


def _cparams(sem=None, **kw):
    if sem is not None:
        kw["dimension_semantics"] = sem
    return pltpu.CompilerParams(vmem_limit_bytes=VMEM_LIMIT, **kw)


def _dot(a, b):
    return jnp.dot(a, b, preferred_element_type=F32)


def _dot_nt(a, b):
    return lax.dot_general(a, b, (((1,), (1,)), ((), ())), preferred_element_type=F32)


def _dot_tn(a, b):
    return lax.dot_general(a, b, (((0,), (0,)), ((), ())), preferred_element_type=F32)


def _sigmoid(x):
    return 1.0 / (1.0 + jnp.exp(-x))


def _gelu(x):
    t = jnp.tanh(GELU_C0 * (x + GELU_C1 * (x * x * x)))
    return 0.5 * x * (1.0 + t), t


def _gelu_grad(x, t):
    return 0.5 * (1.0 + t) + 0.5 * x * (1.0 - t * t) * (GELU_C0 * (1.0 + 3.0 * GELU_C1 * x * x))


def _mean(x):
    return jnp.mean(x, axis=-1, keepdims=True)


def _rms(x):
    r = lax.rsqrt(_mean(x * x) + EPS_RMS)
    return x * r, r


def _rms_bwd(dn, xh, r, g):
    dxh = dn * g
    return r * (dxh - xh * _mean(dxh * xh))


def _ln(x):
    xc = x - _mean(x)
    r = lax.rsqrt(_mean(xc * xc) + EPS_LN)
    return xc * r, r


def _ln_bwd(dy, xh, r, g):
    dxh = dy * g
    return r * (dxh - _mean(dxh) - xh * _mean(dxh * xh))


def _colsum(x):
    return jnp.sum(x, axis=0, keepdims=True)


def _const_spec(shape):
    nd = len(shape)
    return pl.BlockSpec(shape, lambda *_: (0,) * nd, pipeline_mode=pl.Buffered(1))


def _row_spec(ts, width):
    return pl.BlockSpec((ts, width), lambda i: (i, 0))


def _acc_spec(shape):
    nd = len(shape)
    return pl.BlockSpec(shape, lambda *_: (0,) * nd)


def _tile(s, want):
    return min(s, want)


def _ffn_hidden(nb, wgu_ref, j):
    g = _dot(nb, wgu_ref[j])
    u = _dot(nb, wgu_ref[2 + j])
    return g, u


def ffn_fwd(h, gain, wgu, wd, name="ffn1_fwd"):
    s = h.shape[0]
    ts = _tile(s, 512)

    def body(h_ref, g_ref, wgu_ref, wd_ref, o_ref):
        x = h_ref[...]
        xh, _ = _rms(x)
        nb = (xh * g_ref[...]).astype(BF)
        acc = jnp.zeros((ts, D), F32)
        for j in range(2):
            g, u = _ffn_hidden(nb, wgu_ref, j)
            a = (g * _sigmoid(g) * u).astype(BF)
            acc = acc + _dot(a, wd_ref[j * HC:(j + 1) * HC, :])
        o_ref[...] = x + 0.5 * acc

    return pl.pallas_call(
        body, grid=(s // ts,), name=name,
        out_shape=jax.ShapeDtypeStruct((s, D), F32),
        in_specs=[_row_spec(ts, D), _const_spec((1, D)), _const_spec((NSH, D, HC)), _const_spec((FF, D))],
        out_specs=_row_spec(ts, D),
        compiler_params=_cparams(("arbitrary",)),
    )(h, gain, wgu, wd)


def ffn_fwd_loss(h, gain, wgu, wd, gfin, target):
    s = h.shape[0]
    ts = _tile(s, 512)

    def body(h_ref, g_ref, wgu_ref, wd_ref, gf_ref, t_ref, dh_ref, loss_ref, dgf_ref):
        @pl.when(pl.program_id(0) == 0)
        def _():
            loss_ref[...] = jnp.zeros_like(loss_ref)
            dgf_ref[...] = jnp.zeros_like(dgf_ref)

        x = h_ref[...]
        xh, _ = _rms(x)
        nb = (xh * g_ref[...]).astype(BF)
        acc = jnp.zeros((ts, D), F32)
        for j in range(2):
            g, u = _ffn_hidden(nb, wgu_ref, j)
            a = (g * _sigmoid(g) * u).astype(BF)
            acc = acc + _dot(a, wd_ref[j * HC:(j + 1) * HC, :])
        h4 = x + 0.5 * acc
        yh, r4 = _rms(h4)
        gf = gf_ref[...]
        e = yh * gf - t_ref[...]
        loss_ref[...] += _colsum(e * e)
        dy = e * (1.0 / D)
        dgf_ref[...] += _colsum(dy * yh)
        dh_ref[...] = _rms_bwd(dy, yh, r4, gf)

    return pl.pallas_call(
        body, grid=(s // ts,), name="ffn_fwd_loss",
        out_shape=(jax.ShapeDtypeStruct((s, D), F32), jax.ShapeDtypeStruct((1, D), F32),
                   jax.ShapeDtypeStruct((1, D), F32)),
        in_specs=[_row_spec(ts, D), _const_spec((1, D)), _const_spec((NSH, D, HC)), _const_spec((FF, D)),
                  _const_spec((1, D)), _row_spec(ts, D)],
        out_specs=(_row_spec(ts, D), _acc_spec((1, D)), _acc_spec((1, D))),
        compiler_params=_cparams(("arbitrary",)),
    )(h, gain, wgu, wd, gfin, target)


def ffn_bwd(h, dh, gain, wgu, wd, name):
    s = h.shape[0]
    ts = _tile(s, 256)

    def body(h_ref, dh_ref, g_ref, wgu_ref, wd_ref, dx_ref, n_ref, a_ref, dgu_ref, dhb_ref, dg_ref):
        @pl.when(pl.program_id(0) == 0)
        def _():
            dg_ref[...] = jnp.zeros_like(dg_ref)

        x = h_ref[...]
        dh = dh_ref[...]
        gain_v = g_ref[...]
        xh, r = _rms(x)
        nb = (xh * gain_v).astype(BF)
        n_ref[...] = nb
        dhb = (0.5 * dh).astype(BF)
        dhb_ref[...] = dhb
        dn = jnp.zeros((ts, D), F32)
        for j in range(2):
            g, u = _ffn_hidden(nb, wgu_ref, j)
            sg = _sigmoid(g)
            sl = g * sg
            a_ref[:, j * HC:(j + 1) * HC] = (sl * u).astype(BF)
            da = _dot_nt(dhb, wd_ref[j * HC:(j + 1) * HC, :])
            dgb = (da * u * (sg * (1.0 + g * (1.0 - sg)))).astype(BF)
            dub = (da * sl).astype(BF)
            dgu_ref[:, j * HC:(j + 1) * HC] = dgb
            dgu_ref[:, FF + j * HC:FF + (j + 1) * HC] = dub
            dn = dn + _dot_nt(dgb, wgu_ref[j]) + _dot_nt(dub, wgu_ref[2 + j])
        dg_ref[...] += _colsum(dn * xh)
        dx_ref[...] = dh + _rms_bwd(dn, xh, r, gain_v)

    return pl.pallas_call(
        body, grid=(s // ts,), name=name,
        out_shape=(jax.ShapeDtypeStruct((s, D), F32), jax.ShapeDtypeStruct((s, D), BF),
                   jax.ShapeDtypeStruct((s, FF), BF), jax.ShapeDtypeStruct((s, 2 * FF), BF),
                   jax.ShapeDtypeStruct((s, D), BF), jax.ShapeDtypeStruct((1, D), F32)),
        in_specs=[_row_spec(ts, D), _row_spec(ts, D), _const_spec((1, D)), _const_spec((NSH, D, HC)),
                  _const_spec((FF, D))],
        out_specs=(_row_spec(ts, D), _row_spec(ts, D), _row_spec(ts, FF), _row_spec(ts, 2 * FF),
                   _row_spec(ts, D), _acc_spec((1, D))),
        compiler_params=_cparams(("arbitrary",)),
    )(h, dh, gain, wgu, wd)


def dw_matmul(x, dy, nsplit, name):
    s, k = x.shape
    n = dy.shape[1]
    nb = n // nsplit
    ts = _tile(s, 1024)

    def body(x_ref, dy_ref, o_ref):
        @pl.when(pl.program_id(1) == 0)
        def _():
            o_ref[...] = jnp.zeros_like(o_ref)

        o_ref[0] += _dot_tn(x_ref[...], dy_ref[...])

    return pl.pallas_call(
        body, grid=(nsplit, s // ts), name=name,
        out_shape=jax.ShapeDtypeStruct((nsplit, k, nb), F32),
        in_specs=[pl.BlockSpec((ts, k), lambda j, i: (i, 0)), pl.BlockSpec((ts, nb), lambda j, i: (i, j))],
        out_specs=pl.BlockSpec((1, k, nb), lambda j, i: (j, 0, 0)),
        compiler_params=_cparams(("arbitrary", "arbitrary")),
    )(x, dy)


def _split_in_proj(p, b):
    h = INB - D
    a_val = p[0][:, :D] + b[:, 0:D]
    a_gate = jnp.concatenate([p[0][:, D:], p[1][:, :h]], axis=1) + b[:, D:2 * D]
    b_u = p[1][:, h:] + b[:, 2 * D:3 * D]
    b_v = p[2][:, :D] + b[:, 3 * D:4 * D]
    g_a = jnp.concatenate([p[2][:, D:], p[3][:, :h]], axis=1) + b[:, 4 * D:5 * D]
    g_b = p[3][:, h:] + b[:, 5 * D:6 * D]
    return a_val, a_gate, b_u, b_v, g_a, g_b


def _sgu_mix(vnb, ws_ref, sb_ref, mixed_ref, ts):
    for ci in range(ts // CHUNK):
        rows = slice(ci * CHUNK, (ci + 1) * CHUNK)
        for g in range(NG):
            cols = slice(g * GD, (g + 1) * GD)
            mixed_ref[rows, cols] = _dot(ws_ref[g], vnb[rows, cols]) + sb_ref[:, cols]


def _conv_taps(ext_ref, w_ref, ts, first):
    acc = jnp.zeros((ts, D), F32)
    for k in range(CW):
        acc = acc + w_ref[k:k + 1, :] * ext_ref[first + k:first + k + ts, :]
    return acc


def mix_fwd(h, gain, win, b_in, conv_w, conv_b, lna_g, lna_b, wa, lnb_g, lnb_b, ws, sbias, wb, wo):
    s = h.shape[0]
    ts = _tile(s, 256)

    def body(h_ref, g_ref, win_ref, bin_ref, cw_ref, cb_ref, lag_ref, lab_ref, wa_ref, lbg_ref, lbb_ref,
             ws_ref, sb_ref, wb_ref, wo_ref, o_ref, p_ref, n_ref, c_ref, ext_ref, mixed_ref):
        @pl.when(pl.program_id(0) == 0)
        def _():
            ext_ref[0:HALO, :] = jnp.zeros((HALO, D), F32)

        x = h_ref[...]
        xh, _ = _rms(x)
        nb = (xh * g_ref[...]).astype(BF)
        n_ref[...] = nb
        b = bin_ref[...]
        p = []
        for k in range(NSH):
            pk = _dot(nb, win_ref[k])
            p_ref[:, k * INB:(k + 1) * INB] = (pk + b[:, k * INB:(k + 1) * INB]).astype(BF)
            p.append(pk)
        a_val, a_gate, b_u, b_v, g_a, g_b = _split_in_proj(p, b)
        ext_ref[HALO:HALO + ts, :] = a_val * _sigmoid(a_gate)
        c = _conv_taps(ext_ref, cw_ref, ts, HALO - (CW - 1)) + cb_ref[...]
        c_ref[...] = c
        ext_ref[0:HALO, :] = ext_ref[ts:ts + HALO, :]
        ch, _ = _ln(c)
        la = ch * lag_ref[...] + lab_ref[...]
        sa = (la * _sigmoid(la)).astype(BF)
        ya = _dot(sa, wa_ref[...])
        ub, _ = _gelu(b_u)
        gv, _ = _gelu(b_v)
        vh, _ = _ln(gv)
        vnb = (vh * lbg_ref[...] + lbb_ref[...]).astype(BF)
        _sgu_mix(vnb, ws_ref, sb_ref, mixed_ref, ts)
        ob = (ub * mixed_ref[...]).astype(BF)
        yb = _dot(ob, wb_ref[...])
        merged = (_sigmoid(g_a) * ya + _sigmoid(g_b) * yb).astype(BF)
        o_ref[...] = x + _dot(merged, wo_ref[...])

    vec = _const_spec((1, D))
    sq = _const_spec((D, D))
    return pl.pallas_call(
        body, grid=(s // ts,), name="mix_fwd",
        out_shape=(jax.ShapeDtypeStruct((s, D), F32), jax.ShapeDtypeStruct((s, DIN), BF),
                   jax.ShapeDtypeStruct((s, D), BF), jax.ShapeDtypeStruct((s, D), F32)),
        in_specs=[_row_spec(ts, D), vec, _const_spec((NSH, D, INB)), _const_spec((1, DIN)),
                  _const_spec((HALO, D)), vec, vec, vec, sq, vec, vec,
                  _const_spec((NG, CHUNK, CHUNK)), _const_spec((CHUNK, D)), sq, sq],
        out_specs=(_row_spec(ts, D), _row_spec(ts, DIN), _row_spec(ts, D), _row_spec(ts, D)),
        scratch_shapes=[pltpu.VMEM((ts + HALO, D), F32), pltpu.VMEM((ts, D), F32)],
        compiler_params=_cparams(("arbitrary",)),
    )(h, gain, win, b_in, conv_w, conv_b, lna_g, lna_b, wa, lnb_g, lnb_b, ws, sbias, wb, wo)


def mix_bwd_branches(p, c, dh, lna_g, lna_b, wa, lnb_g, lnb_b, ws, wst, sbias, wb, wo):
    s = dh.shape[0]
    ts = _tile(s, 256)
    nsteps = s // ts

    def body(p_ref, c_ref, dh_ref, lag_ref, lab_ref, wa_ref, lbg_ref, lbb_ref, ws_ref, wst_ref, sb_ref,
             wb_ref, wo_ref, dc_ref, dp_ref, sa_ref, dya_ref, ob_ref, dyb_ref, mg_ref, dhb_ref,
             dws_ref, dsb_ref, dlag_ref, dlab_ref, dlbg_ref, dlbb_ref, mixed_ref, dmix_ref, dvn_ref, dsb_acc):
        step = pl.program_id(0)

        @pl.when(step == 0)
        def _():
            for ref in (dws_ref, dsb_acc, dlag_ref, dlab_ref, dlbg_ref, dlbb_ref):
                ref[...] = jnp.zeros_like(ref)

        b_u = p_ref[:, 2 * D:3 * D].astype(F32)
        b_v = p_ref[:, 3 * D:4 * D].astype(F32)
        sga = _sigmoid(p_ref[:, 4 * D:5 * D].astype(F32))
        sgb = _sigmoid(p_ref[:, 5 * D:6 * D].astype(F32))
        lag = lag_ref[...]
        ch, ra = _ln(c_ref[...])
        la = ch * lag + lab_ref[...]
        sla = _sigmoid(la)
        sa = (la * sla).astype(BF)
        sa_ref[...] = sa
        ya = _dot(sa, wa_ref[...])
        lbg = lbg_ref[...]
        ub, tu = _gelu(b_u)
        gv, tv = _gelu(b_v)
        vh, rb = _ln(gv)
        vnb = (vh * lbg + lbb_ref[...]).astype(BF)
        _sgu_mix(vnb, ws_ref, sb_ref, mixed_ref, ts)
        mixed = mixed_ref[...]
        ob = (ub * mixed).astype(BF)
        ob_ref[...] = ob
        yb = _dot(ob, wb_ref[...])
        mg_ref[...] = (sga * ya + sgb * yb).astype(BF)
        dhb = dh_ref[...].astype(BF)
        dhb_ref[...] = dhb
        dm = _dot_nt(dhb, wo_ref[...])
        dp_ref[:, 0:2 * D] = jnp.zeros((ts, 2 * D), BF)
        dp_ref[:, 4 * D:5 * D] = (dm * ya * sga * (1.0 - sga)).astype(BF)
        dp_ref[:, 5 * D:6 * D] = (dm * yb * sgb * (1.0 - sgb)).astype(BF)
        dya = (dm * sga).astype(BF)
        dya_ref[...] = dya
        dyb = (dm * sgb).astype(BF)
        dyb_ref[...] = dyb
        dla = _dot_nt(dya, wa_ref[...]) * (sla * (1.0 + la * (1.0 - sla)))
        dlag_ref[...] += _colsum(dla * ch)
        dlab_ref[...] += _colsum(dla)
        dc_ref[...] = _ln_bwd(dla, ch, ra, lag)
        dob = _dot_nt(dyb, wb_ref[...])
        dp_ref[:, 2 * D:3 * D] = (dob * mixed * _gelu_grad(b_u, tu)).astype(BF)
        dmix = dob * ub
        dmix_ref[...] = dmix.astype(BF)
        dsb = jnp.zeros((CHUNK, D), F32)
        for ci in range(ts // CHUNK):
            rows = slice(ci * CHUNK, (ci + 1) * CHUNK)
            dsb = dsb + dmix[rows, :]
            for g in range(NG):
                cols = slice(g * GD, (g + 1) * GD)
                dmb = dmix_ref[rows, cols]
                dws_ref[g] += _dot_nt(dmb, vnb[rows, cols])
                dvn_ref[rows, cols] = _dot(wst_ref[g], dmb)
        dsb_acc[...] += dsb
        dvn = dvn_ref[...]
        dlbg_ref[...] += _colsum(dvn * vh)
        dlbb_ref[...] += _colsum(dvn)
        dp_ref[:, 3 * D:4 * D] = (_ln_bwd(dvn, vh, rb, lbg) * _gelu_grad(b_v, tv)).astype(BF)

        @pl.when(step == nsteps - 1)
        def _():
            row = lax.broadcasted_iota(jnp.int32, (CHUNK, CHUNK), 0)
            col = lax.broadcasted_iota(jnp.int32, (CHUNK, CHUNK), 1)
            for g in range(NG):
                dws_ref[g] = jnp.where(col <= row, dws_ref[g], 0.0)
            acc = jnp.zeros((CHUNK, CHUNK), F32)
            for g in range(NG):
                tot = jnp.sum(dsb_acc[:, g * GD:(g + 1) * GD], axis=-1, keepdims=True)
                acc = acc + jnp.where(col == g, tot, 0.0)
            dsb_ref[...] = acc

    vec = _const_spec((1, D))
    sq = _const_spec((D, D))
    bf_rows = jax.ShapeDtypeStruct((s, D), BF)
    acc_vec = jax.ShapeDtypeStruct((1, D), F32)
    return pl.pallas_call(
        body, grid=(nsteps,), name="mix_bwd_branches",
        out_shape=(jax.ShapeDtypeStruct((s, D), F32), jax.ShapeDtypeStruct((s, DIN), BF),
                   bf_rows, bf_rows, bf_rows, bf_rows, bf_rows, bf_rows,
                   jax.ShapeDtypeStruct((NG, CHUNK, CHUNK), F32), jax.ShapeDtypeStruct((CHUNK, CHUNK), F32),
                   acc_vec, acc_vec, acc_vec, acc_vec),
        in_specs=[_row_spec(ts, DIN), _row_spec(ts, D), _row_spec(ts, D), vec, vec, sq, vec, vec,
                  _const_spec((NG, CHUNK, CHUNK)), _const_spec((NG, CHUNK, CHUNK)), _const_spec((CHUNK, D)),
                  sq, sq],
        out_specs=(_row_spec(ts, D), _row_spec(ts, DIN)) + (_row_spec(ts, D),) * 6
        + (_acc_spec((NG, CHUNK, CHUNK)), _acc_spec((CHUNK, CHUNK))) + (_acc_spec((1, D)),) * 4,
        scratch_shapes=[pltpu.VMEM((ts, D), F32), pltpu.VMEM((ts, D), BF), pltpu.VMEM((ts, D), F32),
                        pltpu.VMEM((CHUNK, D), F32)],
        compiler_params=_cparams(("arbitrary",)),
    )(p, c, dh, lna_g, lna_b, wa, lnb_g, lnb_b, ws, wst, sbias, wb, wo)


def conv_bwd(p, dc, dp, conv_w):
    s = dc.shape[0]
    ts = _tile(s, 256)
    nsteps = s // ts
    per = ts // HALO

    def body(pm_ref, pp_ref, dcm_ref, dcn_ref, cw_ref, dpin_ref, dp_ref, dw_ref, db_ref, ext_ref, dext_ref,
             dw8_ref):
        del dpin_ref
        step = pl.program_id(0)

        @pl.when(step == 0)
        def _():
            dw8_ref[...] = jnp.zeros_like(dw8_ref)
            db_ref[...] = jnp.zeros_like(db_ref)

        a_val = pm_ref[:, 0:D].astype(F32)
        sg = _sigmoid(pm_ref[:, D:2 * D].astype(F32))
        prev = pp_ref[:, 0:D].astype(F32) * _sigmoid(pp_ref[:, D:2 * D].astype(F32))
        ext_ref[0:HALO, :] = jnp.where(step > 0, prev, 0.0)
        ext_ref[HALO:HALO + ts, :] = a_val * sg
        dcm = dcm_ref[...]
        dext_ref[0:ts, :] = dcm
        dext_ref[ts:ts + HALO, :] = jnp.where(step < nsteps - 1, dcn_ref[...], 0.0)
        db_ref[...] += _colsum(dcm)
        dglu = jnp.zeros((ts, D), F32)
        for k in range(CW):
            dglu = dglu + cw_ref[k:k + 1, :] * dext_ref[CW - 1 - k:CW - 1 - k + ts, :]
            first = HALO - (CW - 1) + k
            prod = dcm * ext_ref[first:first + ts, :]
            dw8_ref[k] += jnp.sum(prod.reshape(ts // 8, 8, D), axis=0)
        dp_ref[:, 0:D] = (dglu * sg).astype(BF)
        dp_ref[:, D:2 * D] = (dglu * a_val * sg * (1.0 - sg)).astype(BF)

        @pl.when(step == nsteps - 1)
        def _():
            dw_ref[...] = jnp.zeros_like(dw_ref)
            for k in range(CW):
                dw_ref[k:k + 1, :] = _colsum(dw8_ref[k])

    return pl.pallas_call(
        body, grid=(nsteps,), name="conv_bwd",
        out_shape=(jax.ShapeDtypeStruct((s, DIN), BF), jax.ShapeDtypeStruct((HALO, D), F32),
                   jax.ShapeDtypeStruct((1, D), F32)),
        in_specs=[pl.BlockSpec((ts, 2 * D), lambda i: (i, 0)),
                  pl.BlockSpec((HALO, 2 * D), lambda i: (jnp.maximum(i * per - 1, 0), 0)),
                  _row_spec(ts, D),
                  pl.BlockSpec((HALO, D), lambda i: (jnp.minimum((i + 1) * per, s // HALO - 1), 0)),
                  _const_spec((HALO, D)),
                  pl.BlockSpec(memory_space=pl.ANY)],
        out_specs=(pl.BlockSpec((ts, 2 * D), lambda i: (i, 0)), _acc_spec((HALO, D)), _acc_spec((1, D))),
        scratch_shapes=[pltpu.VMEM((ts + HALO, D), F32), pltpu.VMEM((ts + HALO, D), F32),
                        pltpu.VMEM((HALO, 8, D), F32)],
        input_output_aliases={5: 0},
        compiler_params=_cparams(("arbitrary",)),
    )(p, p, dc, dc, conv_w, dp)


def mix_bwd_in(dp, h, dh, gain, win):
    s = h.shape[0]
    ts = _tile(s, 512)

    def body(dp_ref, h_ref, dh_ref, g_ref, win_ref, dx_ref, dg_ref, db_ref):
        @pl.when(pl.program_id(0) == 0)
        def _():
            dg_ref[...] = jnp.zeros_like(dg_ref)
            db_ref[...] = jnp.zeros_like(db_ref)

        gain_v = g_ref[...]
        xh, r = _rms(h_ref[...])
        dn = jnp.zeros((ts, D), F32)
        for k in range(NSH):
            dpk = dp_ref[:, k * INB:(k + 1) * INB]
            dn = dn + _dot_nt(dpk, win_ref[k])
            db_ref[:, k * INB:(k + 1) * INB] += _colsum(dpk.astype(F32))
        dg_ref[...] += _colsum(dn * xh)
        dx_ref[...] = dh_ref[...] + _rms_bwd(dn, xh, r, gain_v)

    return pl.pallas_call(
        body, grid=(s // ts,), name="mix_bwd_in",
        out_shape=(jax.ShapeDtypeStruct((s, D), F32), jax.ShapeDtypeStruct((1, D), F32),
                   jax.ShapeDtypeStruct((1, DIN), F32)),
        in_specs=[_row_spec(ts, DIN), _row_spec(ts, D), _row_spec(ts, D), _const_spec((1, D)),
                  _const_spec((NSH, D, INB))],
        out_specs=(_row_spec(ts, D), _acc_spec((1, D)), _acc_spec((1, DIN))),
        compiler_params=_cparams(("arbitrary",)),
    )(dp, h, dh, gain, win)


def kv_proj(mem, gain, wkv):
    def body(m_ref, g_ref, w_ref, k_ref, v_ref, n_ref):
        xh, _ = _rms(m_ref[...])
        nb = (xh * g_ref[...]).astype(BF)
        n_ref[...] = nb
        half = D // 2
        for j in range(2):
            k_ref[:, j * half:(j + 1) * half] = _dot(nb, w_ref[j]).astype(BF)
            v_ref[:, j * half:(j + 1) * half] = _dot(nb, w_ref[2 + j]).astype(BF)

    o = jax.ShapeDtypeStruct((NMEM, D), BF)
    return pl.pallas_call(body, name="kv_proj", out_shape=(o, o, o), compiler_params=_cparams())(mem, gain, wkv)


def kv_bwd(mem, gain, memn, wkv, dk, dv):
    def body(m_ref, g_ref, n_ref, w_ref, dk_ref, dv_ref, dw_ref, dg_ref):
        xh, _ = _rms(m_ref[...])
        nb = n_ref[...]
        half = D // 2
        dn = jnp.zeros((NMEM, D), F32)
        for j in range(2):
            dkb = dk_ref[:, j * half:(j + 1) * half].astype(BF)
            dvb = dv_ref[:, j * half:(j + 1) * half].astype(BF)
            dw_ref[j] = _dot_tn(nb, dkb)
            dw_ref[2 + j] = _dot_tn(nb, dvb)
            dn = dn + _dot_nt(dkb, w_ref[j]) + _dot_nt(dvb, w_ref[2 + j])
        dg_ref[...] = _colsum(dn * xh)

    return pl.pallas_call(
        body, name="kv_bwd",
        out_shape=(jax.ShapeDtypeStruct((NSH, D, D // 2), F32), jax.ShapeDtypeStruct((1, D), F32)),
        compiler_params=_cparams())(mem, gain, memn, wkv, dk, dv)


def _attend(qb, k_ref, v_ref, h):
    cols = slice(h * HD, (h + 1) * HD)
    sc = _dot_nt(qb[:, cols], k_ref[:, cols]) * ATT_SCALE
    e = jnp.exp(sc - jnp.max(sc, axis=-1, keepdims=True))
    pr = e / jnp.sum(e, axis=-1, keepdims=True)
    return pr, _dot(pr.astype(BF), v_ref[:, cols])


def xattn_fwd(h, gain, wq, k, v, wo):
    s = h.shape[0]
    ts = _tile(s, 512)

    def body(h_ref, g_ref, wq_ref, k_ref, v_ref, wo_ref, o_ref, att_ref):
        x = h_ref[...]
        xh, _ = _rms(x)
        nb = (xh * g_ref[...]).astype(BF)
        qb = _dot(nb, wq_ref[...]).astype(BF)
        for hd in range(NH):
            _, oh = _attend(qb, k_ref, v_ref, hd)
            att_ref[:, hd * HD:(hd + 1) * HD] = oh.astype(BF)
        o_ref[...] = x + _dot(att_ref[...], wo_ref[...])

    sq = _const_spec((D, D))
    kvs = _const_spec((NMEM, D))
    return pl.pallas_call(
        body, grid=(s // ts,), name="xattn_fwd",
        out_shape=jax.ShapeDtypeStruct((s, D), F32),
        in_specs=[_row_spec(ts, D), _const_spec((1, D)), sq, kvs, kvs, sq],
        out_specs=_row_spec(ts, D),
        scratch_shapes=[pltpu.VMEM((ts, D), BF)],
        compiler_params=_cparams(("arbitrary",)),
    )(h, gain, wq, k, v, wo)


def xattn_bwd(h, dh, gain, wq, k, v, wo):
    s = h.shape[0]
    ts = _tile(s, 256)

    def body(h_ref, dh_ref, g_ref, wq_ref, k_ref, v_ref, wo_ref,
             dx_ref, n_ref, dq_ref, att_ref, dhb_ref, dk_ref, dv_ref, dg_ref):
        @pl.when(pl.program_id(0) == 0)
        def _():
            for ref in (dk_ref, dv_ref, dg_ref):
                ref[...] = jnp.zeros_like(ref)

        x = h_ref[...]
        dh = dh_ref[...]
        gain_v = g_ref[...]
        xh, r = _rms(x)
        nb = (xh * gain_v).astype(BF)
        n_ref[...] = nb
        qb = _dot(nb, wq_ref[...]).astype(BF)
        dhb = dh.astype(BF)
        dhb_ref[...] = dhb
        dob = _dot_nt(dhb, wo_ref[...]).astype(BF)
        for hd in range(NH):
            cols = slice(hd * HD, (hd + 1) * HD)
            pr, oh = _attend(qb, k_ref, v_ref, hd)
            att_ref[:, cols] = oh.astype(BF)
            doh = dob[:, cols]
            dpr = _dot_nt(doh, v_ref[:, cols])
            dv_ref[:, cols] += _dot_tn(pr.astype(BF), doh)
            dsc = (pr * (dpr - jnp.sum(dpr * pr, axis=-1, keepdims=True)) * ATT_SCALE).astype(BF)
            dq_ref[:, cols] = _dot(dsc, k_ref[:, cols]).astype(BF)
            dk_ref[:, cols] += _dot_tn(dsc, qb[:, cols])
        dn = _dot_nt(dq_ref[...], wq_ref[...])
        dg_ref[...] += _colsum(dn * xh)
        dx_ref[...] = dh + _rms_bwd(dn, xh, r, gain_v)

    sq = _const_spec((D, D))
    kvs = _const_spec((NMEM, D))
    bf_rows = jax.ShapeDtypeStruct((s, D), BF)
    kv_acc = jax.ShapeDtypeStruct((NMEM, D), F32)
    return pl.pallas_call(
        body, grid=(s // ts,), name="xattn_bwd",
        out_shape=(jax.ShapeDtypeStruct((s, D), F32), bf_rows, bf_rows, bf_rows, bf_rows, kv_acc, kv_acc,
                   jax.ShapeDtypeStruct((1, D), F32)),
        in_specs=[_row_spec(ts, D), _row_spec(ts, D), _const_spec((1, D)), sq, kvs, kvs, sq],
        out_specs=(_row_spec(ts, D),) * 5 + (_acc_spec((NMEM, D)), _acc_spec((NMEM, D)), _acc_spec((1, D))),
        compiler_params=_cparams(("arbitrary",)),
    )(h, dh, gain, wq, k, v, wo)


BLOCK_BYTES = 3 << 19


def _row_block(rows, cols):
    rb = rows
    while rb * cols * 4 > BLOCK_BYTES and rb % 32 == 0:
        rb //= 2
    return rb


def cast_bf16(w, name):
    r, c = w.shape
    rb = _row_block(r, c)

    def body(w_ref, o_ref):
        o_ref[...] = w_ref[...].astype(BF)

    return pl.pallas_call(
        body, grid=(r // rb,), name=name, out_shape=jax.ShapeDtypeStruct((r, c), BF),
        in_specs=[_row_spec(rb, c)], out_specs=_row_spec(rb, c),
        compiler_params=_cparams(("arbitrary",)))(w)


def pair_add(g4, recv, core, name):
    nsh, _, rh, c = g4.shape
    rb = _row_block(rh, c)

    def body(core_ref, g_ref, r_ref, o_ref, ob_ref):
        del core_ref
        sm = g_ref[0, 0] + r_ref[0]
        o_ref[0] = sm
        ob_ref[0] = sm.astype(BF)

    spec3 = pl.BlockSpec((1, rb, c), lambda k, i, core_ref: (k, i, 0))
    return pl.pallas_call(
        body, name=name,
        out_shape=(jax.ShapeDtypeStruct((nsh, rh, c), F32), jax.ShapeDtypeStruct((nsh, rh, c), BF)),
        grid_spec=pltpu.PrefetchScalarGridSpec(
            num_scalar_prefetch=1, grid=(nsh, rh // rb),
            in_specs=[pl.BlockSpec((1, 1, rb, c), lambda k, i, core_ref: (k, core_ref[0], i, 0)), spec3],
            out_specs=(spec3, spec3)),
        compiler_params=_cparams(("arbitrary", "arbitrary")))(core, g4, recv)


def chip_sum(psum, recv, chip, name):
    _, rh, c = psum.shape
    rb = _row_block(rh, c)

    def body(chip_ref, p_ref, r_ref, o_ref):
        del chip_ref
        acc = p_ref[0]
        for j in range(NSH - 1):
            acc = acc + r_ref[j].astype(F32)
        o_ref[...] = acc

    return pl.pallas_call(
        body, name=name, out_shape=jax.ShapeDtypeStruct((rh, c), F32),
        grid_spec=pltpu.PrefetchScalarGridSpec(
            num_scalar_prefetch=1, grid=(rh // rb,),
            in_specs=[pl.BlockSpec((1, rb, c), lambda i, chip_ref: (chip_ref[0], i, 0)),
                      pl.BlockSpec((NSH - 1, rb, c), lambda i, chip_ref: (0, i, 0))],
            out_specs=pl.BlockSpec((rb, c), lambda i, chip_ref: (i, 0))),
        compiler_params=_cparams(("arbitrary",)))(chip, psum, recv)


def _adamw_math(w, g, m, v):
    m = ADAM_B1 * m + (1.0 - ADAM_B1) * g
    v = ADAM_B2 * v + (1.0 - ADAM_B2) * (g * g)
    m_hat = m / (1.0 - ADAM_B1 ** ADAM_STEP)
    v_hat = v / (1.0 - ADAM_B2 ** ADAM_STEP)
    delta = -ADAM_LR * (m_hat / (jnp.sqrt(v_hat) + ADAM_EPS) + ADAM_WD * w)
    return delta, m, v


def adamw(w, g, m, v, name):
    r, c = w.shape
    rb = _row_block(r, c)

    def body(w_ref, g_ref, m_ref, v_ref, d_ref, mo_ref, vo_ref):
        d, mn, vn = _adamw_math(w_ref[...], g_ref[...], m_ref[...], v_ref[...])
        d_ref[...] = d
        mo_ref[...] = mn
        vo_ref[...] = vn

    o = jax.ShapeDtypeStruct((r, c), F32)
    spec = _row_spec(rb, c)
    return pl.pallas_call(
        body, grid=(r // rb,), name=name, out_shape=(o, o, o),
        in_specs=[spec] * 4, out_specs=(spec,) * 3,
        compiler_params=_cparams(("arbitrary",)))(w, g, m, v)


HBM_SPEC = pl.BlockSpec(memory_space=pltpu.HBM)


def _place():
    return lax.axis_index("x"), lax.axis_index("y"), lax.axis_index("c")


def _other_chips(x, y):
    return [(1 - x, y), (x, 1 - y), (1 - x, 1 - y)]


def gather_weights(shards):
    nw = len(shards)
    nchip = NSH - 1

    def body(*refs):
        ins, outs = refs[:nw], refs[nw:2 * nw]
        send_sems, recv_sems, local_sems = refs[2 * nw:]
        x, y, c = _place()
        chips = _other_chips(x, y)
        local, sends = [], []
        for w in range(nw):
            cp = pltpu.make_async_copy(ins[w], outs[w].at[2 * x + y], local_sems.at[w])
            cp.start()
            local.append(cp)
            for j, (px, py) in enumerate(chips):
                cp = pltpu.make_async_remote_copy(
                    src_ref=ins[w], dst_ref=outs[w].at[2 * x + y],
                    send_sem=send_sems.at[w * nchip + j], recv_sem=recv_sems.at[w * nchip + j],
                    device_id=(px, py, c), device_id_type=MESH)
                cp.start()
                sends.append(cp)
        for w in range(nw):
            for j, (px, py) in enumerate(chips):
                pltpu.make_async_remote_copy(
                    src_ref=ins[w], dst_ref=outs[w].at[2 * px + py],
                    send_sem=send_sems.at[w * nchip + j], recv_sem=recv_sems.at[w * nchip + j],
                    device_id=(px, py, c), device_id_type=MESH).wait_recv()
        for cp in sends:
            cp.wait_send()
        for cp in local:
            cp.wait()

    return pl.pallas_call(
        body, name="gather_weights",
        out_shape=tuple(jax.ShapeDtypeStruct((NSH,) + s.shape, s.dtype) for s in shards),
        in_specs=[HBM_SPEC] * nw, out_specs=(HBM_SPEC,) * nw,
        scratch_shapes=[pltpu.SemaphoreType.DMA((nw * nchip,)), pltpu.SemaphoreType.DMA((nw * nchip,)),
                        pltpu.SemaphoreType.DMA((nw,))],
        compiler_params=pltpu.CompilerParams(has_side_effects=True),
    )(*shards)


def pair_exchange(grads):
    nw = len(grads)

    def body(*refs):
        ins, outs = refs[:nw], refs[nw:2 * nw]
        send_sems, recv_sems = refs[2 * nw:]
        x, y, c = _place()
        cps = []
        for w in range(nw):
            cp = pltpu.make_async_remote_copy(
                src_ref=ins[w].at[:, 1 - c], dst_ref=outs[w],
                send_sem=send_sems.at[w], recv_sem=recv_sems.at[w],
                device_id=(x, y, 1 - c), device_id_type=MESH)
            cp.start()
            cps.append(cp)
        for cp in cps:
            cp.wait()

    return pl.pallas_call(
        body, name="pair_exchange",
        out_shape=tuple(jax.ShapeDtypeStruct((g.shape[0],) + g.shape[2:], g.dtype) for g in grads),
        in_specs=[HBM_SPEC] * nw, out_specs=(HBM_SPEC,) * nw,
        scratch_shapes=[pltpu.SemaphoreType.DMA((nw,)), pltpu.SemaphoreType.DMA((nw,))],
        compiler_params=pltpu.CompilerParams(has_side_effects=True),
    )(*grads)


def chip_exchange(psums):
    nw = len(psums)
    nchip = NSH - 1

    def body(*refs):
        ins, outs = refs[:nw], refs[nw:2 * nw]
        send_sems, recv_sems = refs[2 * nw:]
        x, y, c = _place()
        chips = _other_chips(x, y)
        cps = []
        for w in range(nw):
            for j, (px, py) in enumerate(chips):
                cp = pltpu.make_async_remote_copy(
                    src_ref=ins[w].at[2 * px + py], dst_ref=outs[w].at[j],
                    send_sem=send_sems.at[w * nchip + j], recv_sem=recv_sems.at[w * nchip + j],
                    device_id=(px, py, c), device_id_type=MESH)
                cp.start()
                cps.append(cp)
        for cp in cps:
            cp.wait()

    return pl.pallas_call(
        body, name="chip_exchange",
        out_shape=tuple(jax.ShapeDtypeStruct((nchip,) + p.shape[1:], p.dtype) for p in psums),
        in_specs=[HBM_SPEC] * nw, out_specs=(HBM_SPEC,) * nw,
        scratch_shapes=[pltpu.SemaphoreType.DMA((nw * nchip,)), pltpu.SemaphoreType.DMA((nw * nchip,))],
        compiler_params=pltpu.CompilerParams(has_side_effects=True),
    )(*psums)


def pair_swap(halves):
    nw = len(halves)

    def body(*refs):
        ins, outs = refs[:nw], refs[nw:2 * nw]
        send_sems, recv_sems, local_sems = refs[2 * nw:]
        x, y, c = _place()
        local, cps = [], []
        for w in range(nw):
            cp = pltpu.make_async_copy(ins[w], outs[w].at[c], local_sems.at[w])
            cp.start()
            local.append(cp)
            cp = pltpu.make_async_remote_copy(
                src_ref=ins[w], dst_ref=outs[w].at[c],
                send_sem=send_sems.at[w], recv_sem=recv_sems.at[w],
                device_id=(x, y, 1 - c), device_id_type=MESH)
            cp.start()
            cps.append(cp)
        for w in range(nw):
            pltpu.make_async_remote_copy(
                src_ref=ins[w], dst_ref=outs[w].at[1 - c],
                send_sem=send_sems.at[w], recv_sem=recv_sems.at[w],
                device_id=(x, y, 1 - c), device_id_type=MESH).wait_recv()
        for cp in cps:
            cp.wait_send()
        for cp in local:
            cp.wait()

    return pl.pallas_call(
        body, name="pair_swap",
        out_shape=tuple(jax.ShapeDtypeStruct((2,) + h.shape, h.dtype) for h in halves),
        in_specs=[HBM_SPEC] * nw, out_specs=(HBM_SPEC,) * nw,
        scratch_shapes=[pltpu.SemaphoreType.DMA((nw,)), pltpu.SemaphoreType.DMA((nw,)),
                        pltpu.SemaphoreType.DMA((nw,))],
        compiler_params=pltpu.CompilerParams(has_side_effects=True),
    )(*halves)


NDEV = 8


def small_allreduce(buf):
    r = buf.shape[0]

    def body(b_ref, o_ref, slots_ref, send_sems, recv_sems):
        x, y, c = _place()
        me = 4 * x + 2 * y + c
        slots_ref[me] = b_ref[...]
        cps = []
        for rel in range(1, NDEV):
            peer = (x ^ (rel >> 2), y ^ ((rel >> 1) & 1), c ^ (rel & 1))
            cp = pltpu.make_async_remote_copy(
                src_ref=b_ref, dst_ref=slots_ref.at[me],
                send_sem=send_sems.at[rel - 1], recv_sem=recv_sems.at[rel - 1],
                device_id=peer, device_id_type=MESH)
            cp.start()
            cps.append(cp)
        for rel in range(1, NDEV):
            peer = (x ^ (rel >> 2), y ^ ((rel >> 1) & 1), c ^ (rel & 1))
            pltpu.make_async_remote_copy(
                src_ref=b_ref, dst_ref=slots_ref.at[4 * peer[0] + 2 * peer[1] + peer[2]],
                send_sem=send_sems.at[rel - 1], recv_sem=recv_sems.at[rel - 1],
                device_id=peer, device_id_type=MESH).wait_recv()
        for cp in cps:
            cp.wait_send()
        acc = slots_ref[0]
        for dev in range(1, NDEV):
            acc = acc + slots_ref[dev]
        o_ref[...] = acc

    vm = pl.BlockSpec(memory_space=pltpu.VMEM)
    return pl.pallas_call(
        body, name="small_allreduce", out_shape=jax.ShapeDtypeStruct((r, D), F32),
        in_specs=[vm], out_specs=vm,
        scratch_shapes=[pltpu.VMEM((NDEV, r, D), F32), pltpu.SemaphoreType.DMA((NDEV - 1,)),
                        pltpu.SemaphoreType.DMA((NDEV - 1,))],
        compiler_params=_cparams(has_side_effects=True),
    )(buf)


BIG = ("ffn1_w_gu", "ffn1_w_down", "w_in", "w_a_out", "w_b_out", "w_out", "w_q", "w_kv", "w_o",
       "ffn2_w_gu", "ffn2_w_down")
SMALL = {"ffn1_norm": (0, 1), "mix_norm": (8, 1), "xattn_norm": (16, 1), "mem_norm": (24, 1),
         "ffn2_norm": (32, 1), "final_norm": (40, 1), "conv_b": (48, 1), "conv_ln_g": (56, 1),
         "conv_ln_b": (64, 1), "sgu_ln_g": (72, 1), "sgu_ln_b": (80, 1), "b_in": (88, 6),
         "conv_w": (96, CW), "sgu_w": (128, 64), "sgu_b": (192, 1)}
LOSS_ROW = 200
SMALL_ROWS = 208


def _pad_rows(a, rows):
    return jnp.pad(a, ((0, rows - a.shape[0]), (0, D - a.shape[1])))


def _pack_small(parts):
    names = sorted(parts, key=lambda n: SMALL[n][0] if n in SMALL else LOSS_ROW)
    rows = []
    for i, n in enumerate(names):
        start = SMALL[n][0] if n in SMALL else LOSS_ROW
        end = SMALL_ROWS if i + 1 == len(names) else (SMALL[names[i + 1]][0] if names[i + 1] in SMALL else LOSS_ROW)
        rows.append(_pad_rows(parts[n], end - start))
    return jnp.concatenate(rows, axis=0)


def _small_views(w):
    return {
        "ffn1_norm": w["ffn1_norm"], "mix_norm": w["mix_norm"], "xattn_norm": w["xattn_norm"],
        "mem_norm": w["mem_norm"], "ffn2_norm": w["ffn2_norm"], "final_norm": w["final_norm"].reshape(1, D),
        "conv_b": w["conv_b"], "conv_ln_g": w["conv_ln_g"], "conv_ln_b": w["conv_ln_b"],
        "sgu_ln_g": w["sgu_ln_g"], "sgu_ln_b": w["sgu_ln_b"], "b_in": w["b_in"].reshape(6, D),
        "conv_w": w["conv_w"][0], "sgu_w": w["sgu_w"].reshape(64, D), "sgu_b": w["sgu_b"].reshape(1, NG * CHUNK),
    }


def _unpack_small(buf, like, chip):
    out = {}
    for n, (start, rows) in SMALL.items():
        blk = buf[start:start + rows]
        if n == "conv_w":
            blk = blk[:, :like[n].shape[-1]] if chip is None else lax.dynamic_slice_in_dim(
                blk, chip * like[n].shape[-1], like[n].shape[-1], axis=1)
        elif n == "sgu_b":
            blk = blk[:, :NG * CHUNK]
        out[n] = blk.reshape(like[n].shape)
    return out


def kernel(x, mem, ffn1_norm, ffn1_w_gu, ffn1_w_down, mix_norm, w_in, b_in, conv_w, conv_b, conv_ln_g, conv_ln_b, w_a_out, sgu_ln_g, sgu_ln_b, sgu_w, sgu_b, w_b_out, w_out, xattn_norm, mem_norm, w_q, w_kv, w_o, ffn2_norm, ffn2_w_gu, ffn2_w_down, final_norm, loss_target, m_ffn1_norm, m_ffn1_w_gu, m_ffn1_w_down, m_mix_norm, m_w_in, m_b_in, m_conv_w, m_conv_b, m_conv_ln_g, m_conv_ln_b, m_w_a_out, m_sgu_ln_g, m_sgu_ln_b, m_sgu_w, m_sgu_b, m_w_b_out, m_w_out, m_xattn_norm, m_mem_norm, m_w_q, m_w_kv, m_w_o, m_ffn2_norm, m_ffn2_w_gu, m_ffn2_w_down, m_final_norm, v_ffn1_norm, v_ffn1_w_gu, v_ffn1_w_down, v_mix_norm, v_w_in, v_b_in, v_conv_w, v_conv_b, v_conv_ln_g, v_conv_ln_b, v_w_a_out, v_sgu_ln_g, v_sgu_ln_b, v_sgu_w, v_sgu_b, v_w_b_out, v_w_out, v_xattn_norm, v_mem_norm, v_w_q, v_w_kv, v_w_o, v_ffn2_norm, v_ffn2_w_gu, v_ffn2_w_down, v_final_norm):
    names = ("ffn1_norm", "ffn1_w_gu", "ffn1_w_down", "mix_norm", "w_in", "b_in", "conv_w", "conv_b",
             "conv_ln_g", "conv_ln_b", "w_a_out", "sgu_ln_g", "sgu_ln_b", "sgu_w", "sgu_b", "w_b_out", "w_out",
             "xattn_norm", "mem_norm", "w_q", "w_kv", "w_o", "ffn2_norm", "ffn2_w_gu", "ffn2_w_down",
             "final_norm")
    wts = dict(zip(names, (ffn1_norm, ffn1_w_gu, ffn1_w_down, mix_norm, w_in, b_in, conv_w, conv_b, conv_ln_g,
                           conv_ln_b, w_a_out, sgu_ln_g, sgu_ln_b, sgu_w, sgu_b, w_b_out, w_out, xattn_norm,
                           mem_norm, w_q, w_kv, w_o, ffn2_norm, ffn2_w_gu, ffn2_w_down, final_norm)))
    mom1 = dict(zip(names, (m_ffn1_norm, m_ffn1_w_gu, m_ffn1_w_down, m_mix_norm, m_w_in, m_b_in, m_conv_w,
                            m_conv_b, m_conv_ln_g, m_conv_ln_b, m_w_a_out, m_sgu_ln_g, m_sgu_ln_b, m_sgu_w,
                            m_sgu_b, m_w_b_out, m_w_out, m_xattn_norm, m_mem_norm, m_w_q, m_w_kv, m_w_o,
                            m_ffn2_norm, m_ffn2_w_gu, m_ffn2_w_down, m_final_norm)))
    mom2 = dict(zip(names, (v_ffn1_norm, v_ffn1_w_gu, v_ffn1_w_down, v_mix_norm, v_w_in, v_b_in, v_conv_w,
                            v_conv_b, v_conv_ln_g, v_conv_ln_b, v_w_a_out, v_sgu_ln_g, v_sgu_ln_b, v_sgu_w,
                            v_sgu_b, v_w_b_out, v_w_out, v_xattn_norm, v_mem_norm, v_w_q, v_w_kv, v_w_o,
                            v_ffn2_norm, v_ffn2_w_gu, v_ffn2_w_down, v_final_norm)))
    xi, yi, ci = _place()
    chip = (2 * xi + yi).astype(jnp.int32)
    chip_arr = chip.reshape(1)
    core_arr = ci.astype(jnp.int32).reshape(1)

    shards = [cast_bf16(wts[n][0], "cast_" + n) for n in BIG]
    shards.append(jnp.pad(conv_w[0], ((0, HALO - CW), (0, 0))))
    full = dict(zip(BIG + ("conv_w",), gather_weights(shards)))
    dx, grads, small_grads = local_step(x[0], mem[0], loss_target[0], full, wts)

    g4 = []
    for n in BIG:
        rs, cs = wts[n].shape[1:]
        g4.append(grads[n].reshape(NSH, 2, rs // 2, cs))
    recv_pair = pair_exchange(g4)
    psum_f32, psum_bf = [], []
    for n, g, r in zip(BIG, g4, recv_pair):
        a, b = pair_add(g, r, core_arr, "pair_add_" + n)
        psum_f32.append(a)
        psum_bf.append(b)
    recv_chip = chip_exchange(psum_bf)
    halves = [chip_sum(p, r, chip_arr, "chip_sum_" + n) for n, p, r in zip(BIG, psum_f32, recv_chip)]
    gshard = {n: g.reshape(wts[n].shape[1:]) for n, g in zip(BIG, pair_swap(halves))}

    small = small_allreduce(_pack_small(small_grads))
    loss = (0.5 / D) * jnp.sum(small[LOSS_ROW])
    gsmall = _unpack_small(small, wts, chip)

    out_g, out_d, out_m, out_v = dict(gsmall), {}, {}, {}
    sw, sm, sv = (_pack_small(_small_views(t))[:LOSS_ROW] for t in (wts, mom1, mom2))
    sg = _pack_small(_small_views({n: gsmall[n] for n in SMALL}))[:LOSS_ROW]
    for dst, packed in zip((out_d, out_m, out_v), adamw(sw, sg, sm, sv, "adamw_small")):
        dst.update(_unpack_small(packed, wts, None))
    for n in BIG:
        shape = wts[n].shape
        out_g[n] = gshard[n].reshape(shape)
        d, mn, vn = adamw(wts[n][0], gshard[n], mom1[n][0], mom2[n][0], "adamw_" + n)
        out_d[n], out_m[n], out_v[n] = d.reshape(shape), mn.reshape(shape), vn.reshape(shape)
    return (loss, dx[None], *[out_g[n] for n in names], *[out_d[n] for n in names],
            *[out_m[n] for n in names], *[out_v[n] for n in names])


def local_step(x2, mem2, tgt, full, wts):
    (ffn1_norm, mix_norm, b_in, conv_b, conv_ln_g, conv_ln_b, sgu_ln_g, sgu_ln_b, sgu_w, sgu_b, xattn_norm,
     mem_norm, ffn2_norm, final_norm) = (wts[n] for n in (
         "ffn1_norm", "mix_norm", "b_in", "conv_b", "conv_ln_g", "conv_ln_b", "sgu_ln_g", "sgu_ln_b", "sgu_w",
         "sgu_b", "xattn_norm", "mem_norm", "ffn2_norm", "final_norm"))
    wgu1, wgu2, win, wkv = full["ffn1_w_gu"], full["ffn2_w_gu"], full["w_in"], full["w_kv"]
    wd1, wd2 = full["ffn1_w_down"].reshape(FF, D), full["ffn2_w_down"].reshape(FF, D)
    wa, wb, wout = (full[n].reshape(D, D) for n in ("w_a_out", "w_b_out", "w_out"))
    wq, wo = full["w_q"].reshape(D, D), full["w_o"].reshape(D, D)
    cw_full = jnp.transpose(full["conv_w"], (1, 0, 2)).reshape(HALO, D)
    tril = jnp.tril(jnp.ones((CHUNK, CHUNK), dtype=bool))
    ws = jnp.where(tril[None], sgu_w[0], 0.0).astype(BF)
    wst = jnp.transpose(ws, (0, 2, 1))
    sbias = jnp.repeat(jnp.transpose(sgu_b[0]), GD, axis=1)
    gfin = final_norm.reshape(1, D)

    h1 = ffn_fwd(x2, ffn1_norm, wgu1, wd1)
    h2, proj, n2b, conv_out = mix_fwd(h1, mix_norm, win, b_in, cw_full, conv_b, conv_ln_g, conv_ln_b, wa,
                                      sgu_ln_g, sgu_ln_b, ws, sbias, wb, wout)
    kb, vb, memn = kv_proj(mem2, mem_norm, wkv)
    h3 = xattn_fwd(h2, xattn_norm, wq, kb, vb, wo)
    dh4, loss_lanes, d_final = ffn_fwd_loss(h3, ffn2_norm, wgu2, wd2, gfin, tgt)

    grads = {}
    dh3, n4, a4, dgu4, dhb4, d_ffn2n = ffn_bwd(h3, dh4, ffn2_norm, wgu2, wd2, "ffn2_bwd")
    grads["ffn2_w_gu"] = dw_matmul(n4, dgu4, NSH, "dw_ffn2_gu")
    grads["ffn2_w_down"] = dw_matmul(a4, dhb4, 1, "dw_ffn2_down")
    dh2, n3, dq, att, dhb3, dk, dv, d_xn = xattn_bwd(h2, dh3, xattn_norm, wq, kb, vb, wo)
    grads["w_q"] = dw_matmul(n3, dq, 1, "dw_q")
    grads["w_o"] = dw_matmul(att, dhb3, 1, "dw_o")
    grads["w_kv"], d_memn = kv_bwd(mem2, mem_norm, memn, wkv, dk, dv)
    (dconv, dproj, sa, dya, ob, dyb, mg, dhb2, d_sgu_w, d_sgu_b, d_lna_g, d_lna_b, d_lnb_g,
     d_lnb_b) = mix_bwd_branches(proj, conv_out, dh2, conv_ln_g, conv_ln_b, wa, sgu_ln_g, sgu_ln_b, ws, wst,
                                 sbias, wb, wout)
    grads["w_a_out"] = dw_matmul(sa, dya, 1, "dw_a_out")
    grads["w_b_out"] = dw_matmul(ob, dyb, 1, "dw_b_out")
    grads["w_out"] = dw_matmul(mg, dhb2, 1, "dw_out")
    dproj, d_conv_w, d_conv_b = conv_bwd(proj, dconv, dproj, cw_full)
    dh1, d_mixn, d_b_in = mix_bwd_in(dproj, h1, dh2, mix_norm, win)
    grads["w_in"] = dw_matmul(n2b, dproj, NSH, "dw_in")
    dx, n1, a1, dgu1, dhb1, d_ffn1n = ffn_bwd(x2, dh1, ffn1_norm, wgu1, wd1, "ffn1_bwd")
    grads["ffn1_w_gu"] = dw_matmul(n1, dgu1, NSH, "dw_ffn1_gu")
    grads["ffn1_w_down"] = dw_matmul(a1, dhb1, 1, "dw_ffn1_down")
    small_grads = {
        "ffn1_norm": d_ffn1n, "mix_norm": d_mixn, "xattn_norm": d_xn, "mem_norm": d_memn, "ffn2_norm": d_ffn2n,
        "final_norm": d_final, "conv_b": d_conv_b, "conv_ln_g": d_lna_g, "conv_ln_b": d_lna_b,
        "sgu_ln_g": d_lnb_g, "sgu_ln_b": d_lnb_b, "b_in": d_b_in.reshape(6, D), "conv_w": d_conv_w[:CW],
        "sgu_w": d_sgu_w.reshape(64, D), "sgu_b": jnp.transpose(d_sgu_b[:, :NG]).reshape(1, NG * CHUNK),
        "loss": loss_lanes}
    return dx, grads, small_grads
```

```python
import functools
import math

import jax
import jax.numpy as jnp
from jax import lax
from jax.experimental import pallas as pl
from jax.experimental.pallas import tpu as pltpu

F32 = jnp.float32
BF = jnp.bfloat16
MESH = pl.DeviceIdType.MESH

D = 1024
FF = 2816
HC = FF // 2
NSH = 4
DIN = 6 * D
INB = DIN // NSH
CW = 31
HALO = 32
CHUNK = 128
NG = 4
GD = D // NG
NH = 4
HD = D // NH
NMEM = 256
EPS_RMS = 1e-6
EPS_LN = 1e-5
GELU_C0 = math.sqrt(2.0 / math.pi)
GELU_C1 = 0.044715
ATT_SCALE = 1.0 / math.sqrt(HD)

ADAM_LR = 0.001
ADAM_B1 = 0.9
ADAM_B2 = 0.999
ADAM_EPS = 1e-08
ADAM_WD = 0.01
ADAM_STEP = 10

VMEM_LIMIT = 56 * 1024 * 1024


def _cparams(sem=None, **kw):
    if sem is not None:
        kw["dimension_semantics"] = sem
    return pltpu.CompilerParams(vmem_limit_bytes=VMEM_LIMIT, **kw)


def _dot(a, b):
    return jnp.dot(a, b, preferred_element_type=F32)


def _dot_nt(a, b):
    return lax.dot_general(a, b, (((1,), (1,)), ((), ())), preferred_element_type=F32)


def _dot_tn(a, b):
    return lax.dot_general(a, b, (((0,), (0,)), ((), ())), preferred_element_type=F32)


def _sigmoid(x):
    return 1.0 / (1.0 + jnp.exp(-x))


def _gelu(x):
    t = jnp.tanh(GELU_C0 * (x + GELU_C1 * (x * x * x)))
    return 0.5 * x * (1.0 + t), t


def _gelu_grad(x, t):
    return 0.5 * (1.0 + t) + 0.5 * x * (1.0 - t * t) * (GELU_C0 * (1.0 + 3.0 * GELU_C1 * x * x))


def _mean(x):
    return jnp.mean(x, axis=-1, keepdims=True)


def _rms(x):
    r = lax.rsqrt(_mean(x * x) + EPS_RMS)
    return x * r, r


def _rms_bwd(dn, xh, r, g):
    dxh = dn * g
    return r * (dxh - xh * _mean(dxh * xh))


def _ln(x):
    xc = x - _mean(x)
    r = lax.rsqrt(_mean(xc * xc) + EPS_LN)
    return xc * r, r


def _ln_bwd(dy, xh, r, g):
    dxh = dy * g
    return r * (dxh - _mean(dxh) - xh * _mean(dxh * xh))


def _colsum(x):
    return jnp.sum(x, axis=0, keepdims=True)


def _const_spec(shape):
    nd = len(shape)
    return pl.BlockSpec(shape, lambda *_: (0,) * nd, pipeline_mode=pl.Buffered(1))


def _row_spec(ts, width):
    return pl.BlockSpec((ts, width), lambda i: (i, 0))


def _acc_spec(shape):
    nd = len(shape)
    return pl.BlockSpec(shape, lambda *_: (0,) * nd)


def _tile(s, want):
    return min(s, want)


HBM_SPEC = pl.BlockSpec(memory_space=pltpu.HBM)


class Rider:
    def __init__(self, ins, outs, aliases, nsem, start, finish):
        self.ins, self.outs, self.aliases, self.nsem = list(ins), list(outs), dict(aliases), nsem
        self.start, self.finish = start, finish


def _pcall(body, *, name, grid, args, in_specs, out_shape, out_specs, scratch=(), rider=None):
    sem = ("arbitrary",) * len(grid)
    n_in, n_out = len(args), len(out_shape)
    if rider is None:
        res = pl.pallas_call(
            body, grid=grid, name=name, out_shape=tuple(out_shape), in_specs=list(in_specs),
            out_specs=tuple(out_specs), scratch_shapes=list(scratch), compiler_params=_cparams(sem))(*args)
        return tuple(res), ()
    r_in, r_out = len(rider.ins), len(rider.outs)

    def wrapped(*refs):
        a, ri = refs[:n_in], refs[n_in:n_in + r_in]
        o = refs[n_in + r_in:n_in + r_in + n_out]
        ro = refs[n_in + r_in + n_out:n_in + r_in + n_out + r_out]
        s, (send, recv) = refs[n_in + r_in + n_out + r_out:-2], refs[-2:]
        first = functools.reduce(jnp.logical_and, [pl.program_id(d) == 0 for d in range(len(grid))])
        last = functools.reduce(jnp.logical_and, [pl.program_id(d) == g - 1 for d, g in enumerate(grid)])

        @pl.when(first)
        def _():
            rider.start(ri, ro, send, recv)

        body(*a, *o, *s)

        @pl.when(last)
        def _():
            rider.finish(ri, ro, send, recv)

    res = pl.pallas_call(
        wrapped, grid=grid, name=name, out_shape=tuple(out_shape) + tuple(rider.outs),
        in_specs=list(in_specs) + [HBM_SPEC] * r_in, out_specs=tuple(out_specs) + (HBM_SPEC,) * r_out,
        scratch_shapes=list(scratch) + [pltpu.SemaphoreType.DMA((rider.nsem,)),
                                        pltpu.SemaphoreType.DMA((rider.nsem,))],
        input_output_aliases={n_in + i: n_out + j for i, j in rider.aliases.items()},
        compiler_params=_cparams(sem, has_side_effects=True))(*args, *rider.ins)
    return tuple(res[:n_out]), tuple(res[n_out:])


def run_rider(rider, name):
    r_in = len(rider.ins)

    def body(*refs):
        ri, ro, (send, recv) = refs[:r_in], refs[r_in:-2], refs[-2:]
        rider.start(ri, ro, send, recv)
        rider.finish(ri, ro, send, recv)

    return pl.pallas_call(
        body, name=name, out_shape=tuple(rider.outs), in_specs=[HBM_SPEC] * r_in,
        out_specs=(HBM_SPEC,) * len(rider.outs),
        scratch_shapes=[pltpu.SemaphoreType.DMA((rider.nsem,)), pltpu.SemaphoreType.DMA((rider.nsem,))],
        input_output_aliases=rider.aliases,
        compiler_params=pltpu.CompilerParams(has_side_effects=True))(*rider.ins)


def _ffn_hidden(nb, wgu_ref, j):
    g = _dot(nb, wgu_ref[j])
    u = _dot(nb, wgu_ref[2 + j])
    return g, u


def ffn_fwd(h, gain, wgu, wd, rider=None):
    s = h.shape[0]
    ts = _tile(s, 512)

    def body(h_ref, g_ref, wgu_ref, wd_ref, o_ref):
        x = h_ref[...]
        xh, _ = _rms(x)
        nb = (xh * g_ref[...]).astype(BF)
        acc = jnp.zeros((ts, D), F32)
        for j in range(2):
            g, u = _ffn_hidden(nb, wgu_ref, j)
            a = (g * _sigmoid(g) * u).astype(BF)
            acc = acc + _dot(a, wd_ref[j * HC:(j + 1) * HC, :])
        o_ref[...] = x + 0.5 * acc

    (out,), rode = _pcall(
        body, grid=(s // ts,), name="ffn1_fwd", args=(h, gain, wgu, wd),
        out_shape=[jax.ShapeDtypeStruct((s, D), F32)],
        in_specs=[_row_spec(ts, D), _const_spec((1, D)), _const_spec((NSH, D, HC)), _const_spec((FF, D))],
        out_specs=[_row_spec(ts, D)], rider=rider)
    return out, rode


def ffn_fwd_loss(h, gain, wgu, wd, gfin, target):
    s = h.shape[0]
    ts = _tile(s, 512)

    def body(h_ref, g_ref, wgu_ref, wd_ref, gf_ref, t_ref, dh_ref, loss_ref, dgf_ref):
        @pl.when(pl.program_id(0) == 0)
        def _():
            loss_ref[...] = jnp.zeros_like(loss_ref)
            dgf_ref[...] = jnp.zeros_like(dgf_ref)

        x = h_ref[...]
        xh, _ = _rms(x)
        nb = (xh * g_ref[...]).astype(BF)
        acc = jnp.zeros((ts, D), F32)
        for j in range(2):
            g, u = _ffn_hidden(nb, wgu_ref, j)
            a = (g * _sigmoid(g) * u).astype(BF)
            acc = acc + _dot(a, wd_ref[j * HC:(j + 1) * HC, :])
        h4 = x + 0.5 * acc
        yh, r4 = _rms(h4)
        gf = gf_ref[...]
        e = yh * gf - t_ref[...]
        loss_ref[...] += _colsum(e * e)
        dy = e * (1.0 / D)
        dgf_ref[...] += _colsum(dy * yh)
        dh_ref[...] = _rms_bwd(dy, yh, r4, gf)

    return pl.pallas_call(
        body, grid=(s // ts,), name="ffn_fwd_loss",
        out_shape=(jax.ShapeDtypeStruct((s, D), F32), jax.ShapeDtypeStruct((1, D), F32),
                   jax.ShapeDtypeStruct((1, D), F32)),
        in_specs=[_row_spec(ts, D), _const_spec((1, D)), _const_spec((NSH, D, HC)), _const_spec((FF, D)),
                  _const_spec((1, D)), _row_spec(ts, D)],
        out_specs=(_row_spec(ts, D), _acc_spec((1, D)), _acc_spec((1, D))),
        compiler_params=_cparams(("arbitrary",)),
    )(h, gain, wgu, wd, gfin, target)


def ffn_bwd(h, dh, gain, wgu, wd, name):
    s = h.shape[0]
    ts = _tile(s, 256)

    def body(h_ref, dh_ref, g_ref, wgu_ref, wd_ref, dx_ref, n_ref, a_ref, dgu_ref, dhb_ref, dg_ref):
        @pl.when(pl.program_id(0) == 0)
        def _():
            dg_ref[...] = jnp.zeros_like(dg_ref)

        x = h_ref[...]
        dh = dh_ref[...]
        gain_v = g_ref[...]
        xh, r = _rms(x)
        nb = (xh * gain_v).astype(BF)
        n_ref[...] = nb
        dhb = (0.5 * dh).astype(BF)
        dhb_ref[...] = dhb
        dn = jnp.zeros((ts, D), F32)
        for j in range(2):
            g, u = _ffn_hidden(nb, wgu_ref, j)
            sg = _sigmoid(g)
            sl = g * sg
            a_ref[:, j * HC:(j + 1) * HC] = (sl * u).astype(BF)
            da = _dot_nt(dhb, wd_ref[j * HC:(j + 1) * HC, :])
            dgb = (da * u * (sg * (1.0 + g * (1.0 - sg)))).astype(BF)
            dub = (da * sl).astype(BF)
            dgu_ref[:, j * HC:(j + 1) * HC] = dgb
            dgu_ref[:, FF + j * HC:FF + (j + 1) * HC] = dub
            dn = dn + _dot_nt(dgb, wgu_ref[j]) + _dot_nt(dub, wgu_ref[2 + j])
        dg_ref[...] += _colsum(dn * xh)
        dx_ref[...] = dh + _rms_bwd(dn, xh, r, gain_v)

    return pl.pallas_call(
        body, grid=(s // ts,), name=name,
        out_shape=(jax.ShapeDtypeStruct((s, D), F32), jax.ShapeDtypeStruct((s, D), BF),
                   jax.ShapeDtypeStruct((s, FF), BF), jax.ShapeDtypeStruct((s, 2 * FF), BF),
                   jax.ShapeDtypeStruct((s, D), BF), jax.ShapeDtypeStruct((1, D), F32)),
        in_specs=[_row_spec(ts, D), _row_spec(ts, D), _const_spec((1, D)), _const_spec((NSH, D, HC)),
                  _const_spec((FF, D))],
        out_specs=(_row_spec(ts, D), _row_spec(ts, D), _row_spec(ts, FF), _row_spec(ts, 2 * FF),
                   _row_spec(ts, D), _acc_spec((1, D))),
        compiler_params=_cparams(("arbitrary",)),
    )(h, dh, gain, wgu, wd)


def dw_matmul(x, dy, nsplit, name, rider=None):
    s, k = x.shape
    n = dy.shape[1]
    nb = n // nsplit
    ts = _tile(s, 1024)

    def body(x_ref, dy_ref, o_ref):
        @pl.when(pl.program_id(1) == 0)
        def _():
            o_ref[...] = jnp.zeros_like(o_ref)

        o_ref[0] += _dot_tn(x_ref[...], dy_ref[...])

    (out,), rode = _pcall(
        body, grid=(nsplit, s // ts), name=name, args=(x, dy),
        out_shape=[jax.ShapeDtypeStruct((nsplit, k, nb), F32)],
        in_specs=[pl.BlockSpec((ts, k), lambda j, i: (i, 0)), pl.BlockSpec((ts, nb), lambda j, i: (i, j))],
        out_specs=[pl.BlockSpec((1, k, nb), lambda j, i: (j, 0, 0))], rider=rider)
    return (out, rode) if rider is not None else out


def _split_in_proj(p, b):
    h = INB - D
    a_val = p[0][:, :D] + b[:, 0:D]
    a_gate = jnp.concatenate([p[0][:, D:], p[1][:, :h]], axis=1) + b[:, D:2 * D]
    b_u = p[1][:, h:] + b[:, 2 * D:3 * D]
    b_v = p[2][:, :D] + b[:, 3 * D:4 * D]
    g_a = jnp.concatenate([p[2][:, D:], p[3][:, :h]], axis=1) + b[:, 4 * D:5 * D]
    g_b = p[3][:, h:] + b[:, 5 * D:6 * D]
    return a_val, a_gate, b_u, b_v, g_a, g_b


def _sgu_mix(vnb, ws_ref, sb_ref, mixed_ref, ts):
    for ci in range(ts // CHUNK):
        rows = slice(ci * CHUNK, (ci + 1) * CHUNK)
        for g in range(NG):
            cols = slice(g * GD, (g + 1) * GD)
            mixed_ref[rows, cols] = _dot(ws_ref[g], vnb[rows, cols]) + sb_ref[:, cols]


SUB = 8
CB = 128
SH_ROWS_EXTRA = HALO - SUB


def _shifted_copies(ext_ref, sh_ref, lanes, ts):
    for b in range(1, SUB):
        sh_ref[b - 1] = ext_ref[b:b + ts + SH_ROWS_EXTRA, lanes]


def _window(ext_ref, sh_ref, lanes, first, r0, nrows):
    b = first % SUB
    a = first - b
    if b == 0:
        return ext_ref[a + r0:a + r0 + nrows, lanes]
    return sh_ref[b - 1, a + r0:a + r0 + nrows, :]


def mix_fwd(h, gain, win, b_in, conv_w, conv_b, lna_g, lna_b, wa, lnb_g, lnb_b, ws, sbias, wb, wo, rider=None):
    s = h.shape[0]
    ts = _tile(s, 256)

    def body(h_ref, g_ref, win_ref, bin_ref, cw_ref, cb_ref, lag_ref, lab_ref, wa_ref, lbg_ref, lbb_ref,
             ws_ref, sb_ref, wb_ref, wo_ref, o_ref, p_ref, n_ref, c_ref, ext_ref, mixed_ref, sh_ref):
        @pl.when(pl.program_id(0) == 0)
        def _():
            ext_ref[0:HALO, :] = jnp.zeros((HALO, D), F32)

        x = h_ref[...]
        xh, _ = _rms(x)
        nb = (xh * g_ref[...]).astype(BF)
        n_ref[...] = nb
        b = bin_ref[...]
        p = []
        for k in range(NSH):
            pk = _dot(nb, win_ref[k])
            p_ref[:, k * INB:(k + 1) * INB] = (pk + b[:, k * INB:(k + 1) * INB]).astype(BF)
            p.append(pk)
        a_val, a_gate, b_u, b_v, g_a, g_b = _split_in_proj(p, b)
        ext_ref[HALO:HALO + ts, :] = a_val * _sigmoid(a_gate)
        for l0 in range(0, D, CB):
            lanes = slice(l0, l0 + CB)
            _shifted_copies(ext_ref, sh_ref, lanes, ts)
            for r0 in range(0, ts, CB):
                acc = jnp.zeros((CB, CB), F32) + cb_ref[:, lanes]
                for k in range(CW):
                    acc = acc + cw_ref[k:k + 1, lanes] * _window(ext_ref, sh_ref, lanes,
                                                                 HALO - (CW - 1) + k, r0, CB)
                c_ref[r0:r0 + CB, lanes] = acc
        ext_ref[0:HALO, :] = ext_ref[ts:ts + HALO, :]
        ch, _ = _ln(c_ref[...])
        la = ch * lag_ref[...] + lab_ref[...]
        sa = (la * _sigmoid(la)).astype(BF)
        ya = _dot(sa, wa_ref[...])
        ub, _ = _gelu(b_u)
        gv, _ = _gelu(b_v)
        vh, _ = _ln(gv)
        vnb = (vh * lbg_ref[...] + lbb_ref[...]).astype(BF)
        _sgu_mix(vnb, ws_ref, sb_ref, mixed_ref, ts)
        ob = (ub * mixed_ref[...]).astype(BF)
        yb = _dot(ob, wb_ref[...])
        merged = (_sigmoid(g_a) * ya + _sigmoid(g_b) * yb).astype(BF)
        o_ref[...] = x + _dot(merged, wo_ref[...])

    vec = _const_spec((1, D))
    sq = _const_spec((D, D))
    return _pcall(
        body, grid=(s // ts,), name="mix_fwd",
        args=(h, gain, win, b_in, conv_w, conv_b, lna_g, lna_b, wa, lnb_g, lnb_b, ws, sbias, wb, wo),
        out_shape=(jax.ShapeDtypeStruct((s, D), F32), jax.ShapeDtypeStruct((s, DIN), BF),
                   jax.ShapeDtypeStruct((s, D), BF), jax.ShapeDtypeStruct((s, D), F32)),
        in_specs=[_row_spec(ts, D), vec, _const_spec((NSH, D, INB)), _const_spec((1, DIN)),
                  _const_spec((HALO, D)), vec, vec, vec, sq, vec, vec,
                  _const_spec((NG, CHUNK, CHUNK)), _const_spec((CHUNK, D)), sq, sq],
        out_specs=(_row_spec(ts, D), _row_spec(ts, DIN), _row_spec(ts, D), _row_spec(ts, D)),
        scratch=[pltpu.VMEM((ts + HALO, D), F32), pltpu.VMEM((ts, D), F32),
                 pltpu.VMEM((SUB - 1, ts + SH_ROWS_EXTRA, CB), F32)], rider=rider)


def mix_bwd_branches(p, c, dh, lna_g, lna_b, wa, lnb_g, lnb_b, ws, wst, sbias, wb, wo, rider=None):
    s = dh.shape[0]
    ts = _tile(s, 256)
    nsteps = s // ts

    def body(p_ref, c_ref, dh_ref, lag_ref, lab_ref, wa_ref, lbg_ref, lbb_ref, ws_ref, wst_ref, sb_ref,
             wb_ref, wo_ref, dc_ref, dp_ref, sa_ref, dya_ref, ob_ref, dyb_ref, mg_ref, dhb_ref,
             dws_ref, dsb_ref, dlag_ref, dlab_ref, dlbg_ref, dlbb_ref, mixed_ref, dmix_ref, dvn_ref, dsb_acc):
        step = pl.program_id(0)

        @pl.when(step == 0)
        def _():
            for ref in (dws_ref, dsb_acc, dlag_ref, dlab_ref, dlbg_ref, dlbb_ref):
                ref[...] = jnp.zeros_like(ref)

        b_u = p_ref[:, 2 * D:3 * D].astype(F32)
        b_v = p_ref[:, 3 * D:4 * D].astype(F32)
        sga = _sigmoid(p_ref[:, 4 * D:5 * D].astype(F32))
        sgb = _sigmoid(p_ref[:, 5 * D:6 * D].astype(F32))
        lag = lag_ref[...]
        ch, ra = _ln(c_ref[...])
        la = ch * lag + lab_ref[...]
        sla = _sigmoid(la)
        sa = (la * sla).astype(BF)
        sa_ref[...] = sa
        ya = _dot(sa, wa_ref[...])
        lbg = lbg_ref[...]
        ub, tu = _gelu(b_u)
        gv, tv = _gelu(b_v)
        vh, rb = _ln(gv)
        vnb = (vh * lbg + lbb_ref[...]).astype(BF)
        _sgu_mix(vnb, ws_ref, sb_ref, mixed_ref, ts)
        mixed = mixed_ref[...]
        ob = (ub * mixed).astype(BF)
        ob_ref[...] = ob
        yb = _dot(ob, wb_ref[...])
        mg_ref[...] = (sga * ya + sgb * yb).astype(BF)
        dhb = dh_ref[...].astype(BF)
        dhb_ref[...] = dhb
        dm = _dot_nt(dhb, wo_ref[...])
        dp_ref[:, 0:2 * D] = jnp.zeros((ts, 2 * D), BF)
        dp_ref[:, 4 * D:5 * D] = (dm * ya * sga * (1.0 - sga)).astype(BF)
        dp_ref[:, 5 * D:6 * D] = (dm * yb * sgb * (1.0 - sgb)).astype(BF)
        dya = (dm * sga).astype(BF)
        dya_ref[...] = dya
        dyb = (dm * sgb).astype(BF)
        dyb_ref[...] = dyb
        dla = _dot_nt(dya, wa_ref[...]) * (sla * (1.0 + la * (1.0 - sla)))
        dlag_ref[...] += _colsum(dla * ch)
        dlab_ref[...] += _colsum(dla)
        dc_ref[...] = _ln_bwd(dla, ch, ra, lag)
        dob = _dot_nt(dyb, wb_ref[...])
        dp_ref[:, 2 * D:3 * D] = (dob * mixed * _gelu_grad(b_u, tu)).astype(BF)
        dmix = dob * ub
        dmix_ref[...] = dmix.astype(BF)
        dsb = jnp.zeros((CHUNK, D), F32)
        for ci in range(ts // CHUNK):
            rows = slice(ci * CHUNK, (ci + 1) * CHUNK)
            dsb = dsb + dmix[rows, :]
            for g in range(NG):
                cols = slice(g * GD, (g + 1) * GD)
                dmb = dmix_ref[rows, cols]
                dws_ref[g] += _dot_nt(dmb, vnb[rows, cols])
                dvn_ref[rows, cols] = _dot(wst_ref[g], dmb)
        dsb_acc[...] += dsb
        dvn = dvn_ref[...]
        dlbg_ref[...] += _colsum(dvn * vh)
        dlbb_ref[...] += _colsum(dvn)
        dp_ref[:, 3 * D:4 * D] = (_ln_bwd(dvn, vh, rb, lbg) * _gelu_grad(b_v, tv)).astype(BF)

        @pl.when(step == nsteps - 1)
        def _():
            row = lax.broadcasted_iota(jnp.int32, (CHUNK, CHUNK), 0)
            col = lax.broadcasted_iota(jnp.int32, (CHUNK, CHUNK), 1)
            for g in range(NG):
                dws_ref[g] = jnp.where(col <= row, dws_ref[g], 0.0)
            acc = jnp.zeros((CHUNK, CHUNK), F32)
            for g in range(NG):
                tot = jnp.sum(dsb_acc[:, g * GD:(g + 1) * GD], axis=-1, keepdims=True)
                acc = acc + jnp.where(col == g, tot, 0.0)
            dsb_ref[...] = acc

    vec = _const_spec((1, D))
    sq = _const_spec((D, D))
    bf_rows = jax.ShapeDtypeStruct((s, D), BF)
    acc_vec = jax.ShapeDtypeStruct((1, D), F32)
    return _pcall(
        body, grid=(nsteps,), name="mix_bwd_branches",
        args=(p, c, dh, lna_g, lna_b, wa, lnb_g, lnb_b, ws, wst, sbias, wb, wo),
        out_shape=(jax.ShapeDtypeStruct((s, D), F32), jax.ShapeDtypeStruct((s, DIN), BF),
                   bf_rows, bf_rows, bf_rows, bf_rows, bf_rows, bf_rows,
                   jax.ShapeDtypeStruct((NG, CHUNK, CHUNK), F32), jax.ShapeDtypeStruct((CHUNK, CHUNK), F32),
                   acc_vec, acc_vec, acc_vec, acc_vec),
        in_specs=[_row_spec(ts, DIN), _row_spec(ts, D), _row_spec(ts, D), vec, vec, sq, vec, vec,
                  _const_spec((NG, CHUNK, CHUNK)), _const_spec((NG, CHUNK, CHUNK)), _const_spec((CHUNK, D)),
                  sq, sq],
        out_specs=(_row_spec(ts, D), _row_spec(ts, DIN)) + (_row_spec(ts, D),) * 6
        + (_acc_spec((NG, CHUNK, CHUNK)), _acc_spec((CHUNK, CHUNK))) + (_acc_spec((1, D)),) * 4,
        scratch=[pltpu.VMEM((ts, D), F32), pltpu.VMEM((ts, D), BF), pltpu.VMEM((ts, D), F32),
                 pltpu.VMEM((CHUNK, D), F32)], rider=rider)


def conv_bwd(p, dc, dp, conv_w):
    s = dc.shape[0]
    ts = _tile(s, 256)
    nsteps = s // ts
    per = ts // HALO

    rb = 64

    def body(pm_ref, pp_ref, dcm_ref, dcn_ref, cw_ref, dpin_ref, dp_ref, dw_ref, db_ref, ext_ref, dext_ref,
             dw8_ref, sh_ref, dsh_ref, dglu_ref):
        del dpin_ref
        step = pl.program_id(0)

        @pl.when(step == 0)
        def _():
            dw8_ref[...] = jnp.zeros_like(dw8_ref)
            db_ref[...] = jnp.zeros_like(db_ref)

        a_val = pm_ref[:, 0:D].astype(F32)
        sg = _sigmoid(pm_ref[:, D:2 * D].astype(F32))
        prev = pp_ref[:, 0:D].astype(F32) * _sigmoid(pp_ref[:, D:2 * D].astype(F32))
        ext_ref[0:HALO, :] = jnp.where(step > 0, prev, 0.0)
        ext_ref[HALO:HALO + ts, :] = a_val * sg
        dcm = dcm_ref[...]
        dext_ref[0:ts, :] = dcm
        dext_ref[ts:ts + HALO, :] = jnp.where(step < nsteps - 1, dcn_ref[...], 0.0)
        db_ref[...] += _colsum(dcm)
        for l0 in range(0, D, CB):
            lanes = slice(l0, l0 + CB)
            _shifted_copies(dext_ref, dsh_ref, lanes, ts)
            for r0 in range(0, ts, CB):
                acc = jnp.zeros((CB, CB), F32)
                for k in range(CW):
                    acc = acc + cw_ref[k:k + 1, lanes] * _window(dext_ref, dsh_ref, lanes, CW - 1 - k, r0, CB)
                dglu_ref[r0:r0 + CB, lanes] = acc
            _shifted_copies(ext_ref, sh_ref, lanes, ts)
            accs = [jnp.zeros((SUB, CB), F32) for _ in range(CW)]
            for r0 in range(0, ts, rb):
                dcb = dext_ref[r0:r0 + rb, lanes]
                for k in range(CW):
                    prod = dcb * _window(ext_ref, sh_ref, lanes, HALO - (CW - 1) + k, r0, rb)
                    accs[k] = accs[k] + jnp.sum(prod.reshape(rb // SUB, SUB, CB), axis=0)
            for k in range(CW):
                dw8_ref[k, :, lanes] += accs[k]
        dglu = dglu_ref[...]
        dp_ref[:, 0:D] = (dglu * sg).astype(BF)
        dp_ref[:, D:2 * D] = (dglu * a_val * sg * (1.0 - sg)).astype(BF)

        @pl.when(step == nsteps - 1)
        def _():
            dw_ref[...] = jnp.zeros_like(dw_ref)
            for k in range(CW):
                dw_ref[k:k + 1, :] = _colsum(dw8_ref[k])

    return pl.pallas_call(
        body, grid=(nsteps,), name="conv_bwd",
        out_shape=(jax.ShapeDtypeStruct((s, DIN), BF), jax.ShapeDtypeStruct((HALO, D), F32),
                   jax.ShapeDtypeStruct((1, D), F32)),
        in_specs=[pl.BlockSpec((ts, 2 * D), lambda i: (i, 0)),
                  pl.BlockSpec((HALO, 2 * D), lambda i: (jnp.maximum(i * per - 1, 0), 0)),
                  _row_spec(ts, D),
                  pl.BlockSpec((HALO, D), lambda i: (jnp.minimum((i + 1) * per, s // HALO - 1), 0)),
                  _const_spec((HALO, D)),
                  pl.BlockSpec(memory_space=pl.ANY)],
        out_specs=(pl.BlockSpec((ts, 2 * D), lambda i: (i, 0)), _acc_spec((HALO, D)), _acc_spec((1, D))),
        scratch_shapes=[pltpu.VMEM((ts + HALO, D), F32), pltpu.VMEM((ts + HALO, D), F32),
                        pltpu.VMEM((HALO, SUB, D), F32),
                        pltpu.VMEM((SUB - 1, ts + SH_ROWS_EXTRA, CB), F32),
                        pltpu.VMEM((SUB - 1, ts + SH_ROWS_EXTRA, CB), F32),
                        pltpu.VMEM((ts, D), F32)],
        input_output_aliases={5: 0},
        compiler_params=_cparams(("arbitrary",)),
    )(p, p, dc, dc, conv_w, dp)


def mix_bwd_in(dp, h, dh, gain, win):
    s = h.shape[0]
    ts = _tile(s, 512)

    def body(dp_ref, h_ref, dh_ref, g_ref, win_ref, dx_ref, dg_ref, db_ref):
        @pl.when(pl.program_id(0) == 0)
        def _():
            dg_ref[...] = jnp.zeros_like(dg_ref)
            db_ref[...] = jnp.zeros_like(db_ref)

        gain_v = g_ref[...]
        xh, r = _rms(h_ref[...])
        dn = jnp.zeros((ts, D), F32)
        for k in range(NSH):
            dpk = dp_ref[:, k * INB:(k + 1) * INB]
            dn = dn + _dot_nt(dpk, win_ref[k])
            db_ref[:, k * INB:(k + 1) * INB] += _colsum(dpk.astype(F32))
        dg_ref[...] += _colsum(dn * xh)
        dx_ref[...] = dh_ref[...] + _rms_bwd(dn, xh, r, gain_v)

    return pl.pallas_call(
        body, grid=(s // ts,), name="mix_bwd_in",
        out_shape=(jax.ShapeDtypeStruct((s, D), F32), jax.ShapeDtypeStruct((1, D), F32),
                   jax.ShapeDtypeStruct((1, DIN), F32)),
        in_specs=[_row_spec(ts, DIN), _row_spec(ts, D), _row_spec(ts, D), _const_spec((1, D)),
                  _const_spec((NSH, D, INB))],
        out_specs=(_row_spec(ts, D), _acc_spec((1, D)), _acc_spec((1, DIN))),
        compiler_params=_cparams(("arbitrary",)),
    )(dp, h, dh, gain, win)


def kv_proj(mem, gain, wkv):
    def body(m_ref, g_ref, w_ref, k_ref, v_ref, n_ref):
        xh, _ = _rms(m_ref[...])
        nb = (xh * g_ref[...]).astype(BF)
        n_ref[...] = nb
        half = D // 2
        for j in range(2):
            k_ref[:, j * half:(j + 1) * half] = _dot(nb, w_ref[j]).astype(BF)
            v_ref[:, j * half:(j + 1) * half] = _dot(nb, w_ref[2 + j]).astype(BF)

    o = jax.ShapeDtypeStruct((NMEM, D), BF)
    return pl.pallas_call(body, name="kv_proj", out_shape=(o, o, o), compiler_params=_cparams())(mem, gain, wkv)


def kv_bwd(mem, gain, memn, wkv, dk, dv):
    def body(m_ref, g_ref, n_ref, w_ref, dk_ref, dv_ref, dw_ref, dg_ref):
        xh, _ = _rms(m_ref[...])
        nb = n_ref[...]
        half = D // 2
        dn = jnp.zeros((NMEM, D), F32)
        for j in range(2):
            dkb = dk_ref[:, j * half:(j + 1) * half].astype(BF)
            dvb = dv_ref[:, j * half:(j + 1) * half].astype(BF)
            dw_ref[j] = _dot_tn(nb, dkb)
            dw_ref[2 + j] = _dot_tn(nb, dvb)
            dn = dn + _dot_nt(dkb, w_ref[j]) + _dot_nt(dvb, w_ref[2 + j])
        dg_ref[...] = _colsum(dn * xh)

    return pl.pallas_call(
        body, name="kv_bwd",
        out_shape=(jax.ShapeDtypeStruct((NSH, D, D // 2), F32), jax.ShapeDtypeStruct((1, D), F32)),
        compiler_params=_cparams())(mem, gain, memn, wkv, dk, dv)


def _attend(qb, k_ref, v_ref, h):
    cols = slice(h * HD, (h + 1) * HD)
    sc = _dot_nt(qb[:, cols], k_ref[:, cols]) * ATT_SCALE
    e = jnp.exp(sc - jnp.max(sc, axis=-1, keepdims=True))
    pr = e / jnp.sum(e, axis=-1, keepdims=True)
    return pr, _dot(pr.astype(BF), v_ref[:, cols])


def xattn_fwd(h, gain, wq, k, v, wo):
    s = h.shape[0]
    ts = _tile(s, 512)

    def body(h_ref, g_ref, wq_ref, k_ref, v_ref, wo_ref, o_ref, att_ref):
        x = h_ref[...]
        xh, _ = _rms(x)
        nb = (xh * g_ref[...]).astype(BF)
        qb = _dot(nb, wq_ref[...]).astype(BF)
        for hd in range(NH):
            _, oh = _attend(qb, k_ref, v_ref, hd)
            att_ref[:, hd * HD:(hd + 1) * HD] = oh.astype(BF)
        o_ref[...] = x + _dot(att_ref[...], wo_ref[...])

    sq = _const_spec((D, D))
    kvs = _const_spec((NMEM, D))
    return pl.pallas_call(
        body, grid=(s // ts,), name="xattn_fwd",
        out_shape=jax.ShapeDtypeStruct((s, D), F32),
        in_specs=[_row_spec(ts, D), _const_spec((1, D)), sq, kvs, kvs, sq],
        out_specs=_row_spec(ts, D),
        scratch_shapes=[pltpu.VMEM((ts, D), BF)],
        compiler_params=_cparams(("arbitrary",)),
    )(h, gain, wq, k, v, wo)


def xattn_bwd(h, dh, gain, wq, k, v, wo, rider=None):
    s = h.shape[0]
    ts = _tile(s, 256)

    def body(h_ref, dh_ref, g_ref, wq_ref, k_ref, v_ref, wo_ref,
             dx_ref, n_ref, dq_ref, att_ref, dhb_ref, dk_ref, dv_ref, dg_ref):
        @pl.when(pl.program_id(0) == 0)
        def _():
            for ref in (dk_ref, dv_ref, dg_ref):
                ref[...] = jnp.zeros_like(ref)

        x = h_ref[...]
        dh = dh_ref[...]
        gain_v = g_ref[...]
        xh, r = _rms(x)
        nb = (xh * gain_v).astype(BF)
        n_ref[...] = nb
        qb = _dot(nb, wq_ref[...]).astype(BF)
        dhb = dh.astype(BF)
        dhb_ref[...] = dhb
        dob = _dot_nt(dhb, wo_ref[...]).astype(BF)
        for hd in range(NH):
            cols = slice(hd * HD, (hd + 1) * HD)
            pr, oh = _attend(qb, k_ref, v_ref, hd)
            att_ref[:, cols] = oh.astype(BF)
            doh = dob[:, cols]
            dpr = _dot_nt(doh, v_ref[:, cols])
            dv_ref[:, cols] += _dot_tn(pr.astype(BF), doh)
            dsc = (pr * (dpr - jnp.sum(dpr * pr, axis=-1, keepdims=True)) * ATT_SCALE).astype(BF)
            dq_ref[:, cols] = _dot(dsc, k_ref[:, cols]).astype(BF)
            dk_ref[:, cols] += _dot_tn(dsc, qb[:, cols])
        dn = _dot_nt(dq_ref[...], wq_ref[...])
        dg_ref[...] += _colsum(dn * xh)
        dx_ref[...] = dh + _rms_bwd(dn, xh, r, gain_v)

    sq = _const_spec((D, D))
    kvs = _const_spec((NMEM, D))
    bf_rows = jax.ShapeDtypeStruct((s, D), BF)
    kv_acc = jax.ShapeDtypeStruct((NMEM, D), F32)
    return _pcall(
        body, grid=(s // ts,), name="xattn_bwd", args=(h, dh, gain, wq, k, v, wo),
        out_shape=(jax.ShapeDtypeStruct((s, D), F32), bf_rows, bf_rows, bf_rows, bf_rows, kv_acc, kv_acc,
                   jax.ShapeDtypeStruct((1, D), F32)),
        in_specs=[_row_spec(ts, D), _row_spec(ts, D), _const_spec((1, D)), sq, kvs, kvs, sq],
        out_specs=(_row_spec(ts, D),) * 5 + (_acc_spec((NMEM, D)), _acc_spec((NMEM, D)), _acc_spec((1, D))),
        rider=rider)


BLOCK_BYTES = 3 << 19


def _row_block(rows, cols):
    rb = rows
    while rb * cols * 4 > BLOCK_BYTES and rb % 32 == 0:
        rb //= 2
    return rb


def cast_bf16(w, chip, name):
    r, c = w.shape
    rb = _row_block(r, c)

    def body(chip_ref, w_ref, o_ref):
        del chip_ref
        o_ref[0] = w_ref[...].astype(BF)

    return pl.pallas_call(
        body, name=name, out_shape=jax.ShapeDtypeStruct((NSH, r, c), BF),
        grid_spec=pltpu.PrefetchScalarGridSpec(
            num_scalar_prefetch=1, grid=(r // rb,),
            in_specs=[pl.BlockSpec((rb, c), lambda i, chip_ref: (i, 0))],
            out_specs=pl.BlockSpec((1, rb, c), lambda i, chip_ref: (chip_ref[0], i, 0))),
        compiler_params=_cparams(("arbitrary",)))(chip, w)


def pair_add(g4, recv, core, name):
    nsh, _, rh, c = g4.shape
    rb = _row_block(rh, c)

    def body(core_ref, g_ref, r_ref, o_ref, ob_ref):
        del core_ref
        sm = g_ref[0, 0] + r_ref[0]
        o_ref[0] = sm
        ob_ref[0] = sm.astype(BF)

    spec3 = pl.BlockSpec((1, rb, c), lambda k, i, core_ref: (k, i, 0))
    return pl.pallas_call(
        body, name=name,
        out_shape=(jax.ShapeDtypeStruct((nsh, rh, c), F32), jax.ShapeDtypeStruct((nsh, rh, c), BF)),
        grid_spec=pltpu.PrefetchScalarGridSpec(
            num_scalar_prefetch=1, grid=(nsh, rh // rb),
            in_specs=[pl.BlockSpec((1, 1, rb, c), lambda k, i, core_ref: (k, core_ref[0], i, 0)), spec3],
            out_specs=(spec3, spec3)),
        compiler_params=_cparams(("arbitrary", "arbitrary")))(core, g4, recv)


def chip_sum(psum, recv, place, name):
    _, rh, c = psum.shape
    rb = _row_block(rh, c)

    def body(place_ref, p_ref, r_ref, o_ref):
        del place_ref
        acc = p_ref[0]
        for j in range(NSH - 1):
            acc = acc + r_ref[j].astype(F32)
        o_ref[0] = acc

    return pl.pallas_call(
        body, name=name, out_shape=jax.ShapeDtypeStruct((2, rh, c), F32),
        grid_spec=pltpu.PrefetchScalarGridSpec(
            num_scalar_prefetch=1, grid=(rh // rb,),
            in_specs=[pl.BlockSpec((1, rb, c), lambda i, place_ref: (place_ref[0], i, 0)),
                      pl.BlockSpec((NSH - 1, rb, c), lambda i, place_ref: (0, i, 0))],
            out_specs=pl.BlockSpec((1, rb, c), lambda i, place_ref: (place_ref[1], i, 0))),
        compiler_params=_cparams(("arbitrary",)))(place, psum, recv)


def _adamw_math(w, g, m, v):
    m = ADAM_B1 * m + (1.0 - ADAM_B1) * g
    v = ADAM_B2 * v + (1.0 - ADAM_B2) * (g * g)
    m_hat = m / (1.0 - ADAM_B1 ** ADAM_STEP)
    v_hat = v / (1.0 - ADAM_B2 ** ADAM_STEP)
    delta = -ADAM_LR * (m_hat / (jnp.sqrt(v_hat) + ADAM_EPS) + ADAM_WD * w)
    return delta, m, v


def adamw(w, g, m, v, name):
    r, c = w.shape
    rb = _row_block(r, c)

    def body(w_ref, g_ref, m_ref, v_ref, d_ref, mo_ref, vo_ref):
        d, mn, vn = _adamw_math(w_ref[...], g_ref[...], m_ref[...], v_ref[...])
        d_ref[...] = d
        mo_ref[...] = mn
        vo_ref[...] = vn

    o = jax.ShapeDtypeStruct((r, c), F32)
    spec = _row_spec(rb, c)
    return pl.pallas_call(
        body, grid=(r // rb,), name=name, out_shape=(o, o, o),
        in_specs=[spec] * 4, out_specs=(spec,) * 3,
        compiler_params=_cparams(("arbitrary",)))(w, g, m, v)


def _place():
    return lax.axis_index("x"), lax.axis_index("y"), lax.axis_index("c")


def _other_chips(x, y):
    return [(1 - x, y), (x, 1 - y), (1 - x, 1 - y)]


NOTHER = NSH - 1


def gather_rider(arrays):
    nw = len(arrays)

    def copies(refs, send_sems, recv_sems):
        x, y, c = _place()
        out = []
        for w in range(nw):
            for j, (px, py) in enumerate(_other_chips(x, y)):
                sems = dict(send_sem=send_sems.at[w * NOTHER + j], recv_sem=recv_sems.at[w * NOTHER + j],
                            device_id=(px, py, c), device_id_type=MESH)
                mine = refs[w].at[2 * x + y]
                out.append((pltpu.make_async_remote_copy(src_ref=mine, dst_ref=mine, **sems),
                            pltpu.make_async_remote_copy(src_ref=mine, dst_ref=refs[w].at[2 * px + py], **sems)))
        return out

    def start(ins, outs, send_sems, recv_sems):
        for send, _ in copies(outs, send_sems, recv_sems):
            send.start()

    def finish(ins, outs, send_sems, recv_sems):
        cps = copies(outs, send_sems, recv_sems)
        for _, recv in cps:
            recv.wait_recv()
        for send, _ in cps:
            send.wait_send()

    return Rider(arrays, [jax.ShapeDtypeStruct(a.shape, a.dtype) for a in arrays], {i: i for i in range(nw)},
                 nw * NOTHER, start, finish)


def exchange_rider(psums):
    nw = len(psums)

    def copies(ins, outs, send_sems, recv_sems):
        x, y, c = _place()
        return [pltpu.make_async_remote_copy(
            src_ref=ins[w].at[2 * px + py], dst_ref=outs[w].at[j],
            send_sem=send_sems.at[w * NOTHER + j], recv_sem=recv_sems.at[w * NOTHER + j],
            device_id=(px, py, c), device_id_type=MESH)
            for w in range(nw) for j, (px, py) in enumerate(_other_chips(x, y))]

    def start(ins, outs, send_sems, recv_sems):
        for cp in copies(ins, outs, send_sems, recv_sems):
            cp.start()

    def finish(ins, outs, send_sems, recv_sems):
        for cp in copies(ins, outs, send_sems, recv_sems):
            cp.wait()

    return Rider(psums, [jax.ShapeDtypeStruct((NOTHER,) + p.shape[1:], p.dtype) for p in psums], {},
                 nw * NOTHER, start, finish)


def pair_exchange(grads, name):
    nw = len(grads)

    def body(*refs):
        ins, outs = refs[:nw], refs[nw:2 * nw]
        send_sems, recv_sems = refs[2 * nw:]
        x, y, c = _place()
        cps = []
        for w in range(nw):
            cp = pltpu.make_async_remote_copy(
                src_ref=ins[w].at[:, 1 - c], dst_ref=outs[w],
                send_sem=send_sems.at[w], recv_sem=recv_sems.at[w],
                device_id=(x, y, 1 - c), device_id_type=MESH)
            cp.start()
            cps.append(cp)
        for cp in cps:
            cp.wait()

    return pl.pallas_call(
        body, name=name,
        out_shape=tuple(jax.ShapeDtypeStruct((g.shape[0],) + g.shape[2:], g.dtype) for g in grads),
        in_specs=[HBM_SPEC] * nw, out_specs=(HBM_SPEC,) * nw,
        scratch_shapes=[pltpu.SemaphoreType.DMA((nw,)), pltpu.SemaphoreType.DMA((nw,))],
        compiler_params=pltpu.CompilerParams(has_side_effects=True),
    )(*grads)


def swap_rider(halves):
    nw = len(halves)

    def copies(refs, send_sems, recv_sems):
        x, y, c = _place()
        out = []
        for w in range(nw):
            sems = dict(send_sem=send_sems.at[w], recv_sem=recv_sems.at[w],
                        device_id=(x, y, 1 - c), device_id_type=MESH)
            mine = refs[w].at[c]
            out.append((pltpu.make_async_remote_copy(src_ref=mine, dst_ref=mine, **sems),
                        pltpu.make_async_remote_copy(src_ref=mine, dst_ref=refs[w].at[1 - c], **sems)))
        return out

    def start(ins, outs, send_sems, recv_sems):
        for send, _ in copies(outs, send_sems, recv_sems):
            send.start()

    def finish(ins, outs, send_sems, recv_sems):
        cps = copies(outs, send_sems, recv_sems)
        for _, recv in cps:
            recv.wait_recv()
        for send, _ in cps:
            send.wait_send()

    return Rider(halves, [jax.ShapeDtypeStruct(h.shape, h.dtype) for h in halves], {i: i for i in range(nw)},
                 nw, start, finish)


NDEV = 8


def small_allreduce(buf):
    r = buf.shape[0]

    def body(b_ref, o_ref, slots_ref, send_sems, recv_sems):
        x, y, c = _place()
        me = 4 * x + 2 * y + c
        slots_ref[me] = b_ref[...]
        cps = []
        for rel in range(1, NDEV):
            peer = (x ^ (rel >> 2), y ^ ((rel >> 1) & 1), c ^ (rel & 1))
            cp = pltpu.make_async_remote_copy(
                src_ref=b_ref, dst_ref=slots_ref.at[me],
                send_sem=send_sems.at[rel - 1], recv_sem=recv_sems.at[rel - 1],
                device_id=peer, device_id_type=MESH)
            cp.start()
            cps.append(cp)
        for rel in range(1, NDEV):
            peer = (x ^ (rel >> 2), y ^ ((rel >> 1) & 1), c ^ (rel & 1))
            pltpu.make_async_remote_copy(
                src_ref=b_ref, dst_ref=slots_ref.at[4 * peer[0] + 2 * peer[1] + peer[2]],
                send_sem=send_sems.at[rel - 1], recv_sem=recv_sems.at[rel - 1],
                device_id=peer, device_id_type=MESH).wait_recv()
        for cp in cps:
            cp.wait_send()
        acc = slots_ref[0]
        for dev in range(1, NDEV):
            acc = acc + slots_ref[dev]
        o_ref[...] = acc

    vm = pl.BlockSpec(memory_space=pltpu.VMEM)
    return pl.pallas_call(
        body, name="small_allreduce", out_shape=jax.ShapeDtypeStruct((r, D), F32),
        in_specs=[vm], out_specs=vm,
        scratch_shapes=[pltpu.VMEM((NDEV, r, D), F32), pltpu.SemaphoreType.DMA((NDEV - 1,)),
                        pltpu.SemaphoreType.DMA((NDEV - 1,))],
        compiler_params=_cparams(has_side_effects=True),
    )(buf)


BIG = ("ffn1_w_gu", "ffn1_w_down", "w_in", "w_a_out", "w_b_out", "w_out", "w_q", "w_kv", "w_o",
       "ffn2_w_gu", "ffn2_w_down")
SMALL = {"ffn1_norm": (0, 1), "mix_norm": (8, 1), "xattn_norm": (16, 1), "mem_norm": (24, 1),
         "ffn2_norm": (32, 1), "final_norm": (40, 1), "conv_b": (48, 1), "conv_ln_g": (56, 1),
         "conv_ln_b": (64, 1), "sgu_ln_g": (72, 1), "sgu_ln_b": (80, 1), "b_in": (88, 6),
         "conv_w": (96, CW), "sgu_w": (128, 64), "sgu_b": (192, 1)}
LOSS_ROW = 200
SMALL_ROWS = 208


def _pad_rows(a, rows):
    return jnp.pad(a, ((0, rows - a.shape[0]), (0, D - a.shape[1])))


def _pack_small(parts):
    names = sorted(parts, key=lambda n: SMALL[n][0] if n in SMALL else LOSS_ROW)
    rows = []
    for i, n in enumerate(names):
        start = SMALL[n][0] if n in SMALL else LOSS_ROW
        end = SMALL_ROWS if i + 1 == len(names) else (SMALL[names[i + 1]][0] if names[i + 1] in SMALL else LOSS_ROW)
        rows.append(_pad_rows(parts[n], end - start))
    return jnp.concatenate(rows, axis=0)


def _small_views(w):
    return {
        "ffn1_norm": w["ffn1_norm"], "mix_norm": w["mix_norm"], "xattn_norm": w["xattn_norm"],
        "mem_norm": w["mem_norm"], "ffn2_norm": w["ffn2_norm"], "final_norm": w["final_norm"].reshape(1, D),
        "conv_b": w["conv_b"], "conv_ln_g": w["conv_ln_g"], "conv_ln_b": w["conv_ln_b"],
        "sgu_ln_g": w["sgu_ln_g"], "sgu_ln_b": w["sgu_ln_b"], "b_in": w["b_in"].reshape(6, D),
        "conv_w": w["conv_w"][0], "sgu_w": w["sgu_w"].reshape(64, D), "sgu_b": w["sgu_b"].reshape(1, NG * CHUNK),
    }


def _unpack_small(buf, like, chip):
    out = {}
    for n, (start, rows) in SMALL.items():
        blk = buf[start:start + rows]
        if n == "conv_w":
            blk = blk[:, :like[n].shape[-1]] if chip is None else lax.dynamic_slice_in_dim(
                blk, chip * like[n].shape[-1], like[n].shape[-1], axis=1)
        elif n == "sgu_b":
            blk = blk[:, :NG * CHUNK]
        out[n] = blk.reshape(like[n].shape)
    return out


def kernel(x, mem, ffn1_norm, ffn1_w_gu, ffn1_w_down, mix_norm, w_in, b_in, conv_w, conv_b, conv_ln_g, conv_ln_b, w_a_out, sgu_ln_g, sgu_ln_b, sgu_w, sgu_b, w_b_out, w_out, xattn_norm, mem_norm, w_q, w_kv, w_o, ffn2_norm, ffn2_w_gu, ffn2_w_down, final_norm, loss_target, m_ffn1_norm, m_ffn1_w_gu, m_ffn1_w_down, m_mix_norm, m_w_in, m_b_in, m_conv_w, m_conv_b, m_conv_ln_g, m_conv_ln_b, m_w_a_out, m_sgu_ln_g, m_sgu_ln_b, m_sgu_w, m_sgu_b, m_w_b_out, m_w_out, m_xattn_norm, m_mem_norm, m_w_q, m_w_kv, m_w_o, m_ffn2_norm, m_ffn2_w_gu, m_ffn2_w_down, m_final_norm, v_ffn1_norm, v_ffn1_w_gu, v_ffn1_w_down, v_mix_norm, v_w_in, v_b_in, v_conv_w, v_conv_b, v_conv_ln_g, v_conv_ln_b, v_w_a_out, v_sgu_ln_g, v_sgu_ln_b, v_sgu_w, v_sgu_b, v_w_b_out, v_w_out, v_xattn_norm, v_mem_norm, v_w_q, v_w_kv, v_w_o, v_ffn2_norm, v_ffn2_w_gu, v_ffn2_w_down, v_final_norm):
    names = ("ffn1_norm", "ffn1_w_gu", "ffn1_w_down", "mix_norm", "w_in", "b_in", "conv_w", "conv_b",
             "conv_ln_g", "conv_ln_b", "w_a_out", "sgu_ln_g", "sgu_ln_b", "sgu_w", "sgu_b", "w_b_out", "w_out",
             "xattn_norm", "mem_norm", "w_q", "w_kv", "w_o", "ffn2_norm", "ffn2_w_gu", "ffn2_w_down",
             "final_norm")
    wts = dict(zip(names, (ffn1_norm, ffn1_w_gu, ffn1_w_down, mix_norm, w_in, b_in, conv_w, conv_b, conv_ln_g,
                           conv_ln_b, w_a_out, sgu_ln_g, sgu_ln_b, sgu_w, sgu_b, w_b_out, w_out, xattn_norm,
                           mem_norm, w_q, w_kv, w_o, ffn2_norm, ffn2_w_gu, ffn2_w_down, final_norm)))
    mom1 = dict(zip(names, (m_ffn1_norm, m_ffn1_w_gu, m_ffn1_w_down, m_mix_norm, m_w_in, m_b_in, m_conv_w,
                            m_conv_b, m_conv_ln_g, m_conv_ln_b, m_w_a_out, m_sgu_ln_g, m_sgu_ln_b, m_sgu_w,
                            m_sgu_b, m_w_b_out, m_w_out, m_xattn_norm, m_mem_norm, m_w_q, m_w_kv, m_w_o,
                            m_ffn2_norm, m_ffn2_w_gu, m_ffn2_w_down, m_final_norm)))
    mom2 = dict(zip(names, (v_ffn1_norm, v_ffn1_w_gu, v_ffn1_w_down, v_mix_norm, v_w_in, v_b_in, v_conv_w,
                            v_conv_b, v_conv_ln_g, v_conv_ln_b, v_w_a_out, v_sgu_ln_g, v_sgu_ln_b, v_sgu_w,
                            v_sgu_b, v_w_b_out, v_w_out, v_xattn_norm, v_mem_norm, v_w_q, v_w_kv, v_w_o,
                            v_ffn2_norm, v_ffn2_w_gu, v_ffn2_w_down, v_final_norm)))
    xi, yi, ci = _place()
    chip = (2 * xi + yi).astype(jnp.int32)
    core = ci.astype(jnp.int32)
    core_arr = core.reshape(1)
    chip_arr = chip.reshape(1)
    place_arr = jnp.stack([chip, core])
    x2, mem2, tgt = x[0], mem[0], loss_target[0]

    slot = {n: cast_bf16(wts[n][0], chip_arr, "cast_" + n) for n in BIG}
    cw_pad = jnp.pad(conv_w[0], ((0, HALO - CW), (0, 0)))
    slot["conv_w"] = lax.dynamic_update_slice(jnp.zeros((NSH,) + cw_pad.shape, F32), cw_pad[None], (chip, 0, 0))
    g_first = ("ffn1_w_gu", "ffn1_w_down")
    g_mix = ("w_in", "w_a_out", "w_b_out", "w_out", "conv_w")
    g_rest = ("w_q", "w_kv", "w_o", "ffn2_w_gu", "ffn2_w_down")
    full = dict(zip(g_first, run_rider(gather_rider([slot[n] for n in g_first]), "gather_ffn1")))
    wgu1, wd1 = full["ffn1_w_gu"], full["ffn1_w_down"].reshape(FF, D)
    tril = jnp.tril(jnp.ones((CHUNK, CHUNK), dtype=bool))
    ws = jnp.where(tril[None], sgu_w[0], 0.0).astype(BF)
    wst = jnp.transpose(ws, (0, 2, 1))
    sbias = jnp.repeat(jnp.transpose(sgu_b[0]), GD, axis=1)
    gfin = final_norm.reshape(1, D)

    h1, rode = ffn_fwd(x2, ffn1_norm, wgu1, wd1, rider=gather_rider([slot[n] for n in g_mix]))
    full.update(zip(g_mix, rode))
    win = full["w_in"]
    wa, wb, wout = (full[n].reshape(D, D) for n in ("w_a_out", "w_b_out", "w_out"))
    cw_full = jnp.transpose(full["conv_w"], (1, 0, 2)).reshape(HALO, D)
    (h2, proj, n2b, conv_out), rode = mix_fwd(
        h1, mix_norm, win, b_in, cw_full, conv_b, conv_ln_g, conv_ln_b, wa, sgu_ln_g, sgu_ln_b, ws, sbias, wb,
        wout, rider=gather_rider([slot[n] for n in g_rest]))
    full.update(zip(g_rest, rode))
    wgu2, wd2, wkv = full["ffn2_w_gu"], full["ffn2_w_down"].reshape(FF, D), full["w_kv"]
    wq, wo = full["w_q"].reshape(D, D), full["w_o"].reshape(D, D)
    kb, vb, memn = kv_proj(mem2, mem_norm, wkv)
    h3 = xattn_fwd(h2, xattn_norm, wq, kb, vb, wo)
    dh4, loss_lanes, d_final = ffn_fwd_loss(h3, ffn2_norm, wgu2, wd2, gfin, tgt)

    def pair_sums(group, grads, tag):
        g4 = []
        for n, g in zip(group, grads):
            rs, cs = wts[n].shape[1:]
            g4.append(g.reshape(NSH, 2, rs // 2, cs))
        recv = pair_exchange(g4, "pair_exchange_" + tag)
        sums = [pair_add(g, r, core_arr, "pair_add_" + n) for n, g, r in zip(group, g4, recv)]
        return [s[0] for s in sums], exchange_rider([s[1] for s in sums])

    halves = {}

    def chip_sums(group, psums, recv):
        for n, p, r in zip(group, psums, recv):
            halves[n] = chip_sum(p, r, place_arr, "chip_sum_" + n)

    dh3, n4, a4, dgu4, dhb4, d_ffn2n = ffn_bwd(h3, dh4, ffn2_norm, wgu2, wd2, "ffn2_bwd")
    g_ffn2 = ("ffn2_w_gu", "ffn2_w_down")
    ps_ffn2, ride = pair_sums(g_ffn2, [dw_matmul(n4, dgu4, NSH, "dw_ffn2_gu"),
                                       dw_matmul(a4, dhb4, 1, "dw_ffn2_down")], "ffn2")
    (dh2, n3, dq, att, dhb3, dk, dv, d_xn), rode = xattn_bwd(h2, dh3, xattn_norm, wq, kb, vb, wo, rider=ride)
    chip_sums(g_ffn2, ps_ffn2, rode)
    g_att = ("w_q", "w_o", "w_kv")
    d_wkv, d_memn = kv_bwd(mem2, mem_norm, memn, wkv, dk, dv)
    ps_att, ride = pair_sums(g_att, [dw_matmul(n3, dq, 1, "dw_q"), dw_matmul(att, dhb3, 1, "dw_o"), d_wkv], "att")
    ((dconv, dproj, sa, dya, ob, dyb, mg, dhb2, d_sgu_w, d_sgu_b, d_lna_g, d_lna_b, d_lnb_g, d_lnb_b),
     rode) = mix_bwd_branches(proj, conv_out, dh2, conv_ln_g, conv_ln_b, wa, sgu_ln_g, sgu_ln_b, ws, wst,
                              sbias, wb, wout, rider=ride)
    chip_sums(g_att, ps_att, rode)
    dproj, d_conv_w, d_conv_b = conv_bwd(proj, dconv, dproj, cw_full)
    dh1, d_mixn, d_b_in = mix_bwd_in(dproj, h1, dh2, mix_norm, win)
    g_mixw = ("w_a_out", "w_b_out", "w_out", "w_in")
    ps_mix, ride = pair_sums(g_mixw, [dw_matmul(sa, dya, 1, "dw_a_out"), dw_matmul(ob, dyb, 1, "dw_b_out"),
                                      dw_matmul(mg, dhb2, 1, "dw_out"), dw_matmul(n2b, dproj, NSH, "dw_in")],
                             "mix")
    dx, n1, a1, dgu1, dhb1, d_ffn1n = ffn_bwd(x2, dh1, ffn1_norm, wgu1, wd1, "ffn1_bwd")
    d_wgu1, rode = dw_matmul(n1, dgu1, NSH, "dw_ffn1_gu", rider=ride)
    chip_sums(g_mixw, ps_mix, rode)
    ps_ffn1, ride = pair_sums(g_first, [d_wgu1, dw_matmul(a1, dhb1, 1, "dw_ffn1_down")], "ffn1")
    chip_sums(g_first, ps_ffn1, run_rider(ride, "chip_exchange_ffn1"))
    swapped = run_rider(swap_rider([halves[n] for n in BIG]), "pair_swap")
    gshard = {n: g.reshape(wts[n].shape[1:]) for n, g in zip(BIG, swapped)}

    small_grads = {
        "ffn1_norm": d_ffn1n, "mix_norm": d_mixn, "xattn_norm": d_xn, "mem_norm": d_memn, "ffn2_norm": d_ffn2n,
        "final_norm": d_final, "conv_b": d_conv_b, "conv_ln_g": d_lna_g, "conv_ln_b": d_lna_b,
        "sgu_ln_g": d_lnb_g, "sgu_ln_b": d_lnb_b, "b_in": d_b_in.reshape(6, D), "conv_w": d_conv_w[:CW],
        "sgu_w": d_sgu_w.reshape(64, D), "sgu_b": jnp.transpose(d_sgu_b[:, :NG]).reshape(1, NG * CHUNK),
        "loss": loss_lanes}
    small = small_allreduce(_pack_small(small_grads))
    loss = (0.5 / D) * jnp.sum(small[LOSS_ROW])
    gsmall = _unpack_small(small, wts, chip)

    out_g, out_d, out_m, out_v = dict(gsmall), {}, {}, {}
    sw, sm, sv = (_pack_small(_small_views(t))[:LOSS_ROW] for t in (wts, mom1, mom2))
    sg = _pack_small(_small_views({n: gsmall[n] for n in SMALL}))[:LOSS_ROW]
    for dst, packed in zip((out_d, out_m, out_v), adamw(sw, sg, sm, sv, "adamw_small")):
        dst.update(_unpack_small(packed, wts, None))
    for n in BIG:
        shape = wts[n].shape
        out_g[n] = gshard[n].reshape(shape)
        d, mn, vn = adamw(wts[n][0], gshard[n], mom1[n][0], mom2[n][0], "adamw_" + n)
        out_d[n], out_m[n], out_v[n] = d.reshape(shape), mn.reshape(shape), vn.reshape(shape)
    return (loss, dx[None], *[out_g[n] for n in names], *[out_d[n] for n in names],
            *[out_m[n] for n in names], *[out_v[n] for n in names])
```

```python
import functools
import math

import jax
import jax.numpy as jnp
from jax import lax
from jax.experimental import pallas as pl
from jax.experimental.pallas import tpu as pltpu

F32 = jnp.float32
BF = jnp.bfloat16
MESH = pl.DeviceIdType.MESH

D = 1024
FF = 2816
HC = FF // 2
NSH = 4
DIN = 6 * D
INB = DIN // NSH
CW = 31
HALO = 32
CHUNK = 128
NG = 4
GD = D // NG
NH = 4
HD = D // NH
NMEM = 256
EPS_RMS = 1e-6
EPS_LN = 1e-5
GELU_C0 = math.sqrt(2.0 / math.pi)
GELU_C1 = 0.044715
ATT_SCALE = 1.0 / math.sqrt(HD)

ADAM_LR = 0.001
ADAM_B1 = 0.9
ADAM_B2 = 0.999
ADAM_EPS = 1e-08
ADAM_WD = 0.01
ADAM_STEP = 10

VMEM_LIMIT = 56 * 1024 * 1024


def _cparams(sem=None, **kw):
    if sem is not None:
        kw["dimension_semantics"] = sem
    return pltpu.CompilerParams(vmem_limit_bytes=VMEM_LIMIT, **kw)


def _dot(a, b):
    return jnp.dot(a, b, preferred_element_type=F32)


def _dot_nt(a, b):
    return lax.dot_general(a, b, (((1,), (1,)), ((), ())), preferred_element_type=F32)


def _dot_tn(a, b):
    return lax.dot_general(a, b, (((0,), (0,)), ((), ())), preferred_element_type=F32)


def _sigmoid(x):
    return 1.0 / (1.0 + jnp.exp(-x))


def _gelu(x):
    t = jnp.tanh(GELU_C0 * (x + GELU_C1 * (x * x * x)))
    return 0.5 * x * (1.0 + t), t


def _gelu_grad(x, t):
    return 0.5 * (1.0 + t) + 0.5 * x * (1.0 - t * t) * (GELU_C0 * (1.0 + 3.0 * GELU_C1 * x * x))


def _mean(x):
    return jnp.mean(x, axis=-1, keepdims=True)


def _rms(x):
    r = lax.rsqrt(_mean(x * x) + EPS_RMS)
    return x * r, r


def _rms_bwd(dn, xh, r, g):
    dxh = dn * g
    return r * (dxh - xh * _mean(dxh * xh))


def _ln(x):
    xc = x - _mean(x)
    r = lax.rsqrt(_mean(xc * xc) + EPS_LN)
    return xc * r, r


def _ln_bwd(dy, xh, r, g):
    dxh = dy * g
    return r * (dxh - _mean(dxh) - xh * _mean(dxh * xh))


def _colsum(x):
    return jnp.sum(x, axis=0, keepdims=True)


def _const_spec(shape):
    nd = len(shape)
    return pl.BlockSpec(shape, lambda *_: (0,) * nd, pipeline_mode=pl.Buffered(1))


def _row_spec(ts, width):
    return pl.BlockSpec((ts, width), lambda i: (i, 0))


def _acc_spec(shape):
    nd = len(shape)
    return pl.BlockSpec(shape, lambda *_: (0,) * nd)


def _tile(s, want):
    return min(s, want)


HBM_SPEC = pl.BlockSpec(memory_space=pltpu.HBM)


class Rider:
    def __init__(self, ins, outs, aliases, nsem, start, finish):
        self.ins, self.outs, self.aliases, self.nsem = list(ins), list(outs), dict(aliases), nsem
        self.start, self.finish = start, finish


def _pcall(body, *, name, grid, args, in_specs, out_shape, out_specs, scratch=(), rider=None):
    sem = ("arbitrary",) * len(grid)
    n_in, n_out = len(args), len(out_shape)
    if rider is None:
        res = pl.pallas_call(
            body, grid=grid, name=name, out_shape=tuple(out_shape), in_specs=list(in_specs),
            out_specs=tuple(out_specs), scratch_shapes=list(scratch), compiler_params=_cparams(sem))(*args)
        return tuple(res), ()
    r_in, r_out = len(rider.ins), len(rider.outs)

    def wrapped(*refs):
        a, ri = refs[:n_in], refs[n_in:n_in + r_in]
        o = refs[n_in + r_in:n_in + r_in + n_out]
        ro = refs[n_in + r_in + n_out:n_in + r_in + n_out + r_out]
        s, (send, recv) = refs[n_in + r_in + n_out + r_out:-2], refs[-2:]
        first = functools.reduce(jnp.logical_and, [pl.program_id(d) == 0 for d in range(len(grid))])
        last = functools.reduce(jnp.logical_and, [pl.program_id(d) == g - 1 for d, g in enumerate(grid)])

        @pl.when(first)
        def _():
            rider.start(ri, ro, send, recv)

        body(*a, *o, *s)

        @pl.when(last)
        def _():
            rider.finish(ri, ro, send, recv)

    res = pl.pallas_call(
        wrapped, grid=grid, name=name, out_shape=tuple(out_shape) + tuple(rider.outs),
        in_specs=list(in_specs) + [HBM_SPEC] * r_in, out_specs=tuple(out_specs) + (HBM_SPEC,) * r_out,
        scratch_shapes=list(scratch) + [pltpu.SemaphoreType.DMA((rider.nsem,)),
                                        pltpu.SemaphoreType.DMA((rider.nsem,))],
        input_output_aliases={n_in + i: n_out + j for i, j in rider.aliases.items()},
        compiler_params=_cparams(sem, has_side_effects=True))(*args, *rider.ins)
    return tuple(res[:n_out]), tuple(res[n_out:])


def run_rider(rider, name):
    r_in = len(rider.ins)

    def body(*refs):
        ri, ro, (send, recv) = refs[:r_in], refs[r_in:-2], refs[-2:]
        rider.start(ri, ro, send, recv)
        rider.finish(ri, ro, send, recv)

    return pl.pallas_call(
        body, name=name, out_shape=tuple(rider.outs), in_specs=[HBM_SPEC] * r_in,
        out_specs=(HBM_SPEC,) * len(rider.outs),
        scratch_shapes=[pltpu.SemaphoreType.DMA((rider.nsem,)), pltpu.SemaphoreType.DMA((rider.nsem,))],
        input_output_aliases=rider.aliases,
        compiler_params=pltpu.CompilerParams(has_side_effects=True))(*rider.ins)


FFN_BWD_TILE = 256


def _ffn_apply(x, g_ref, wgu_ref, wd_ref, gu_ref):
    xh, _ = _rms(x)
    nb = (xh * g_ref[...]).astype(BF)
    acc = jnp.zeros(x.shape, F32)
    for j in range(2):
        g = _dot(nb, wgu_ref[j])
        u = _dot(nb, wgu_ref[2 + j])
        gu_ref[:, j * HC:(j + 1) * HC] = g.astype(BF)
        gu_ref[:, FF + j * HC:FF + (j + 1) * HC] = u.astype(BF)
        a = (g * _sigmoid(g) * u).astype(BF)
        acc = acc + _dot(a, wd_ref[j * HC:(j + 1) * HC, :])
    return x + 0.5 * acc


def ffn_fwd(h, gain, wgu, wd, rider=None):
    s = h.shape[0]
    ts = _tile(s, 512)

    def body(h_ref, g_ref, wgu_ref, wd_ref, o_ref, gu_ref):
        o_ref[...] = _ffn_apply(h_ref[...], g_ref, wgu_ref, wd_ref, gu_ref)

    return _pcall(
        body, grid=(s // ts,), name="ffn1_fwd", args=(h, gain, wgu, wd),
        out_shape=[jax.ShapeDtypeStruct((s, D), F32), jax.ShapeDtypeStruct((s, 2 * FF), BF)],
        in_specs=[_row_spec(ts, D), _const_spec((1, D)), _const_spec((NSH, D, HC)), _const_spec((FF, D))],
        out_specs=[_row_spec(ts, D), _row_spec(ts, 2 * FF)], rider=rider)


def ffn_fwd_loss(h, gain, wgu, wd, gfin, target):
    s = h.shape[0]
    ts = _tile(s, 512)

    def body(h_ref, g_ref, wgu_ref, wd_ref, gf_ref, t_ref, dh_ref, gu_ref, loss_ref, dgf_ref):
        @pl.when(pl.program_id(0) == 0)
        def _():
            loss_ref[...] = jnp.zeros_like(loss_ref)
            dgf_ref[...] = jnp.zeros_like(dgf_ref)

        h4 = _ffn_apply(h_ref[...], g_ref, wgu_ref, wd_ref, gu_ref)
        yh, r4 = _rms(h4)
        gf = gf_ref[...]
        e = yh * gf - t_ref[...]
        loss_ref[...] += _colsum(e * e)
        dy = e * (1.0 / D)
        dgf_ref[...] += _colsum(dy * yh)
        dh_ref[...] = _rms_bwd(dy, yh, r4, gf)

    return pl.pallas_call(
        body, grid=(s // ts,), name="ffn_fwd_loss",
        out_shape=(jax.ShapeDtypeStruct((s, D), F32), jax.ShapeDtypeStruct((s, 2 * FF), BF),
                   jax.ShapeDtypeStruct((1, D), F32), jax.ShapeDtypeStruct((1, D), F32)),
        in_specs=[_row_spec(ts, D), _const_spec((1, D)), _const_spec((NSH, D, HC)), _const_spec((FF, D)),
                  _const_spec((1, D)), _row_spec(ts, D)],
        out_specs=(_row_spec(ts, D), _row_spec(ts, 2 * FF), _acc_spec((1, D)), _acc_spec((1, D))),
        compiler_params=_cparams(("arbitrary",)),
    )(h, gain, wgu, wd, gfin, target)


def ffn_bwd(h, gu, dh, gain, wgu, wd, name):
    s = h.shape[0]
    ts = _tile(s, FFN_BWD_TILE)

    def body(h_ref, gu_ref, dh_ref, g_ref, wgu_ref, wd_ref, dx_ref, n_ref, a_ref, dgu_ref, dhb_ref, dg_ref):
        @pl.when(pl.program_id(0) == 0)
        def _():
            dg_ref[...] = jnp.zeros_like(dg_ref)

        x = h_ref[...]
        dh = dh_ref[...]
        gain_v = g_ref[...]
        xh, r = _rms(x)
        n_ref[...] = (xh * gain_v).astype(BF)
        dhb = (0.5 * dh).astype(BF)
        dhb_ref[...] = dhb
        dn = jnp.zeros((ts, D), F32)
        for j in range(2):
            g = gu_ref[:, j * HC:(j + 1) * HC].astype(F32)
            u = gu_ref[:, FF + j * HC:FF + (j + 1) * HC].astype(F32)
            sg = _sigmoid(g)
            sl = g * sg
            a_ref[:, j * HC:(j + 1) * HC] = (sl * u).astype(BF)
            da = _dot_nt(dhb, wd_ref[j * HC:(j + 1) * HC, :])
            dgb = (da * u * (sg * (1.0 + g * (1.0 - sg)))).astype(BF)
            dub = (da * sl).astype(BF)
            dgu_ref[:, j * HC:(j + 1) * HC] = dgb
            dgu_ref[:, FF + j * HC:FF + (j + 1) * HC] = dub
            dn = dn + _dot_nt(dgb, wgu_ref[j]) + _dot_nt(dub, wgu_ref[2 + j])
        dg_ref[...] += _colsum(dn * xh)
        dx_ref[...] = dh + _rms_bwd(dn, xh, r, gain_v)

    return pl.pallas_call(
        body, grid=(s // ts,), name=name,
        out_shape=(jax.ShapeDtypeStruct((s, D), F32), jax.ShapeDtypeStruct((s, D), BF),
                   jax.ShapeDtypeStruct((s, FF), BF), jax.ShapeDtypeStruct((s, 2 * FF), BF),
                   jax.ShapeDtypeStruct((s, D), BF), jax.ShapeDtypeStruct((1, D), F32)),
        in_specs=[_row_spec(ts, D), _row_spec(ts, 2 * FF), _row_spec(ts, D), _const_spec((1, D)),
                  _const_spec((NSH, D, HC)), _const_spec((FF, D))],
        out_specs=(_row_spec(ts, D), _row_spec(ts, D), _row_spec(ts, FF), _row_spec(ts, 2 * FF),
                   _row_spec(ts, D), _acc_spec((1, D))),
        compiler_params=_cparams(("arbitrary",)),
    )(h, gu, dh, gain, wgu, wd)


def dw_matmul(x, dy, nsplit, name, rider=None):
    s, k = x.shape
    n = dy.shape[1]
    nb = n // nsplit
    ts = _tile(s, 1024)

    def body(x_ref, dy_ref, o_ref):
        @pl.when(pl.program_id(1) == 0)
        def _():
            o_ref[...] = jnp.zeros_like(o_ref)

        o_ref[0] += _dot_tn(x_ref[...], dy_ref[...])

    (out,), rode = _pcall(
        body, grid=(nsplit, s // ts), name=name, args=(x, dy),
        out_shape=[jax.ShapeDtypeStruct((nsplit, k, nb), F32)],
        in_specs=[pl.BlockSpec((ts, k), lambda j, i: (i, 0)), pl.BlockSpec((ts, nb), lambda j, i: (i, j))],
        out_specs=[pl.BlockSpec((1, k, nb), lambda j, i: (j, 0, 0))], rider=rider)
    return (out, rode) if rider is not None else out


def _split_in_proj(p, b):
    h = INB - D
    a_val = p[0][:, :D] + b[:, 0:D]
    a_gate = jnp.concatenate([p[0][:, D:], p[1][:, :h]], axis=1) + b[:, D:2 * D]
    b_u = p[1][:, h:] + b[:, 2 * D:3 * D]
    b_v = p[2][:, :D] + b[:, 3 * D:4 * D]
    g_a = jnp.concatenate([p[2][:, D:], p[3][:, :h]], axis=1) + b[:, 4 * D:5 * D]
    g_b = p[3][:, h:] + b[:, 5 * D:6 * D]
    return a_val, a_gate, b_u, b_v, g_a, g_b


def _sgu_mix(vnb, ws_ref, sb_ref, mixed_ref, ts):
    for ci in range(ts // CHUNK):
        rows = slice(ci * CHUNK, (ci + 1) * CHUNK)
        for g in range(NG):
            cols = slice(g * GD, (g + 1) * GD)
            mixed_ref[rows, cols] = _dot(ws_ref[g], vnb[rows, cols]) + sb_ref[:, cols]


SUB = 8
CB = 128
SH_ROWS_EXTRA = HALO - SUB


def _shifted_copies(ext_ref, sh_ref, lanes, ts):
    for b in range(1, SUB):
        sh_ref[b - 1] = ext_ref[b:b + ts + SH_ROWS_EXTRA, lanes]


def _window(ext_ref, sh_ref, lanes, first, r0, nrows):
    b = first % SUB
    a = first - b
    if b == 0:
        return ext_ref[a + r0:a + r0 + nrows, lanes]
    return sh_ref[b - 1, a + r0:a + r0 + nrows, :]


def mix_fwd(h, gain, win, b_in, conv_w, conv_b, lna_g, lna_b, wa, lnb_g, lnb_b, ws, sbias, wb, wo, rider=None):
    s = h.shape[0]
    ts = _tile(s, 256)

    def body(h_ref, g_ref, win_ref, bin_ref, cw_ref, cb_ref, lag_ref, lab_ref, wa_ref, lbg_ref, lbb_ref,
             ws_ref, sb_ref, wb_ref, wo_ref, o_ref, p_ref, n_ref, c_ref, ext_ref, mixed_ref, sh_ref):
        @pl.when(pl.program_id(0) == 0)
        def _():
            ext_ref[0:HALO, :] = jnp.zeros((HALO, D), F32)

        x = h_ref[...]
        xh, _ = _rms(x)
        nb = (xh * g_ref[...]).astype(BF)
        n_ref[...] = nb
        b = bin_ref[...]
        p = []
        for k in range(NSH):
            pk = _dot(nb, win_ref[k])
            p_ref[:, k * INB:(k + 1) * INB] = (pk + b[:, k * INB:(k + 1) * INB]).astype(BF)
            p.append(pk)
        a_val, a_gate, b_u, b_v, g_a, g_b = _split_in_proj(p, b)
        ext_ref[HALO:HALO + ts, :] = a_val * _sigmoid(a_gate)
        for l0 in range(0, D, CB):
            lanes = slice(l0, l0 + CB)
            _shifted_copies(ext_ref, sh_ref, lanes, ts)
            for r0 in range(0, ts, CB):
                acc = jnp.zeros((CB, CB), F32) + cb_ref[:, lanes]
                for k in range(CW):
                    acc = acc + cw_ref[k:k + 1, lanes] * _window(ext_ref, sh_ref, lanes,
                                                                 HALO - (CW - 1) + k, r0, CB)
                c_ref[r0:r0 + CB, lanes] = acc
        ext_ref[0:HALO, :] = ext_ref[ts:ts + HALO, :]
        ch, _ = _ln(c_ref[...])
        la = ch * lag_ref[...] + lab_ref[...]
        sa = (la * _sigmoid(la)).astype(BF)
        ya = _dot(sa, wa_ref[...])
        ub, _ = _gelu(b_u)
        gv, _ = _gelu(b_v)
        vh, _ = _ln(gv)
        vnb = (vh * lbg_ref[...] + lbb_ref[...]).astype(BF)
        _sgu_mix(vnb, ws_ref, sb_ref, mixed_ref, ts)
        ob = (ub * mixed_ref[...]).astype(BF)
        yb = _dot(ob, wb_ref[...])
        merged = (_sigmoid(g_a) * ya + _sigmoid(g_b) * yb).astype(BF)
        o_ref[...] = x + _dot(merged, wo_ref[...])

    vec = _const_spec((1, D))
    sq = _const_spec((D, D))
    return _pcall(
        body, grid=(s // ts,), name="mix_fwd",
        args=(h, gain, win, b_in, conv_w, conv_b, lna_g, lna_b, wa, lnb_g, lnb_b, ws, sbias, wb, wo),
        out_shape=(jax.ShapeDtypeStruct((s, D), F32), jax.ShapeDtypeStruct((s, DIN), BF),
                   jax.ShapeDtypeStruct((s, D), BF), jax.ShapeDtypeStruct((s, D), F32)),
        in_specs=[_row_spec(ts, D), vec, _const_spec((NSH, D, INB)), _const_spec((1, DIN)),
                  _const_spec((HALO, D)), vec, vec, vec, sq, vec, vec,
                  _const_spec((NG, CHUNK, CHUNK)), _const_spec((CHUNK, D)), sq, sq],
        out_specs=(_row_spec(ts, D), _row_spec(ts, DIN), _row_spec(ts, D), _row_spec(ts, D)),
        scratch=[pltpu.VMEM((ts + HALO, D), F32), pltpu.VMEM((ts, D), F32),
                 pltpu.VMEM((SUB - 1, ts + SH_ROWS_EXTRA, CB), F32)], rider=rider)


def mix_bwd_branches(p, c, dh, lna_g, lna_b, wa, lnb_g, lnb_b, ws, wst, sbias, wb, wo, rider=None):
    s = dh.shape[0]
    ts = _tile(s, 256)
    nsteps = s // ts

    def body(p_ref, c_ref, dh_ref, lag_ref, lab_ref, wa_ref, lbg_ref, lbb_ref, ws_ref, wst_ref, sb_ref,
             wb_ref, wo_ref, dc_ref, dp_ref, sa_ref, dya_ref, ob_ref, dyb_ref, mg_ref, dhb_ref,
             dws_ref, dsb_ref, dlag_ref, dlab_ref, dlbg_ref, dlbb_ref, mixed_ref, dmix_ref, dvn_ref, dsb_acc):
        step = pl.program_id(0)

        @pl.when(step == 0)
        def _():
            for ref in (dws_ref, dsb_acc, dlag_ref, dlab_ref, dlbg_ref, dlbb_ref):
                ref[...] = jnp.zeros_like(ref)

        b_u = p_ref[:, 2 * D:3 * D].astype(F32)
        b_v = p_ref[:, 3 * D:4 * D].astype(F32)
        sga = _sigmoid(p_ref[:, 4 * D:5 * D].astype(F32))
        sgb = _sigmoid(p_ref[:, 5 * D:6 * D].astype(F32))
        lag = lag_ref[...]
        ch, ra = _ln(c_ref[...])
        la = ch * lag + lab_ref[...]
        sla = _sigmoid(la)
        sa = (la * sla).astype(BF)
        sa_ref[...] = sa
        ya = _dot(sa, wa_ref[...])
        lbg = lbg_ref[...]
        ub, tu = _gelu(b_u)
        gv, tv = _gelu(b_v)
        vh, rb = _ln(gv)
        vnb = (vh * lbg + lbb_ref[...]).astype(BF)
        _sgu_mix(vnb, ws_ref, sb_ref, mixed_ref, ts)
        mixed = mixed_ref[...]
        ob = (ub * mixed).astype(BF)
        ob_ref[...] = ob
        yb = _dot(ob, wb_ref[...])
        mg_ref[...] = (sga * ya + sgb * yb).astype(BF)
        dhb = dh_ref[...].astype(BF)
        dhb_ref[...] = dhb
        dm = _dot_nt(dhb, wo_ref[...])
        dp_ref[:, 0:2 * D] = jnp.zeros((ts, 2 * D), BF)
        dp_ref[:, 4 * D:5 * D] = (dm * ya * sga * (1.0 - sga)).astype(BF)
        dp_ref[:, 5 * D:6 * D] = (dm * yb * sgb * (1.0 - sgb)).astype(BF)
        dya = (dm * sga).astype(BF)
        dya_ref[...] = dya
        dyb = (dm * sgb).astype(BF)
        dyb_ref[...] = dyb
        dla = _dot_nt(dya, wa_ref[...]) * (sla * (1.0 + la * (1.0 - sla)))
        dlag_ref[...] += _colsum(dla * ch)
        dlab_ref[...] += _colsum(dla)
        dc_ref[...] = _ln_bwd(dla, ch, ra, lag)
        dob = _dot_nt(dyb, wb_ref[...])
        dp_ref[:, 2 * D:3 * D] = (dob * mixed * _gelu_grad(b_u, tu)).astype(BF)
        dmix = dob * ub
        dmix_ref[...] = dmix.astype(BF)
        dsb = jnp.zeros((CHUNK, D), F32)
        for ci in range(ts // CHUNK):
            rows = slice(ci * CHUNK, (ci + 1) * CHUNK)
            dsb = dsb + dmix[rows, :]
            for g in range(NG):
                cols = slice(g * GD, (g + 1) * GD)
                dmb = dmix_ref[rows, cols]
                dws_ref[g] += _dot_nt(dmb, vnb[rows, cols])
                dvn_ref[rows, cols] = _dot(wst_ref[g], dmb)
        dsb_acc[...] += dsb
        dvn = dvn_ref[...]
        dlbg_ref[...] += _colsum(dvn * vh)
        dlbb_ref[...] += _colsum(dvn)
        dp_ref[:, 3 * D:4 * D] = (_ln_bwd(dvn, vh, rb, lbg) * _gelu_grad(b_v, tv)).astype(BF)

        @pl.when(step == nsteps - 1)
        def _():
            row = lax.broadcasted_iota(jnp.int32, (CHUNK, CHUNK), 0)
            col = lax.broadcasted_iota(jnp.int32, (CHUNK, CHUNK), 1)
            for g in range(NG):
                dws_ref[g] = jnp.where(col <= row, dws_ref[g], 0.0)
            acc = jnp.zeros((CHUNK, CHUNK), F32)
            for g in range(NG):
                tot = jnp.sum(dsb_acc[:, g * GD:(g + 1) * GD], axis=-1, keepdims=True)
                acc = acc + jnp.where(col == g, tot, 0.0)
            dsb_ref[...] = acc

    vec = _const_spec((1, D))
    sq = _const_spec((D, D))
    bf_rows = jax.ShapeDtypeStruct((s, D), BF)
    acc_vec = jax.ShapeDtypeStruct((1, D), F32)
    return _pcall(
        body, grid=(nsteps,), name="mix_bwd_branches",
        args=(p, c, dh, lna_g, lna_b, wa, lnb_g, lnb_b, ws, wst, sbias, wb, wo),
        out_shape=(jax.ShapeDtypeStruct((s, D), F32), jax.ShapeDtypeStruct((s, DIN), BF),
                   bf_rows, bf_rows, bf_rows, bf_rows, bf_rows, bf_rows,
                   jax.ShapeDtypeStruct((NG, CHUNK, CHUNK), F32), jax.ShapeDtypeStruct((CHUNK, CHUNK), F32),
                   acc_vec, acc_vec, acc_vec, acc_vec),
        in_specs=[_row_spec(ts, DIN), _row_spec(ts, D), _row_spec(ts, D), vec, vec, sq, vec, vec,
                  _const_spec((NG, CHUNK, CHUNK)), _const_spec((NG, CHUNK, CHUNK)), _const_spec((CHUNK, D)),
                  sq, sq],
        out_specs=(_row_spec(ts, D), _row_spec(ts, DIN)) + (_row_spec(ts, D),) * 6
        + (_acc_spec((NG, CHUNK, CHUNK)), _acc_spec((CHUNK, CHUNK))) + (_acc_spec((1, D)),) * 4,
        scratch=[pltpu.VMEM((ts, D), F32), pltpu.VMEM((ts, D), BF), pltpu.VMEM((ts, D), F32),
                 pltpu.VMEM((CHUNK, D), F32)], rider=rider)


def conv_bwd(p, dc, dp, conv_w):
    s = dc.shape[0]
    ts = _tile(s, 256)
    nsteps = s // ts
    per = ts // HALO

    rb = 64

    def body(pm_ref, pp_ref, dcm_ref, dcn_ref, cw_ref, dpin_ref, dp_ref, dw_ref, db_ref, ext_ref, dext_ref,
             dw8_ref, sh_ref, dsh_ref, dglu_ref):
        del dpin_ref
        step = pl.program_id(0)

        @pl.when(step == 0)
        def _():
            dw8_ref[...] = jnp.zeros_like(dw8_ref)
            db_ref[...] = jnp.zeros_like(db_ref)

        a_val = pm_ref[:, 0:D].astype(F32)
        sg = _sigmoid(pm_ref[:, D:2 * D].astype(F32))
        prev = pp_ref[:, 0:D].astype(F32) * _sigmoid(pp_ref[:, D:2 * D].astype(F32))
        ext_ref[0:HALO, :] = jnp.where(step > 0, prev, 0.0)
        ext_ref[HALO:HALO + ts, :] = a_val * sg
        dcm = dcm_ref[...]
        dext_ref[0:ts, :] = dcm
        dext_ref[ts:ts + HALO, :] = jnp.where(step < nsteps - 1, dcn_ref[...], 0.0)
        db_ref[...] += _colsum(dcm)
        for l0 in range(0, D, CB):
            lanes = slice(l0, l0 + CB)
            _shifted_copies(dext_ref, dsh_ref, lanes, ts)
            for r0 in range(0, ts, CB):
                acc = jnp.zeros((CB, CB), F32)
                for k in range(CW):
                    acc = acc + cw_ref[k:k + 1, lanes] * _window(dext_ref, dsh_ref, lanes, CW - 1 - k, r0, CB)
                dglu_ref[r0:r0 + CB, lanes] = acc
            _shifted_copies(ext_ref, sh_ref, lanes, ts)
            accs = [jnp.zeros((SUB, CB), F32) for _ in range(CW)]
            for r0 in range(0, ts, rb):
                dcb = dext_ref[r0:r0 + rb, lanes]
                for k in range(CW):
                    prod = dcb * _window(ext_ref, sh_ref, lanes, HALO - (CW - 1) + k, r0, rb)
                    accs[k] = accs[k] + jnp.sum(prod.reshape(rb // SUB, SUB, CB), axis=0)
            for k in range(CW):
                dw8_ref[k, :, lanes] += accs[k]
        dglu = dglu_ref[...]
        dp_ref[:, 0:D] = (dglu * sg).astype(BF)
        dp_ref[:, D:2 * D] = (dglu * a_val * sg * (1.0 - sg)).astype(BF)

        @pl.when(step == nsteps - 1)
        def _():
            dw_ref[...] = jnp.zeros_like(dw_ref)
            for k in range(CW):
                dw_ref[k:k + 1, :] = _colsum(dw8_ref[k])

    return pl.pallas_call(
        body, grid=(nsteps,), name="conv_bwd",
        out_shape=(jax.ShapeDtypeStruct((s, DIN), BF), jax.ShapeDtypeStruct((HALO, D), F32),
                   jax.ShapeDtypeStruct((1, D), F32)),
        in_specs=[pl.BlockSpec((ts, 2 * D), lambda i: (i, 0)),
                  pl.BlockSpec((HALO, 2 * D), lambda i: (jnp.maximum(i * per - 1, 0), 0)),
                  _row_spec(ts, D),
                  pl.BlockSpec((HALO, D), lambda i: (jnp.minimum((i + 1) * per, s // HALO - 1), 0)),
                  _const_spec((HALO, D)),
                  pl.BlockSpec(memory_space=pl.ANY)],
        out_specs=(pl.BlockSpec((ts, 2 * D), lambda i: (i, 0)), _acc_spec((HALO, D)), _acc_spec((1, D))),
        scratch_shapes=[pltpu.VMEM((ts + HALO, D), F32), pltpu.VMEM((ts + HALO, D), F32),
                        pltpu.VMEM((HALO, SUB, D), F32),
                        pltpu.VMEM((SUB - 1, ts + SH_ROWS_EXTRA, CB), F32),
                        pltpu.VMEM((SUB - 1, ts + SH_ROWS_EXTRA, CB), F32),
                        pltpu.VMEM((ts, D), F32)],
        input_output_aliases={5: 0},
        compiler_params=_cparams(("arbitrary",)),
    )(p, p, dc, dc, conv_w, dp)


def mix_bwd_in(dp, h, dh, gain, win):
    s = h.shape[0]
    ts = _tile(s, 512)

    def body(dp_ref, h_ref, dh_ref, g_ref, win_ref, dx_ref, dg_ref, db_ref):
        @pl.when(pl.program_id(0) == 0)
        def _():
            dg_ref[...] = jnp.zeros_like(dg_ref)
            db_ref[...] = jnp.zeros_like(db_ref)

        gain_v = g_ref[...]
        xh, r = _rms(h_ref[...])
        dn = jnp.zeros((ts, D), F32)
        for k in range(NSH):
            dpk = dp_ref[:, k * INB:(k + 1) * INB]
            dn = dn + _dot_nt(dpk, win_ref[k])
            db_ref[:, k * INB:(k + 1) * INB] += _colsum(dpk.astype(F32))
        dg_ref[...] += _colsum(dn * xh)
        dx_ref[...] = dh_ref[...] + _rms_bwd(dn, xh, r, gain_v)

    return pl.pallas_call(
        body, grid=(s // ts,), name="mix_bwd_in",
        out_shape=(jax.ShapeDtypeStruct((s, D), F32), jax.ShapeDtypeStruct((1, D), F32),
                   jax.ShapeDtypeStruct((1, DIN), F32)),
        in_specs=[_row_spec(ts, DIN), _row_spec(ts, D), _row_spec(ts, D), _const_spec((1, D)),
                  _const_spec((NSH, D, INB))],
        out_specs=(_row_spec(ts, D), _acc_spec((1, D)), _acc_spec((1, DIN))),
        compiler_params=_cparams(("arbitrary",)),
    )(dp, h, dh, gain, win)


def kv_proj(mem, gain, wkv):
    def body(m_ref, g_ref, w_ref, k_ref, v_ref, n_ref):
        xh, _ = _rms(m_ref[...])
        nb = (xh * g_ref[...]).astype(BF)
        n_ref[...] = nb
        half = D // 2
        for j in range(2):
            k_ref[:, j * half:(j + 1) * half] = _dot(nb, w_ref[j]).astype(BF)
            v_ref[:, j * half:(j + 1) * half] = _dot(nb, w_ref[2 + j]).astype(BF)

    o = jax.ShapeDtypeStruct((NMEM, D), BF)
    return pl.pallas_call(body, name="kv_proj", out_shape=(o, o, o), compiler_params=_cparams())(mem, gain, wkv)


def kv_bwd(mem, gain, memn, wkv, dk, dv):
    def body(m_ref, g_ref, n_ref, w_ref, dk_ref, dv_ref, dw_ref, dg_ref):
        xh, _ = _rms(m_ref[...])
        nb = n_ref[...]
        half = D // 2
        dn = jnp.zeros((NMEM, D), F32)
        for j in range(2):
            dkb = dk_ref[:, j * half:(j + 1) * half].astype(BF)
            dvb = dv_ref[:, j * half:(j + 1) * half].astype(BF)
            dw_ref[j] = _dot_tn(nb, dkb)
            dw_ref[2 + j] = _dot_tn(nb, dvb)
            dn = dn + _dot_nt(dkb, w_ref[j]) + _dot_nt(dvb, w_ref[2 + j])
        dg_ref[...] = _colsum(dn * xh)

    return pl.pallas_call(
        body, name="kv_bwd",
        out_shape=(jax.ShapeDtypeStruct((NSH, D, D // 2), F32), jax.ShapeDtypeStruct((1, D), F32)),
        compiler_params=_cparams())(mem, gain, memn, wkv, dk, dv)


def _attend(qb, k_ref, v_ref, h):
    cols = slice(h * HD, (h + 1) * HD)
    sc = _dot_nt(qb[:, cols], k_ref[:, cols]) * ATT_SCALE
    e = jnp.exp(sc - jnp.max(sc, axis=-1, keepdims=True))
    pr = e / jnp.sum(e, axis=-1, keepdims=True)
    return pr, _dot(pr.astype(BF), v_ref[:, cols])


def xattn_fwd(h, gain, wq, k, v, wo):
    s = h.shape[0]
    ts = _tile(s, 512)

    def body(h_ref, g_ref, wq_ref, k_ref, v_ref, wo_ref, o_ref, att_ref):
        x = h_ref[...]
        xh, _ = _rms(x)
        nb = (xh * g_ref[...]).astype(BF)
        qb = _dot(nb, wq_ref[...]).astype(BF)
        for hd in range(NH):
            _, oh = _attend(qb, k_ref, v_ref, hd)
            att_ref[:, hd * HD:(hd + 1) * HD] = oh.astype(BF)
        o_ref[...] = x + _dot(att_ref[...], wo_ref[...])

    sq = _const_spec((D, D))
    kvs = _const_spec((NMEM, D))
    return pl.pallas_call(
        body, grid=(s // ts,), name="xattn_fwd",
        out_shape=jax.ShapeDtypeStruct((s, D), F32),
        in_specs=[_row_spec(ts, D), _const_spec((1, D)), sq, kvs, kvs, sq],
        out_specs=_row_spec(ts, D),
        scratch_shapes=[pltpu.VMEM((ts, D), BF)],
        compiler_params=_cparams(("arbitrary",)),
    )(h, gain, wq, k, v, wo)


def xattn_bwd(h, dh, gain, wq, k, v, wo, rider=None):
    s = h.shape[0]
    ts = _tile(s, 256)

    def body(h_ref, dh_ref, g_ref, wq_ref, k_ref, v_ref, wo_ref,
             dx_ref, n_ref, dq_ref, att_ref, dhb_ref, dk_ref, dv_ref, dg_ref):
        @pl.when(pl.program_id(0) == 0)
        def _():
            for ref in (dk_ref, dv_ref, dg_ref):
                ref[...] = jnp.zeros_like(ref)

        x = h_ref[...]
        dh = dh_ref[...]
        gain_v = g_ref[...]
        xh, r = _rms(x)
        nb = (xh * gain_v).astype(BF)
        n_ref[...] = nb
        qb = _dot(nb, wq_ref[...]).astype(BF)
        dhb = dh.astype(BF)
        dhb_ref[...] = dhb
        dob = _dot_nt(dhb, wo_ref[...]).astype(BF)
        for hd in range(NH):
            cols = slice(hd * HD, (hd + 1) * HD)
            pr, oh = _attend(qb, k_ref, v_ref, hd)
            att_ref[:, cols] = oh.astype(BF)
            doh = dob[:, cols]
            dpr = _dot_nt(doh, v_ref[:, cols])
            dv_ref[:, cols] += _dot_tn(pr.astype(BF), doh)
            dsc = (pr * (dpr - jnp.sum(dpr * pr, axis=-1, keepdims=True)) * ATT_SCALE).astype(BF)
            dq_ref[:, cols] = _dot(dsc, k_ref[:, cols]).astype(BF)
            dk_ref[:, cols] += _dot_tn(dsc, qb[:, cols])
        dn = _dot_nt(dq_ref[...], wq_ref[...])
        dg_ref[...] += _colsum(dn * xh)
        dx_ref[...] = dh + _rms_bwd(dn, xh, r, gain_v)

    sq = _const_spec((D, D))
    kvs = _const_spec((NMEM, D))
    bf_rows = jax.ShapeDtypeStruct((s, D), BF)
    kv_acc = jax.ShapeDtypeStruct((NMEM, D), F32)
    return _pcall(
        body, grid=(s // ts,), name="xattn_bwd", args=(h, dh, gain, wq, k, v, wo),
        out_shape=(jax.ShapeDtypeStruct((s, D), F32), bf_rows, bf_rows, bf_rows, bf_rows, kv_acc, kv_acc,
                   jax.ShapeDtypeStruct((1, D), F32)),
        in_specs=[_row_spec(ts, D), _row_spec(ts, D), _const_spec((1, D)), sq, kvs, kvs, sq],
        out_specs=(_row_spec(ts, D),) * 5 + (_acc_spec((NMEM, D)), _acc_spec((NMEM, D)), _acc_spec((1, D))),
        rider=rider)


BLOCK_BYTES = 3 << 19


def _row_block(rows, cols):
    rb = rows
    while rb * cols * 4 > BLOCK_BYTES and rb % 32 == 0:
        rb //= 2
    return rb


def cast_bf16(w, chip, name):
    r, c = w.shape
    rb = _row_block(r, c)

    def body(chip_ref, w_ref, o_ref):
        del chip_ref
        o_ref[0] = w_ref[...].astype(BF)

    return pl.pallas_call(
        body, name=name, out_shape=jax.ShapeDtypeStruct((NSH, r, c), BF),
        grid_spec=pltpu.PrefetchScalarGridSpec(
            num_scalar_prefetch=1, grid=(r // rb,),
            in_specs=[pl.BlockSpec((rb, c), lambda i, chip_ref: (i, 0))],
            out_specs=pl.BlockSpec((1, rb, c), lambda i, chip_ref: (chip_ref[0], i, 0))),
        compiler_params=_cparams(("arbitrary",)))(chip, w)


def pair_add(g4, recv, core, name):
    nsh, _, rh, c = g4.shape
    rb = _row_block(rh, c)

    def body(core_ref, g_ref, r_ref, o_ref, ob_ref):
        del core_ref
        sm = g_ref[0, 0] + r_ref[0]
        o_ref[0] = sm
        ob_ref[0] = sm.astype(BF)

    spec3 = pl.BlockSpec((1, rb, c), lambda k, i, core_ref: (k, i, 0))
    return pl.pallas_call(
        body, name=name,
        out_shape=(jax.ShapeDtypeStruct((nsh, rh, c), F32), jax.ShapeDtypeStruct((nsh, rh, c), BF)),
        grid_spec=pltpu.PrefetchScalarGridSpec(
            num_scalar_prefetch=1, grid=(nsh, rh // rb),
            in_specs=[pl.BlockSpec((1, 1, rb, c), lambda k, i, core_ref: (k, core_ref[0], i, 0)), spec3],
            out_specs=(spec3, spec3)),
        compiler_params=_cparams(("arbitrary", "arbitrary")))(core, g4, recv)


def chip_sum(psum, recv, place, name):
    _, rh, c = psum.shape
    rb = _row_block(rh, c)

    def body(place_ref, p_ref, r_ref, o_ref):
        del place_ref
        acc = p_ref[0]
        for j in range(NSH - 1):
            acc = acc + r_ref[j].astype(F32)
        o_ref[0] = acc

    return pl.pallas_call(
        body, name=name, out_shape=jax.ShapeDtypeStruct((2, rh, c), F32),
        grid_spec=pltpu.PrefetchScalarGridSpec(
            num_scalar_prefetch=1, grid=(rh // rb,),
            in_specs=[pl.BlockSpec((1, rb, c), lambda i, place_ref: (place_ref[0], i, 0)),
                      pl.BlockSpec((NSH - 1, rb, c), lambda i, place_ref: (0, i, 0))],
            out_specs=pl.BlockSpec((1, rb, c), lambda i, place_ref: (place_ref[1], i, 0))),
        compiler_params=_cparams(("arbitrary",)))(place, psum, recv)


def _adamw_math(w, g, m, v):
    m = ADAM_B1 * m + (1.0 - ADAM_B1) * g
    v = ADAM_B2 * v + (1.0 - ADAM_B2) * (g * g)
    m_hat = m / (1.0 - ADAM_B1 ** ADAM_STEP)
    v_hat = v / (1.0 - ADAM_B2 ** ADAM_STEP)
    delta = -ADAM_LR * (m_hat / (jnp.sqrt(v_hat) + ADAM_EPS) + ADAM_WD * w)
    return delta, m, v


def adamw(w, g, m, v, name):
    r, c = w.shape
    rb = _row_block(r, c)

    def body(w_ref, g_ref, m_ref, v_ref, d_ref, mo_ref, vo_ref):
        d, mn, vn = _adamw_math(w_ref[...], g_ref[...], m_ref[...], v_ref[...])
        d_ref[...] = d
        mo_ref[...] = mn
        vo_ref[...] = vn

    o = jax.ShapeDtypeStruct((r, c), F32)
    spec = _row_spec(rb, c)
    return pl.pallas_call(
        body, grid=(r // rb,), name=name, out_shape=(o, o, o),
        in_specs=[spec] * 4, out_specs=(spec,) * 3,
        compiler_params=_cparams(("arbitrary",)))(w, g, m, v)


def _place():
    return lax.axis_index("x"), lax.axis_index("y"), lax.axis_index("c")


def _other_chips(x, y):
    return [(1 - x, y), (x, 1 - y), (1 - x, 1 - y)]


NOTHER = NSH - 1


def gather_rider(arrays):
    nw = len(arrays)
    nici = nw * NOTHER

    def copies(refs, send_sems, recv_sems):
        x, y, c = _place()
        ici, d2d = [], []
        for w in range(nw):
            for j, (px, py) in enumerate(_other_chips(x, y)):
                n = w * NOTHER + j
                sems = dict(send_sem=send_sems.at[n], recv_sem=recv_sems.at[n],
                            device_id=(px, py, c), device_id_type=MESH)
                mine = refs[w].at[2 * x + y, c]
                theirs = refs[w].at[2 * px + py, c]
                ici.append((pltpu.make_async_remote_copy(src_ref=mine, dst_ref=mine, **sems),
                            pltpu.make_async_remote_copy(src_ref=mine, dst_ref=theirs, **sems)))
                sems = dict(send_sem=send_sems.at[nici + n], recv_sem=recv_sems.at[nici + n],
                            device_id=(x, y, 1 - c), device_id_type=MESH)
                d2d.append((pltpu.make_async_remote_copy(src_ref=theirs, dst_ref=theirs, **sems),
                            pltpu.make_async_remote_copy(src_ref=theirs, dst_ref=refs[w].at[2 * px + py, 1 - c],
                                                         **sems)))
        return ici, d2d

    def start(ins, outs, send_sems, recv_sems):
        ici, _ = copies(outs, send_sems, recv_sems)
        for send, _ in ici:
            send.start()

    def finish(ins, outs, send_sems, recv_sems):
        ici, d2d = copies(outs, send_sems, recv_sems)
        for (_, landed), (forward, _) in zip(ici, d2d):
            landed.wait_recv()
            forward.start()
        for _, landed in d2d:
            landed.wait_recv()
        for send, _ in ici + d2d:
            send.wait_send()

    return Rider(arrays, [jax.ShapeDtypeStruct(a.shape, a.dtype) for a in arrays], {i: i for i in range(nw)},
                 2 * nici, start, finish)


def exchange_rider(psums):
    nw = len(psums)

    def copies(ins, outs, send_sems, recv_sems):
        x, y, c = _place()
        return [pltpu.make_async_remote_copy(
            src_ref=ins[w].at[2 * px + py], dst_ref=outs[w].at[j],
            send_sem=send_sems.at[w * NOTHER + j], recv_sem=recv_sems.at[w * NOTHER + j],
            device_id=(px, py, c), device_id_type=MESH)
            for w in range(nw) for j, (px, py) in enumerate(_other_chips(x, y))]

    def start(ins, outs, send_sems, recv_sems):
        for cp in copies(ins, outs, send_sems, recv_sems):
            cp.start()

    def finish(ins, outs, send_sems, recv_sems):
        for cp in copies(ins, outs, send_sems, recv_sems):
            cp.wait()

    return Rider(psums, [jax.ShapeDtypeStruct((NOTHER,) + p.shape[1:], p.dtype) for p in psums], {},
                 nw * NOTHER, start, finish)


def pair_exchange(grads, name):
    nw = len(grads)

    def body(*refs):
        ins, outs = refs[:nw], refs[nw:2 * nw]
        send_sems, recv_sems = refs[2 * nw:]
        x, y, c = _place()
        cps = []
        for w in range(nw):
            cp = pltpu.make_async_remote_copy(
                src_ref=ins[w].at[:, 1 - c], dst_ref=outs[w],
                send_sem=send_sems.at[w], recv_sem=recv_sems.at[w],
                device_id=(x, y, 1 - c), device_id_type=MESH)
            cp.start()
            cps.append(cp)
        for cp in cps:
            cp.wait()

    return pl.pallas_call(
        body, name=name,
        out_shape=tuple(jax.ShapeDtypeStruct((g.shape[0],) + g.shape[2:], g.dtype) for g in grads),
        in_specs=[HBM_SPEC] * nw, out_specs=(HBM_SPEC,) * nw,
        scratch_shapes=[pltpu.SemaphoreType.DMA((nw,)), pltpu.SemaphoreType.DMA((nw,))],
        compiler_params=pltpu.CompilerParams(has_side_effects=True),
    )(*grads)


def swap_rider(halves):
    nw = len(halves)

    def copies(refs, send_sems, recv_sems):
        x, y, c = _place()
        out = []
        for w in range(nw):
            sems = dict(send_sem=send_sems.at[w], recv_sem=recv_sems.at[w],
                        device_id=(x, y, 1 - c), device_id_type=MESH)
            mine = refs[w].at[c]
            out.append((pltpu.make_async_remote_copy(src_ref=mine, dst_ref=mine, **sems),
                        pltpu.make_async_remote_copy(src_ref=mine, dst_ref=refs[w].at[1 - c], **sems)))
        return out

    def start(ins, outs, send_sems, recv_sems):
        for send, _ in copies(outs, send_sems, recv_sems):
            send.start()

    def finish(ins, outs, send_sems, recv_sems):
        cps = copies(outs, send_sems, recv_sems)
        for _, recv in cps:
            recv.wait_recv()
        for send, _ in cps:
            send.wait_send()

    return Rider(halves, [jax.ShapeDtypeStruct(h.shape, h.dtype) for h in halves], {i: i for i in range(nw)},
                 nw, start, finish)


NDEV = 8


def small_allreduce(buf):
    r = buf.shape[0]

    def body(b_ref, o_ref, slots_ref, send_sems, recv_sems):
        x, y, c = _place()
        me = 4 * x + 2 * y + c
        slots_ref[me] = b_ref[...]
        cps = []
        for rel in range(1, NDEV):
            peer = (x ^ (rel >> 2), y ^ ((rel >> 1) & 1), c ^ (rel & 1))
            cp = pltpu.make_async_remote_copy(
                src_ref=b_ref, dst_ref=slots_ref.at[me],
                send_sem=send_sems.at[rel - 1], recv_sem=recv_sems.at[rel - 1],
                device_id=peer, device_id_type=MESH)
            cp.start()
            cps.append(cp)
        for rel in range(1, NDEV):
            peer = (x ^ (rel >> 2), y ^ ((rel >> 1) & 1), c ^ (rel & 1))
            pltpu.make_async_remote_copy(
                src_ref=b_ref, dst_ref=slots_ref.at[4 * peer[0] + 2 * peer[1] + peer[2]],
                send_sem=send_sems.at[rel - 1], recv_sem=recv_sems.at[rel - 1],
                device_id=peer, device_id_type=MESH).wait_recv()
        for cp in cps:
            cp.wait_send()
        acc = slots_ref[0]
        for dev in range(1, NDEV):
            acc = acc + slots_ref[dev]
        o_ref[...] = acc

    vm = pl.BlockSpec(memory_space=pltpu.VMEM)
    return pl.pallas_call(
        body, name="small_allreduce", out_shape=jax.ShapeDtypeStruct((r, D), F32),
        in_specs=[vm], out_specs=vm,
        scratch_shapes=[pltpu.VMEM((NDEV, r, D), F32), pltpu.SemaphoreType.DMA((NDEV - 1,)),
                        pltpu.SemaphoreType.DMA((NDEV - 1,))],
        compiler_params=_cparams(has_side_effects=True),
    )(buf)


BIG = ("ffn1_w_gu", "ffn1_w_down", "w_in", "w_a_out", "w_b_out", "w_out", "w_q", "w_kv", "w_o",
       "ffn2_w_gu", "ffn2_w_down")
SMALL = {"ffn1_norm": (0, 1), "mix_norm": (8, 1), "xattn_norm": (16, 1), "mem_norm": (24, 1),
         "ffn2_norm": (32, 1), "final_norm": (40, 1), "conv_b": (48, 1), "conv_ln_g": (56, 1),
         "conv_ln_b": (64, 1), "sgu_ln_g": (72, 1), "sgu_ln_b": (80, 1), "b_in": (88, 6),
         "conv_w": (96, CW), "sgu_w": (128, 64), "sgu_b": (192, 1)}
LOSS_ROW = 200
SMALL_ROWS = 208


def _pad_rows(a, rows):
    return jnp.pad(a, ((0, rows - a.shape[0]), (0, D - a.shape[1])))


def _pack_small(parts):
    names = sorted(parts, key=lambda n: SMALL[n][0] if n in SMALL else LOSS_ROW)
    rows = []
    for i, n in enumerate(names):
        start = SMALL[n][0] if n in SMALL else LOSS_ROW
        end = SMALL_ROWS if i + 1 == len(names) else (SMALL[names[i + 1]][0] if names[i + 1] in SMALL else LOSS_ROW)
        rows.append(_pad_rows(parts[n], end - start))
    return jnp.concatenate(rows, axis=0)


def _small_views(w):
    return {
        "ffn1_norm": w["ffn1_norm"], "mix_norm": w["mix_norm"], "xattn_norm": w["xattn_norm"],
        "mem_norm": w["mem_norm"], "ffn2_norm": w["ffn2_norm"], "final_norm": w["final_norm"].reshape(1, D),
        "conv_b": w["conv_b"], "conv_ln_g": w["conv_ln_g"], "conv_ln_b": w["conv_ln_b"],
        "sgu_ln_g": w["sgu_ln_g"], "sgu_ln_b": w["sgu_ln_b"], "b_in": w["b_in"].reshape(6, D),
        "conv_w": w["conv_w"][0], "sgu_w": w["sgu_w"].reshape(64, D), "sgu_b": w["sgu_b"].reshape(1, NG * CHUNK),
    }


def _unpack_small(buf, like, chip):
    out = {}
    for n, (start, rows) in SMALL.items():
        blk = buf[start:start + rows]
        if n == "conv_w":
            blk = blk[:, :like[n].shape[-1]] if chip is None else lax.dynamic_slice_in_dim(
                blk, chip * like[n].shape[-1], like[n].shape[-1], axis=1)
        elif n == "sgu_b":
            blk = blk[:, :NG * CHUNK]
        out[n] = blk.reshape(like[n].shape)
    return out


def kernel(x, mem, ffn1_norm, ffn1_w_gu, ffn1_w_down, mix_norm, w_in, b_in, conv_w, conv_b, conv_ln_g, conv_ln_b, w_a_out, sgu_ln_g, sgu_ln_b, sgu_w, sgu_b, w_b_out, w_out, xattn_norm, mem_norm, w_q, w_kv, w_o, ffn2_norm, ffn2_w_gu, ffn2_w_down, final_norm, loss_target, m_ffn1_norm, m_ffn1_w_gu, m_ffn1_w_down, m_mix_norm, m_w_in, m_b_in, m_conv_w, m_conv_b, m_conv_ln_g, m_conv_ln_b, m_w_a_out, m_sgu_ln_g, m_sgu_ln_b, m_sgu_w, m_sgu_b, m_w_b_out, m_w_out, m_xattn_norm, m_mem_norm, m_w_q, m_w_kv, m_w_o, m_ffn2_norm, m_ffn2_w_gu, m_ffn2_w_down, m_final_norm, v_ffn1_norm, v_ffn1_w_gu, v_ffn1_w_down, v_mix_norm, v_w_in, v_b_in, v_conv_w, v_conv_b, v_conv_ln_g, v_conv_ln_b, v_w_a_out, v_sgu_ln_g, v_sgu_ln_b, v_sgu_w, v_sgu_b, v_w_b_out, v_w_out, v_xattn_norm, v_mem_norm, v_w_q, v_w_kv, v_w_o, v_ffn2_norm, v_ffn2_w_gu, v_ffn2_w_down, v_final_norm):
    names = ("ffn1_norm", "ffn1_w_gu", "ffn1_w_down", "mix_norm", "w_in", "b_in", "conv_w", "conv_b",
             "conv_ln_g", "conv_ln_b", "w_a_out", "sgu_ln_g", "sgu_ln_b", "sgu_w", "sgu_b", "w_b_out", "w_out",
             "xattn_norm", "mem_norm", "w_q", "w_kv", "w_o", "ffn2_norm", "ffn2_w_gu", "ffn2_w_down",
             "final_norm")
    wts = dict(zip(names, (ffn1_norm, ffn1_w_gu, ffn1_w_down, mix_norm, w_in, b_in, conv_w, conv_b, conv_ln_g,
                           conv_ln_b, w_a_out, sgu_ln_g, sgu_ln_b, sgu_w, sgu_b, w_b_out, w_out, xattn_norm,
                           mem_norm, w_q, w_kv, w_o, ffn2_norm, ffn2_w_gu, ffn2_w_down, final_norm)))
    mom1 = dict(zip(names, (m_ffn1_norm, m_ffn1_w_gu, m_ffn1_w_down, m_mix_norm, m_w_in, m_b_in, m_conv_w,
                            m_conv_b, m_conv_ln_g, m_conv_ln_b, m_w_a_out, m_sgu_ln_g, m_sgu_ln_b, m_sgu_w,
                            m_sgu_b, m_w_b_out, m_w_out, m_xattn_norm, m_mem_norm, m_w_q, m_w_kv, m_w_o,
                            m_ffn2_norm, m_ffn2_w_gu, m_ffn2_w_down, m_final_norm)))
    mom2 = dict(zip(names, (v_ffn1_norm, v_ffn1_w_gu, v_ffn1_w_down, v_mix_norm, v_w_in, v_b_in, v_conv_w,
                            v_conv_b, v_conv_ln_g, v_conv_ln_b, v_w_a_out, v_sgu_ln_g, v_sgu_ln_b, v_sgu_w,
                            v_sgu_b, v_w_b_out, v_w_out, v_xattn_norm, v_mem_norm, v_w_q, v_w_kv, v_w_o,
                            v_ffn2_norm, v_ffn2_w_gu, v_ffn2_w_down, v_final_norm)))
    xi, yi, ci = _place()
    chip = (2 * xi + yi).astype(jnp.int32)
    core = ci.astype(jnp.int32)
    core_arr = core.reshape(1)
    chip_arr = chip.reshape(1)
    place_arr = jnp.stack([chip, core])
    x2, mem2, tgt = x[0], mem[0], loss_target[0]

    slot = {n: cast_bf16(wts[n][0], chip_arr, "cast_" + n) for n in BIG}
    cw_pad = jnp.pad(conv_w[0], ((0, HALO - CW), (0, 0)))
    slot["conv_w"] = lax.dynamic_update_slice(jnp.zeros((NSH,) + cw_pad.shape, F32), cw_pad[None], (chip, 0, 0))
    g_first = ("ffn1_w_gu", "ffn1_w_down")
    g_mix = ("w_in", "w_a_out", "w_b_out", "w_out", "conv_w")
    g_rest = ("w_q", "w_kv", "w_o", "ffn2_w_gu", "ffn2_w_down")
    def gather(group):
        return gather_rider([slot[n].reshape(NSH, 2, slot[n].shape[1] // 2, slot[n].shape[2]) for n in group])

    def gathered(group, res):
        return {n: r.reshape(slot[n].shape) for n, r in zip(group, res)}

    full = gathered(g_first, run_rider(gather(g_first), "gather_ffn1"))
    wgu1, wd1 = full["ffn1_w_gu"], full["ffn1_w_down"].reshape(FF, D)
    tril = jnp.tril(jnp.ones((CHUNK, CHUNK), dtype=bool))
    ws = jnp.where(tril[None], sgu_w[0], 0.0).astype(BF)
    wst = jnp.transpose(ws, (0, 2, 1))
    sbias = jnp.repeat(jnp.transpose(sgu_b[0]), GD, axis=1)
    gfin = final_norm.reshape(1, D)

    (h1, gu1), rode = ffn_fwd(x2, ffn1_norm, wgu1, wd1, rider=gather(g_mix))
    full.update(gathered(g_mix, rode))
    win = full["w_in"]
    wa, wb, wout = (full[n].reshape(D, D) for n in ("w_a_out", "w_b_out", "w_out"))
    cw_full = jnp.transpose(full["conv_w"], (1, 0, 2)).reshape(HALO, D)
    (h2, proj, n2b, conv_out), rode = mix_fwd(
        h1, mix_norm, win, b_in, cw_full, conv_b, conv_ln_g, conv_ln_b, wa, sgu_ln_g, sgu_ln_b, ws, sbias, wb,
        wout, rider=gather(g_rest))
    full.update(gathered(g_rest, rode))
    wgu2, wd2, wkv = full["ffn2_w_gu"], full["ffn2_w_down"].reshape(FF, D), full["w_kv"]
    wq, wo = full["w_q"].reshape(D, D), full["w_o"].reshape(D, D)
    kb, vb, memn = kv_proj(mem2, mem_norm, wkv)
    h3 = xattn_fwd(h2, xattn_norm, wq, kb, vb, wo)
    dh4, gu2, loss_lanes, d_final = ffn_fwd_loss(h3, ffn2_norm, wgu2, wd2, gfin, tgt)

    def pair_sums(group, grads, tag):
        g4 = []
        for n, g in zip(group, grads):
            rs, cs = wts[n].shape[1:]
            g4.append(g.reshape(NSH, 2, rs // 2, cs))
        recv = pair_exchange(g4, "pair_exchange_" + tag)
        sums = [pair_add(g, r, core_arr, "pair_add_" + n) for n, g, r in zip(group, g4, recv)]
        return [s[0] for s in sums], exchange_rider([s[1] for s in sums])

    halves = {}

    def chip_sums(group, psums, recv):
        for n, p, r in zip(group, psums, recv):
            halves[n] = chip_sum(p, r, place_arr, "chip_sum_" + n)

    dh3, n4, a4, dgu4, dhb4, d_ffn2n = ffn_bwd(h3, gu2, dh4, ffn2_norm, wgu2, wd2, "ffn2_bwd")
    g_ffn2 = ("ffn2_w_gu", "ffn2_w_down")
    ps_ffn2, ride = pair_sums(g_ffn2, [dw_matmul(n4, dgu4, NSH, "dw_ffn2_gu"),
                                       dw_matmul(a4, dhb4, 1, "dw_ffn2_down")], "ffn2")
    (dh2, n3, dq, att, dhb3, dk, dv, d_xn), rode = xattn_bwd(h2, dh3, xattn_norm, wq, kb, vb, wo, rider=ride)
    chip_sums(g_ffn2, ps_ffn2, rode)
    g_att = ("w_q", "w_o", "w_kv")
    d_wkv, d_memn = kv_bwd(mem2, mem_norm, memn, wkv, dk, dv)
    ps_att, ride = pair_sums(g_att, [dw_matmul(n3, dq, 1, "dw_q"), dw_matmul(att, dhb3, 1, "dw_o"), d_wkv], "att")
    ((dconv, dproj, sa, dya, ob, dyb, mg, dhb2, d_sgu_w, d_sgu_b, d_lna_g, d_lna_b, d_lnb_g, d_lnb_b),
     rode) = mix_bwd_branches(proj, conv_out, dh2, conv_ln_g, conv_ln_b, wa, sgu_ln_g, sgu_ln_b, ws, wst,
                              sbias, wb, wout, rider=ride)
    chip_sums(g_att, ps_att, rode)
    dproj, d_conv_w, d_conv_b = conv_bwd(proj, dconv, dproj, cw_full)
    dh1, d_mixn, d_b_in = mix_bwd_in(dproj, h1, dh2, mix_norm, win)
    g_mixw = ("w_a_out", "w_b_out", "w_out", "w_in")
    ps_mix, ride = pair_sums(g_mixw, [dw_matmul(sa, dya, 1, "dw_a_out"), dw_matmul(ob, dyb, 1, "dw_b_out"),
                                      dw_matmul(mg, dhb2, 1, "dw_out"), dw_matmul(n2b, dproj, NSH, "dw_in")],
                             "mix")
    dx, n1, a1, dgu1, dhb1, d_ffn1n = ffn_bwd(x2, gu1, dh1, ffn1_norm, wgu1, wd1, "ffn1_bwd")
    d_wgu1, rode = dw_matmul(n1, dgu1, NSH, "dw_ffn1_gu", rider=ride)
    chip_sums(g_mixw, ps_mix, rode)
    ps_ffn1, ride = pair_sums(g_first, [d_wgu1, dw_matmul(a1, dhb1, 1, "dw_ffn1_down")], "ffn1")
    chip_sums(g_first, ps_ffn1, run_rider(ride, "chip_exchange_ffn1"))
    swapped = run_rider(swap_rider([halves[n] for n in BIG]), "pair_swap")
    gshard = {n: g.reshape(wts[n].shape[1:]) for n, g in zip(BIG, swapped)}

    small_grads = {
        "ffn1_norm": d_ffn1n, "mix_norm": d_mixn, "xattn_norm": d_xn, "mem_norm": d_memn, "ffn2_norm": d_ffn2n,
        "final_norm": d_final, "conv_b": d_conv_b, "conv_ln_g": d_lna_g, "conv_ln_b": d_lna_b,
        "sgu_ln_g": d_lnb_g, "sgu_ln_b": d_lnb_b, "b_in": d_b_in.reshape(6, D), "conv_w": d_conv_w[:CW],
        "sgu_w": d_sgu_w.reshape(64, D), "sgu_b": jnp.transpose(d_sgu_b[:, :NG]).reshape(1, NG * CHUNK),
        "loss": loss_lanes}
    small = small_allreduce(_pack_small(small_grads))
    loss = (0.5 / D) * jnp.sum(small[LOSS_ROW])
    gsmall = _unpack_small(small, wts, chip)

    out_g, out_d, out_m, out_v = dict(gsmall), {}, {}, {}
    sw, sm, sv = (_pack_small(_small_views(t))[:LOSS_ROW] for t in (wts, mom1, mom2))
    sg = _pack_small(_small_views({n: gsmall[n] for n in SMALL}))[:LOSS_ROW]
    for dst, packed in zip((out_d, out_m, out_v), adamw(sw, sg, sm, sv, "adamw_small")):
        dst.update(_unpack_small(packed, wts, None))
    for n in BIG:
        shape = wts[n].shape
        out_g[n] = gshard[n].reshape(shape)
        d, mn, vn = adamw(wts[n][0], gshard[n], mom1[n][0], mom2[n][0], "adamw_" + n)
        out_d[n], out_m[n], out_v[n] = d.reshape(shape), mn.reshape(shape), vn.reshape(shape)
    return (loss, dx[None], *[out_g[n] for n in names], *[out_d[n] for n in names],
            *[out_m[n] for n in names], *[out_v[n] for n in names])
```

```python
import functools
import math

import jax
import jax.numpy as jnp
from jax import lax
from jax.experimental import pallas as pl
from jax.experimental.pallas import tpu as pltpu

F32 = jnp.float32
BF = jnp.bfloat16
MESH = pl.DeviceIdType.MESH

D = 1024
FF = 2816
HC = FF // 2
NSH = 4
DIN = 6 * D
INB = DIN // NSH
CW = 31
HALO = 32
CHUNK = 128
NG = 4
GD = D // NG
NH = 4
HD = D // NH
NMEM = 256
EPS_RMS = 1e-6
EPS_LN = 1e-5
GELU_C0 = math.sqrt(2.0 / math.pi)
GELU_C1 = 0.044715
ATT_SCALE = 1.0 / math.sqrt(HD)

ADAM_LR = 0.001
ADAM_B1 = 0.9
ADAM_B2 = 0.999
ADAM_EPS = 1e-08
ADAM_WD = 0.01
ADAM_STEP = 10

VMEM_LIMIT = 56 * 1024 * 1024


def _cparams(sem=None, **kw):
    if sem is not None:
        kw["dimension_semantics"] = sem
    return pltpu.CompilerParams(vmem_limit_bytes=VMEM_LIMIT, **kw)


def _dot(a, b):
    return jnp.dot(a, b, preferred_element_type=F32)


def _dot_nt(a, b):
    return lax.dot_general(a, b, (((1,), (1,)), ((), ())), preferred_element_type=F32)


def _dot_tn(a, b):
    return lax.dot_general(a, b, (((0,), (0,)), ((), ())), preferred_element_type=F32)


def _sigmoid(x):
    return 1.0 / (1.0 + jnp.exp(-x))


def _gelu(x):
    t = jnp.tanh(GELU_C0 * (x + GELU_C1 * (x * x * x)))
    return 0.5 * x * (1.0 + t), t


def _gelu_grad(x, t):
    return 0.5 * (1.0 + t) + 0.5 * x * (1.0 - t * t) * (GELU_C0 * (1.0 + 3.0 * GELU_C1 * x * x))


def _mean(x):
    return jnp.mean(x, axis=-1, keepdims=True)


def _rms(x):
    r = lax.rsqrt(_mean(x * x) + EPS_RMS)
    return x * r, r


def _rms_bwd(dn, xh, r, g):
    dxh = dn * g
    return r * (dxh - xh * _mean(dxh * xh))


def _ln(x):
    xc = x - _mean(x)
    r = lax.rsqrt(_mean(xc * xc) + EPS_LN)
    return xc * r, r


def _ln_bwd(dy, xh, r, g):
    dxh = dy * g
    return r * (dxh - _mean(dxh) - xh * _mean(dxh * xh))


def _colsum(x):
    return jnp.sum(x, axis=0, keepdims=True)


def _const_spec(shape):
    nd = len(shape)
    return pl.BlockSpec(shape, lambda *_: (0,) * nd, pipeline_mode=pl.Buffered(1))


def _row_spec(ts, width):
    return pl.BlockSpec((ts, width), lambda i: (i, 0))


def _acc_spec(shape):
    nd = len(shape)
    return pl.BlockSpec(shape, lambda *_: (0,) * nd)


def _tile(s, want):
    return min(s, want)


HBM_SPEC = pl.BlockSpec(memory_space=pltpu.HBM)


class Rider:
    def __init__(self, ins, outs, aliases, nsem, start, finish):
        self.ins, self.outs, self.aliases, self.nsem = list(ins), list(outs), dict(aliases), nsem
        self.start, self.finish = start, finish


def _pcall(body, *, name, grid, args, in_specs, out_shape, out_specs, scratch=(), rider=None):
    sem = ("arbitrary",) * len(grid)
    n_in, n_out = len(args), len(out_shape)
    if rider is None:
        res = pl.pallas_call(
            body, grid=grid, name=name, out_shape=tuple(out_shape), in_specs=list(in_specs),
            out_specs=tuple(out_specs), scratch_shapes=list(scratch), compiler_params=_cparams(sem))(*args)
        return tuple(res), ()
    r_in, r_out = len(rider.ins), len(rider.outs)

    def wrapped(*refs):
        a, ri = refs[:n_in], refs[n_in:n_in + r_in]
        o = refs[n_in + r_in:n_in + r_in + n_out]
        ro = refs[n_in + r_in + n_out:n_in + r_in + n_out + r_out]
        s, (send, recv) = refs[n_in + r_in + n_out + r_out:-2], refs[-2:]
        first = functools.reduce(jnp.logical_and, [pl.program_id(d) == 0 for d in range(len(grid))])
        last = functools.reduce(jnp.logical_and, [pl.program_id(d) == g - 1 for d, g in enumerate(grid)])

        @pl.when(first)
        def _():
            rider.start(ri, ro, send, recv)

        body(*a, *o, *s)

        @pl.when(last)
        def _():
            rider.finish(ri, ro, send, recv)

    res = pl.pallas_call(
        wrapped, grid=grid, name=name, out_shape=tuple(out_shape) + tuple(rider.outs),
        in_specs=list(in_specs) + [HBM_SPEC] * r_in, out_specs=tuple(out_specs) + (HBM_SPEC,) * r_out,
        scratch_shapes=list(scratch) + [pltpu.SemaphoreType.DMA((rider.nsem,)),
                                        pltpu.SemaphoreType.DMA((rider.nsem,))],
        input_output_aliases={n_in + i: n_out + j for i, j in rider.aliases.items()},
        compiler_params=_cparams(sem, has_side_effects=True))(*args, *rider.ins)
    return tuple(res[:n_out]), tuple(res[n_out:])


def run_rider(rider, name):
    r_in = len(rider.ins)

    def body(*refs):
        ri, ro, (send, recv) = refs[:r_in], refs[r_in:-2], refs[-2:]
        rider.start(ri, ro, send, recv)
        rider.finish(ri, ro, send, recv)

    return pl.pallas_call(
        body, name=name, out_shape=tuple(rider.outs), in_specs=[HBM_SPEC] * r_in,
        out_specs=(HBM_SPEC,) * len(rider.outs),
        scratch_shapes=[pltpu.SemaphoreType.DMA((rider.nsem,)), pltpu.SemaphoreType.DMA((rider.nsem,))],
        input_output_aliases=rider.aliases,
        compiler_params=pltpu.CompilerParams(has_side_effects=True))(*rider.ins)


FFN_BWD_TILE = 256


def _ffn_apply(x, g_ref, wgu_ref, wd_ref, gu_ref):
    xh, _ = _rms(x)
    nb = (xh * g_ref[...]).astype(BF)
    acc = jnp.zeros(x.shape, F32)
    for j in range(2):
        g = _dot(nb, wgu_ref[j])
        u = _dot(nb, wgu_ref[2 + j])
        gu_ref[:, j * HC:(j + 1) * HC] = g.astype(BF)
        gu_ref[:, FF + j * HC:FF + (j + 1) * HC] = u.astype(BF)
        a = (g * _sigmoid(g) * u).astype(BF)
        acc = acc + _dot(a, wd_ref[j * HC:(j + 1) * HC, :])
    return x + 0.5 * acc


def ffn_fwd(h, gain, wgu, wd, rider=None):
    s = h.shape[0]
    ts = _tile(s, 512)

    def body(h_ref, g_ref, wgu_ref, wd_ref, o_ref, gu_ref):
        o_ref[...] = _ffn_apply(h_ref[...], g_ref, wgu_ref, wd_ref, gu_ref)

    return _pcall(
        body, grid=(s // ts,), name="ffn1_fwd", args=(h, gain, wgu, wd),
        out_shape=[jax.ShapeDtypeStruct((s, D), F32), jax.ShapeDtypeStruct((s, 2 * FF), BF)],
        in_specs=[_row_spec(ts, D), _const_spec((1, D)), _const_spec((NSH, D, HC)), _const_spec((FF, D))],
        out_specs=[_row_spec(ts, D), _row_spec(ts, 2 * FF)], rider=rider)


def ffn_fwd_loss(h, gain, wgu, wd, gfin, target):
    s = h.shape[0]
    ts = _tile(s, 512)

    def body(h_ref, g_ref, wgu_ref, wd_ref, gf_ref, t_ref, dh_ref, gu_ref, loss_ref, dgf_ref):
        @pl.when(pl.program_id(0) == 0)
        def _():
            loss_ref[...] = jnp.zeros_like(loss_ref)
            dgf_ref[...] = jnp.zeros_like(dgf_ref)

        h4 = _ffn_apply(h_ref[...], g_ref, wgu_ref, wd_ref, gu_ref)
        yh, r4 = _rms(h4)
        gf = gf_ref[...]
        e = yh * gf - t_ref[...]
        loss_ref[...] += _colsum(e * e)
        dy = e * (1.0 / D)
        dgf_ref[...] += _colsum(dy * yh)
        dh_ref[...] = _rms_bwd(dy, yh, r4, gf)

    return pl.pallas_call(
        body, grid=(s // ts,), name="ffn_fwd_loss",
        out_shape=(jax.ShapeDtypeStruct((s, D), F32), jax.ShapeDtypeStruct((s, 2 * FF), BF),
                   jax.ShapeDtypeStruct((1, D), F32), jax.ShapeDtypeStruct((1, D), F32)),
        in_specs=[_row_spec(ts, D), _const_spec((1, D)), _const_spec((NSH, D, HC)), _const_spec((FF, D)),
                  _const_spec((1, D)), _row_spec(ts, D)],
        out_specs=(_row_spec(ts, D), _row_spec(ts, 2 * FF), _acc_spec((1, D)), _acc_spec((1, D))),
        compiler_params=_cparams(("arbitrary",)),
    )(h, gain, wgu, wd, gfin, target)


def ffn_bwd(h, gu, dh, gain, wgu, wd, name):
    s = h.shape[0]
    ts = _tile(s, FFN_BWD_TILE)

    def body(h_ref, gu_ref, dh_ref, g_ref, wgu_ref, wd_ref, dx_ref, n_ref, a_ref, dgu_ref, dhb_ref, dg_ref):
        @pl.when(pl.program_id(0) == 0)
        def _():
            dg_ref[...] = jnp.zeros_like(dg_ref)

        x = h_ref[...]
        dh = dh_ref[...]
        gain_v = g_ref[...]
        xh, r = _rms(x)
        n_ref[...] = (xh * gain_v).astype(BF)
        dhb = (0.5 * dh).astype(BF)
        dhb_ref[...] = dhb
        dn = jnp.zeros((ts, D), F32)
        for j in range(2):
            g = gu_ref[:, j * HC:(j + 1) * HC].astype(F32)
            u = gu_ref[:, FF + j * HC:FF + (j + 1) * HC].astype(F32)
            sg = _sigmoid(g)
            sl = g * sg
            a_ref[:, j * HC:(j + 1) * HC] = (sl * u).astype(BF)
            da = _dot_nt(dhb, wd_ref[j * HC:(j + 1) * HC, :])
            dgb = (da * u * (sg * (1.0 + g * (1.0 - sg)))).astype(BF)
            dub = (da * sl).astype(BF)
            dgu_ref[:, j * HC:(j + 1) * HC] = dgb
            dgu_ref[:, FF + j * HC:FF + (j + 1) * HC] = dub
            dn = dn + _dot_nt(dgb, wgu_ref[j]) + _dot_nt(dub, wgu_ref[2 + j])
        dg_ref[...] += _colsum(dn * xh)
        dx_ref[...] = dh + _rms_bwd(dn, xh, r, gain_v)

    return pl.pallas_call(
        body, grid=(s // ts,), name=name,
        out_shape=(jax.ShapeDtypeStruct((s, D), F32), jax.ShapeDtypeStruct((s, D), BF),
                   jax.ShapeDtypeStruct((s, FF), BF), jax.ShapeDtypeStruct((s, 2 * FF), BF),
                   jax.ShapeDtypeStruct((s, D), BF), jax.ShapeDtypeStruct((1, D), F32)),
        in_specs=[_row_spec(ts, D), _row_spec(ts, 2 * FF), _row_spec(ts, D), _const_spec((1, D)),
                  _const_spec((NSH, D, HC)), _const_spec((FF, D))],
        out_specs=(_row_spec(ts, D), _row_spec(ts, D), _row_spec(ts, FF), _row_spec(ts, 2 * FF),
                   _row_spec(ts, D), _acc_spec((1, D))),
        compiler_params=_cparams(("arbitrary",)),
    )(h, gu, dh, gain, wgu, wd)


def dw_matmul(x, dy, nsplit, name, rider=None):
    s, k = x.shape
    n = dy.shape[1]
    nb = n // nsplit
    ts = _tile(s, 1024)

    def body(x_ref, dy_ref, o_ref):
        @pl.when(pl.program_id(1) == 0)
        def _():
            o_ref[...] = jnp.zeros_like(o_ref)

        o_ref[0] += _dot_tn(x_ref[...], dy_ref[...])

    (out,), rode = _pcall(
        body, grid=(nsplit, s // ts), name=name, args=(x, dy),
        out_shape=[jax.ShapeDtypeStruct((nsplit, k, nb), F32)],
        in_specs=[pl.BlockSpec((ts, k), lambda j, i: (i, 0)), pl.BlockSpec((ts, nb), lambda j, i: (i, j))],
        out_specs=[pl.BlockSpec((1, k, nb), lambda j, i: (j, 0, 0))], rider=rider)
    return (out, rode) if rider is not None else out


def _split_in_proj(p, b):
    h = INB - D
    a_val = p[0][:, :D] + b[:, 0:D]
    a_gate = jnp.concatenate([p[0][:, D:], p[1][:, :h]], axis=1) + b[:, D:2 * D]
    b_u = p[1][:, h:] + b[:, 2 * D:3 * D]
    b_v = p[2][:, :D] + b[:, 3 * D:4 * D]
    g_a = jnp.concatenate([p[2][:, D:], p[3][:, :h]], axis=1) + b[:, 4 * D:5 * D]
    g_b = p[3][:, h:] + b[:, 5 * D:6 * D]
    return a_val, a_gate, b_u, b_v, g_a, g_b


def _sgu_mix(vnb, ws_ref, sb_ref, mixed_ref, ts):
    for ci in range(ts // CHUNK):
        rows = slice(ci * CHUNK, (ci + 1) * CHUNK)
        for g in range(NG):
            cols = slice(g * GD, (g + 1) * GD)
            mixed_ref[rows, cols] = _dot(ws_ref[g], vnb[rows, cols]) + sb_ref[:, cols]


SUB = 8
CB = 128
SH_ROWS_EXTRA = HALO - SUB


def _shifted_copies(ext_ref, sh_ref, lanes, ts):
    for b in range(1, SUB):
        sh_ref[b - 1] = ext_ref[b:b + ts + SH_ROWS_EXTRA, lanes]


def _window(ext_ref, sh_ref, lanes, first, r0, nrows):
    b = first % SUB
    a = first - b
    if b == 0:
        return ext_ref[a + r0:a + r0 + nrows, lanes]
    return sh_ref[b - 1, a + r0:a + r0 + nrows, :]


def mix_fwd(h, gain, win, b_in, conv_w, conv_b, lna_g, lna_b, wa, lnb_g, lnb_b, ws, sbias, wb, wo, rider=None):
    s = h.shape[0]
    ts = _tile(s, 256)

    def body(h_ref, g_ref, win_ref, bin_ref, cw_ref, cb_ref, lag_ref, lab_ref, wa_ref, lbg_ref, lbb_ref,
             ws_ref, sb_ref, wb_ref, wo_ref, o_ref, p_ref, n_ref, c_ref, ext_ref, mixed_ref, sh_ref):
        @pl.when(pl.program_id(0) == 0)
        def _():
            ext_ref[0:HALO, :] = jnp.zeros((HALO, D), F32)

        x = h_ref[...]
        xh, _ = _rms(x)
        nb = (xh * g_ref[...]).astype(BF)
        n_ref[...] = nb
        b = bin_ref[...]
        p = []
        for k in range(NSH):
            pk = _dot(nb, win_ref[k])
            p_ref[:, k * INB:(k + 1) * INB] = (pk + b[:, k * INB:(k + 1) * INB]).astype(BF)
            p.append(pk)
        a_val, a_gate, b_u, b_v, g_a, g_b = _split_in_proj(p, b)
        ext_ref[HALO:HALO + ts, :] = a_val * _sigmoid(a_gate)
        for l0 in range(0, D, CB):
            lanes = slice(l0, l0 + CB)
            _shifted_copies(ext_ref, sh_ref, lanes, ts)
            for r0 in range(0, ts, CB):
                acc = jnp.zeros((CB, CB), F32) + cb_ref[:, lanes]
                for k in range(CW):
                    acc = acc + cw_ref[k:k + 1, lanes] * _window(ext_ref, sh_ref, lanes,
                                                                 HALO - (CW - 1) + k, r0, CB)
                c_ref[r0:r0 + CB, lanes] = acc
        ext_ref[0:HALO, :] = ext_ref[ts:ts + HALO, :]
        ch, _ = _ln(c_ref[...])
        la = ch * lag_ref[...] + lab_ref[...]
        sa = (la * _sigmoid(la)).astype(BF)
        ya = _dot(sa, wa_ref[...])
        ub, _ = _gelu(b_u)
        gv, _ = _gelu(b_v)
        vh, _ = _ln(gv)
        vnb = (vh * lbg_ref[...] + lbb_ref[...]).astype(BF)
        _sgu_mix(vnb, ws_ref, sb_ref, mixed_ref, ts)
        ob = (ub * mixed_ref[...]).astype(BF)
        yb = _dot(ob, wb_ref[...])
        merged = (_sigmoid(g_a) * ya + _sigmoid(g_b) * yb).astype(BF)
        o_ref[...] = x + _dot(merged, wo_ref[...])

    vec = _const_spec((1, D))
    sq = _const_spec((D, D))
    return _pcall(
        body, grid=(s // ts,), name="mix_fwd",
        args=(h, gain, win, b_in, conv_w, conv_b, lna_g, lna_b, wa, lnb_g, lnb_b, ws, sbias, wb, wo),
        out_shape=(jax.ShapeDtypeStruct((s, D), F32), jax.ShapeDtypeStruct((s, DIN), BF),
                   jax.ShapeDtypeStruct((s, D), BF), jax.ShapeDtypeStruct((s, D), F32)),
        in_specs=[_row_spec(ts, D), vec, _const_spec((NSH, D, INB)), _const_spec((1, DIN)),
                  _const_spec((HALO, D)), vec, vec, vec, sq, vec, vec,
                  _const_spec((NG, CHUNK, CHUNK)), _const_spec((CHUNK, D)), sq, sq],
        out_specs=(_row_spec(ts, D), _row_spec(ts, DIN), _row_spec(ts, D), _row_spec(ts, D)),
        scratch=[pltpu.VMEM((ts + HALO, D), F32), pltpu.VMEM((ts, D), F32),
                 pltpu.VMEM((SUB - 1, ts + SH_ROWS_EXTRA, CB), F32)], rider=rider)


def mix_bwd_branches(p, c, dh, lna_g, lna_b, wa, lnb_g, lnb_b, ws, wst, sbias, wb, wo, rider=None):
    s = dh.shape[0]
    ts = _tile(s, 256)
    nsteps = s // ts

    def body(p_ref, c_ref, dh_ref, lag_ref, lab_ref, wa_ref, lbg_ref, lbb_ref, ws_ref, wst_ref, sb_ref,
             wb_ref, wo_ref, dc_ref, dp_ref, sa_ref, dya_ref, ob_ref, dyb_ref, mg_ref, dhb_ref,
             dws_ref, dsb_ref, dlag_ref, dlab_ref, dlbg_ref, dlbb_ref, mixed_ref, dmix_ref, dvn_ref, dsb_acc):
        step = pl.program_id(0)

        @pl.when(step == 0)
        def _():
            for ref in (dws_ref, dsb_acc, dlag_ref, dlab_ref, dlbg_ref, dlbb_ref):
                ref[...] = jnp.zeros_like(ref)

        b_u = p_ref[:, 2 * D:3 * D].astype(F32)
        b_v = p_ref[:, 3 * D:4 * D].astype(F32)
        sga = _sigmoid(p_ref[:, 4 * D:5 * D].astype(F32))
        sgb = _sigmoid(p_ref[:, 5 * D:6 * D].astype(F32))
        lag = lag_ref[...]
        ch, ra = _ln(c_ref[...])
        la = ch * lag + lab_ref[...]
        sla = _sigmoid(la)
        sa = (la * sla).astype(BF)
        sa_ref[...] = sa
        ya = _dot(sa, wa_ref[...])
        lbg = lbg_ref[...]
        ub, tu = _gelu(b_u)
        gv, tv = _gelu(b_v)
        vh, rb = _ln(gv)
        vnb = (vh * lbg + lbb_ref[...]).astype(BF)
        _sgu_mix(vnb, ws_ref, sb_ref, mixed_ref, ts)
        mixed = mixed_ref[...]
        ob = (ub * mixed).astype(BF)
        ob_ref[...] = ob
        yb = _dot(ob, wb_ref[...])
        mg_ref[...] = (sga * ya + sgb * yb).astype(BF)
        dhb = dh_ref[...].astype(BF)
        dhb_ref[...] = dhb
        dm = _dot_nt(dhb, wo_ref[...])
        dp_ref[:, 0:2 * D] = jnp.zeros((ts, 2 * D), BF)
        dp_ref[:, 4 * D:5 * D] = (dm * ya * sga * (1.0 - sga)).astype(BF)
        dp_ref[:, 5 * D:6 * D] = (dm * yb * sgb * (1.0 - sgb)).astype(BF)
        dya = (dm * sga).astype(BF)
        dya_ref[...] = dya
        dyb = (dm * sgb).astype(BF)
        dyb_ref[...] = dyb
        dla = _dot_nt(dya, wa_ref[...]) * (sla * (1.0 + la * (1.0 - sla)))
        dlag_ref[...] += _colsum(dla * ch)
        dlab_ref[...] += _colsum(dla)
        dc_ref[...] = _ln_bwd(dla, ch, ra, lag)
        dob = _dot_nt(dyb, wb_ref[...])
        dp_ref[:, 2 * D:3 * D] = (dob * mixed * _gelu_grad(b_u, tu)).astype(BF)
        dmix = dob * ub
        dmix_ref[...] = dmix.astype(BF)
        dsb = jnp.zeros((CHUNK, D), F32)
        for ci in range(ts // CHUNK):
            rows = slice(ci * CHUNK, (ci + 1) * CHUNK)
            dsb = dsb + dmix[rows, :]
            for g in range(NG):
                cols = slice(g * GD, (g + 1) * GD)
                dmb = dmix_ref[rows, cols]
                dws_ref[g] += _dot_nt(dmb, vnb[rows, cols])
                dvn_ref[rows, cols] = _dot(wst_ref[g], dmb)
        dsb_acc[...] += dsb
        dvn = dvn_ref[...]
        dlbg_ref[...] += _colsum(dvn * vh)
        dlbb_ref[...] += _colsum(dvn)
        dp_ref[:, 3 * D:4 * D] = (_ln_bwd(dvn, vh, rb, lbg) * _gelu_grad(b_v, tv)).astype(BF)

        @pl.when(step == nsteps - 1)
        def _():
            row = lax.broadcasted_iota(jnp.int32, (CHUNK, CHUNK), 0)
            col = lax.broadcasted_iota(jnp.int32, (CHUNK, CHUNK), 1)
            for g in range(NG):
                dws_ref[g] = jnp.where(col <= row, dws_ref[g], 0.0)
            acc = jnp.zeros((CHUNK, CHUNK), F32)
            for g in range(NG):
                tot = jnp.sum(dsb_acc[:, g * GD:(g + 1) * GD], axis=-1, keepdims=True)
                acc = acc + jnp.where(col == g, tot, 0.0)
            dsb_ref[...] = acc

    vec = _const_spec((1, D))
    sq = _const_spec((D, D))
    bf_rows = jax.ShapeDtypeStruct((s, D), BF)
    acc_vec = jax.ShapeDtypeStruct((1, D), F32)
    return _pcall(
        body, grid=(nsteps,), name="mix_bwd_branches",
        args=(p, c, dh, lna_g, lna_b, wa, lnb_g, lnb_b, ws, wst, sbias, wb, wo),
        out_shape=(jax.ShapeDtypeStruct((s, D), F32), jax.ShapeDtypeStruct((s, DIN), BF),
                   bf_rows, bf_rows, bf_rows, bf_rows, bf_rows, bf_rows,
                   jax.ShapeDtypeStruct((NG, CHUNK, CHUNK), F32), jax.ShapeDtypeStruct((CHUNK, CHUNK), F32),
                   acc_vec, acc_vec, acc_vec, acc_vec),
        in_specs=[_row_spec(ts, DIN), _row_spec(ts, D), _row_spec(ts, D), vec, vec, sq, vec, vec,
                  _const_spec((NG, CHUNK, CHUNK)), _const_spec((NG, CHUNK, CHUNK)), _const_spec((CHUNK, D)),
                  sq, sq],
        out_specs=(_row_spec(ts, D), _row_spec(ts, DIN)) + (_row_spec(ts, D),) * 6
        + (_acc_spec((NG, CHUNK, CHUNK)), _acc_spec((CHUNK, CHUNK))) + (_acc_spec((1, D)),) * 4,
        scratch=[pltpu.VMEM((ts, D), F32), pltpu.VMEM((ts, D), BF), pltpu.VMEM((ts, D), F32),
                 pltpu.VMEM((CHUNK, D), F32)], rider=rider)


def conv_bwd(p, dc, dp, conv_w):
    s = dc.shape[0]
    ts = _tile(s, 256)
    nsteps = s // ts
    per = ts // HALO

    rb = 64

    def body(pm_ref, pp_ref, dcm_ref, dcn_ref, cw_ref, dpin_ref, dp_ref, dw_ref, db_ref, ext_ref, dext_ref,
             dw8_ref, sh_ref, dsh_ref, dglu_ref):
        del dpin_ref
        step = pl.program_id(0)

        @pl.when(step == 0)
        def _():
            dw8_ref[...] = jnp.zeros_like(dw8_ref)
            db_ref[...] = jnp.zeros_like(db_ref)

        a_val = pm_ref[:, 0:D].astype(F32)
        sg = _sigmoid(pm_ref[:, D:2 * D].astype(F32))
        prev = pp_ref[:, 0:D].astype(F32) * _sigmoid(pp_ref[:, D:2 * D].astype(F32))
        ext_ref[0:HALO, :] = jnp.where(step > 0, prev, 0.0)
        ext_ref[HALO:HALO + ts, :] = a_val * sg
        dcm = dcm_ref[...]
        dext_ref[0:ts, :] = dcm
        dext_ref[ts:ts + HALO, :] = jnp.where(step < nsteps - 1, dcn_ref[...], 0.0)
        db_ref[...] += _colsum(dcm)
        for l0 in range(0, D, CB):
            lanes = slice(l0, l0 + CB)
            _shifted_copies(dext_ref, dsh_ref, lanes, ts)
            for r0 in range(0, ts, CB):
                acc = jnp.zeros((CB, CB), F32)
                for k in range(CW):
                    acc = acc + cw_ref[k:k + 1, lanes] * _window(dext_ref, dsh_ref, lanes, CW - 1 - k, r0, CB)
                dglu_ref[r0:r0 + CB, lanes] = acc
            _shifted_copies(ext_ref, sh_ref, lanes, ts)
            accs = [jnp.zeros((SUB, CB), F32) for _ in range(CW)]
            for r0 in range(0, ts, rb):
                dcb = dext_ref[r0:r0 + rb, lanes]
                for k in range(CW):
                    prod = dcb * _window(ext_ref, sh_ref, lanes, HALO - (CW - 1) + k, r0, rb)
                    accs[k] = accs[k] + jnp.sum(prod.reshape(rb // SUB, SUB, CB), axis=0)
            for k in range(CW):
                dw8_ref[k, :, lanes] += accs[k]
        dglu = dglu_ref[...]
        dp_ref[:, 0:D] = (dglu * sg).astype(BF)
        dp_ref[:, D:2 * D] = (dglu * a_val * sg * (1.0 - sg)).astype(BF)

        @pl.when(step == nsteps - 1)
        def _():
            dw_ref[...] = jnp.zeros_like(dw_ref)
            for k in range(CW):
                dw_ref[k:k + 1, :] = _colsum(dw8_ref[k])

    return pl.pallas_call(
        body, grid=(nsteps,), name="conv_bwd",
        out_shape=(jax.ShapeDtypeStruct((s, DIN), BF), jax.ShapeDtypeStruct((HALO, D), F32),
                   jax.ShapeDtypeStruct((1, D), F32)),
        in_specs=[pl.BlockSpec((ts, 2 * D), lambda i: (i, 0)),
                  pl.BlockSpec((HALO, 2 * D), lambda i: (jnp.maximum(i * per - 1, 0), 0)),
                  _row_spec(ts, D),
                  pl.BlockSpec((HALO, D), lambda i: (jnp.minimum((i + 1) * per, s // HALO - 1), 0)),
                  _const_spec((HALO, D)),
                  pl.BlockSpec(memory_space=pl.ANY)],
        out_specs=(pl.BlockSpec((ts, 2 * D), lambda i: (i, 0)), _acc_spec((HALO, D)), _acc_spec((1, D))),
        scratch_shapes=[pltpu.VMEM((ts + HALO, D), F32), pltpu.VMEM((ts + HALO, D), F32),
                        pltpu.VMEM((HALO, SUB, D), F32),
                        pltpu.VMEM((SUB - 1, ts + SH_ROWS_EXTRA, CB), F32),
                        pltpu.VMEM((SUB - 1, ts + SH_ROWS_EXTRA, CB), F32),
                        pltpu.VMEM((ts, D), F32)],
        input_output_aliases={5: 0},
        compiler_params=_cparams(("arbitrary",)),
    )(p, p, dc, dc, conv_w, dp)


def mix_bwd_in(dp, h, dh, gain, win):
    s = h.shape[0]
    ts = _tile(s, 512)

    def body(dp_ref, h_ref, dh_ref, g_ref, win_ref, dx_ref, dg_ref, db_ref):
        @pl.when(pl.program_id(0) == 0)
        def _():
            dg_ref[...] = jnp.zeros_like(dg_ref)
            db_ref[...] = jnp.zeros_like(db_ref)

        gain_v = g_ref[...]
        xh, r = _rms(h_ref[...])
        dn = jnp.zeros((ts, D), F32)
        for k in range(NSH):
            dpk = dp_ref[:, k * INB:(k + 1) * INB]
            dn = dn + _dot_nt(dpk, win_ref[k])
            db_ref[:, k * INB:(k + 1) * INB] += _colsum(dpk.astype(F32))
        dg_ref[...] += _colsum(dn * xh)
        dx_ref[...] = dh_ref[...] + _rms_bwd(dn, xh, r, gain_v)

    return pl.pallas_call(
        body, grid=(s // ts,), name="mix_bwd_in",
        out_shape=(jax.ShapeDtypeStruct((s, D), F32), jax.ShapeDtypeStruct((1, D), F32),
                   jax.ShapeDtypeStruct((1, DIN), F32)),
        in_specs=[_row_spec(ts, DIN), _row_spec(ts, D), _row_spec(ts, D), _const_spec((1, D)),
                  _const_spec((NSH, D, INB))],
        out_specs=(_row_spec(ts, D), _acc_spec((1, D)), _acc_spec((1, DIN))),
        compiler_params=_cparams(("arbitrary",)),
    )(dp, h, dh, gain, win)


def kv_proj(mem, gain, wkv):
    def body(m_ref, g_ref, w_ref, k_ref, v_ref, n_ref):
        xh, _ = _rms(m_ref[...])
        nb = (xh * g_ref[...]).astype(BF)
        n_ref[...] = nb
        half = D // 2
        for j in range(2):
            k_ref[:, j * half:(j + 1) * half] = _dot(nb, w_ref[j]).astype(BF)
            v_ref[:, j * half:(j + 1) * half] = _dot(nb, w_ref[2 + j]).astype(BF)

    o = jax.ShapeDtypeStruct((NMEM, D), BF)
    return pl.pallas_call(body, name="kv_proj", out_shape=(o, o, o), compiler_params=_cparams())(mem, gain, wkv)


def kv_bwd(mem, gain, memn, wkv, dk, dv):
    def body(m_ref, g_ref, n_ref, w_ref, dk_ref, dv_ref, dw_ref, dg_ref):
        xh, _ = _rms(m_ref[...])
        nb = n_ref[...]
        half = D // 2
        dn = jnp.zeros((NMEM, D), F32)
        for j in range(2):
            dkb = dk_ref[:, j * half:(j + 1) * half].astype(BF)
            dvb = dv_ref[:, j * half:(j + 1) * half].astype(BF)
            dw_ref[j] = _dot_tn(nb, dkb)
            dw_ref[2 + j] = _dot_tn(nb, dvb)
            dn = dn + _dot_nt(dkb, w_ref[j]) + _dot_nt(dvb, w_ref[2 + j])
        dg_ref[...] = _colsum(dn * xh)

    return pl.pallas_call(
        body, name="kv_bwd",
        out_shape=(jax.ShapeDtypeStruct((NSH, D, D // 2), F32), jax.ShapeDtypeStruct((1, D), F32)),
        compiler_params=_cparams())(mem, gain, memn, wkv, dk, dv)


def _attend(qb, k_ref, v_ref, h):
    cols = slice(h * HD, (h + 1) * HD)
    sc = _dot_nt(qb[:, cols], k_ref[:, cols]) * ATT_SCALE
    e = jnp.exp(sc - jnp.max(sc, axis=-1, keepdims=True))
    pr = e / jnp.sum(e, axis=-1, keepdims=True)
    return pr, _dot(pr.astype(BF), v_ref[:, cols])


def xattn_fwd(h, gain, wq, k, v, wo):
    s = h.shape[0]
    ts = _tile(s, 512)

    def body(h_ref, g_ref, wq_ref, k_ref, v_ref, wo_ref, o_ref, att_ref):
        x = h_ref[...]
        xh, _ = _rms(x)
        nb = (xh * g_ref[...]).astype(BF)
        qb = _dot(nb, wq_ref[...]).astype(BF)
        for hd in range(NH):
            _, oh = _attend(qb, k_ref, v_ref, hd)
            att_ref[:, hd * HD:(hd + 1) * HD] = oh.astype(BF)
        o_ref[...] = x + _dot(att_ref[...], wo_ref[...])

    sq = _const_spec((D, D))
    kvs = _const_spec((NMEM, D))
    return pl.pallas_call(
        body, grid=(s // ts,), name="xattn_fwd",
        out_shape=jax.ShapeDtypeStruct((s, D), F32),
        in_specs=[_row_spec(ts, D), _const_spec((1, D)), sq, kvs, kvs, sq],
        out_specs=_row_spec(ts, D),
        scratch_shapes=[pltpu.VMEM((ts, D), BF)],
        compiler_params=_cparams(("arbitrary",)),
    )(h, gain, wq, k, v, wo)


def xattn_bwd(h, dh, gain, wq, k, v, wo, rider=None):
    s = h.shape[0]
    ts = _tile(s, 512)

    def body(h_ref, dh_ref, g_ref, wq_ref, k_ref, v_ref, wo_ref,
             dx_ref, n_ref, dq_ref, att_ref, dhb_ref, dk_ref, dv_ref, dg_ref):
        @pl.when(pl.program_id(0) == 0)
        def _():
            for ref in (dk_ref, dv_ref, dg_ref):
                ref[...] = jnp.zeros_like(ref)

        x = h_ref[...]
        dh = dh_ref[...]
        gain_v = g_ref[...]
        xh, r = _rms(x)
        nb = (xh * gain_v).astype(BF)
        n_ref[...] = nb
        qb = _dot(nb, wq_ref[...]).astype(BF)
        dhb = dh.astype(BF)
        dhb_ref[...] = dhb
        dob = _dot_nt(dhb, wo_ref[...]).astype(BF)
        for hd in range(NH):
            cols = slice(hd * HD, (hd + 1) * HD)
            pr, oh = _attend(qb, k_ref, v_ref, hd)
            att_ref[:, cols] = oh.astype(BF)
            doh = dob[:, cols]
            dpr = _dot_nt(doh, v_ref[:, cols])
            dv_ref[:, cols] += _dot_tn(pr.astype(BF), doh)
            dsc = (pr * (dpr - jnp.sum(dpr * pr, axis=-1, keepdims=True)) * ATT_SCALE).astype(BF)
            dq_ref[:, cols] = _dot(dsc, k_ref[:, cols]).astype(BF)
            dk_ref[:, cols] += _dot_tn(dsc, qb[:, cols])
        dn = _dot_nt(dq_ref[...], wq_ref[...])
        dg_ref[...] += _colsum(dn * xh)
        dx_ref[...] = dh + _rms_bwd(dn, xh, r, gain_v)

    sq = _const_spec((D, D))
    kvs = _const_spec((NMEM, D))
    bf_rows = jax.ShapeDtypeStruct((s, D), BF)
    kv_acc = jax.ShapeDtypeStruct((NMEM, D), F32)
    return _pcall(
        body, grid=(s // ts,), name="xattn_bwd", args=(h, dh, gain, wq, k, v, wo),
        out_shape=(jax.ShapeDtypeStruct((s, D), F32), bf_rows, bf_rows, bf_rows, bf_rows, kv_acc, kv_acc,
                   jax.ShapeDtypeStruct((1, D), F32)),
        in_specs=[_row_spec(ts, D), _row_spec(ts, D), _const_spec((1, D)), sq, kvs, kvs, sq],
        out_specs=(_row_spec(ts, D),) * 5 + (_acc_spec((NMEM, D)), _acc_spec((NMEM, D)), _acc_spec((1, D))),
        rider=rider)


BLOCK_BYTES = 3 << 19


def _row_block(rows, cols):
    rb = rows
    while rb * cols * 4 > BLOCK_BYTES and rb % 32 == 0:
        rb //= 2
    return rb


def cast_bf16(w, chip, name):
    r, c = w.shape
    rb = _row_block(r, c)

    def body(chip_ref, w_ref, o_ref):
        del chip_ref
        o_ref[0] = w_ref[...].astype(BF)

    return pl.pallas_call(
        body, name=name, out_shape=jax.ShapeDtypeStruct((NSH, r, c), BF),
        grid_spec=pltpu.PrefetchScalarGridSpec(
            num_scalar_prefetch=1, grid=(r // rb,),
            in_specs=[pl.BlockSpec((rb, c), lambda i, chip_ref: (i, 0))],
            out_specs=pl.BlockSpec((1, rb, c), lambda i, chip_ref: (chip_ref[0], i, 0))),
        compiler_params=_cparams(("arbitrary",)))(chip, w)


def pair_add(g4, recv, core, name):
    nsh, _, rh, c = g4.shape
    rb = _row_block(rh, c)

    def body(core_ref, g_ref, r_ref, o_ref, ob_ref):
        del core_ref
        sm = g_ref[0, 0] + r_ref[0]
        o_ref[0] = sm
        ob_ref[0] = sm.astype(BF)

    spec3 = pl.BlockSpec((1, rb, c), lambda k, i, core_ref: (k, i, 0))
    return pl.pallas_call(
        body, name=name,
        out_shape=(jax.ShapeDtypeStruct((nsh, rh, c), F32), jax.ShapeDtypeStruct((nsh, rh, c), BF)),
        grid_spec=pltpu.PrefetchScalarGridSpec(
            num_scalar_prefetch=1, grid=(nsh, rh // rb),
            in_specs=[pl.BlockSpec((1, 1, rb, c), lambda k, i, core_ref: (k, core_ref[0], i, 0)), spec3],
            out_specs=(spec3, spec3)),
        compiler_params=_cparams(("arbitrary", "arbitrary")))(core, g4, recv)


def chip_sum(psum, recv, place, name):
    _, rh, c = psum.shape
    rb = _row_block(rh, c)

    def body(place_ref, p_ref, r_ref, o_ref):
        del place_ref
        acc = p_ref[0]
        for j in range(NSH - 1):
            acc = acc + r_ref[j].astype(F32)
        o_ref[0] = acc

    return pl.pallas_call(
        body, name=name, out_shape=jax.ShapeDtypeStruct((2, rh, c), F32),
        grid_spec=pltpu.PrefetchScalarGridSpec(
            num_scalar_prefetch=1, grid=(rh // rb,),
            in_specs=[pl.BlockSpec((1, rb, c), lambda i, place_ref: (place_ref[0], i, 0)),
                      pl.BlockSpec((NSH - 1, rb, c), lambda i, place_ref: (0, i, 0))],
            out_specs=pl.BlockSpec((1, rb, c), lambda i, place_ref: (place_ref[1], i, 0))),
        compiler_params=_cparams(("arbitrary",)))(place, psum, recv)


def _adamw_math(w, g, m, v):
    m = ADAM_B1 * m + (1.0 - ADAM_B1) * g
    v = ADAM_B2 * v + (1.0 - ADAM_B2) * (g * g)
    m_hat = m / (1.0 - ADAM_B1 ** ADAM_STEP)
    v_hat = v / (1.0 - ADAM_B2 ** ADAM_STEP)
    delta = -ADAM_LR * (m_hat / (jnp.sqrt(v_hat) + ADAM_EPS) + ADAM_WD * w)
    return delta, m, v


def adamw(w, g, m, v, name):
    r, c = w.shape
    rb = _row_block(r, c)

    def body(w_ref, g_ref, m_ref, v_ref, d_ref, mo_ref, vo_ref):
        d, mn, vn = _adamw_math(w_ref[...], g_ref[...], m_ref[...], v_ref[...])
        d_ref[...] = d
        mo_ref[...] = mn
        vo_ref[...] = vn

    o = jax.ShapeDtypeStruct((r, c), F32)
    spec = _row_spec(rb, c)
    return pl.pallas_call(
        body, grid=(r // rb,), name=name, out_shape=(o, o, o),
        in_specs=[spec] * 4, out_specs=(spec,) * 3,
        compiler_params=_cparams(("arbitrary",)))(w, g, m, v)


def _place():
    return lax.axis_index("x"), lax.axis_index("y"), lax.axis_index("c")


def _other_chips(x, y):
    return [(1 - x, y), (x, 1 - y), (1 - x, 1 - y)]


NOTHER = NSH - 1


def gather_rider(arrays):
    nw = len(arrays)
    nici = nw * NOTHER

    def copies(refs, send_sems, recv_sems):
        x, y, c = _place()
        ici, d2d = [], []
        for w in range(nw):
            for j, (px, py) in enumerate(_other_chips(x, y)):
                n = w * NOTHER + j
                sems = dict(send_sem=send_sems.at[n], recv_sem=recv_sems.at[n],
                            device_id=(px, py, c), device_id_type=MESH)
                mine = refs[w].at[2 * x + y, c]
                theirs = refs[w].at[2 * px + py, c]
                ici.append((pltpu.make_async_remote_copy(src_ref=mine, dst_ref=mine, **sems),
                            pltpu.make_async_remote_copy(src_ref=mine, dst_ref=theirs, **sems)))
                sems = dict(send_sem=send_sems.at[nici + n], recv_sem=recv_sems.at[nici + n],
                            device_id=(x, y, 1 - c), device_id_type=MESH)
                d2d.append((pltpu.make_async_remote_copy(src_ref=theirs, dst_ref=theirs, **sems),
                            pltpu.make_async_remote_copy(src_ref=theirs, dst_ref=refs[w].at[2 * px + py, 1 - c],
                                                         **sems)))
        return ici, d2d

    def start(ins, outs, send_sems, recv_sems):
        ici, _ = copies(outs, send_sems, recv_sems)
        for send, _ in ici:
            send.start()

    def finish(ins, outs, send_sems, recv_sems):
        ici, d2d = copies(outs, send_sems, recv_sems)
        for (_, landed), (forward, _) in zip(ici, d2d):
            landed.wait_recv()
            forward.start()
        for _, landed in d2d:
            landed.wait_recv()
        for send, _ in ici + d2d:
            send.wait_send()

    return Rider(arrays, [jax.ShapeDtypeStruct(a.shape, a.dtype) for a in arrays], {i: i for i in range(nw)},
                 2 * nici, start, finish)


def exchange_rider(psums):
    nw = len(psums)

    def copies(ins, outs, send_sems, recv_sems):
        x, y, c = _place()
        return [pltpu.make_async_remote_copy(
            src_ref=ins[w].at[2 * px + py], dst_ref=outs[w].at[j],
            send_sem=send_sems.at[w * NOTHER + j], recv_sem=recv_sems.at[w * NOTHER + j],
            device_id=(px, py, c), device_id_type=MESH)
            for w in range(nw) for j, (px, py) in enumerate(_other_chips(x, y))]

    def start(ins, outs, send_sems, recv_sems):
        for cp in copies(ins, outs, send_sems, recv_sems):
            cp.start()

    def finish(ins, outs, send_sems, recv_sems):
        for cp in copies(ins, outs, send_sems, recv_sems):
            cp.wait()

    return Rider(psums, [jax.ShapeDtypeStruct((NOTHER,) + p.shape[1:], p.dtype) for p in psums], {},
                 nw * NOTHER, start, finish)


def pair_rider(grads):
    nw = len(grads)

    def copies(ins, outs, send_sems, recv_sems):
        x, y, c = _place()
        return [pltpu.make_async_remote_copy(
            src_ref=ins[w].at[:, 1 - c], dst_ref=outs[w], send_sem=send_sems.at[w], recv_sem=recv_sems.at[w],
            device_id=(x, y, 1 - c), device_id_type=MESH) for w in range(nw)]

    def start(ins, outs, send_sems, recv_sems):
        for cp in copies(ins, outs, send_sems, recv_sems):
            cp.start()

    def finish(ins, outs, send_sems, recv_sems):
        for cp in copies(ins, outs, send_sems, recv_sems):
            cp.wait()

    return Rider(grads, [jax.ShapeDtypeStruct((g.shape[0],) + g.shape[2:], g.dtype) for g in grads], {},
                 nw, start, finish)


class _Offset:
    def __init__(self, ref, base):
        self.ref, self.base = ref, base

    @property
    def at(self):
        return self

    def __getitem__(self, i):
        return self.ref.at[self.base + i]


def merge_riders(riders):
    ins, outs, aliases, spans, nsem = [], [], {}, [], 0
    for r in riders:
        spans.append((len(ins), len(outs), nsem))
        aliases.update({len(ins) + i: len(outs) + j for i, j in r.aliases.items()})
        ins, outs, nsem = ins + r.ins, outs + r.outs, nsem + r.nsem

    def each(step):
        def run(in_refs, out_refs, send_sems, recv_sems):
            for r, (i0, o0, s0) in zip(riders, spans):
                getattr(r, step)(in_refs[i0:i0 + len(r.ins)], out_refs[o0:o0 + len(r.outs)],
                                 _Offset(send_sems, s0), _Offset(recv_sems, s0))
        return run

    return Rider(ins, outs, aliases, nsem, each("start"), each("finish"))


def split_results(riders, results):
    out, o0 = [], 0
    for r in riders:
        out.append(tuple(results[o0:o0 + len(r.outs)]))
        o0 += len(r.outs)
    return out


def swap_rider(halves):
    nw = len(halves)

    def copies(refs, send_sems, recv_sems):
        x, y, c = _place()
        out = []
        for w in range(nw):
            sems = dict(send_sem=send_sems.at[w], recv_sem=recv_sems.at[w],
                        device_id=(x, y, 1 - c), device_id_type=MESH)
            mine = refs[w].at[c]
            out.append((pltpu.make_async_remote_copy(src_ref=mine, dst_ref=mine, **sems),
                        pltpu.make_async_remote_copy(src_ref=mine, dst_ref=refs[w].at[1 - c], **sems)))
        return out

    def start(ins, outs, send_sems, recv_sems):
        for send, _ in copies(outs, send_sems, recv_sems):
            send.start()

    def finish(ins, outs, send_sems, recv_sems):
        cps = copies(outs, send_sems, recv_sems)
        for _, recv in cps:
            recv.wait_recv()
        for send, _ in cps:
            send.wait_send()

    return Rider(halves, [jax.ShapeDtypeStruct(h.shape, h.dtype) for h in halves], {i: i for i in range(nw)},
                 nw, start, finish)


NDEV = 8


def allgather_rider(slots):
    def copies(ref, send_sems, recv_sems):
        x, y, c = _place()
        mine = ref.at[4 * x + 2 * y + c]
        out = []
        for rel in range(1, NDEV):
            peer = (x ^ (rel >> 2), y ^ ((rel >> 1) & 1), c ^ (rel & 1))
            sems = dict(send_sem=send_sems.at[rel - 1], recv_sem=recv_sems.at[rel - 1],
                        device_id=peer, device_id_type=MESH)
            out.append((pltpu.make_async_remote_copy(src_ref=mine, dst_ref=mine, **sems),
                        pltpu.make_async_remote_copy(
                            src_ref=mine, dst_ref=ref.at[4 * peer[0] + 2 * peer[1] + peer[2]], **sems)))
        return out

    def start(ins, outs, send_sems, recv_sems):
        for send, _ in copies(outs[0], send_sems, recv_sems):
            send.start()

    def finish(ins, outs, send_sems, recv_sems):
        cps = copies(outs[0], send_sems, recv_sems)
        for _, recv in cps:
            recv.wait_recv()
        for send, _ in cps:
            send.wait_send()

    return Rider([slots], [jax.ShapeDtypeStruct(slots.shape, slots.dtype)], {0: 0}, NDEV - 1, start, finish)


def sum_slots(slots):
    def body(s_ref, o_ref):
        acc = s_ref[0]
        for dev in range(1, NDEV):
            acc = acc + s_ref[dev]
        o_ref[...] = acc

    return pl.pallas_call(body, name="sum_slots", out_shape=jax.ShapeDtypeStruct(slots.shape[1:], F32),
                          compiler_params=_cparams())(slots)


BIG = ("ffn1_w_gu", "ffn1_w_down", "w_in", "w_a_out", "w_b_out", "w_out", "w_q", "w_kv", "w_o",
       "ffn2_w_gu", "ffn2_w_down")
SMALL = {"ffn1_norm": (0, 1), "mix_norm": (8, 1), "xattn_norm": (16, 1), "mem_norm": (24, 1),
         "ffn2_norm": (32, 1), "final_norm": (40, 1), "conv_b": (48, 1), "conv_ln_g": (56, 1),
         "conv_ln_b": (64, 1), "sgu_ln_g": (72, 1), "sgu_ln_b": (80, 1), "b_in": (88, 6),
         "conv_w": (96, CW), "sgu_w": (128, 64), "sgu_b": (192, 1)}
LOSS_ROW = 200
SMALL_ROWS = 208


def _pad_rows(a, rows):
    return jnp.pad(a, ((0, rows - a.shape[0]), (0, D - a.shape[1])))


def _pack_small(parts):
    names = sorted(parts, key=lambda n: SMALL[n][0] if n in SMALL else LOSS_ROW)
    rows = []
    for i, n in enumerate(names):
        start = SMALL[n][0] if n in SMALL else LOSS_ROW
        end = SMALL_ROWS if i + 1 == len(names) else (SMALL[names[i + 1]][0] if names[i + 1] in SMALL else LOSS_ROW)
        rows.append(_pad_rows(parts[n], end - start))
    return jnp.concatenate(rows, axis=0)


def _small_views(w):
    return {
        "ffn1_norm": w["ffn1_norm"], "mix_norm": w["mix_norm"], "xattn_norm": w["xattn_norm"],
        "mem_norm": w["mem_norm"], "ffn2_norm": w["ffn2_norm"], "final_norm": w["final_norm"].reshape(1, D),
        "conv_b": w["conv_b"], "conv_ln_g": w["conv_ln_g"], "conv_ln_b": w["conv_ln_b"],
        "sgu_ln_g": w["sgu_ln_g"], "sgu_ln_b": w["sgu_ln_b"], "b_in": w["b_in"].reshape(6, D),
        "conv_w": w["conv_w"][0], "sgu_w": w["sgu_w"].reshape(64, D), "sgu_b": w["sgu_b"].reshape(1, NG * CHUNK),
    }


def _unpack_small(buf, like, chip):
    out = {}
    for n, (start, rows) in SMALL.items():
        blk = buf[start:start + rows]
        if n == "conv_w":
            blk = blk[:, :like[n].shape[-1]] if chip is None else lax.dynamic_slice_in_dim(
                blk, chip * like[n].shape[-1], like[n].shape[-1], axis=1)
        elif n == "sgu_b":
            blk = blk[:, :NG * CHUNK]
        out[n] = blk.reshape(like[n].shape)
    return out


def kernel(x, mem, ffn1_norm, ffn1_w_gu, ffn1_w_down, mix_norm, w_in, b_in, conv_w, conv_b, conv_ln_g, conv_ln_b, w_a_out, sgu_ln_g, sgu_ln_b, sgu_w, sgu_b, w_b_out, w_out, xattn_norm, mem_norm, w_q, w_kv, w_o, ffn2_norm, ffn2_w_gu, ffn2_w_down, final_norm, loss_target, m_ffn1_norm, m_ffn1_w_gu, m_ffn1_w_down, m_mix_norm, m_w_in, m_b_in, m_conv_w, m_conv_b, m_conv_ln_g, m_conv_ln_b, m_w_a_out, m_sgu_ln_g, m_sgu_ln_b, m_sgu_w, m_sgu_b, m_w_b_out, m_w_out, m_xattn_norm, m_mem_norm, m_w_q, m_w_kv, m_w_o, m_ffn2_norm, m_ffn2_w_gu, m_ffn2_w_down, m_final_norm, v_ffn1_norm, v_ffn1_w_gu, v_ffn1_w_down, v_mix_norm, v_w_in, v_b_in, v_conv_w, v_conv_b, v_conv_ln_g, v_conv_ln_b, v_w_a_out, v_sgu_ln_g, v_sgu_ln_b, v_sgu_w, v_sgu_b, v_w_b_out, v_w_out, v_xattn_norm, v_mem_norm, v_w_q, v_w_kv, v_w_o, v_ffn2_norm, v_ffn2_w_gu, v_ffn2_w_down, v_final_norm):
    names = ("ffn1_norm", "ffn1_w_gu", "ffn1_w_down", "mix_norm", "w_in", "b_in", "conv_w", "conv_b",
             "conv_ln_g", "conv_ln_b", "w_a_out", "sgu_ln_g", "sgu_ln_b", "sgu_w", "sgu_b", "w_b_out", "w_out",
             "xattn_norm", "mem_norm", "w_q", "w_kv", "w_o", "ffn2_norm", "ffn2_w_gu", "ffn2_w_down",
             "final_norm")
    wts = dict(zip(names, (ffn1_norm, ffn1_w_gu, ffn1_w_down, mix_norm, w_in, b_in, conv_w, conv_b, conv_ln_g,
                           conv_ln_b, w_a_out, sgu_ln_g, sgu_ln_b, sgu_w, sgu_b, w_b_out, w_out, xattn_norm,
                           mem_norm, w_q, w_kv, w_o, ffn2_norm, ffn2_w_gu, ffn2_w_down, final_norm)))
    mom1 = dict(zip(names, (m_ffn1_norm, m_ffn1_w_gu, m_ffn1_w_down, m_mix_norm, m_w_in, m_b_in, m_conv_w,
                            m_conv_b, m_conv_ln_g, m_conv_ln_b, m_w_a_out, m_sgu_ln_g, m_sgu_ln_b, m_sgu_w,
                            m_sgu_b, m_w_b_out, m_w_out, m_xattn_norm, m_mem_norm, m_w_q, m_w_kv, m_w_o,
                            m_ffn2_norm, m_ffn2_w_gu, m_ffn2_w_down, m_final_norm)))
    mom2 = dict(zip(names, (v_ffn1_norm, v_ffn1_w_gu, v_ffn1_w_down, v_mix_norm, v_w_in, v_b_in, v_conv_w,
                            v_conv_b, v_conv_ln_g, v_conv_ln_b, v_w_a_out, v_sgu_ln_g, v_sgu_ln_b, v_sgu_w,
                            v_sgu_b, v_w_b_out, v_w_out, v_xattn_norm, v_mem_norm, v_w_q, v_w_kv, v_w_o,
                            v_ffn2_norm, v_ffn2_w_gu, v_ffn2_w_down, v_final_norm)))
    xi, yi, ci = _place()
    chip = (2 * xi + yi).astype(jnp.int32)
    core = ci.astype(jnp.int32)
    core_arr = core.reshape(1)
    chip_arr = chip.reshape(1)
    place_arr = jnp.stack([chip, core])
    x2, mem2, tgt = x[0], mem[0], loss_target[0]

    slot = {n: cast_bf16(wts[n][0], chip_arr, "cast_" + n) for n in BIG}
    cw_pad = jnp.pad(conv_w[0], ((0, HALO - CW), (0, 0)))
    slot["conv_w"] = lax.dynamic_update_slice(jnp.zeros((NSH,) + cw_pad.shape, F32), cw_pad[None], (chip, 0, 0))
    g_first = ("ffn1_w_gu", "ffn1_w_down")
    g_mix = ("w_in", "w_a_out", "w_b_out", "w_out", "conv_w")
    g_rest = ("w_q", "w_kv", "w_o", "ffn2_w_gu", "ffn2_w_down")
    def gather(group):
        return gather_rider([slot[n].reshape(NSH, 2, slot[n].shape[1] // 2, slot[n].shape[2]) for n in group])

    def gathered(group, res):
        return {n: r.reshape(slot[n].shape) for n, r in zip(group, res)}

    full = gathered(g_first, run_rider(gather(g_first), "gather_ffn1"))
    wgu1, wd1 = full["ffn1_w_gu"], full["ffn1_w_down"].reshape(FF, D)
    tril = jnp.tril(jnp.ones((CHUNK, CHUNK), dtype=bool))
    ws = jnp.where(tril[None], sgu_w[0], 0.0).astype(BF)
    wst = jnp.transpose(ws, (0, 2, 1))
    sbias = jnp.repeat(jnp.transpose(sgu_b[0]), GD, axis=1)
    gfin = final_norm.reshape(1, D)

    (h1, gu1), rode = ffn_fwd(x2, ffn1_norm, wgu1, wd1, rider=gather(g_mix))
    full.update(gathered(g_mix, rode))
    win = full["w_in"]
    wa, wb, wout = (full[n].reshape(D, D) for n in ("w_a_out", "w_b_out", "w_out"))
    cw_full = jnp.transpose(full["conv_w"], (1, 0, 2)).reshape(HALO, D)
    (h2, proj, n2b, conv_out), rode = mix_fwd(
        h1, mix_norm, win, b_in, cw_full, conv_b, conv_ln_g, conv_ln_b, wa, sgu_ln_g, sgu_ln_b, ws, sbias, wb,
        wout, rider=gather(g_rest))
    full.update(gathered(g_rest, rode))
    wgu2, wd2, wkv = full["ffn2_w_gu"], full["ffn2_w_down"].reshape(FF, D), full["w_kv"]
    wq, wo = full["w_q"].reshape(D, D), full["w_o"].reshape(D, D)
    kb, vb, memn = kv_proj(mem2, mem_norm, wkv)
    h3 = xattn_fwd(h2, xattn_norm, wq, kb, vb, wo)
    dh4, gu2, loss_lanes, d_final = ffn_fwd_loss(h3, ffn2_norm, wgu2, wd2, gfin, tgt)

    def halves_of(n, g):
        rs, cs = wts[n].shape[1:]
        return g.reshape(NSH, 2, rs // 2, cs)

    def pair_adds(group, g4s, recvs):
        sums = [pair_add(g, r, core_arr, "pair_add_" + n) for n, g, r in zip(group, g4s, recvs)]
        return [s[0] for s in sums], exchange_rider([s[1] for s in sums])

    halves = {}

    def chip_sums(group, psums, recv):
        for n, p, r in zip(group, psums, recv):
            halves[n] = chip_sum(p, r, place_arr, "chip_sum_" + n)

    dh3, n4, a4, dgu4, dhb4, d_ffn2n = ffn_bwd(h3, gu2, dh4, ffn2_norm, wgu2, wd2, "ffn2_bwd")
    g_ffn2 = ("ffn2_w_gu", "ffn2_w_down")
    g_gu = halves_of("ffn2_w_gu", dw_matmul(n4, dgu4, NSH, "dw_ffn2_gu"))
    d_down, recv_gu = dw_matmul(a4, dhb4, 1, "dw_ffn2_down", rider=pair_rider([g_gu]))
    g_down = halves_of("ffn2_w_down", d_down)
    recv_down = run_rider(pair_rider([g_down]), "pair_exchange_ffn2")
    ps_ffn2, ride = pair_adds(g_ffn2, [g_gu, g_down], [recv_gu[0], recv_down[0]])
    (dh2, n3, dq, att, dhb3, dk, dv, d_xn), rode = xattn_bwd(h2, dh3, xattn_norm, wq, kb, vb, wo, rider=ride)
    chip_sums(g_ffn2, ps_ffn2, rode)
    g_att = ("w_q", "w_o", "w_kv")
    d_wkv, d_memn = kv_bwd(mem2, mem_norm, memn, wkv, dk, dv)
    g4s = [halves_of(n, g) for n, g in zip(g_att, [dw_matmul(n3, dq, 1, "dw_q"), dw_matmul(att, dhb3, 1, "dw_o"),
                                                     d_wkv])]
    ps_att, ride = pair_adds(g_att, g4s, run_rider(pair_rider(g4s), "pair_exchange_att"))
    ((dconv, dproj, sa, dya, ob, dyb, mg, dhb2, d_sgu_w, d_sgu_b, d_lna_g, d_lna_b, d_lnb_g, d_lnb_b),
     rode) = mix_bwd_branches(proj, conv_out, dh2, conv_ln_g, conv_ln_b, wa, sgu_ln_g, sgu_ln_b, ws, wst,
                              sbias, wb, wout, rider=ride)
    chip_sums(g_att, ps_att, rode)
    dproj, d_conv_w, d_conv_b = conv_bwd(proj, dconv, dproj, cw_full)
    dh1, d_mixn, d_b_in = mix_bwd_in(dproj, h1, dh2, mix_norm, win)
    g_mixw = ("w_in", "w_a_out", "w_b_out", "w_out")
    g_in = halves_of("w_in", dw_matmul(n2b, dproj, NSH, "dw_in"))
    d_a, recv_in = dw_matmul(sa, dya, 1, "dw_a_out", rider=pair_rider([g_in]))
    g4s = [halves_of(n, g) for n, g in zip(g_mixw[1:], [d_a, dw_matmul(ob, dyb, 1, "dw_b_out"),
                                                         dw_matmul(mg, dhb2, 1, "dw_out")])]
    ps_mix, ride = pair_adds(g_mixw, [g_in] + g4s,
                             [recv_in[0]] + list(run_rider(pair_rider(g4s), "pair_exchange_mix")))
    dx, n1, a1, dgu1, dhb1, d_ffn1n = ffn_bwd(x2, gu1, dh1, ffn1_norm, wgu1, wd1, "ffn1_bwd")
    d_wgu1, rode = dw_matmul(n1, dgu1, NSH, "dw_ffn1_gu", rider=ride)
    chip_sums(g_mixw, ps_mix, rode)
    small_grads = {
        "ffn1_norm": d_ffn1n, "mix_norm": d_mixn, "xattn_norm": d_xn, "mem_norm": d_memn, "ffn2_norm": d_ffn2n,
        "final_norm": d_final, "conv_b": d_conv_b, "conv_ln_g": d_lna_g, "conv_ln_b": d_lna_b,
        "sgu_ln_g": d_lnb_g, "sgu_ln_b": d_lnb_b, "b_in": d_b_in.reshape(6, D), "conv_w": d_conv_w[:CW],
        "sgu_w": d_sgu_w.reshape(64, D), "sgu_b": jnp.transpose(d_sgu_b[:, :NG]).reshape(1, NG * CHUNK),
        "loss": loss_lanes}
    slots = lax.dynamic_update_slice(jnp.zeros((NDEV, SMALL_ROWS, D), F32), _pack_small(small_grads)[None],
                                     (2 * chip + core, 0, 0))
    g_gu = halves_of("ffn1_w_gu", d_wgu1)
    riders = [pair_rider([g_gu]), allgather_rider(slots)]
    d_down, rode = dw_matmul(a1, dhb1, 1, "dw_ffn1_down", rider=merge_riders(riders))
    recv_gu, all_slots = split_results(riders, rode)
    g_down = halves_of("ffn1_w_down", d_down)
    recv_down = run_rider(pair_rider([g_down]), "pair_exchange_ffn1")
    ps_ffn1, ride = pair_adds(g_first, [g_gu, g_down], [recv_gu[0], recv_down[0]])
    chip_sums(g_first, ps_ffn1, run_rider(ride, "chip_exchange_ffn1"))
    swapped = run_rider(swap_rider([halves[n] for n in BIG]), "pair_swap")
    gshard = {n: g.reshape(wts[n].shape[1:]) for n, g in zip(BIG, swapped)}

    small = sum_slots(all_slots[0])
    loss = (0.5 / D) * jnp.sum(small[LOSS_ROW])
    gsmall = _unpack_small(small, wts, chip)

    out_g, out_d, out_m, out_v = dict(gsmall), {}, {}, {}
    sw, sm, sv = (_pack_small(_small_views(t))[:LOSS_ROW] for t in (wts, mom1, mom2))
    sg = _pack_small(_small_views({n: gsmall[n] for n in SMALL}))[:LOSS_ROW]
    for dst, packed in zip((out_d, out_m, out_v), adamw(sw, sg, sm, sv, "adamw_small")):
        dst.update(_unpack_small(packed, wts, None))
    for n in BIG:
        shape = wts[n].shape
        out_g[n] = gshard[n].reshape(shape)
        d, mn, vn = adamw(wts[n][0], gshard[n], mom1[n][0], mom2[n][0], "adamw_" + n)
        out_d[n], out_m[n], out_v[n] = d.reshape(shape), mn.reshape(shape), vn.reshape(shape)
    return (loss, dx[None], *[out_g[n] for n in names], *[out_d[n] for n in names],
            *[out_m[n] for n in names], *[out_v[n] for n in names])
```

```python
import functools
import math

import jax
import jax.numpy as jnp
from jax import lax
from jax.experimental import pallas as pl
from jax.experimental.pallas import tpu as pltpu

F32 = jnp.float32
BF = jnp.bfloat16
MESH = pl.DeviceIdType.MESH

D = 1024
FF = 2816
HC = FF // 2
NSH = 4
DIN = 6 * D
INB = DIN // NSH
CW = 31
HALO = 32
CHUNK = 128
NG = 4
GD = D // NG
NH = 4
HD = D // NH
NMEM = 256
EPS_RMS = 1e-6
EPS_LN = 1e-5
GELU_C0 = math.sqrt(2.0 / math.pi)
GELU_C1 = 0.044715
ATT_SCALE = 1.0 / math.sqrt(HD)

ADAM_LR = 0.001
ADAM_B1 = 0.9
ADAM_B2 = 0.999
ADAM_EPS = 1e-08
ADAM_WD = 0.01
ADAM_STEP = 10

VMEM_LIMIT = 56 * 1024 * 1024


def _cparams(sem=None, **kw):
    if sem is not None:
        kw["dimension_semantics"] = sem
    return pltpu.CompilerParams(vmem_limit_bytes=VMEM_LIMIT, **kw)


def _dot(a, b):
    return jnp.dot(a, b, preferred_element_type=F32)


def _dot_nt(a, b):
    return lax.dot_general(a, b, (((1,), (1,)), ((), ())), preferred_element_type=F32)


def _dot_tn(a, b):
    return lax.dot_general(a, b, (((0,), (0,)), ((), ())), preferred_element_type=F32)


def _sigmoid(x):
    return 1.0 / (1.0 + jnp.exp(-x))


def _gelu(x):
    t = jnp.tanh(GELU_C0 * (x + GELU_C1 * (x * x * x)))
    return 0.5 * x * (1.0 + t), t


def _gelu_grad(x, t):
    return 0.5 * (1.0 + t) + 0.5 * x * (1.0 - t * t) * (GELU_C0 * (1.0 + 3.0 * GELU_C1 * x * x))


def _mean(x):
    return jnp.mean(x, axis=-1, keepdims=True)


def _rms(x):
    r = lax.rsqrt(_mean(x * x) + EPS_RMS)
    return x * r, r


def _rms_bwd(dn, xh, r, g):
    dxh = dn * g
    return r * (dxh - xh * _mean(dxh * xh))


def _ln(x):
    xc = x - _mean(x)
    r = lax.rsqrt(_mean(xc * xc) + EPS_LN)
    return xc * r, r


def _ln_bwd(dy, xh, r, g):
    dxh = dy * g
    return r * (dxh - _mean(dxh) - xh * _mean(dxh * xh))


def _colsum(x):
    return jnp.sum(x, axis=0, keepdims=True)


def _const_spec(shape):
    nd = len(shape)
    return pl.BlockSpec(shape, lambda *_: (0,) * nd, pipeline_mode=pl.Buffered(1))


def _row_spec(ts, width):
    return pl.BlockSpec((ts, width), lambda i: (i, 0))


def _acc_spec(shape):
    nd = len(shape)
    return pl.BlockSpec(shape, lambda *_: (0,) * nd)


def _tile(s, want):
    return min(s, want)


HBM_SPEC = pl.BlockSpec(memory_space=pltpu.HBM)


class Rider:
    def __init__(self, ins, outs, aliases, nsem, start, finish):
        self.ins, self.outs, self.aliases, self.nsem = list(ins), list(outs), dict(aliases), nsem
        self.start, self.finish = start, finish


def _pcall(body, *, name, grid, args, in_specs, out_shape, out_specs, scratch=(), rider=None):
    sem = ("arbitrary",) * len(grid)
    n_in, n_out = len(args), len(out_shape)
    if rider is None:
        res = pl.pallas_call(
            body, grid=grid, name=name, out_shape=tuple(out_shape), in_specs=list(in_specs),
            out_specs=tuple(out_specs), scratch_shapes=list(scratch), compiler_params=_cparams(sem))(*args)
        return tuple(res), ()
    r_in, r_out = len(rider.ins), len(rider.outs)

    def wrapped(*refs):
        a, ri = refs[:n_in], refs[n_in:n_in + r_in]
        o = refs[n_in + r_in:n_in + r_in + n_out]
        ro = refs[n_in + r_in + n_out:n_in + r_in + n_out + r_out]
        s, (send, recv) = refs[n_in + r_in + n_out + r_out:-2], refs[-2:]
        first = functools.reduce(jnp.logical_and, [pl.program_id(d) == 0 for d in range(len(grid))])
        last = functools.reduce(jnp.logical_and, [pl.program_id(d) == g - 1 for d, g in enumerate(grid)])

        @pl.when(first)
        def _():
            rider.start(ri, ro, send, recv)

        body(*a, *o, *s)

        @pl.when(last)
        def _():
            rider.finish(ri, ro, send, recv)

    res = pl.pallas_call(
        wrapped, grid=grid, name=name, out_shape=tuple(out_shape) + tuple(rider.outs),
        in_specs=list(in_specs) + [HBM_SPEC] * r_in, out_specs=tuple(out_specs) + (HBM_SPEC,) * r_out,
        scratch_shapes=list(scratch) + [pltpu.SemaphoreType.DMA((rider.nsem,)),
                                        pltpu.SemaphoreType.DMA((rider.nsem,))],
        input_output_aliases={n_in + i: n_out + j for i, j in rider.aliases.items()},
        compiler_params=_cparams(sem, has_side_effects=True))(*args, *rider.ins)
    return tuple(res[:n_out]), tuple(res[n_out:])


def run_rider(rider, name):
    r_in = len(rider.ins)

    def body(*refs):
        ri, ro, (send, recv) = refs[:r_in], refs[r_in:-2], refs[-2:]
        rider.start(ri, ro, send, recv)
        rider.finish(ri, ro, send, recv)

    return pl.pallas_call(
        body, name=name, out_shape=tuple(rider.outs), in_specs=[HBM_SPEC] * r_in,
        out_specs=(HBM_SPEC,) * len(rider.outs),
        scratch_shapes=[pltpu.SemaphoreType.DMA((rider.nsem,)), pltpu.SemaphoreType.DMA((rider.nsem,))],
        input_output_aliases=rider.aliases,
        compiler_params=pltpu.CompilerParams(has_side_effects=True))(*rider.ins)


FFN_BWD_TILE = 256


def _ffn_apply(x, g_ref, wgu_ref, wd_ref, gu_ref):
    xh, _ = _rms(x)
    nb = (xh * g_ref[...]).astype(BF)
    acc = jnp.zeros(x.shape, F32)
    for j in range(2):
        g = _dot(nb, wgu_ref[j])
        u = _dot(nb, wgu_ref[2 + j])
        gu_ref[:, j * HC:(j + 1) * HC] = g.astype(BF)
        gu_ref[:, FF + j * HC:FF + (j + 1) * HC] = u.astype(BF)
        a = (g * _sigmoid(g) * u).astype(BF)
        acc = acc + _dot(a, wd_ref[j * HC:(j + 1) * HC, :])
    return x + 0.5 * acc


def ffn_fwd(h, gain, wgu, wd, rider=None):
    s = h.shape[0]
    ts = _tile(s, 512)

    def body(h_ref, g_ref, wgu_ref, wd_ref, o_ref, gu_ref):
        o_ref[...] = _ffn_apply(h_ref[...], g_ref, wgu_ref, wd_ref, gu_ref)

    return _pcall(
        body, grid=(s // ts,), name="ffn1_fwd", args=(h, gain, wgu, wd),
        out_shape=[jax.ShapeDtypeStruct((s, D), F32), jax.ShapeDtypeStruct((s, 2 * FF), BF)],
        in_specs=[_row_spec(ts, D), _const_spec((1, D)), _const_spec((NSH, D, HC)), _const_spec((FF, D))],
        out_specs=[_row_spec(ts, D), _row_spec(ts, 2 * FF)], rider=rider)


def ffn_fwd_loss(h, gain, wgu, wd, gfin, target):
    s = h.shape[0]
    ts = _tile(s, 512)

    def body(h_ref, g_ref, wgu_ref, wd_ref, gf_ref, t_ref, dh_ref, gu_ref, loss_ref, dgf_ref):
        @pl.when(pl.program_id(0) == 0)
        def _():
            loss_ref[...] = jnp.zeros_like(loss_ref)
            dgf_ref[...] = jnp.zeros_like(dgf_ref)

        h4 = _ffn_apply(h_ref[...], g_ref, wgu_ref, wd_ref, gu_ref)
        yh, r4 = _rms(h4)
        gf = gf_ref[...]
        e = yh * gf - t_ref[...]
        loss_ref[...] += _colsum(e * e)
        dy = e * (1.0 / D)
        dgf_ref[...] += _colsum(dy * yh)
        dh_ref[...] = _rms_bwd(dy, yh, r4, gf)

    return pl.pallas_call(
        body, grid=(s // ts,), name="ffn_fwd_loss",
        out_shape=(jax.ShapeDtypeStruct((s, D), F32), jax.ShapeDtypeStruct((s, 2 * FF), BF),
                   jax.ShapeDtypeStruct((1, D), F32), jax.ShapeDtypeStruct((1, D), F32)),
        in_specs=[_row_spec(ts, D), _const_spec((1, D)), _const_spec((NSH, D, HC)), _const_spec((FF, D)),
                  _const_spec((1, D)), _row_spec(ts, D)],
        out_specs=(_row_spec(ts, D), _row_spec(ts, 2 * FF), _acc_spec((1, D)), _acc_spec((1, D))),
        compiler_params=_cparams(("arbitrary",)),
    )(h, gain, wgu, wd, gfin, target)


def ffn_bwd(h, gu, dh, gain, wgu, wd, name):
    s = h.shape[0]
    ts = _tile(s, FFN_BWD_TILE)

    def body(h_ref, gu_ref, dh_ref, g_ref, wgu_ref, wd_ref, dx_ref, n_ref, a_ref, dgu_ref, dhb_ref, dg_ref):
        @pl.when(pl.program_id(0) == 0)
        def _():
            dg_ref[...] = jnp.zeros_like(dg_ref)

        x = h_ref[...]
        dh = dh_ref[...]
        gain_v = g_ref[...]
        xh, r = _rms(x)
        n_ref[...] = (xh * gain_v).astype(BF)
        dhb = (0.5 * dh).astype(BF)
        dhb_ref[...] = dhb
        dn = jnp.zeros((ts, D), F32)
        for j in range(2):
            g = gu_ref[:, j * HC:(j + 1) * HC].astype(F32)
            u = gu_ref[:, FF + j * HC:FF + (j + 1) * HC].astype(F32)
            sg = _sigmoid(g)
            sl = g * sg
            a_ref[:, j * HC:(j + 1) * HC] = (sl * u).astype(BF)
            da = _dot_nt(dhb, wd_ref[j * HC:(j + 1) * HC, :])
            dgb = (da * u * (sg * (1.0 + g * (1.0 - sg)))).astype(BF)
            dub = (da * sl).astype(BF)
            dgu_ref[:, j * HC:(j + 1) * HC] = dgb
            dgu_ref[:, FF + j * HC:FF + (j + 1) * HC] = dub
            dn = dn + _dot_nt(dgb, wgu_ref[j]) + _dot_nt(dub, wgu_ref[2 + j])
        dg_ref[...] += _colsum(dn * xh)
        dx_ref[...] = dh + _rms_bwd(dn, xh, r, gain_v)

    return pl.pallas_call(
        body, grid=(s // ts,), name=name,
        out_shape=(jax.ShapeDtypeStruct((s, D), F32), jax.ShapeDtypeStruct((s, D), BF),
                   jax.ShapeDtypeStruct((s, FF), BF), jax.ShapeDtypeStruct((s, 2 * FF), BF),
                   jax.ShapeDtypeStruct((s, D), BF), jax.ShapeDtypeStruct((1, D), F32)),
        in_specs=[_row_spec(ts, D), _row_spec(ts, 2 * FF), _row_spec(ts, D), _const_spec((1, D)),
                  _const_spec((NSH, D, HC)), _const_spec((FF, D))],
        out_specs=(_row_spec(ts, D), _row_spec(ts, D), _row_spec(ts, FF), _row_spec(ts, 2 * FF),
                   _row_spec(ts, D), _acc_spec((1, D))),
        compiler_params=_cparams(("arbitrary",)),
    )(h, gu, dh, gain, wgu, wd)


def dw_matmul(x, dy, nsplit, name, rider=None):
    s, k = x.shape
    n = dy.shape[1]
    nb = n // nsplit
    ts = _tile(s, 1024)

    def body(x_ref, dy_ref, o_ref):
        @pl.when(pl.program_id(1) == 0)
        def _():
            o_ref[...] = jnp.zeros_like(o_ref)

        o_ref[0] += _dot_tn(x_ref[...], dy_ref[...])

    (out,), rode = _pcall(
        body, grid=(nsplit, s // ts), name=name, args=(x, dy),
        out_shape=[jax.ShapeDtypeStruct((nsplit, k, nb), F32)],
        in_specs=[pl.BlockSpec((ts, k), lambda j, i: (i, 0)), pl.BlockSpec((ts, nb), lambda j, i: (i, j))],
        out_specs=[pl.BlockSpec((1, k, nb), lambda j, i: (j, 0, 0))], rider=rider)
    return (out, rode) if rider is not None else out


def _split_in_proj(p, b):
    h = INB - D
    a_val = p[0][:, :D] + b[:, 0:D]
    a_gate = jnp.concatenate([p[0][:, D:], p[1][:, :h]], axis=1) + b[:, D:2 * D]
    b_u = p[1][:, h:] + b[:, 2 * D:3 * D]
    b_v = p[2][:, :D] + b[:, 3 * D:4 * D]
    g_a = jnp.concatenate([p[2][:, D:], p[3][:, :h]], axis=1) + b[:, 4 * D:5 * D]
    g_b = p[3][:, h:] + b[:, 5 * D:6 * D]
    return a_val, a_gate, b_u, b_v, g_a, g_b


def _sgu_mix(vnb, ws_ref, sb_ref, mixed_ref, ts):
    for ci in range(ts // CHUNK):
        rows = slice(ci * CHUNK, (ci + 1) * CHUNK)
        for g in range(NG):
            cols = slice(g * GD, (g + 1) * GD)
            mixed_ref[rows, cols] = _dot(ws_ref[g], vnb[rows, cols]) + sb_ref[:, cols]


SUB = 8
CB = 128
SH_ROWS_EXTRA = HALO - SUB


def _shifted_copies(ext_ref, sh_ref, lanes, ts):
    for b in range(1, SUB):
        sh_ref[b - 1] = ext_ref[b:b + ts + SH_ROWS_EXTRA, lanes]


def _window(ext_ref, sh_ref, lanes, first, r0, nrows):
    b = first % SUB
    a = first - b
    if b == 0:
        return ext_ref[a + r0:a + r0 + nrows, lanes]
    return sh_ref[b - 1, a + r0:a + r0 + nrows, :]


def mix_fwd(h, gain, win, b_in, conv_w, conv_b, lna_g, lna_b, wa, lnb_g, lnb_b, ws, sbias, wb, wo, rider=None):
    s = h.shape[0]
    ts = _tile(s, 256)

    def body(h_ref, g_ref, win_ref, bin_ref, cw_ref, cb_ref, lag_ref, lab_ref, wa_ref, lbg_ref, lbb_ref,
             ws_ref, sb_ref, wb_ref, wo_ref, o_ref, p_ref, n_ref, c_ref, ext_ref, mixed_ref, sh_ref):
        @pl.when(pl.program_id(0) == 0)
        def _():
            ext_ref[0:HALO, :] = jnp.zeros((HALO, D), F32)

        x = h_ref[...]
        xh, _ = _rms(x)
        nb = (xh * g_ref[...]).astype(BF)
        n_ref[...] = nb
        b = bin_ref[...]
        p = []
        for k in range(NSH):
            pk = _dot(nb, win_ref[k])
            p_ref[:, k * INB:(k + 1) * INB] = (pk + b[:, k * INB:(k + 1) * INB]).astype(BF)
            p.append(pk)
        a_val, a_gate, b_u, b_v, g_a, g_b = _split_in_proj(p, b)
        ext_ref[HALO:HALO + ts, :] = a_val * _sigmoid(a_gate)
        for l0 in range(0, D, CB):
            lanes = slice(l0, l0 + CB)
            _shifted_copies(ext_ref, sh_ref, lanes, ts)
            for r0 in range(0, ts, CB):
                acc = jnp.zeros((CB, CB), F32) + cb_ref[:, lanes]
                for k in range(CW):
                    acc = acc + cw_ref[k:k + 1, lanes] * _window(ext_ref, sh_ref, lanes,
                                                                 HALO - (CW - 1) + k, r0, CB)
                c_ref[r0:r0 + CB, lanes] = acc
        ext_ref[0:HALO, :] = ext_ref[ts:ts + HALO, :]
        ch, _ = _ln(c_ref[...])
        la = ch * lag_ref[...] + lab_ref[...]
        sa = (la * _sigmoid(la)).astype(BF)
        ya = _dot(sa, wa_ref[...])
        ub, _ = _gelu(b_u)
        gv, _ = _gelu(b_v)
        vh, _ = _ln(gv)
        vnb = (vh * lbg_ref[...] + lbb_ref[...]).astype(BF)
        _sgu_mix(vnb, ws_ref, sb_ref, mixed_ref, ts)
        ob = (ub * mixed_ref[...]).astype(BF)
        yb = _dot(ob, wb_ref[...])
        merged = (_sigmoid(g_a) * ya + _sigmoid(g_b) * yb).astype(BF)
        o_ref[...] = x + _dot(merged, wo_ref[...])

    vec = _const_spec((1, D))
    sq = _const_spec((D, D))
    return _pcall(
        body, grid=(s // ts,), name="mix_fwd",
        args=(h, gain, win, b_in, conv_w, conv_b, lna_g, lna_b, wa, lnb_g, lnb_b, ws, sbias, wb, wo),
        out_shape=(jax.ShapeDtypeStruct((s, D), F32), jax.ShapeDtypeStruct((s, DIN), BF),
                   jax.ShapeDtypeStruct((s, D), BF), jax.ShapeDtypeStruct((s, D), F32)),
        in_specs=[_row_spec(ts, D), vec, _const_spec((NSH, D, INB)), _const_spec((1, DIN)),
                  _const_spec((HALO, D)), vec, vec, vec, sq, vec, vec,
                  _const_spec((NG, CHUNK, CHUNK)), _const_spec((CHUNK, D)), sq, sq],
        out_specs=(_row_spec(ts, D), _row_spec(ts, DIN), _row_spec(ts, D), _row_spec(ts, D)),
        scratch=[pltpu.VMEM((ts + HALO, D), F32), pltpu.VMEM((ts, D), F32),
                 pltpu.VMEM((SUB - 1, ts + SH_ROWS_EXTRA, CB), F32)], rider=rider)


def mix_bwd_branches(p, c, dh, lna_g, lna_b, wa, lnb_g, lnb_b, ws, wst, sbias, wb, wo, rider=None):
    s = dh.shape[0]
    ts = _tile(s, 256)
    nsteps = s // ts

    def body(p_ref, c_ref, dh_ref, lag_ref, lab_ref, wa_ref, lbg_ref, lbb_ref, ws_ref, wst_ref, sb_ref,
             wb_ref, wo_ref, dc_ref, dp_ref, sa_ref, dya_ref, ob_ref, dyb_ref, mg_ref, dhb_ref,
             dws_ref, dsb_ref, dlag_ref, dlab_ref, dlbg_ref, dlbb_ref, mixed_ref, dmix_ref, dvn_ref, dsb_acc):
        step = pl.program_id(0)

        @pl.when(step == 0)
        def _():
            for ref in (dws_ref, dsb_acc, dlag_ref, dlab_ref, dlbg_ref, dlbb_ref):
                ref[...] = jnp.zeros_like(ref)

        b_u = p_ref[:, 2 * D:3 * D].astype(F32)
        b_v = p_ref[:, 3 * D:4 * D].astype(F32)
        sga = _sigmoid(p_ref[:, 4 * D:5 * D].astype(F32))
        sgb = _sigmoid(p_ref[:, 5 * D:6 * D].astype(F32))
        lag = lag_ref[...]
        ch, ra = _ln(c_ref[...])
        la = ch * lag + lab_ref[...]
        sla = _sigmoid(la)
        sa = (la * sla).astype(BF)
        sa_ref[...] = sa
        ya = _dot(sa, wa_ref[...])
        lbg = lbg_ref[...]
        ub, tu = _gelu(b_u)
        gv, tv = _gelu(b_v)
        vh, rb = _ln(gv)
        vnb = (vh * lbg + lbb_ref[...]).astype(BF)
        _sgu_mix(vnb, ws_ref, sb_ref, mixed_ref, ts)
        mixed = mixed_ref[...]
        ob = (ub * mixed).astype(BF)
        ob_ref[...] = ob
        yb = _dot(ob, wb_ref[...])
        mg_ref[...] = (sga * ya + sgb * yb).astype(BF)
        dhb = dh_ref[...].astype(BF)
        dhb_ref[...] = dhb
        dm = _dot_nt(dhb, wo_ref[...])
        dp_ref[:, 0:2 * D] = jnp.zeros((ts, 2 * D), BF)
        dp_ref[:, 4 * D:5 * D] = (dm * ya * sga * (1.0 - sga)).astype(BF)
        dp_ref[:, 5 * D:6 * D] = (dm * yb * sgb * (1.0 - sgb)).astype(BF)
        dya = (dm * sga).astype(BF)
        dya_ref[...] = dya
        dyb = (dm * sgb).astype(BF)
        dyb_ref[...] = dyb
        dla = _dot_nt(dya, wa_ref[...]) * (sla * (1.0 + la * (1.0 - sla)))
        dlag_ref[...] += _colsum(dla * ch)
        dlab_ref[...] += _colsum(dla)
        dc_ref[...] = _ln_bwd(dla, ch, ra, lag)
        dob = _dot_nt(dyb, wb_ref[...])
        dp_ref[:, 2 * D:3 * D] = (dob * mixed * _gelu_grad(b_u, tu)).astype(BF)
        dmix = dob * ub
        dmix_ref[...] = dmix.astype(BF)
        dsb = jnp.zeros((CHUNK, D), F32)
        for ci in range(ts // CHUNK):
            rows = slice(ci * CHUNK, (ci + 1) * CHUNK)
            dsb = dsb + dmix[rows, :]
            for g in range(NG):
                cols = slice(g * GD, (g + 1) * GD)
                dmb = dmix_ref[rows, cols]
                dws_ref[g] += _dot_nt(dmb, vnb[rows, cols])
                dvn_ref[rows, cols] = _dot(wst_ref[g], dmb)
        dsb_acc[...] += dsb
        dvn = dvn_ref[...]
        dlbg_ref[...] += _colsum(dvn * vh)
        dlbb_ref[...] += _colsum(dvn)
        dp_ref[:, 3 * D:4 * D] = (_ln_bwd(dvn, vh, rb, lbg) * _gelu_grad(b_v, tv)).astype(BF)

        @pl.when(step == nsteps - 1)
        def _():
            row = lax.broadcasted_iota(jnp.int32, (CHUNK, CHUNK), 0)
            col = lax.broadcasted_iota(jnp.int32, (CHUNK, CHUNK), 1)
            for g in range(NG):
                dws_ref[g] = jnp.where(col <= row, dws_ref[g], 0.0)
            acc = jnp.zeros((CHUNK, CHUNK), F32)
            for g in range(NG):
                tot = jnp.sum(dsb_acc[:, g * GD:(g + 1) * GD], axis=-1, keepdims=True)
                acc = acc + jnp.where(col == g, tot, 0.0)
            dsb_ref[...] = acc

    vec = _const_spec((1, D))
    sq = _const_spec((D, D))
    bf_rows = jax.ShapeDtypeStruct((s, D), BF)
    acc_vec = jax.ShapeDtypeStruct((1, D), F32)
    return _pcall(
        body, grid=(nsteps,), name="mix_bwd_branches",
        args=(p, c, dh, lna_g, lna_b, wa, lnb_g, lnb_b, ws, wst, sbias, wb, wo),
        out_shape=(jax.ShapeDtypeStruct((s, D), F32), jax.ShapeDtypeStruct((s, DIN), BF),
                   bf_rows, bf_rows, bf_rows, bf_rows, bf_rows, bf_rows,
                   jax.ShapeDtypeStruct((NG, CHUNK, CHUNK), F32), jax.ShapeDtypeStruct((CHUNK, CHUNK), F32),
                   acc_vec, acc_vec, acc_vec, acc_vec),
        in_specs=[_row_spec(ts, DIN), _row_spec(ts, D), _row_spec(ts, D), vec, vec, sq, vec, vec,
                  _const_spec((NG, CHUNK, CHUNK)), _const_spec((NG, CHUNK, CHUNK)), _const_spec((CHUNK, D)),
                  sq, sq],
        out_specs=(_row_spec(ts, D), _row_spec(ts, DIN)) + (_row_spec(ts, D),) * 6
        + (_acc_spec((NG, CHUNK, CHUNK)), _acc_spec((CHUNK, CHUNK))) + (_acc_spec((1, D)),) * 4,
        scratch=[pltpu.VMEM((ts, D), F32), pltpu.VMEM((ts, D), BF), pltpu.VMEM((ts, D), F32),
                 pltpu.VMEM((CHUNK, D), F32)], rider=rider)


def conv_bwd(p, dc, dp, conv_w):
    s = dc.shape[0]
    ts = _tile(s, 256)
    nsteps = s // ts
    per = ts // HALO

    rb = 16

    def body(pm_ref, pp_ref, dcm_ref, dcn_ref, cw_ref, dpin_ref, dp_ref, dw_ref, db_ref, ext_ref, dext_ref,
             dw8_ref, sh_ref, dsh_ref, dglu_ref):
        del dpin_ref
        step = pl.program_id(0)

        @pl.when(step == 0)
        def _():
            dw8_ref[...] = jnp.zeros_like(dw8_ref)
            db_ref[...] = jnp.zeros_like(db_ref)

        a_val = pm_ref[:, 0:D].astype(F32)
        sg = _sigmoid(pm_ref[:, D:2 * D].astype(F32))
        prev = pp_ref[:, 0:D].astype(F32) * _sigmoid(pp_ref[:, D:2 * D].astype(F32))
        ext_ref[0:HALO, :] = jnp.where(step > 0, prev, 0.0)
        ext_ref[HALO:HALO + ts, :] = a_val * sg
        dcm = dcm_ref[...]
        dext_ref[0:ts, :] = dcm
        dext_ref[ts:ts + HALO, :] = jnp.where(step < nsteps - 1, dcn_ref[...], 0.0)
        db_ref[...] += _colsum(dcm)
        for l0 in range(0, D, CB):
            lanes = slice(l0, l0 + CB)
            _shifted_copies(dext_ref, dsh_ref, lanes, ts)
            for r0 in range(0, ts, CB):
                acc = jnp.zeros((CB, CB), F32)
                for k in range(CW):
                    acc = acc + cw_ref[k:k + 1, lanes] * _window(dext_ref, dsh_ref, lanes, CW - 1 - k, r0, CB)
                dglu_ref[r0:r0 + CB, lanes] = acc
            _shifted_copies(ext_ref, sh_ref, lanes, ts)
            accs = [jnp.zeros((SUB, CB), F32) for _ in range(CW)]
            for r0 in range(0, ts, rb):
                dcb = dext_ref[r0:r0 + rb, lanes]
                for k in range(CW):
                    prod = dcb * _window(ext_ref, sh_ref, lanes, HALO - (CW - 1) + k, r0, rb)
                    accs[k] = accs[k] + jnp.sum(prod.reshape(rb // SUB, SUB, CB), axis=0)
            for k in range(CW):
                dw8_ref[k, :, lanes] += accs[k]
        dglu = dglu_ref[...]
        dp_ref[:, 0:D] = (dglu * sg).astype(BF)
        dp_ref[:, D:2 * D] = (dglu * a_val * sg * (1.0 - sg)).astype(BF)

        @pl.when(step == nsteps - 1)
        def _():
            dw_ref[...] = jnp.zeros_like(dw_ref)
            for k in range(CW):
                dw_ref[k:k + 1, :] = _colsum(dw8_ref[k])

    return pl.pallas_call(
        body, grid=(nsteps,), name="conv_bwd",
        out_shape=(jax.ShapeDtypeStruct((s, DIN), BF), jax.ShapeDtypeStruct((HALO, D), F32),
                   jax.ShapeDtypeStruct((1, D), F32)),
        in_specs=[pl.BlockSpec((ts, 2 * D), lambda i: (i, 0)),
                  pl.BlockSpec((HALO, 2 * D), lambda i: (jnp.maximum(i * per - 1, 0), 0)),
                  _row_spec(ts, D),
                  pl.BlockSpec((HALO, D), lambda i: (jnp.minimum((i + 1) * per, s // HALO - 1), 0)),
                  _const_spec((HALO, D)),
                  pl.BlockSpec(memory_space=pl.ANY)],
        out_specs=(pl.BlockSpec((ts, 2 * D), lambda i: (i, 0)), _acc_spec((HALO, D)), _acc_spec((1, D))),
        scratch_shapes=[pltpu.VMEM((ts + HALO, D), F32), pltpu.VMEM((ts + HALO, D), F32),
                        pltpu.VMEM((HALO, SUB, D), F32),
                        pltpu.VMEM((SUB - 1, ts + SH_ROWS_EXTRA, CB), F32),
                        pltpu.VMEM((SUB - 1, ts + SH_ROWS_EXTRA, CB), F32),
                        pltpu.VMEM((ts, D), F32)],
        input_output_aliases={5: 0},
        compiler_params=_cparams(("arbitrary",)),
    )(p, p, dc, dc, conv_w, dp)


def mix_bwd_in(dp, h, dh, gain, win):
    s = h.shape[0]
    ts = _tile(s, 512)

    def body(dp_ref, h_ref, dh_ref, g_ref, win_ref, dx_ref, dg_ref, db_ref):
        @pl.when(pl.program_id(0) == 0)
        def _():
            dg_ref[...] = jnp.zeros_like(dg_ref)
            db_ref[...] = jnp.zeros_like(db_ref)

        gain_v = g_ref[...]
        xh, r = _rms(h_ref[...])
        dn = jnp.zeros((ts, D), F32)
        for k in range(NSH):
            dpk = dp_ref[:, k * INB:(k + 1) * INB]
            dn = dn + _dot_nt(dpk, win_ref[k])
            db_ref[:, k * INB:(k + 1) * INB] += _colsum(dpk.astype(F32))
        dg_ref[...] += _colsum(dn * xh)
        dx_ref[...] = dh_ref[...] + _rms_bwd(dn, xh, r, gain_v)

    return pl.pallas_call(
        body, grid=(s // ts,), name="mix_bwd_in",
        out_shape=(jax.ShapeDtypeStruct((s, D), F32), jax.ShapeDtypeStruct((1, D), F32),
                   jax.ShapeDtypeStruct((1, DIN), F32)),
        in_specs=[_row_spec(ts, DIN), _row_spec(ts, D), _row_spec(ts, D), _const_spec((1, D)),
                  _const_spec((NSH, D, INB))],
        out_specs=(_row_spec(ts, D), _acc_spec((1, D)), _acc_spec((1, DIN))),
        compiler_params=_cparams(("arbitrary",)),
    )(dp, h, dh, gain, win)


def kv_proj(mem, gain, wkv):
    def body(m_ref, g_ref, w_ref, k_ref, v_ref, n_ref):
        xh, _ = _rms(m_ref[...])
        nb = (xh * g_ref[...]).astype(BF)
        n_ref[...] = nb
        half = D // 2
        for j in range(2):
            k_ref[:, j * half:(j + 1) * half] = _dot(nb, w_ref[j]).astype(BF)
            v_ref[:, j * half:(j + 1) * half] = _dot(nb, w_ref[2 + j]).astype(BF)

    o = jax.ShapeDtypeStruct((NMEM, D), BF)
    return pl.pallas_call(body, name="kv_proj", out_shape=(o, o, o), compiler_params=_cparams())(mem, gain, wkv)


def kv_bwd(mem, gain, memn, wkv, dk, dv):
    def body(m_ref, g_ref, n_ref, w_ref, dk_ref, dv_ref, dw_ref, dg_ref):
        xh, _ = _rms(m_ref[...])
        nb = n_ref[...]
        half = D // 2
        dn = jnp.zeros((NMEM, D), F32)
        for j in range(2):
            dkb = dk_ref[:, j * half:(j + 1) * half].astype(BF)
            dvb = dv_ref[:, j * half:(j + 1) * half].astype(BF)
            dw_ref[j] = _dot_tn(nb, dkb)
            dw_ref[2 + j] = _dot_tn(nb, dvb)
            dn = dn + _dot_nt(dkb, w_ref[j]) + _dot_nt(dvb, w_ref[2 + j])
        dg_ref[...] = _colsum(dn * xh)

    return pl.pallas_call(
        body, name="kv_bwd",
        out_shape=(jax.ShapeDtypeStruct((NSH, D, D // 2), F32), jax.ShapeDtypeStruct((1, D), F32)),
        compiler_params=_cparams())(mem, gain, memn, wkv, dk, dv)


def _attend(qb, k_ref, v_ref, h):
    cols = slice(h * HD, (h + 1) * HD)
    sc = _dot_nt(qb[:, cols], k_ref[:, cols]) * ATT_SCALE
    e = jnp.exp(sc - jnp.max(sc, axis=-1, keepdims=True))
    pr = e / jnp.sum(e, axis=-1, keepdims=True)
    return pr, _dot(pr.astype(BF), v_ref[:, cols])


def xattn_fwd(h, gain, wq, k, v, wo):
    s = h.shape[0]
    ts = _tile(s, 512)

    def body(h_ref, g_ref, wq_ref, k_ref, v_ref, wo_ref, o_ref, att_ref):
        x = h_ref[...]
        xh, _ = _rms(x)
        nb = (xh * g_ref[...]).astype(BF)
        qb = _dot(nb, wq_ref[...]).astype(BF)
        for hd in range(NH):
            _, oh = _attend(qb, k_ref, v_ref, hd)
            att_ref[:, hd * HD:(hd + 1) * HD] = oh.astype(BF)
        o_ref[...] = x + _dot(att_ref[...], wo_ref[...])

    sq = _const_spec((D, D))
    kvs = _const_spec((NMEM, D))
    return pl.pallas_call(
        body, grid=(s // ts,), name="xattn_fwd",
        out_shape=jax.ShapeDtypeStruct((s, D), F32),
        in_specs=[_row_spec(ts, D), _const_spec((1, D)), sq, kvs, kvs, sq],
        out_specs=_row_spec(ts, D),
        scratch_shapes=[pltpu.VMEM((ts, D), BF)],
        compiler_params=_cparams(("arbitrary",)),
    )(h, gain, wq, k, v, wo)


def xattn_bwd(h, dh, gain, wq, k, v, wo, rider=None):
    s = h.shape[0]
    ts = _tile(s, 512)

    def body(h_ref, dh_ref, g_ref, wq_ref, k_ref, v_ref, wo_ref,
             dx_ref, n_ref, dq_ref, att_ref, dhb_ref, dk_ref, dv_ref, dg_ref):
        @pl.when(pl.program_id(0) == 0)
        def _():
            for ref in (dk_ref, dv_ref, dg_ref):
                ref[...] = jnp.zeros_like(ref)

        x = h_ref[...]
        dh = dh_ref[...]
        gain_v = g_ref[...]
        xh, r = _rms(x)
        nb = (xh * gain_v).astype(BF)
        n_ref[...] = nb
        qb = _dot(nb, wq_ref[...]).astype(BF)
        dhb = dh.astype(BF)
        dhb_ref[...] = dhb
        dob = _dot_nt(dhb, wo_ref[...]).astype(BF)
        for hd in range(NH):
            cols = slice(hd * HD, (hd + 1) * HD)
            pr, oh = _attend(qb, k_ref, v_ref, hd)
            att_ref[:, cols] = oh.astype(BF)
            doh = dob[:, cols]
            dpr = _dot_nt(doh, v_ref[:, cols])
            dv_ref[:, cols] += _dot_tn(pr.astype(BF), doh)
            dsc = (pr * (dpr - jnp.sum(dpr * pr, axis=-1, keepdims=True)) * ATT_SCALE).astype(BF)
            dq_ref[:, cols] = _dot(dsc, k_ref[:, cols]).astype(BF)
            dk_ref[:, cols] += _dot_tn(dsc, qb[:, cols])
        dn = _dot_nt(dq_ref[...], wq_ref[...])
        dg_ref[...] += _colsum(dn * xh)
        dx_ref[...] = dh + _rms_bwd(dn, xh, r, gain_v)

    sq = _const_spec((D, D))
    kvs = _const_spec((NMEM, D))
    bf_rows = jax.ShapeDtypeStruct((s, D), BF)
    kv_acc = jax.ShapeDtypeStruct((NMEM, D), F32)
    return _pcall(
        body, grid=(s // ts,), name="xattn_bwd", args=(h, dh, gain, wq, k, v, wo),
        out_shape=(jax.ShapeDtypeStruct((s, D), F32), bf_rows, bf_rows, bf_rows, bf_rows, kv_acc, kv_acc,
                   jax.ShapeDtypeStruct((1, D), F32)),
        in_specs=[_row_spec(ts, D), _row_spec(ts, D), _const_spec((1, D)), sq, kvs, kvs, sq],
        out_specs=(_row_spec(ts, D),) * 5 + (_acc_spec((NMEM, D)), _acc_spec((NMEM, D)), _acc_spec((1, D))),
        rider=rider)


BLOCK_BYTES = 3 << 19


def _row_block(rows, cols):
    rb = rows
    while rb * cols * 4 > BLOCK_BYTES and rb % 32 == 0:
        rb //= 2
    return rb


def cast_bf16(w, chip, name):
    r, c = w.shape
    rb = _row_block(r, c)

    def body(chip_ref, w_ref, o_ref):
        del chip_ref
        o_ref[0] = w_ref[...].astype(BF)

    return pl.pallas_call(
        body, name=name, out_shape=jax.ShapeDtypeStruct((NSH, r, c), BF),
        grid_spec=pltpu.PrefetchScalarGridSpec(
            num_scalar_prefetch=1, grid=(r // rb,),
            in_specs=[pl.BlockSpec((rb, c), lambda i, chip_ref: (i, 0))],
            out_specs=pl.BlockSpec((1, rb, c), lambda i, chip_ref: (chip_ref[0], i, 0))),
        compiler_params=_cparams(("arbitrary",)))(chip, w)


def pair_add(g4, recv, core, name):
    nsh, _, rh, c = g4.shape
    rb = _row_block(rh, c)

    def body(core_ref, g_ref, r_ref, o_ref, ob_ref):
        del core_ref
        sm = g_ref[0, 0] + r_ref[0]
        o_ref[0] = sm
        ob_ref[0] = sm.astype(BF)

    spec3 = pl.BlockSpec((1, rb, c), lambda k, i, core_ref: (k, i, 0))
    return pl.pallas_call(
        body, name=name,
        out_shape=(jax.ShapeDtypeStruct((nsh, rh, c), F32), jax.ShapeDtypeStruct((nsh, rh, c), BF)),
        grid_spec=pltpu.PrefetchScalarGridSpec(
            num_scalar_prefetch=1, grid=(nsh, rh // rb),
            in_specs=[pl.BlockSpec((1, 1, rb, c), lambda k, i, core_ref: (k, core_ref[0], i, 0)), spec3],
            out_specs=(spec3, spec3)),
        compiler_params=_cparams(("arbitrary", "arbitrary")))(core, g4, recv)


def chip_sum(psum, recv, place, name):
    _, rh, c = psum.shape
    rb = _row_block(rh, c)

    def body(place_ref, p_ref, r_ref, o_ref):
        del place_ref
        acc = p_ref[0]
        for j in range(NSH - 1):
            acc = acc + r_ref[j].astype(F32)
        o_ref[0] = acc

    return pl.pallas_call(
        body, name=name, out_shape=jax.ShapeDtypeStruct((2, rh, c), F32),
        grid_spec=pltpu.PrefetchScalarGridSpec(
            num_scalar_prefetch=1, grid=(rh // rb,),
            in_specs=[pl.BlockSpec((1, rb, c), lambda i, place_ref: (place_ref[0], i, 0)),
                      pl.BlockSpec((NSH - 1, rb, c), lambda i, place_ref: (0, i, 0))],
            out_specs=pl.BlockSpec((1, rb, c), lambda i, place_ref: (place_ref[1], i, 0))),
        compiler_params=_cparams(("arbitrary",)))(place, psum, recv)


def _adamw_math(w, g, m, v):
    m = ADAM_B1 * m + (1.0 - ADAM_B1) * g
    v = ADAM_B2 * v + (1.0 - ADAM_B2) * (g * g)
    m_hat = m / (1.0 - ADAM_B1 ** ADAM_STEP)
    v_hat = v / (1.0 - ADAM_B2 ** ADAM_STEP)
    delta = -ADAM_LR * (m_hat / (jnp.sqrt(v_hat) + ADAM_EPS) + ADAM_WD * w)
    return delta, m, v


def adamw(w, g, m, v, name):
    r, c = w.shape
    rb = _row_block(r, c)

    def body(w_ref, g_ref, m_ref, v_ref, d_ref, mo_ref, vo_ref):
        d, mn, vn = _adamw_math(w_ref[...], g_ref[...], m_ref[...], v_ref[...])
        d_ref[...] = d
        mo_ref[...] = mn
        vo_ref[...] = vn

    o = jax.ShapeDtypeStruct((r, c), F32)
    spec = _row_spec(rb, c)
    return pl.pallas_call(
        body, grid=(r // rb,), name=name, out_shape=(o, o, o),
        in_specs=[spec] * 4, out_specs=(spec,) * 3,
        compiler_params=_cparams(("arbitrary",)))(w, g, m, v)


def _place():
    return lax.axis_index("x"), lax.axis_index("y"), lax.axis_index("c")


def _other_chips(x, y):
    return [(1 - x, y), (x, 1 - y), (1 - x, 1 - y)]


NOTHER = NSH - 1


def gather_rider(arrays):
    nw = len(arrays)
    nici = nw * NOTHER

    def copies(refs, send_sems, recv_sems):
        x, y, c = _place()
        ici, d2d = [], []
        for w in range(nw):
            for j, (px, py) in enumerate(_other_chips(x, y)):
                n = w * NOTHER + j
                sems = dict(send_sem=send_sems.at[n], recv_sem=recv_sems.at[n],
                            device_id=(px, py, c), device_id_type=MESH)
                mine = refs[w].at[2 * x + y, c]
                theirs = refs[w].at[2 * px + py, c]
                ici.append((pltpu.make_async_remote_copy(src_ref=mine, dst_ref=mine, **sems),
                            pltpu.make_async_remote_copy(src_ref=mine, dst_ref=theirs, **sems)))
                sems = dict(send_sem=send_sems.at[nici + n], recv_sem=recv_sems.at[nici + n],
                            device_id=(x, y, 1 - c), device_id_type=MESH)
                d2d.append((pltpu.make_async_remote_copy(src_ref=theirs, dst_ref=theirs, **sems),
                            pltpu.make_async_remote_copy(src_ref=theirs, dst_ref=refs[w].at[2 * px + py, 1 - c],
                                                         **sems)))
        return ici, d2d

    def start(ins, outs, send_sems, recv_sems):
        ici, _ = copies(outs, send_sems, recv_sems)
        for send, _ in ici:
            send.start()

    def finish(ins, outs, send_sems, recv_sems):
        ici, d2d = copies(outs, send_sems, recv_sems)
        for (_, landed), (forward, _) in zip(ici, d2d):
            landed.wait_recv()
            forward.start()
        for _, landed in d2d:
            landed.wait_recv()
        for send, _ in ici + d2d:
            send.wait_send()

    return Rider(arrays, [jax.ShapeDtypeStruct(a.shape, a.dtype) for a in arrays], {i: i for i in range(nw)},
                 2 * nici, start, finish)


def exchange_rider(psums):
    nw = len(psums)

    def copies(ins, outs, send_sems, recv_sems):
        x, y, c = _place()
        return [pltpu.make_async_remote_copy(
            src_ref=ins[w].at[2 * px + py], dst_ref=outs[w].at[j],
            send_sem=send_sems.at[w * NOTHER + j], recv_sem=recv_sems.at[w * NOTHER + j],
            device_id=(px, py, c), device_id_type=MESH)
            for w in range(nw) for j, (px, py) in enumerate(_other_chips(x, y))]

    def start(ins, outs, send_sems, recv_sems):
        for cp in copies(ins, outs, send_sems, recv_sems):
            cp.start()

    def finish(ins, outs, send_sems, recv_sems):
        for cp in copies(ins, outs, send_sems, recv_sems):
            cp.wait()

    return Rider(psums, [jax.ShapeDtypeStruct((NOTHER,) + p.shape[1:], p.dtype) for p in psums], {},
                 nw * NOTHER, start, finish)


def pair_rider(grads):
    nw = len(grads)

    def copies(ins, outs, send_sems, recv_sems):
        x, y, c = _place()
        return [pltpu.make_async_remote_copy(
            src_ref=ins[w].at[:, 1 - c], dst_ref=outs[w], send_sem=send_sems.at[w], recv_sem=recv_sems.at[w],
            device_id=(x, y, 1 - c), device_id_type=MESH) for w in range(nw)]

    def start(ins, outs, send_sems, recv_sems):
        for cp in copies(ins, outs, send_sems, recv_sems):
            cp.start()

    def finish(ins, outs, send_sems, recv_sems):
        for cp in copies(ins, outs, send_sems, recv_sems):
            cp.wait()

    return Rider(grads, [jax.ShapeDtypeStruct((g.shape[0],) + g.shape[2:], g.dtype) for g in grads], {},
                 nw, start, finish)


class _Offset:
    def __init__(self, ref, base):
        self.ref, self.base = ref, base

    @property
    def at(self):
        return self

    def __getitem__(self, i):
        return self.ref.at[self.base + i]


def merge_riders(riders):
    ins, outs, aliases, spans, nsem = [], [], {}, [], 0
    for r in riders:
        spans.append((len(ins), len(outs), nsem))
        aliases.update({len(ins) + i: len(outs) + j for i, j in r.aliases.items()})
        ins, outs, nsem = ins + r.ins, outs + r.outs, nsem + r.nsem

    def each(step):
        def run(in_refs, out_refs, send_sems, recv_sems):
            for r, (i0, o0, s0) in zip(riders, spans):
                getattr(r, step)(in_refs[i0:i0 + len(r.ins)], out_refs[o0:o0 + len(r.outs)],
                                 _Offset(send_sems, s0), _Offset(recv_sems, s0))
        return run

    return Rider(ins, outs, aliases, nsem, each("start"), each("finish"))


def split_results(riders, results):
    out, o0 = [], 0
    for r in riders:
        out.append(tuple(results[o0:o0 + len(r.outs)]))
        o0 += len(r.outs)
    return out


def swap_rider(halves):
    nw = len(halves)

    def copies(refs, send_sems, recv_sems):
        x, y, c = _place()
        out = []
        for w in range(nw):
            sems = dict(send_sem=send_sems.at[w], recv_sem=recv_sems.at[w],
                        device_id=(x, y, 1 - c), device_id_type=MESH)
            mine = refs[w].at[c]
            out.append((pltpu.make_async_remote_copy(src_ref=mine, dst_ref=mine, **sems),
                        pltpu.make_async_remote_copy(src_ref=mine, dst_ref=refs[w].at[1 - c], **sems)))
        return out

    def start(ins, outs, send_sems, recv_sems):
        for send, _ in copies(outs, send_sems, recv_sems):
            send.start()

    def finish(ins, outs, send_sems, recv_sems):
        cps = copies(outs, send_sems, recv_sems)
        for _, recv in cps:
            recv.wait_recv()
        for send, _ in cps:
            send.wait_send()

    return Rider(halves, [jax.ShapeDtypeStruct(h.shape, h.dtype) for h in halves], {i: i for i in range(nw)},
                 nw, start, finish)


NDEV = 8


def allgather_rider(slots):
    def copies(ref, send_sems, recv_sems):
        x, y, c = _place()
        mine = ref.at[4 * x + 2 * y + c]
        out = []
        for rel in range(1, NDEV):
            peer = (x ^ (rel >> 2), y ^ ((rel >> 1) & 1), c ^ (rel & 1))
            sems = dict(send_sem=send_sems.at[rel - 1], recv_sem=recv_sems.at[rel - 1],
                        device_id=peer, device_id_type=MESH)
            out.append((pltpu.make_async_remote_copy(src_ref=mine, dst_ref=mine, **sems),
                        pltpu.make_async_remote_copy(
                            src_ref=mine, dst_ref=ref.at[4 * peer[0] + 2 * peer[1] + peer[2]], **sems)))
        return out

    def start(ins, outs, send_sems, recv_sems):
        for send, _ in copies(outs[0], send_sems, recv_sems):
            send.start()

    def finish(ins, outs, send_sems, recv_sems):
        cps = copies(outs[0], send_sems, recv_sems)
        for _, recv in cps:
            recv.wait_recv()
        for send, _ in cps:
            send.wait_send()

    return Rider([slots], [jax.ShapeDtypeStruct(slots.shape, slots.dtype)], {0: 0}, NDEV - 1, start, finish)


def sum_slots(slots):
    def body(s_ref, o_ref):
        acc = s_ref[0]
        for dev in range(1, NDEV):
            acc = acc + s_ref[dev]
        o_ref[...] = acc

    return pl.pallas_call(body, name="sum_slots", out_shape=jax.ShapeDtypeStruct(slots.shape[1:], F32),
                          compiler_params=_cparams())(slots)


BIG = ("ffn1_w_gu", "ffn1_w_down", "w_in", "w_a_out", "w_b_out", "w_out", "w_q", "w_kv", "w_o",
       "ffn2_w_gu", "ffn2_w_down")
SMALL = {"ffn1_norm": (0, 1), "mix_norm": (8, 1), "xattn_norm": (16, 1), "mem_norm": (24, 1),
         "ffn2_norm": (32, 1), "final_norm": (40, 1), "conv_b": (48, 1), "conv_ln_g": (56, 1),
         "conv_ln_b": (64, 1), "sgu_ln_g": (72, 1), "sgu_ln_b": (80, 1), "b_in": (88, 6),
         "conv_w": (96, CW), "sgu_w": (128, 64), "sgu_b": (192, 1)}
LOSS_ROW = 200
SMALL_ROWS = 208


def _pad_rows(a, rows):
    return jnp.pad(a, ((0, rows - a.shape[0]), (0, D - a.shape[1])))


def _pack_small(parts):
    names = sorted(parts, key=lambda n: SMALL[n][0] if n in SMALL else LOSS_ROW)
    rows = []
    for i, n in enumerate(names):
        start = SMALL[n][0] if n in SMALL else LOSS_ROW
        end = SMALL_ROWS if i + 1 == len(names) else (SMALL[names[i + 1]][0] if names[i + 1] in SMALL else LOSS_ROW)
        rows.append(_pad_rows(parts[n], end - start))
    return jnp.concatenate(rows, axis=0)


def _small_views(w):
    return {
        "ffn1_norm": w["ffn1_norm"], "mix_norm": w["mix_norm"], "xattn_norm": w["xattn_norm"],
        "mem_norm": w["mem_norm"], "ffn2_norm": w["ffn2_norm"], "final_norm": w["final_norm"].reshape(1, D),
        "conv_b": w["conv_b"], "conv_ln_g": w["conv_ln_g"], "conv_ln_b": w["conv_ln_b"],
        "sgu_ln_g": w["sgu_ln_g"], "sgu_ln_b": w["sgu_ln_b"], "b_in": w["b_in"].reshape(6, D),
        "conv_w": w["conv_w"][0], "sgu_w": w["sgu_w"].reshape(64, D), "sgu_b": w["sgu_b"].reshape(1, NG * CHUNK),
    }


def _unpack_small(buf, like, chip):
    out = {}
    for n, (start, rows) in SMALL.items():
        blk = buf[start:start + rows]
        if n == "conv_w":
            blk = blk[:, :like[n].shape[-1]] if chip is None else lax.dynamic_slice_in_dim(
                blk, chip * like[n].shape[-1], like[n].shape[-1], axis=1)
        elif n == "sgu_b":
            blk = blk[:, :NG * CHUNK]
        out[n] = blk.reshape(like[n].shape)
    return out


def kernel(x, mem, ffn1_norm, ffn1_w_gu, ffn1_w_down, mix_norm, w_in, b_in, conv_w, conv_b, conv_ln_g, conv_ln_b, w_a_out, sgu_ln_g, sgu_ln_b, sgu_w, sgu_b, w_b_out, w_out, xattn_norm, mem_norm, w_q, w_kv, w_o, ffn2_norm, ffn2_w_gu, ffn2_w_down, final_norm, loss_target, m_ffn1_norm, m_ffn1_w_gu, m_ffn1_w_down, m_mix_norm, m_w_in, m_b_in, m_conv_w, m_conv_b, m_conv_ln_g, m_conv_ln_b, m_w_a_out, m_sgu_ln_g, m_sgu_ln_b, m_sgu_w, m_sgu_b, m_w_b_out, m_w_out, m_xattn_norm, m_mem_norm, m_w_q, m_w_kv, m_w_o, m_ffn2_norm, m_ffn2_w_gu, m_ffn2_w_down, m_final_norm, v_ffn1_norm, v_ffn1_w_gu, v_ffn1_w_down, v_mix_norm, v_w_in, v_b_in, v_conv_w, v_conv_b, v_conv_ln_g, v_conv_ln_b, v_w_a_out, v_sgu_ln_g, v_sgu_ln_b, v_sgu_w, v_sgu_b, v_w_b_out, v_w_out, v_xattn_norm, v_mem_norm, v_w_q, v_w_kv, v_w_o, v_ffn2_norm, v_ffn2_w_gu, v_ffn2_w_down, v_final_norm):
    names = ("ffn1_norm", "ffn1_w_gu", "ffn1_w_down", "mix_norm", "w_in", "b_in", "conv_w", "conv_b",
             "conv_ln_g", "conv_ln_b", "w_a_out", "sgu_ln_g", "sgu_ln_b", "sgu_w", "sgu_b", "w_b_out", "w_out",
             "xattn_norm", "mem_norm", "w_q", "w_kv", "w_o", "ffn2_norm", "ffn2_w_gu", "ffn2_w_down",
             "final_norm")
    wts = dict(zip(names, (ffn1_norm, ffn1_w_gu, ffn1_w_down, mix_norm, w_in, b_in, conv_w, conv_b, conv_ln_g,
                           conv_ln_b, w_a_out, sgu_ln_g, sgu_ln_b, sgu_w, sgu_b, w_b_out, w_out, xattn_norm,
                           mem_norm, w_q, w_kv, w_o, ffn2_norm, ffn2_w_gu, ffn2_w_down, final_norm)))
    mom1 = dict(zip(names, (m_ffn1_norm, m_ffn1_w_gu, m_ffn1_w_down, m_mix_norm, m_w_in, m_b_in, m_conv_w,
                            m_conv_b, m_conv_ln_g, m_conv_ln_b, m_w_a_out, m_sgu_ln_g, m_sgu_ln_b, m_sgu_w,
                            m_sgu_b, m_w_b_out, m_w_out, m_xattn_norm, m_mem_norm, m_w_q, m_w_kv, m_w_o,
                            m_ffn2_norm, m_ffn2_w_gu, m_ffn2_w_down, m_final_norm)))
    mom2 = dict(zip(names, (v_ffn1_norm, v_ffn1_w_gu, v_ffn1_w_down, v_mix_norm, v_w_in, v_b_in, v_conv_w,
                            v_conv_b, v_conv_ln_g, v_conv_ln_b, v_w_a_out, v_sgu_ln_g, v_sgu_ln_b, v_sgu_w,
                            v_sgu_b, v_w_b_out, v_w_out, v_xattn_norm, v_mem_norm, v_w_q, v_w_kv, v_w_o,
                            v_ffn2_norm, v_ffn2_w_gu, v_ffn2_w_down, v_final_norm)))
    xi, yi, ci = _place()
    chip = (2 * xi + yi).astype(jnp.int32)
    core = ci.astype(jnp.int32)
    core_arr = core.reshape(1)
    chip_arr = chip.reshape(1)
    place_arr = jnp.stack([chip, core])
    x2, mem2, tgt = x[0], mem[0], loss_target[0]

    slot = {n: cast_bf16(wts[n][0], chip_arr, "cast_" + n) for n in BIG}
    cw_pad = jnp.pad(conv_w[0], ((0, HALO - CW), (0, 0)))
    slot["conv_w"] = lax.dynamic_update_slice(jnp.zeros((NSH,) + cw_pad.shape, F32), cw_pad[None], (chip, 0, 0))
    g_first = ("ffn1_w_gu", "ffn1_w_down")
    g_mix = ("w_in", "w_a_out", "w_b_out", "w_out", "conv_w")
    g_rest = ("w_q", "w_kv", "w_o", "ffn2_w_gu", "ffn2_w_down")
    def gather(group):
        return gather_rider([slot[n].reshape(NSH, 2, slot[n].shape[1] // 2, slot[n].shape[2]) for n in group])

    def gathered(group, res):
        return {n: r.reshape(slot[n].shape) for n, r in zip(group, res)}

    full = gathered(g_first, run_rider(gather(g_first), "gather_ffn1"))
    wgu1, wd1 = full["ffn1_w_gu"], full["ffn1_w_down"].reshape(FF, D)
    tril = jnp.tril(jnp.ones((CHUNK, CHUNK), dtype=bool))
    ws = jnp.where(tril[None], sgu_w[0], 0.0).astype(BF)
    wst = jnp.transpose(ws, (0, 2, 1))
    sbias = jnp.repeat(jnp.transpose(sgu_b[0]), GD, axis=1)
    gfin = final_norm.reshape(1, D)

    (h1, gu1), rode = ffn_fwd(x2, ffn1_norm, wgu1, wd1, rider=gather(g_mix))
    full.update(gathered(g_mix, rode))
    win = full["w_in"]
    wa, wb, wout = (full[n].reshape(D, D) for n in ("w_a_out", "w_b_out", "w_out"))
    cw_full = jnp.transpose(full["conv_w"], (1, 0, 2)).reshape(HALO, D)
    (h2, proj, n2b, conv_out), rode = mix_fwd(
        h1, mix_norm, win, b_in, cw_full, conv_b, conv_ln_g, conv_ln_b, wa, sgu_ln_g, sgu_ln_b, ws, sbias, wb,
        wout, rider=gather(g_rest))
    full.update(gathered(g_rest, rode))
    wgu2, wd2, wkv = full["ffn2_w_gu"], full["ffn2_w_down"].reshape(FF, D), full["w_kv"]
    wq, wo = full["w_q"].reshape(D, D), full["w_o"].reshape(D, D)
    kb, vb, memn = kv_proj(mem2, mem_norm, wkv)
    h3 = xattn_fwd(h2, xattn_norm, wq, kb, vb, wo)
    dh4, gu2, loss_lanes, d_final = ffn_fwd_loss(h3, ffn2_norm, wgu2, wd2, gfin, tgt)

    def halves_of(n, g):
        rs, cs = wts[n].shape[1:]
        return g.reshape(NSH, 2, rs // 2, cs)

    def pair_adds(group, g4s, recvs):
        sums = [pair_add(g, r, core_arr, "pair_add_" + n) for n, g, r in zip(group, g4s, recvs)]
        return [s[0] for s in sums], exchange_rider([s[1] for s in sums])

    halves = {}

    def chip_sums(group, psums, recv):
        for n, p, r in zip(group, psums, recv):
            halves[n] = chip_sum(p, r, place_arr, "chip_sum_" + n)

    dh3, n4, a4, dgu4, dhb4, d_ffn2n = ffn_bwd(h3, gu2, dh4, ffn2_norm, wgu2, wd2, "ffn2_bwd")
    g_ffn2 = ("ffn2_w_gu", "ffn2_w_down")
    g_gu = halves_of("ffn2_w_gu", dw_matmul(n4, dgu4, NSH, "dw_ffn2_gu"))
    d_down, recv_gu = dw_matmul(a4, dhb4, 1, "dw_ffn2_down", rider=pair_rider([g_gu]))
    g_down = halves_of("ffn2_w_down", d_down)
    recv_down = run_rider(pair_rider([g_down]), "pair_exchange_ffn2")
    ps_ffn2, ride = pair_adds(g_ffn2, [g_gu, g_down], [recv_gu[0], recv_down[0]])
    (dh2, n3, dq, att, dhb3, dk, dv, d_xn), rode = xattn_bwd(h2, dh3, xattn_norm, wq, kb, vb, wo, rider=ride)
    chip_sums(g_ffn2, ps_ffn2, rode)
    g_att = ("w_q", "w_o", "w_kv")
    d_wkv, d_memn = kv_bwd(mem2, mem_norm, memn, wkv, dk, dv)
    g4s = [halves_of(n, g) for n, g in zip(g_att, [dw_matmul(n3, dq, 1, "dw_q"), dw_matmul(att, dhb3, 1, "dw_o"),
                                                     d_wkv])]
    ps_att, ride = pair_adds(g_att, g4s, run_rider(pair_rider(g4s), "pair_exchange_att"))
    ((dconv, dproj, sa, dya, ob, dyb, mg, dhb2, d_sgu_w, d_sgu_b, d_lna_g, d_lna_b, d_lnb_g, d_lnb_b),
     rode) = mix_bwd_branches(proj, conv_out, dh2, conv_ln_g, conv_ln_b, wa, sgu_ln_g, sgu_ln_b, ws, wst,
                              sbias, wb, wout, rider=ride)
    chip_sums(g_att, ps_att, rode)
    dproj, d_conv_w, d_conv_b = conv_bwd(proj, dconv, dproj, cw_full)
    dh1, d_mixn, d_b_in = mix_bwd_in(dproj, h1, dh2, mix_norm, win)
    g_mixw = ("w_in", "w_a_out", "w_b_out")
    g_in = halves_of("w_in", dw_matmul(n2b, dproj, NSH, "dw_in"))
    d_a, recv_in = dw_matmul(sa, dya, 1, "dw_a_out", rider=pair_rider([g_in]))
    g4s = [halves_of(n, g) for n, g in zip(g_mixw[1:], [d_a, dw_matmul(ob, dyb, 1, "dw_b_out")])]
    ps_mix, ride = pair_adds(g_mixw, [g_in] + g4s,
                             [recv_in[0]] + list(run_rider(pair_rider(g4s), "pair_exchange_mix")))
    dx, n1, a1, dgu1, dhb1, d_ffn1n = ffn_bwd(x2, gu1, dh1, ffn1_norm, wgu1, wd1, "ffn1_bwd")
    d_wgu1, rode = dw_matmul(n1, dgu1, NSH, "dw_ffn1_gu", rider=ride)
    chip_sums(g_mixw, ps_mix, rode)
    small_grads = {
        "ffn1_norm": d_ffn1n, "mix_norm": d_mixn, "xattn_norm": d_xn, "mem_norm": d_memn, "ffn2_norm": d_ffn2n,
        "final_norm": d_final, "conv_b": d_conv_b, "conv_ln_g": d_lna_g, "conv_ln_b": d_lna_b,
        "sgu_ln_g": d_lnb_g, "sgu_ln_b": d_lnb_b, "b_in": d_b_in.reshape(6, D), "conv_w": d_conv_w[:CW],
        "sgu_w": d_sgu_w.reshape(64, D), "sgu_b": jnp.transpose(d_sgu_b[:, :NG]).reshape(1, NG * CHUNK),
        "loss": loss_lanes}
    slots = lax.dynamic_update_slice(jnp.zeros((NDEV, SMALL_ROWS, D), F32), _pack_small(small_grads)[None],
                                     (2 * chip + core, 0, 0))
    def last_link(n, grad, tag):
        g4 = halves_of(n, grad)
        return pair_adds((n,), [g4], run_rider(pair_rider([g4]), "pair_exchange_" + tag))

    ps_gu, ride_gu = last_link("ffn1_w_gu", d_wgu1, "ffn1_gu")
    riders = [ride_gu, allgather_rider(slots)]
    d_down, rode = dw_matmul(a1, dhb1, 1, "dw_ffn1_down", rider=merge_riders(riders))
    recv_gu, all_slots = split_results(riders, rode)
    chip_sums(("ffn1_w_gu",), ps_gu, recv_gu)
    ps_down, ride_down = last_link("ffn1_w_down", d_down, "ffn1_down")
    d_out, rode = dw_matmul(mg, dhb2, 1, "dw_out", rider=ride_down)
    chip_sums(("ffn1_w_down",), ps_down, rode)
    ps_out, ride_out = last_link("w_out", d_out, "out")
    chip_sums(("w_out",), ps_out, run_rider(ride_out, "chip_exchange_out"))
    swapped = run_rider(swap_rider([halves[n] for n in BIG]), "pair_swap")
    gshard = {n: g.reshape(wts[n].shape[1:]) for n, g in zip(BIG, swapped)}

    small = sum_slots(all_slots[0])
    loss = (0.5 / D) * jnp.sum(small[LOSS_ROW])
    gsmall = _unpack_small(small, wts, chip)

    out_g, out_d, out_m, out_v = dict(gsmall), {}, {}, {}
    sw, sm, sv = (_pack_small(_small_views(t))[:LOSS_ROW] for t in (wts, mom1, mom2))
    sg = _pack_small(_small_views({n: gsmall[n] for n in SMALL}))[:LOSS_ROW]
    for dst, packed in zip((out_d, out_m, out_v), adamw(sw, sg, sm, sv, "adamw_small")):
        dst.update(_unpack_small(packed, wts, None))
    for n in BIG:
        shape = wts[n].shape
        out_g[n] = gshard[n].reshape(shape)
        d, mn, vn = adamw(wts[n][0], gshard[n], mom1[n][0], mom2[n][0], "adamw_" + n)
        out_d[n], out_m[n], out_v[n] = d.reshape(shape), mn.reshape(shape), vn.reshape(shape)
    return (loss, dx[None], *[out_g[n] for n in names], *[out_d[n] for n in names],
            *[out_m[n] for n in names], *[out_v[n] for n in names])
```

```python
import functools
import math

import jax
import jax.numpy as jnp
from jax import lax
from jax.experimental import pallas as pl
from jax.experimental.pallas import tpu as pltpu

F32 = jnp.float32
BF = jnp.bfloat16
MESH = pl.DeviceIdType.MESH

D = 1024
FF = 2816
HC = FF // 2
NSH = 4
DIN = 6 * D
INB = DIN // NSH
CW = 31
HALO = 32
CHUNK = 128
NG = 4
GD = D // NG
NH = 4
HD = D // NH
NMEM = 256
EPS_RMS = 1e-6
EPS_LN = 1e-5
GELU_C0 = math.sqrt(2.0 / math.pi)
GELU_C1 = 0.044715
ATT_SCALE = 1.0 / math.sqrt(HD)

ADAM_LR = 0.001
ADAM_B1 = 0.9
ADAM_B2 = 0.999
ADAM_EPS = 1e-08
ADAM_WD = 0.01
ADAM_STEP = 10

VMEM_LIMIT = 56 * 1024 * 1024


def _cparams(sem=None, **kw):
    if sem is not None:
        kw["dimension_semantics"] = sem
    return pltpu.CompilerParams(vmem_limit_bytes=VMEM_LIMIT, **kw)


def _dot(a, b):
    return jnp.dot(a, b, preferred_element_type=F32)


def _dot_nt(a, b):
    return lax.dot_general(a, b, (((1,), (1,)), ((), ())), preferred_element_type=F32)


def _dot_tn(a, b):
    return lax.dot_general(a, b, (((0,), (0,)), ((), ())), preferred_element_type=F32)


def _sigmoid(x):
    return 1.0 / (1.0 + jnp.exp(-x))


def _gelu(x):
    t = jnp.tanh(GELU_C0 * (x + GELU_C1 * (x * x * x)))
    return 0.5 * x * (1.0 + t), t


def _gelu_with_grad(x):
    x2 = x * x
    t = jnp.tanh(GELU_C0 * (x + GELU_C1 * (x2 * x)))
    onep = 1.0 + t
    hx = 0.5 * x
    grad = 0.5 * onep + hx * (1.0 - t * t) * (GELU_C0 + (3.0 * GELU_C0 * GELU_C1) * x2)
    return hx * onep, grad


def _mean(x):
    return jnp.mean(x, axis=-1, keepdims=True)


def _rms(x):
    r = lax.rsqrt(_mean(x * x) + EPS_RMS)
    return x * r, r


def _rms_bwd(dn, xh, r, g):
    dxh = dn * g
    return r * (dxh - xh * _mean(dxh * xh))


def _ln(x):
    xc = x - _mean(x)
    r = lax.rsqrt(_mean(xc * xc) + EPS_LN)
    return xc * r, r


def _ln_bwd(dy, xh, r, g):
    dxh = dy * g
    return r * (dxh - _mean(dxh) - xh * _mean(dxh * xh))


def _colsum(x):
    return jnp.sum(x, axis=0, keepdims=True)


def _const_spec(shape):
    nd = len(shape)
    return pl.BlockSpec(shape, lambda *_: (0,) * nd, pipeline_mode=pl.Buffered(1))


def _row_spec(ts, width):
    return pl.BlockSpec((ts, width), lambda i: (i, 0))


def _acc_spec(shape):
    nd = len(shape)
    return pl.BlockSpec(shape, lambda *_: (0,) * nd)


def _tile(s, want):
    return min(s, want)


HBM_SPEC = pl.BlockSpec(memory_space=pltpu.HBM)


class Rider:
    def __init__(self, ins, outs, aliases, nsem, start, finish):
        self.ins, self.outs, self.aliases, self.nsem = list(ins), list(outs), dict(aliases), nsem
        self.start, self.finish = start, finish


def _pcall(body, *, name, grid, args, in_specs, out_shape, out_specs, scratch=(), rider=None):
    sem = ("arbitrary",) * len(grid)
    n_in, n_out = len(args), len(out_shape)
    if rider is None:
        res = pl.pallas_call(
            body, grid=grid, name=name, out_shape=tuple(out_shape), in_specs=list(in_specs),
            out_specs=tuple(out_specs), scratch_shapes=list(scratch), compiler_params=_cparams(sem))(*args)
        return tuple(res), ()
    r_in, r_out = len(rider.ins), len(rider.outs)

    def wrapped(*refs):
        a, ri = refs[:n_in], refs[n_in:n_in + r_in]
        o = refs[n_in + r_in:n_in + r_in + n_out]
        ro = refs[n_in + r_in + n_out:n_in + r_in + n_out + r_out]
        s, (send, recv) = refs[n_in + r_in + n_out + r_out:-2], refs[-2:]
        first = functools.reduce(jnp.logical_and, [pl.program_id(d) == 0 for d in range(len(grid))])
        last = functools.reduce(jnp.logical_and, [pl.program_id(d) == g - 1 for d, g in enumerate(grid)])

        @pl.when(first)
        def _():
            rider.start(ri, ro, send, recv)

        body(*a, *o, *s)

        @pl.when(last)
        def _():
            rider.finish(ri, ro, send, recv)

    res = pl.pallas_call(
        wrapped, grid=grid, name=name, out_shape=tuple(out_shape) + tuple(rider.outs),
        in_specs=list(in_specs) + [HBM_SPEC] * r_in, out_specs=tuple(out_specs) + (HBM_SPEC,) * r_out,
        scratch_shapes=list(scratch) + [pltpu.SemaphoreType.DMA((rider.nsem,)),
                                        pltpu.SemaphoreType.DMA((rider.nsem,))],
        input_output_aliases={n_in + i: n_out + j for i, j in rider.aliases.items()},
        compiler_params=_cparams(sem, has_side_effects=True))(*args, *rider.ins)
    return tuple(res[:n_out]), tuple(res[n_out:])


def run_rider(rider, name):
    r_in = len(rider.ins)

    def body(*refs):
        ri, ro, (send, recv) = refs[:r_in], refs[r_in:-2], refs[-2:]
        rider.start(ri, ro, send, recv)
        rider.finish(ri, ro, send, recv)

    return pl.pallas_call(
        body, name=name, out_shape=tuple(rider.outs), in_specs=[HBM_SPEC] * r_in,
        out_specs=(HBM_SPEC,) * len(rider.outs),
        scratch_shapes=[pltpu.SemaphoreType.DMA((rider.nsem,)), pltpu.SemaphoreType.DMA((rider.nsem,))],
        input_output_aliases=rider.aliases,
        compiler_params=pltpu.CompilerParams(has_side_effects=True))(*rider.ins)


FFN_BWD_TILE = 256


def _ffn_apply(x, g_ref, wgu_ref, wd_ref, gu_ref):
    xh, _ = _rms(x)
    nb = (xh * g_ref[...]).astype(BF)
    acc = jnp.zeros(x.shape, F32)
    for j in range(2):
        g = _dot(nb, wgu_ref[j])
        u = _dot(nb, wgu_ref[2 + j])
        gu_ref[:, j * HC:(j + 1) * HC] = g.astype(BF)
        gu_ref[:, FF + j * HC:FF + (j + 1) * HC] = u.astype(BF)
        a = (g * _sigmoid(g) * u).astype(BF)
        acc = acc + _dot(a, wd_ref[j * HC:(j + 1) * HC, :])
    return x + 0.5 * acc


def ffn_fwd(h, gain, wgu, wd, rider=None):
    s = h.shape[0]
    ts = _tile(s, 512)

    def body(h_ref, g_ref, wgu_ref, wd_ref, o_ref, gu_ref):
        o_ref[...] = _ffn_apply(h_ref[...], g_ref, wgu_ref, wd_ref, gu_ref)

    return _pcall(
        body, grid=(s // ts,), name="ffn1_fwd", args=(h, gain, wgu, wd),
        out_shape=[jax.ShapeDtypeStruct((s, D), F32), jax.ShapeDtypeStruct((s, 2 * FF), BF)],
        in_specs=[_row_spec(ts, D), _const_spec((1, D)), _const_spec((NSH, D, HC)), _const_spec((FF, D))],
        out_specs=[_row_spec(ts, D), _row_spec(ts, 2 * FF)], rider=rider)


def ffn_fwd_loss(h, gain, wgu, wd, gfin, target):
    s = h.shape[0]
    ts = _tile(s, 512)

    def body(h_ref, g_ref, wgu_ref, wd_ref, gf_ref, t_ref, dh_ref, gu_ref, loss_ref, dgf_ref):
        @pl.when(pl.program_id(0) == 0)
        def _():
            loss_ref[...] = jnp.zeros_like(loss_ref)
            dgf_ref[...] = jnp.zeros_like(dgf_ref)

        h4 = _ffn_apply(h_ref[...], g_ref, wgu_ref, wd_ref, gu_ref)
        yh, r4 = _rms(h4)
        gf = gf_ref[...]
        e = yh * gf - t_ref[...]
        loss_ref[...] += _colsum(e * e)
        dy = e * (1.0 / D)
        dgf_ref[...] += _colsum(dy * yh)
        dh_ref[...] = _rms_bwd(dy, yh, r4, gf)

    return pl.pallas_call(
        body, grid=(s // ts,), name="ffn_fwd_loss",
        out_shape=(jax.ShapeDtypeStruct((s, D), F32), jax.ShapeDtypeStruct((s, 2 * FF), BF),
                   jax.ShapeDtypeStruct((1, D), F32), jax.ShapeDtypeStruct((1, D), F32)),
        in_specs=[_row_spec(ts, D), _const_spec((1, D)), _const_spec((NSH, D, HC)), _const_spec((FF, D)),
                  _const_spec((1, D)), _row_spec(ts, D)],
        out_specs=(_row_spec(ts, D), _row_spec(ts, 2 * FF), _acc_spec((1, D)), _acc_spec((1, D))),
        compiler_params=_cparams(("arbitrary",)),
    )(h, gain, wgu, wd, gfin, target)


def ffn_bwd(h, gu, dh, gain, wgu, wd, name):
    s = h.shape[0]
    ts = _tile(s, FFN_BWD_TILE)

    def body(h_ref, gu_ref, dh_ref, g_ref, wgu_ref, wd_ref, dx_ref, n_ref, a_ref, dgu_ref, dhb_ref, dg_ref):
        @pl.when(pl.program_id(0) == 0)
        def _():
            dg_ref[...] = jnp.zeros_like(dg_ref)

        x = h_ref[...]
        dh = dh_ref[...]
        gain_v = g_ref[...]
        xh, r = _rms(x)
        n_ref[...] = (xh * gain_v).astype(BF)
        dhb = (0.5 * dh).astype(BF)
        dhb_ref[...] = dhb
        dn = jnp.zeros((ts, D), F32)
        for j in range(2):
            g = gu_ref[:, j * HC:(j + 1) * HC].astype(F32)
            u = gu_ref[:, FF + j * HC:FF + (j + 1) * HC].astype(F32)
            sg = _sigmoid(g)
            sl = g * sg
            a_ref[:, j * HC:(j + 1) * HC] = (sl * u).astype(BF)
            da = _dot_nt(dhb, wd_ref[j * HC:(j + 1) * HC, :])
            dgb = (da * u * (sg * (1.0 + g * (1.0 - sg)))).astype(BF)
            dub = (da * sl).astype(BF)
            dgu_ref[:, j * HC:(j + 1) * HC] = dgb
            dgu_ref[:, FF + j * HC:FF + (j + 1) * HC] = dub
            dn = dn + _dot_nt(dgb, wgu_ref[j]) + _dot_nt(dub, wgu_ref[2 + j])
        dg_ref[...] += _colsum(dn * xh)
        dx_ref[...] = dh + _rms_bwd(dn, xh, r, gain_v)

    return pl.pallas_call(
        body, grid=(s // ts,), name=name,
        out_shape=(jax.ShapeDtypeStruct((s, D), F32), jax.ShapeDtypeStruct((s, D), BF),
                   jax.ShapeDtypeStruct((s, FF), BF), jax.ShapeDtypeStruct((s, 2 * FF), BF),
                   jax.ShapeDtypeStruct((s, D), BF), jax.ShapeDtypeStruct((1, D), F32)),
        in_specs=[_row_spec(ts, D), _row_spec(ts, 2 * FF), _row_spec(ts, D), _const_spec((1, D)),
                  _const_spec((NSH, D, HC)), _const_spec((FF, D))],
        out_specs=(_row_spec(ts, D), _row_spec(ts, D), _row_spec(ts, FF), _row_spec(ts, 2 * FF),
                   _row_spec(ts, D), _acc_spec((1, D))),
        compiler_params=_cparams(("arbitrary",)),
    )(h, gu, dh, gain, wgu, wd)


def dw_matmul(x, dy, nsplit, name, rider=None):
    s, k = x.shape
    n = dy.shape[1]
    nb = n // nsplit
    ts = _tile(s, 1024)

    def body(x_ref, dy_ref, o_ref):
        @pl.when(pl.program_id(1) == 0)
        def _():
            o_ref[...] = jnp.zeros_like(o_ref)

        o_ref[0] += _dot_tn(x_ref[...], dy_ref[...])

    (out,), rode = _pcall(
        body, grid=(nsplit, s // ts), name=name, args=(x, dy),
        out_shape=[jax.ShapeDtypeStruct((nsplit, k, nb), F32)],
        in_specs=[pl.BlockSpec((ts, k), lambda j, i: (i, 0)), pl.BlockSpec((ts, nb), lambda j, i: (i, j))],
        out_specs=[pl.BlockSpec((1, k, nb), lambda j, i: (j, 0, 0))], rider=rider)
    return (out, rode) if rider is not None else out


def _split_in_proj(p, b):
    h = INB - D
    a_val = p[0][:, :D] + b[:, 0:D]
    a_gate = jnp.concatenate([p[0][:, D:], p[1][:, :h]], axis=1) + b[:, D:2 * D]
    b_u = p[1][:, h:] + b[:, 2 * D:3 * D]
    b_v = p[2][:, :D] + b[:, 3 * D:4 * D]
    g_a = jnp.concatenate([p[2][:, D:], p[3][:, :h]], axis=1) + b[:, 4 * D:5 * D]
    g_b = p[3][:, h:] + b[:, 5 * D:6 * D]
    return a_val, a_gate, b_u, b_v, g_a, g_b


def _sgu_mix(vnb, ws_ref, sb_ref, mixed_ref, ts):
    for ci in range(ts // CHUNK):
        rows = slice(ci * CHUNK, (ci + 1) * CHUNK)
        for g in range(NG):
            cols = slice(g * GD, (g + 1) * GD)
            mixed_ref[rows, cols] = _dot(ws_ref[g], vnb[rows, cols]) + sb_ref[:, cols]


SUB = 8
CB = 128
SH_ROWS_EXTRA = HALO - SUB


def _shifted_copies(ext_ref, sh_ref, lanes, ts):
    for b in range(1, SUB):
        sh_ref[b - 1] = ext_ref[b:b + ts + SH_ROWS_EXTRA, lanes]


def _window(ext_ref, sh_ref, lanes, first, r0, nrows):
    b = first % SUB
    a = first - b
    if b == 0:
        return ext_ref[a + r0:a + r0 + nrows, lanes]
    return sh_ref[b - 1, a + r0:a + r0 + nrows, :]


def mix_fwd(h, gain, win, b_in, conv_w, conv_b, lna_g, lna_b, wa, lnb_g, lnb_b, ws, sbias, wb, wo, rider=None):
    s = h.shape[0]
    ts = _tile(s, 256)

    def body(h_ref, g_ref, win_ref, bin_ref, cw_ref, cb_ref, lag_ref, lab_ref, wa_ref, lbg_ref, lbb_ref,
             ws_ref, sb_ref, wb_ref, wo_ref, o_ref, p_ref, n_ref, c_ref, ext_ref, mixed_ref, sh_ref):
        @pl.when(pl.program_id(0) == 0)
        def _():
            ext_ref[0:HALO, :] = jnp.zeros((HALO, D), F32)

        x = h_ref[...]
        xh, _ = _rms(x)
        nb = (xh * g_ref[...]).astype(BF)
        n_ref[...] = nb
        b = bin_ref[...]
        p = []
        for k in range(NSH):
            pk = _dot(nb, win_ref[k])
            p_ref[:, k * INB:(k + 1) * INB] = (pk + b[:, k * INB:(k + 1) * INB]).astype(BF)
            p.append(pk)
        a_val, a_gate, b_u, b_v, g_a, g_b = _split_in_proj(p, b)
        ext_ref[HALO:HALO + ts, :] = a_val * _sigmoid(a_gate)
        for l0 in range(0, D, CB):
            lanes = slice(l0, l0 + CB)
            _shifted_copies(ext_ref, sh_ref, lanes, ts)
            for r0 in range(0, ts, CB):
                acc = jnp.zeros((CB, CB), F32) + cb_ref[:, lanes]
                for k in range(CW):
                    acc = acc + cw_ref[k:k + 1, lanes] * _window(ext_ref, sh_ref, lanes,
                                                                 HALO - (CW - 1) + k, r0, CB)
                c_ref[r0:r0 + CB, lanes] = acc
        ext_ref[0:HALO, :] = ext_ref[ts:ts + HALO, :]
        ch, _ = _ln(c_ref[...])
        la = ch * lag_ref[...] + lab_ref[...]
        sa = (la * _sigmoid(la)).astype(BF)
        ya = _dot(sa, wa_ref[...])
        ub, _ = _gelu(b_u)
        gv, _ = _gelu(b_v)
        vh, _ = _ln(gv)
        vnb = (vh * lbg_ref[...] + lbb_ref[...]).astype(BF)
        _sgu_mix(vnb, ws_ref, sb_ref, mixed_ref, ts)
        ob = (ub * mixed_ref[...]).astype(BF)
        yb = _dot(ob, wb_ref[...])
        merged = (_sigmoid(g_a) * ya + _sigmoid(g_b) * yb).astype(BF)
        o_ref[...] = x + _dot(merged, wo_ref[...])

    vec = _const_spec((1, D))
    sq = _const_spec((D, D))
    return _pcall(
        body, grid=(s // ts,), name="mix_fwd",
        args=(h, gain, win, b_in, conv_w, conv_b, lna_g, lna_b, wa, lnb_g, lnb_b, ws, sbias, wb, wo),
        out_shape=(jax.ShapeDtypeStruct((s, D), F32), jax.ShapeDtypeStruct((s, DIN), BF),
                   jax.ShapeDtypeStruct((s, D), BF), jax.ShapeDtypeStruct((s, D), F32)),
        in_specs=[_row_spec(ts, D), vec, _const_spec((NSH, D, INB)), _const_spec((1, DIN)),
                  _const_spec((HALO, D)), vec, vec, vec, sq, vec, vec,
                  _const_spec((NG, CHUNK, CHUNK)), _const_spec((CHUNK, D)), sq, sq],
        out_specs=(_row_spec(ts, D), _row_spec(ts, DIN), _row_spec(ts, D), _row_spec(ts, D)),
        scratch=[pltpu.VMEM((ts + HALO, D), F32), pltpu.VMEM((ts, D), F32),
                 pltpu.VMEM((SUB - 1, ts + SH_ROWS_EXTRA, CB), F32)], rider=rider)


def mix_bwd_branches(p, c, dh, lna_g, lna_b, wa, lnb_g, lnb_b, ws, wst, sbias, wb, wo, rider=None):
    s = dh.shape[0]
    ts = _tile(s, 256)
    nsteps = s // ts

    def body(p_ref, c_ref, dh_ref, lag_ref, lab_ref, wa_ref, lbg_ref, lbb_ref, ws_ref, wst_ref, sb_ref,
             wb_ref, wo_ref, dc_ref, dp_ref, sa_ref, dya_ref, ob_ref, dyb_ref, mg_ref, dhb_ref,
             dws_ref, dsb_ref, dlag_ref, dlab_ref, dlbg_ref, dlbb_ref, mixed_ref, dmix_ref, dvn_ref, dsb_acc):
        step = pl.program_id(0)

        @pl.when(step == 0)
        def _():
            for ref in (dws_ref, dsb_acc, dlag_ref, dlab_ref, dlbg_ref, dlbb_ref):
                ref[...] = jnp.zeros_like(ref)

        b_u = p_ref[:, 2 * D:3 * D].astype(F32)
        b_v = p_ref[:, 3 * D:4 * D].astype(F32)
        sga = _sigmoid(p_ref[:, 4 * D:5 * D].astype(F32))
        sgb = _sigmoid(p_ref[:, 5 * D:6 * D].astype(F32))
        lag = lag_ref[...]
        ch, ra = _ln(c_ref[...])
        la = ch * lag + lab_ref[...]
        sla = _sigmoid(la)
        sa = (la * sla).astype(BF)
        sa_ref[...] = sa
        ya = _dot(sa, wa_ref[...])
        lbg = lbg_ref[...]
        ub, dub = _gelu_with_grad(b_u)
        gv, dgv = _gelu_with_grad(b_v)
        vh, rb = _ln(gv)
        vnb = (vh * lbg + lbb_ref[...]).astype(BF)
        _sgu_mix(vnb, ws_ref, sb_ref, mixed_ref, ts)
        mixed = mixed_ref[...]
        ob = (ub * mixed).astype(BF)
        ob_ref[...] = ob
        yb = _dot(ob, wb_ref[...])
        mg_ref[...] = (sga * ya + sgb * yb).astype(BF)
        dhb = dh_ref[...].astype(BF)
        dhb_ref[...] = dhb
        dm = _dot_nt(dhb, wo_ref[...])
        dp_ref[:, 0:2 * D] = jnp.zeros((ts, 2 * D), BF)
        dp_ref[:, 4 * D:5 * D] = (dm * ya * sga * (1.0 - sga)).astype(BF)
        dp_ref[:, 5 * D:6 * D] = (dm * yb * sgb * (1.0 - sgb)).astype(BF)
        dya = (dm * sga).astype(BF)
        dya_ref[...] = dya
        dyb = (dm * sgb).astype(BF)
        dyb_ref[...] = dyb
        dla = _dot_nt(dya, wa_ref[...]) * (sla * (1.0 + la * (1.0 - sla)))
        dlag_ref[...] += _colsum(dla * ch)
        dlab_ref[...] += _colsum(dla)
        dc_ref[...] = _ln_bwd(dla, ch, ra, lag)
        dob = _dot_nt(dyb, wb_ref[...])
        dp_ref[:, 2 * D:3 * D] = (dob * mixed * dub).astype(BF)
        dmix = dob * ub
        dmix_ref[...] = dmix.astype(BF)
        dsb = jnp.zeros((CHUNK, D), F32)
        for ci in range(ts // CHUNK):
            rows = slice(ci * CHUNK, (ci + 1) * CHUNK)
            dsb = dsb + dmix[rows, :]
            for g in range(NG):
                cols = slice(g * GD, (g + 1) * GD)
                dmb = dmix_ref[rows, cols]
                dws_ref[g] += _dot_nt(dmb, vnb[rows, cols])
                dvn_ref[rows, cols] = _dot(wst_ref[g], dmb)
        dsb_acc[...] += dsb
        dvn = dvn_ref[...]
        dlbg_ref[...] += _colsum(dvn * vh)
        dlbb_ref[...] += _colsum(dvn)
        dp_ref[:, 3 * D:4 * D] = (_ln_bwd(dvn, vh, rb, lbg) * dgv).astype(BF)

        @pl.when(step == nsteps - 1)
        def _():
            row = lax.broadcasted_iota(jnp.int32, (CHUNK, CHUNK), 0)
            col = lax.broadcasted_iota(jnp.int32, (CHUNK, CHUNK), 1)
            for g in range(NG):
                dws_ref[g] = jnp.where(col <= row, dws_ref[g], 0.0)
            acc = jnp.zeros((CHUNK, CHUNK), F32)
            for g in range(NG):
                tot = jnp.sum(dsb_acc[:, g * GD:(g + 1) * GD], axis=-1, keepdims=True)
                acc = acc + jnp.where(col == g, tot, 0.0)
            dsb_ref[...] = acc

    vec = _const_spec((1, D))
    sq = _const_spec((D, D))
    bf_rows = jax.ShapeDtypeStruct((s, D), BF)
    acc_vec = jax.ShapeDtypeStruct((1, D), F32)
    return _pcall(
        body, grid=(nsteps,), name="mix_bwd_branches",
        args=(p, c, dh, lna_g, lna_b, wa, lnb_g, lnb_b, ws, wst, sbias, wb, wo),
        out_shape=(jax.ShapeDtypeStruct((s, D), F32), jax.ShapeDtypeStruct((s, DIN), BF),
                   bf_rows, bf_rows, bf_rows, bf_rows, bf_rows, bf_rows,
                   jax.ShapeDtypeStruct((NG, CHUNK, CHUNK), F32), jax.ShapeDtypeStruct((CHUNK, CHUNK), F32),
                   acc_vec, acc_vec, acc_vec, acc_vec),
        in_specs=[_row_spec(ts, DIN), _row_spec(ts, D), _row_spec(ts, D), vec, vec, sq, vec, vec,
                  _const_spec((NG, CHUNK, CHUNK)), _const_spec((NG, CHUNK, CHUNK)), _const_spec((CHUNK, D)),
                  sq, sq],
        out_specs=(_row_spec(ts, D), _row_spec(ts, DIN)) + (_row_spec(ts, D),) * 6
        + (_acc_spec((NG, CHUNK, CHUNK)), _acc_spec((CHUNK, CHUNK))) + (_acc_spec((1, D)),) * 4,
        scratch=[pltpu.VMEM((ts, D), F32), pltpu.VMEM((ts, D), BF), pltpu.VMEM((ts, D), F32),
                 pltpu.VMEM((CHUNK, D), F32)], rider=rider)


def conv_bwd(p, dc, dp, conv_w):
    s = dc.shape[0]
    ts = _tile(s, 256)
    nsteps = s // ts
    per = ts // HALO

    rb = 16

    def body(pm_ref, pp_ref, dcm_ref, dcn_ref, cw_ref, dpin_ref, dp_ref, dw_ref, db_ref, ext_ref, dext_ref,
             dw8_ref, sh_ref, dsh_ref, dglu_ref):
        del dpin_ref
        step = pl.program_id(0)

        @pl.when(step == 0)
        def _():
            dw8_ref[...] = jnp.zeros_like(dw8_ref)
            db_ref[...] = jnp.zeros_like(db_ref)

        a_val = pm_ref[:, 0:D].astype(F32)
        sg = _sigmoid(pm_ref[:, D:2 * D].astype(F32))
        prev = pp_ref[:, 0:D].astype(F32) * _sigmoid(pp_ref[:, D:2 * D].astype(F32))
        ext_ref[0:HALO, :] = jnp.where(step > 0, prev, 0.0)
        ext_ref[HALO:HALO + ts, :] = a_val * sg
        dcm = dcm_ref[...]
        dext_ref[0:ts, :] = dcm
        dext_ref[ts:ts + HALO, :] = jnp.where(step < nsteps - 1, dcn_ref[...], 0.0)
        db_ref[...] += _colsum(dcm)
        for l0 in range(0, D, CB):
            lanes = slice(l0, l0 + CB)
            _shifted_copies(dext_ref, dsh_ref, lanes, ts)
            for r0 in range(0, ts, CB):
                acc = jnp.zeros((CB, CB), F32)
                for k in range(CW):
                    acc = acc + cw_ref[k:k + 1, lanes] * _window(dext_ref, dsh_ref, lanes, CW - 1 - k, r0, CB)
                dglu_ref[r0:r0 + CB, lanes] = acc
            _shifted_copies(ext_ref, sh_ref, lanes, ts)
            accs = [jnp.zeros((SUB, CB), F32) for _ in range(CW)]
            for r0 in range(0, ts, rb):
                dcb = dext_ref[r0:r0 + rb, lanes]
                for k in range(CW):
                    prod = dcb * _window(ext_ref, sh_ref, lanes, HALO - (CW - 1) + k, r0, rb)
                    accs[k] = accs[k] + jnp.sum(prod.reshape(rb // SUB, SUB, CB), axis=0)
            for k in range(CW):
                dw8_ref[k, :, lanes] += accs[k]
        dglu = dglu_ref[...]
        dp_ref[:, 0:D] = (dglu * sg).astype(BF)
        dp_ref[:, D:2 * D] = (dglu * a_val * sg * (1.0 - sg)).astype(BF)

        @pl.when(step == nsteps - 1)
        def _():
            dw_ref[...] = jnp.zeros_like(dw_ref)
            for k in range(CW):
                dw_ref[k:k + 1, :] = _colsum(dw8_ref[k])

    return pl.pallas_call(
        body, grid=(nsteps,), name="conv_bwd",
        out_shape=(jax.ShapeDtypeStruct((s, DIN), BF), jax.ShapeDtypeStruct((HALO, D), F32),
                   jax.ShapeDtypeStruct((1, D), F32)),
        in_specs=[pl.BlockSpec((ts, 2 * D), lambda i: (i, 0)),
                  pl.BlockSpec((HALO, 2 * D), lambda i: (jnp.maximum(i * per - 1, 0), 0)),
                  _row_spec(ts, D),
                  pl.BlockSpec((HALO, D), lambda i: (jnp.minimum((i + 1) * per, s // HALO - 1), 0)),
                  _const_spec((HALO, D)),
                  pl.BlockSpec(memory_space=pl.ANY)],
        out_specs=(pl.BlockSpec((ts, 2 * D), lambda i: (i, 0)), _acc_spec((HALO, D)), _acc_spec((1, D))),
        scratch_shapes=[pltpu.VMEM((ts + HALO, D), F32), pltpu.VMEM((ts + HALO, D), F32),
                        pltpu.VMEM((HALO, SUB, D), F32),
                        pltpu.VMEM((SUB - 1, ts + SH_ROWS_EXTRA, CB), F32),
                        pltpu.VMEM((SUB - 1, ts + SH_ROWS_EXTRA, CB), F32),
                        pltpu.VMEM((ts, D), F32)],
        input_output_aliases={5: 0},
        compiler_params=_cparams(("arbitrary",)),
    )(p, p, dc, dc, conv_w, dp)


def mix_bwd_in(dp, h, dh, gain, win):
    s = h.shape[0]
    ts = _tile(s, 512)

    def body(dp_ref, h_ref, dh_ref, g_ref, win_ref, dx_ref, dg_ref, db_ref):
        @pl.when(pl.program_id(0) == 0)
        def _():
            dg_ref[...] = jnp.zeros_like(dg_ref)
            db_ref[...] = jnp.zeros_like(db_ref)

        gain_v = g_ref[...]
        xh, r = _rms(h_ref[...])
        dn = jnp.zeros((ts, D), F32)
        for k in range(NSH):
            dpk = dp_ref[:, k * INB:(k + 1) * INB]
            dn = dn + _dot_nt(dpk, win_ref[k])
            db_ref[:, k * INB:(k + 1) * INB] += _colsum(dpk.astype(F32))
        dg_ref[...] += _colsum(dn * xh)
        dx_ref[...] = dh_ref[...] + _rms_bwd(dn, xh, r, gain_v)

    return pl.pallas_call(
        body, grid=(s // ts,), name="mix_bwd_in",
        out_shape=(jax.ShapeDtypeStruct((s, D), F32), jax.ShapeDtypeStruct((1, D), F32),
                   jax.ShapeDtypeStruct((1, DIN), F32)),
        in_specs=[_row_spec(ts, DIN), _row_spec(ts, D), _row_spec(ts, D), _const_spec((1, D)),
                  _const_spec((NSH, D, INB))],
        out_specs=(_row_spec(ts, D), _acc_spec((1, D)), _acc_spec((1, DIN))),
        compiler_params=_cparams(("arbitrary",)),
    )(dp, h, dh, gain, win)


def kv_proj(mem, gain, wkv):
    def body(m_ref, g_ref, w_ref, k_ref, v_ref, n_ref):
        xh, _ = _rms(m_ref[...])
        nb = (xh * g_ref[...]).astype(BF)
        n_ref[...] = nb
        half = D // 2
        for j in range(2):
            k_ref[:, j * half:(j + 1) * half] = _dot(nb, w_ref[j]).astype(BF)
            v_ref[:, j * half:(j + 1) * half] = _dot(nb, w_ref[2 + j]).astype(BF)

    o = jax.ShapeDtypeStruct((NMEM, D), BF)
    return pl.pallas_call(body, name="kv_proj", out_shape=(o, o, o), compiler_params=_cparams())(mem, gain, wkv)


def kv_bwd(mem, gain, memn, wkv, dk, dv):
    def body(m_ref, g_ref, n_ref, w_ref, dk_ref, dv_ref, dw_ref, dg_ref):
        xh, _ = _rms(m_ref[...])
        nb = n_ref[...]
        half = D // 2
        dn = jnp.zeros((NMEM, D), F32)
        for j in range(2):
            dkb = dk_ref[:, j * half:(j + 1) * half].astype(BF)
            dvb = dv_ref[:, j * half:(j + 1) * half].astype(BF)
            dw_ref[j] = _dot_tn(nb, dkb)
            dw_ref[2 + j] = _dot_tn(nb, dvb)
            dn = dn + _dot_nt(dkb, w_ref[j]) + _dot_nt(dvb, w_ref[2 + j])
        dg_ref[...] = _colsum(dn * xh)

    return pl.pallas_call(
        body, name="kv_bwd",
        out_shape=(jax.ShapeDtypeStruct((NSH, D, D // 2), F32), jax.ShapeDtypeStruct((1, D), F32)),
        compiler_params=_cparams())(mem, gain, memn, wkv, dk, dv)


def _attend(qb, k_ref, v_ref, h):
    cols = slice(h * HD, (h + 1) * HD)
    sc = _dot_nt(qb[:, cols], k_ref[:, cols]) * ATT_SCALE
    e = jnp.exp(sc - jnp.max(sc, axis=-1, keepdims=True))
    pr = e / jnp.sum(e, axis=-1, keepdims=True)
    return pr, _dot(pr.astype(BF), v_ref[:, cols])


def xattn_fwd(h, gain, wq, k, v, wo):
    s = h.shape[0]
    ts = _tile(s, 512)

    def body(h_ref, g_ref, wq_ref, k_ref, v_ref, wo_ref, o_ref, att_ref):
        x = h_ref[...]
        xh, _ = _rms(x)
        nb = (xh * g_ref[...]).astype(BF)
        qb = _dot(nb, wq_ref[...]).astype(BF)
        for hd in range(NH):
            _, oh = _attend(qb, k_ref, v_ref, hd)
            att_ref[:, hd * HD:(hd + 1) * HD] = oh.astype(BF)
        o_ref[...] = x + _dot(att_ref[...], wo_ref[...])

    sq = _const_spec((D, D))
    kvs = _const_spec((NMEM, D))
    return pl.pallas_call(
        body, grid=(s // ts,), name="xattn_fwd",
        out_shape=jax.ShapeDtypeStruct((s, D), F32),
        in_specs=[_row_spec(ts, D), _const_spec((1, D)), sq, kvs, kvs, sq],
        out_specs=_row_spec(ts, D),
        scratch_shapes=[pltpu.VMEM((ts, D), BF)],
        compiler_params=_cparams(("arbitrary",)),
    )(h, gain, wq, k, v, wo)


def xattn_bwd(h, dh, gain, wq, k, v, wo, rider=None):
    s = h.shape[0]
    ts = _tile(s, 512)

    def body(h_ref, dh_ref, g_ref, wq_ref, k_ref, v_ref, wo_ref,
             dx_ref, n_ref, dq_ref, att_ref, dhb_ref, dk_ref, dv_ref, dg_ref):
        @pl.when(pl.program_id(0) == 0)
        def _():
            for ref in (dk_ref, dv_ref, dg_ref):
                ref[...] = jnp.zeros_like(ref)

        x = h_ref[...]
        dh = dh_ref[...]
        gain_v = g_ref[...]
        xh, r = _rms(x)
        nb = (xh * gain_v).astype(BF)
        n_ref[...] = nb
        qb = _dot(nb, wq_ref[...]).astype(BF)
        dhb = dh.astype(BF)
        dhb_ref[...] = dhb
        dob = _dot_nt(dhb, wo_ref[...]).astype(BF)
        for hd in range(NH):
            cols = slice(hd * HD, (hd + 1) * HD)
            pr, oh = _attend(qb, k_ref, v_ref, hd)
            att_ref[:, cols] = oh.astype(BF)
            doh = dob[:, cols]
            dpr = _dot_nt(doh, v_ref[:, cols])
            dv_ref[:, cols] += _dot_tn(pr.astype(BF), doh)
            dsc = (pr * (dpr - jnp.sum(dpr * pr, axis=-1, keepdims=True)) * ATT_SCALE).astype(BF)
            dq_ref[:, cols] = _dot(dsc, k_ref[:, cols]).astype(BF)
            dk_ref[:, cols] += _dot_tn(dsc, qb[:, cols])
        dn = _dot_nt(dq_ref[...], wq_ref[...])
        dg_ref[...] += _colsum(dn * xh)
        dx_ref[...] = dh + _rms_bwd(dn, xh, r, gain_v)

    sq = _const_spec((D, D))
    kvs = _const_spec((NMEM, D))
    bf_rows = jax.ShapeDtypeStruct((s, D), BF)
    kv_acc = jax.ShapeDtypeStruct((NMEM, D), F32)
    return _pcall(
        body, grid=(s // ts,), name="xattn_bwd", args=(h, dh, gain, wq, k, v, wo),
        out_shape=(jax.ShapeDtypeStruct((s, D), F32), bf_rows, bf_rows, bf_rows, bf_rows, kv_acc, kv_acc,
                   jax.ShapeDtypeStruct((1, D), F32)),
        in_specs=[_row_spec(ts, D), _row_spec(ts, D), _const_spec((1, D)), sq, kvs, kvs, sq],
        out_specs=(_row_spec(ts, D),) * 5 + (_acc_spec((NMEM, D)), _acc_spec((NMEM, D)), _acc_spec((1, D))),
        rider=rider)


BLOCK_BYTES = 3 << 19


def _row_block(rows, cols):
    rb = rows
    while rb * cols * 4 > BLOCK_BYTES and rb % 32 == 0:
        rb //= 2
    return rb


def cast_bf16(w, chip, name):
    r, c = w.shape
    rb = _row_block(r, c)

    def body(chip_ref, w_ref, o_ref):
        del chip_ref
        o_ref[0] = w_ref[...].astype(BF)

    return pl.pallas_call(
        body, name=name, out_shape=jax.ShapeDtypeStruct((NSH, r, c), BF),
        grid_spec=pltpu.PrefetchScalarGridSpec(
            num_scalar_prefetch=1, grid=(r // rb,),
            in_specs=[pl.BlockSpec((rb, c), lambda i, chip_ref: (i, 0))],
            out_specs=pl.BlockSpec((1, rb, c), lambda i, chip_ref: (chip_ref[0], i, 0))),
        compiler_params=_cparams(("arbitrary",)))(chip, w)


def pair_add(g4, recv, core, name):
    nsh, _, rh, c = g4.shape
    rb = _row_block(rh, c)

    def body(core_ref, g_ref, r_ref, o_ref, ob_ref):
        del core_ref
        sm = g_ref[0, 0] + r_ref[0]
        o_ref[0] = sm
        ob_ref[0] = sm.astype(BF)

    spec3 = pl.BlockSpec((1, rb, c), lambda k, i, core_ref: (k, i, 0))
    return pl.pallas_call(
        body, name=name,
        out_shape=(jax.ShapeDtypeStruct((nsh, rh, c), F32), jax.ShapeDtypeStruct((nsh, rh, c), BF)),
        grid_spec=pltpu.PrefetchScalarGridSpec(
            num_scalar_prefetch=1, grid=(nsh, rh // rb),
            in_specs=[pl.BlockSpec((1, 1, rb, c), lambda k, i, core_ref: (k, core_ref[0], i, 0)), spec3],
            out_specs=(spec3, spec3)),
        compiler_params=_cparams(("arbitrary", "arbitrary")))(core, g4, recv)


def chip_sum(psum, recv, place, name):
    _, rh, c = psum.shape
    rb = _row_block(rh, c)

    def body(place_ref, p_ref, r_ref, o_ref):
        del place_ref
        acc = p_ref[0]
        for j in range(NSH - 1):
            acc = acc + r_ref[j].astype(F32)
        o_ref[0] = acc

    return pl.pallas_call(
        body, name=name, out_shape=jax.ShapeDtypeStruct((2, rh, c), F32),
        grid_spec=pltpu.PrefetchScalarGridSpec(
            num_scalar_prefetch=1, grid=(rh // rb,),
            in_specs=[pl.BlockSpec((1, rb, c), lambda i, place_ref: (place_ref[0], i, 0)),
                      pl.BlockSpec((NSH - 1, rb, c), lambda i, place_ref: (0, i, 0))],
            out_specs=pl.BlockSpec((1, rb, c), lambda i, place_ref: (place_ref[1], i, 0))),
        compiler_params=_cparams(("arbitrary",)))(place, psum, recv)


def _adamw_math(w, g, m, v):
    m = ADAM_B1 * m + (1.0 - ADAM_B1) * g
    v = ADAM_B2 * v + (1.0 - ADAM_B2) * (g * g)
    m_hat = m / (1.0 - ADAM_B1 ** ADAM_STEP)
    v_hat = v / (1.0 - ADAM_B2 ** ADAM_STEP)
    delta = -ADAM_LR * (m_hat / (jnp.sqrt(v_hat) + ADAM_EPS) + ADAM_WD * w)
    return delta, m, v


def adamw(w, g, m, v, name):
    r, c = w.shape
    rb = _row_block(r, c)

    def body(w_ref, g_ref, m_ref, v_ref, d_ref, mo_ref, vo_ref):
        d, mn, vn = _adamw_math(w_ref[...], g_ref[...], m_ref[...], v_ref[...])
        d_ref[...] = d
        mo_ref[...] = mn
        vo_ref[...] = vn

    o = jax.ShapeDtypeStruct((r, c), F32)
    spec = _row_spec(rb, c)
    return pl.pallas_call(
        body, grid=(r // rb,), name=name, out_shape=(o, o, o),
        in_specs=[spec] * 4, out_specs=(spec,) * 3,
        compiler_params=_cparams(("arbitrary",)))(w, g, m, v)


def _place():
    return lax.axis_index("x"), lax.axis_index("y"), lax.axis_index("c")


def _other_chips(x, y):
    return [(1 - x, y), (x, 1 - y), (1 - x, 1 - y)]


NOTHER = NSH - 1


def gather_rider(arrays):
    nw = len(arrays)
    nici = nw * NOTHER

    def copies(refs, send_sems, recv_sems):
        x, y, c = _place()
        ici, d2d = [], []
        for w in range(nw):
            for j, (px, py) in enumerate(_other_chips(x, y)):
                n = w * NOTHER + j
                sems = dict(send_sem=send_sems.at[n], recv_sem=recv_sems.at[n],
                            device_id=(px, py, c), device_id_type=MESH)
                mine = refs[w].at[2 * x + y, c]
                theirs = refs[w].at[2 * px + py, c]
                ici.append((pltpu.make_async_remote_copy(src_ref=mine, dst_ref=mine, **sems),
                            pltpu.make_async_remote_copy(src_ref=mine, dst_ref=theirs, **sems)))
                sems = dict(send_sem=send_sems.at[nici + n], recv_sem=recv_sems.at[nici + n],
                            device_id=(x, y, 1 - c), device_id_type=MESH)
                d2d.append((pltpu.make_async_remote_copy(src_ref=theirs, dst_ref=theirs, **sems),
                            pltpu.make_async_remote_copy(src_ref=theirs, dst_ref=refs[w].at[2 * px + py, 1 - c],
                                                         **sems)))
        return ici, d2d

    def start(ins, outs, send_sems, recv_sems):
        ici, _ = copies(outs, send_sems, recv_sems)
        for send, _ in ici:
            send.start()

    def finish(ins, outs, send_sems, recv_sems):
        ici, d2d = copies(outs, send_sems, recv_sems)
        for (_, landed), (forward, _) in zip(ici, d2d):
            landed.wait_recv()
            forward.start()
        for _, landed in d2d:
            landed.wait_recv()
        for send, _ in ici + d2d:
            send.wait_send()

    return Rider(arrays, [jax.ShapeDtypeStruct(a.shape, a.dtype) for a in arrays], {i: i for i in range(nw)},
                 2 * nici, start, finish)


def exchange_rider(psums):
    nw = len(psums)

    def copies(ins, outs, send_sems, recv_sems):
        x, y, c = _place()
        return [pltpu.make_async_remote_copy(
            src_ref=ins[w].at[2 * px + py], dst_ref=outs[w].at[j],
            send_sem=send_sems.at[w * NOTHER + j], recv_sem=recv_sems.at[w * NOTHER + j],
            device_id=(px, py, c), device_id_type=MESH)
            for w in range(nw) for j, (px, py) in enumerate(_other_chips(x, y))]

    def start(ins, outs, send_sems, recv_sems):
        for cp in copies(ins, outs, send_sems, recv_sems):
            cp.start()

    def finish(ins, outs, send_sems, recv_sems):
        for cp in copies(ins, outs, send_sems, recv_sems):
            cp.wait()

    return Rider(psums, [jax.ShapeDtypeStruct((NOTHER,) + p.shape[1:], p.dtype) for p in psums], {},
                 nw * NOTHER, start, finish)


def pair_rider(grads):
    nw = len(grads)

    def copies(ins, outs, send_sems, recv_sems):
        x, y, c = _place()
        return [pltpu.make_async_remote_copy(
            src_ref=ins[w].at[:, 1 - c], dst_ref=outs[w], send_sem=send_sems.at[w], recv_sem=recv_sems.at[w],
            device_id=(x, y, 1 - c), device_id_type=MESH) for w in range(nw)]

    def start(ins, outs, send_sems, recv_sems):
        for cp in copies(ins, outs, send_sems, recv_sems):
            cp.start()

    def finish(ins, outs, send_sems, recv_sems):
        for cp in copies(ins, outs, send_sems, recv_sems):
            cp.wait()

    return Rider(grads, [jax.ShapeDtypeStruct((g.shape[0],) + g.shape[2:], g.dtype) for g in grads], {},
                 nw, start, finish)


class _Offset:
    def __init__(self, ref, base):
        self.ref, self.base = ref, base

    @property
    def at(self):
        return self

    def __getitem__(self, i):
        return self.ref.at[self.base + i]


def merge_riders(riders):
    ins, outs, aliases, spans, nsem = [], [], {}, [], 0
    for r in riders:
        spans.append((len(ins), len(outs), nsem))
        aliases.update({len(ins) + i: len(outs) + j for i, j in r.aliases.items()})
        ins, outs, nsem = ins + r.ins, outs + r.outs, nsem + r.nsem

    def each(step):
        def run(in_refs, out_refs, send_sems, recv_sems):
            for r, (i0, o0, s0) in zip(riders, spans):
                getattr(r, step)(in_refs[i0:i0 + len(r.ins)], out_refs[o0:o0 + len(r.outs)],
                                 _Offset(send_sems, s0), _Offset(recv_sems, s0))
        return run

    return Rider(ins, outs, aliases, nsem, each("start"), each("finish"))


def split_results(riders, results):
    out, o0 = [], 0
    for r in riders:
        out.append(tuple(results[o0:o0 + len(r.outs)]))
        o0 += len(r.outs)
    return out


def swap_rider(halves):
    nw = len(halves)

    def copies(refs, send_sems, recv_sems):
        x, y, c = _place()
        out = []
        for w in range(nw):
            sems = dict(send_sem=send_sems.at[w], recv_sem=recv_sems.at[w],
                        device_id=(x, y, 1 - c), device_id_type=MESH)
            mine = refs[w].at[c]
            out.append((pltpu.make_async_remote_copy(src_ref=mine, dst_ref=mine, **sems),
                        pltpu.make_async_remote_copy(src_ref=mine, dst_ref=refs[w].at[1 - c], **sems)))
        return out

    def start(ins, outs, send_sems, recv_sems):
        for send, _ in copies(outs, send_sems, recv_sems):
            send.start()

    def finish(ins, outs, send_sems, recv_sems):
        cps = copies(outs, send_sems, recv_sems)
        for _, recv in cps:
            recv.wait_recv()
        for send, _ in cps:
            send.wait_send()

    return Rider(halves, [jax.ShapeDtypeStruct(h.shape, h.dtype) for h in halves], {i: i for i in range(nw)},
                 nw, start, finish)


NDEV = 8


def allgather_rider(slots):
    def copies(ref, send_sems, recv_sems):
        x, y, c = _place()
        mine = ref.at[4 * x + 2 * y + c]
        out = []
        for rel in range(1, NDEV):
            peer = (x ^ (rel >> 2), y ^ ((rel >> 1) & 1), c ^ (rel & 1))
            sems = dict(send_sem=send_sems.at[rel - 1], recv_sem=recv_sems.at[rel - 1],
                        device_id=peer, device_id_type=MESH)
            out.append((pltpu.make_async_remote_copy(src_ref=mine, dst_ref=mine, **sems),
                        pltpu.make_async_remote_copy(
                            src_ref=mine, dst_ref=ref.at[4 * peer[0] + 2 * peer[1] + peer[2]], **sems)))
        return out

    def start(ins, outs, send_sems, recv_sems):
        for send, _ in copies(outs[0], send_sems, recv_sems):
            send.start()

    def finish(ins, outs, send_sems, recv_sems):
        cps = copies(outs[0], send_sems, recv_sems)
        for _, recv in cps:
            recv.wait_recv()
        for send, _ in cps:
            send.wait_send()

    return Rider([slots], [jax.ShapeDtypeStruct(slots.shape, slots.dtype)], {0: 0}, NDEV - 1, start, finish)


def sum_slots(slots):
    def body(s_ref, o_ref):
        acc = s_ref[0]
        for dev in range(1, NDEV):
            acc = acc + s_ref[dev]
        o_ref[...] = acc

    return pl.pallas_call(body, name="sum_slots", out_shape=jax.ShapeDtypeStruct(slots.shape[1:], F32),
                          compiler_params=_cparams())(slots)


BIG = ("ffn1_w_gu", "ffn1_w_down", "w_in", "w_a_out", "w_b_out", "w_out", "w_q", "w_kv", "w_o",
       "ffn2_w_gu", "ffn2_w_down")
SMALL = {"ffn1_norm": (0, 1), "mix_norm": (8, 1), "xattn_norm": (16, 1), "mem_norm": (24, 1),
         "ffn2_norm": (32, 1), "final_norm": (40, 1), "conv_b": (48, 1), "conv_ln_g": (56, 1),
         "conv_ln_b": (64, 1), "sgu_ln_g": (72, 1), "sgu_ln_b": (80, 1), "b_in": (88, 6),
         "conv_w": (96, CW), "sgu_w": (128, 64), "sgu_b": (192, 1)}
LOSS_ROW = 200
SMALL_ROWS = 208


def _pad_rows(a, rows):
    return jnp.pad(a, ((0, rows - a.shape[0]), (0, D - a.shape[1])))


def _pack_small(parts):
    names = sorted(parts, key=lambda n: SMALL[n][0] if n in SMALL else LOSS_ROW)
    rows = []
    for i, n in enumerate(names):
        start = SMALL[n][0] if n in SMALL else LOSS_ROW
        end = SMALL_ROWS if i + 1 == len(names) else (SMALL[names[i + 1]][0] if names[i + 1] in SMALL else LOSS_ROW)
        rows.append(_pad_rows(parts[n], end - start))
    return jnp.concatenate(rows, axis=0)


def _small_views(w):
    return {
        "ffn1_norm": w["ffn1_norm"], "mix_norm": w["mix_norm"], "xattn_norm": w["xattn_norm"],
        "mem_norm": w["mem_norm"], "ffn2_norm": w["ffn2_norm"], "final_norm": w["final_norm"].reshape(1, D),
        "conv_b": w["conv_b"], "conv_ln_g": w["conv_ln_g"], "conv_ln_b": w["conv_ln_b"],
        "sgu_ln_g": w["sgu_ln_g"], "sgu_ln_b": w["sgu_ln_b"], "b_in": w["b_in"].reshape(6, D),
        "conv_w": w["conv_w"][0], "sgu_w": w["sgu_w"].reshape(64, D), "sgu_b": w["sgu_b"].reshape(1, NG * CHUNK),
    }


def _unpack_small(buf, like, chip):
    out = {}
    for n, (start, rows) in SMALL.items():
        blk = buf[start:start + rows]
        if n == "conv_w":
            blk = blk[:, :like[n].shape[-1]] if chip is None else lax.dynamic_slice_in_dim(
                blk, chip * like[n].shape[-1], like[n].shape[-1], axis=1)
        elif n == "sgu_b":
            blk = blk[:, :NG * CHUNK]
        out[n] = blk.reshape(like[n].shape)
    return out


def kernel(x, mem, ffn1_norm, ffn1_w_gu, ffn1_w_down, mix_norm, w_in, b_in, conv_w, conv_b, conv_ln_g, conv_ln_b, w_a_out, sgu_ln_g, sgu_ln_b, sgu_w, sgu_b, w_b_out, w_out, xattn_norm, mem_norm, w_q, w_kv, w_o, ffn2_norm, ffn2_w_gu, ffn2_w_down, final_norm, loss_target, m_ffn1_norm, m_ffn1_w_gu, m_ffn1_w_down, m_mix_norm, m_w_in, m_b_in, m_conv_w, m_conv_b, m_conv_ln_g, m_conv_ln_b, m_w_a_out, m_sgu_ln_g, m_sgu_ln_b, m_sgu_w, m_sgu_b, m_w_b_out, m_w_out, m_xattn_norm, m_mem_norm, m_w_q, m_w_kv, m_w_o, m_ffn2_norm, m_ffn2_w_gu, m_ffn2_w_down, m_final_norm, v_ffn1_norm, v_ffn1_w_gu, v_ffn1_w_down, v_mix_norm, v_w_in, v_b_in, v_conv_w, v_conv_b, v_conv_ln_g, v_conv_ln_b, v_w_a_out, v_sgu_ln_g, v_sgu_ln_b, v_sgu_w, v_sgu_b, v_w_b_out, v_w_out, v_xattn_norm, v_mem_norm, v_w_q, v_w_kv, v_w_o, v_ffn2_norm, v_ffn2_w_gu, v_ffn2_w_down, v_final_norm):
    names = ("ffn1_norm", "ffn1_w_gu", "ffn1_w_down", "mix_norm", "w_in", "b_in", "conv_w", "conv_b",
             "conv_ln_g", "conv_ln_b", "w_a_out", "sgu_ln_g", "sgu_ln_b", "sgu_w", "sgu_b", "w_b_out", "w_out",
             "xattn_norm", "mem_norm", "w_q", "w_kv", "w_o", "ffn2_norm", "ffn2_w_gu", "ffn2_w_down",
             "final_norm")
    wts = dict(zip(names, (ffn1_norm, ffn1_w_gu, ffn1_w_down, mix_norm, w_in, b_in, conv_w, conv_b, conv_ln_g,
                           conv_ln_b, w_a_out, sgu_ln_g, sgu_ln_b, sgu_w, sgu_b, w_b_out, w_out, xattn_norm,
                           mem_norm, w_q, w_kv, w_o, ffn2_norm, ffn2_w_gu, ffn2_w_down, final_norm)))
    mom1 = dict(zip(names, (m_ffn1_norm, m_ffn1_w_gu, m_ffn1_w_down, m_mix_norm, m_w_in, m_b_in, m_conv_w,
                            m_conv_b, m_conv_ln_g, m_conv_ln_b, m_w_a_out, m_sgu_ln_g, m_sgu_ln_b, m_sgu_w,
                            m_sgu_b, m_w_b_out, m_w_out, m_xattn_norm, m_mem_norm, m_w_q, m_w_kv, m_w_o,
                            m_ffn2_norm, m_ffn2_w_gu, m_ffn2_w_down, m_final_norm)))
    mom2 = dict(zip(names, (v_ffn1_norm, v_ffn1_w_gu, v_ffn1_w_down, v_mix_norm, v_w_in, v_b_in, v_conv_w,
                            v_conv_b, v_conv_ln_g, v_conv_ln_b, v_w_a_out, v_sgu_ln_g, v_sgu_ln_b, v_sgu_w,
                            v_sgu_b, v_w_b_out, v_w_out, v_xattn_norm, v_mem_norm, v_w_q, v_w_kv, v_w_o,
                            v_ffn2_norm, v_ffn2_w_gu, v_ffn2_w_down, v_final_norm)))
    xi, yi, ci = _place()
    chip = (2 * xi + yi).astype(jnp.int32)
    core = ci.astype(jnp.int32)
    core_arr = core.reshape(1)
    chip_arr = chip.reshape(1)
    place_arr = jnp.stack([chip, core])
    x2, mem2, tgt = x[0], mem[0], loss_target[0]

    slot = {n: cast_bf16(wts[n][0], chip_arr, "cast_" + n) for n in BIG}
    cw_pad = jnp.pad(conv_w[0], ((0, HALO - CW), (0, 0)))
    slot["conv_w"] = lax.dynamic_update_slice(jnp.zeros((NSH,) + cw_pad.shape, F32), cw_pad[None], (chip, 0, 0))
    g_first = ("ffn1_w_gu", "ffn1_w_down")
    g_mix = ("w_in", "w_a_out", "w_b_out", "w_out", "conv_w")
    g_rest = ("w_q", "w_kv", "w_o", "ffn2_w_gu", "ffn2_w_down")
    def gather(group):
        return gather_rider([slot[n].reshape(NSH, 2, slot[n].shape[1] // 2, slot[n].shape[2]) for n in group])

    def gathered(group, res):
        return {n: r.reshape(slot[n].shape) for n, r in zip(group, res)}

    full = gathered(g_first, run_rider(gather(g_first), "gather_ffn1"))
    wgu1, wd1 = full["ffn1_w_gu"], full["ffn1_w_down"].reshape(FF, D)
    tril = jnp.tril(jnp.ones((CHUNK, CHUNK), dtype=bool))
    ws = jnp.where(tril[None], sgu_w[0], 0.0).astype(BF)
    wst = jnp.transpose(ws, (0, 2, 1))
    sbias = jnp.repeat(jnp.transpose(sgu_b[0]), GD, axis=1)
    gfin = final_norm.reshape(1, D)

    (h1, gu1), rode = ffn_fwd(x2, ffn1_norm, wgu1, wd1, rider=gather(g_mix))
    full.update(gathered(g_mix, rode))
    win = full["w_in"]
    wa, wb, wout = (full[n].reshape(D, D) for n in ("w_a_out", "w_b_out", "w_out"))
    cw_full = jnp.transpose(full["conv_w"], (1, 0, 2)).reshape(HALO, D)
    (h2, proj, n2b, conv_out), rode = mix_fwd(
        h1, mix_norm, win, b_in, cw_full, conv_b, conv_ln_g, conv_ln_b, wa, sgu_ln_g, sgu_ln_b, ws, sbias, wb,
        wout, rider=gather(g_rest))
    full.update(gathered(g_rest, rode))
    wgu2, wd2, wkv = full["ffn2_w_gu"], full["ffn2_w_down"].reshape(FF, D), full["w_kv"]
    wq, wo = full["w_q"].reshape(D, D), full["w_o"].reshape(D, D)
    kb, vb, memn = kv_proj(mem2, mem_norm, wkv)
    h3 = xattn_fwd(h2, xattn_norm, wq, kb, vb, wo)
    dh4, gu2, loss_lanes, d_final = ffn_fwd_loss(h3, ffn2_norm, wgu2, wd2, gfin, tgt)

    def halves_of(n, g):
        rs, cs = wts[n].shape[1:]
        return g.reshape(NSH, 2, rs // 2, cs)

    def pair_adds(group, g4s, recvs):
        sums = [pair_add(g, r, core_arr, "pair_add_" + n) for n, g, r in zip(group, g4s, recvs)]
        return [s[0] for s in sums], exchange_rider([s[1] for s in sums])

    halves = {}

    def chip_sums(group, psums, recv):
        for n, p, r in zip(group, psums, recv):
            halves[n] = chip_sum(p, r, place_arr, "chip_sum_" + n)

    dh3, n4, a4, dgu4, dhb4, d_ffn2n = ffn_bwd(h3, gu2, dh4, ffn2_norm, wgu2, wd2, "ffn2_bwd")
    g_ffn2 = ("ffn2_w_gu", "ffn2_w_down")
    g_gu = halves_of("ffn2_w_gu", dw_matmul(n4, dgu4, NSH, "dw_ffn2_gu"))
    d_down, recv_gu = dw_matmul(a4, dhb4, 1, "dw_ffn2_down", rider=pair_rider([g_gu]))
    g_down = halves_of("ffn2_w_down", d_down)
    recv_down = run_rider(pair_rider([g_down]), "pair_exchange_ffn2")
    ps_ffn2, ride = pair_adds(g_ffn2, [g_gu, g_down], [recv_gu[0], recv_down[0]])
    (dh2, n3, dq, att, dhb3, dk, dv, d_xn), rode = xattn_bwd(h2, dh3, xattn_norm, wq, kb, vb, wo, rider=ride)
    chip_sums(g_ffn2, ps_ffn2, rode)
    g_att = ("w_q", "w_o", "w_kv")
    d_wkv, d_memn = kv_bwd(mem2, mem_norm, memn, wkv, dk, dv)
    g4s = [halves_of(n, g) for n, g in zip(g_att, [dw_matmul(n3, dq, 1, "dw_q"), dw_matmul(att, dhb3, 1, "dw_o"),
                                                     d_wkv])]
    ps_att, ride = pair_adds(g_att, g4s, run_rider(pair_rider(g4s), "pair_exchange_att"))
    ((dconv, dproj, sa, dya, ob, dyb, mg, dhb2, d_sgu_w, d_sgu_b, d_lna_g, d_lna_b, d_lnb_g, d_lnb_b),
     rode) = mix_bwd_branches(proj, conv_out, dh2, conv_ln_g, conv_ln_b, wa, sgu_ln_g, sgu_ln_b, ws, wst,
                              sbias, wb, wout, rider=ride)
    chip_sums(g_att, ps_att, rode)
    dproj, d_conv_w, d_conv_b = conv_bwd(proj, dconv, dproj, cw_full)
    dh1, d_mixn, d_b_in = mix_bwd_in(dproj, h1, dh2, mix_norm, win)
    g_mixw = ("w_in", "w_a_out", "w_b_out")
    g_in = halves_of("w_in", dw_matmul(n2b, dproj, NSH, "dw_in"))
    d_a, recv_in = dw_matmul(sa, dya, 1, "dw_a_out", rider=pair_rider([g_in]))
    g4s = [halves_of(n, g) for n, g in zip(g_mixw[1:], [d_a, dw_matmul(ob, dyb, 1, "dw_b_out")])]
    ps_mix, ride = pair_adds(g_mixw, [g_in] + g4s,
                             [recv_in[0]] + list(run_rider(pair_rider(g4s), "pair_exchange_mix")))
    dx, n1, a1, dgu1, dhb1, d_ffn1n = ffn_bwd(x2, gu1, dh1, ffn1_norm, wgu1, wd1, "ffn1_bwd")
    small_grads = {
        "ffn1_norm": d_ffn1n, "mix_norm": d_mixn, "xattn_norm": d_xn, "mem_norm": d_memn, "ffn2_norm": d_ffn2n,
        "final_norm": d_final, "conv_b": d_conv_b, "conv_ln_g": d_lna_g, "conv_ln_b": d_lna_b,
        "sgu_ln_g": d_lnb_g, "sgu_ln_b": d_lnb_b, "b_in": d_b_in.reshape(6, D), "conv_w": d_conv_w[:CW],
        "sgu_w": d_sgu_w.reshape(64, D), "sgu_b": jnp.transpose(d_sgu_b[:, :NG]).reshape(1, NG * CHUNK),
        "loss": loss_lanes}
    slots = lax.dynamic_update_slice(jnp.zeros((NDEV, SMALL_ROWS, D), F32), _pack_small(small_grads)[None],
                                     (2 * chip + core, 0, 0))
    riders = [ride, allgather_rider(slots)]
    d_wgu1, rode = dw_matmul(n1, dgu1, NSH, "dw_ffn1_gu", rider=merge_riders(riders))
    recv_mix, all_slots = split_results(riders, rode)
    chip_sums(g_mixw, ps_mix, recv_mix)

    def last_link(n, grad, tag):
        g4 = halves_of(n, grad)
        return pair_adds((n,), [g4], run_rider(pair_rider([g4]), "pair_exchange_" + tag))

    ps_gu, ride_gu = last_link("ffn1_w_gu", d_wgu1, "ffn1_gu")
    d_down, recv_gu = dw_matmul(a1, dhb1, 1, "dw_ffn1_down", rider=ride_gu)
    chip_sums(("ffn1_w_gu",), ps_gu, recv_gu)
    ps_down, ride_down = last_link("ffn1_w_down", d_down, "ffn1_down")
    d_out, rode = dw_matmul(mg, dhb2, 1, "dw_out", rider=ride_down)
    chip_sums(("ffn1_w_down",), ps_down, rode)
    ps_out, ride_out = last_link("w_out", d_out, "out")
    chip_sums(("w_out",), ps_out, run_rider(ride_out, "chip_exchange_out"))
    swapped = run_rider(swap_rider([halves[n] for n in BIG]), "pair_swap")
    gshard = {n: g.reshape(wts[n].shape[1:]) for n, g in zip(BIG, swapped)}

    small = sum_slots(all_slots[0])
    loss = (0.5 / D) * jnp.sum(small[LOSS_ROW])
    gsmall = _unpack_small(small, wts, chip)

    out_g, out_d, out_m, out_v = dict(gsmall), {}, {}, {}
    sw, sm, sv = (_pack_small(_small_views(t))[:LOSS_ROW] for t in (wts, mom1, mom2))
    sg = _pack_small(_small_views({n: gsmall[n] for n in SMALL}))[:LOSS_ROW]
    for dst, packed in zip((out_d, out_m, out_v), adamw(sw, sg, sm, sv, "adamw_small")):
        dst.update(_unpack_small(packed, wts, None))
    for n in BIG:
        shape = wts[n].shape
        out_g[n] = gshard[n].reshape(shape)
        d, mn, vn = adamw(wts[n][0], gshard[n], mom1[n][0], mom2[n][0], "adamw_" + n)
        out_d[n], out_m[n], out_v[n] = d.reshape(shape), mn.reshape(shape), vn.reshape(shape)
    return (loss, dx[None], *[out_g[n] for n in names], *[out_d[n] for n in names],
            *[out_m[n] for n in names], *[out_v[n] for n in names])
```

```python
import functools
import math

import jax
import jax.numpy as jnp
from jax import lax
from jax.experimental import pallas as pl
from jax.experimental.pallas import tpu as pltpu

F32 = jnp.float32
BF = jnp.bfloat16
MESH = pl.DeviceIdType.MESH

D = 1024
FF = 2816
HC = FF // 2
NSH = 4
DIN = 6 * D
INB = DIN // NSH
CW = 31
HALO = 32
CHUNK = 128
NG = 4
GD = D // NG
NH = 4
HD = D // NH
NMEM = 256
EPS_RMS = 1e-6
EPS_LN = 1e-5
GELU_C0 = math.sqrt(2.0 / math.pi)
GELU_C1 = 0.044715
ATT_SCALE = 1.0 / math.sqrt(HD)

ADAM_LR = 0.001
ADAM_B1 = 0.9
ADAM_B2 = 0.999
ADAM_EPS = 1e-08
ADAM_WD = 0.01
ADAM_STEP = 10

VMEM_LIMIT = 56 * 1024 * 1024


def _cparams(sem=None, **kw):
    if sem is not None:
        kw["dimension_semantics"] = sem
    return pltpu.CompilerParams(vmem_limit_bytes=VMEM_LIMIT, **kw)


def _dot(a, b):
    return jnp.dot(a, b, preferred_element_type=F32)


def _dot_nt(a, b):
    return lax.dot_general(a, b, (((1,), (1,)), ((), ())), preferred_element_type=F32)


def _dot_tn(a, b):
    return lax.dot_general(a, b, (((0,), (0,)), ((), ())), preferred_element_type=F32)


def _sigmoid(x):
    return 1.0 / (1.0 + jnp.exp(-x))


def _gelu(x):
    t = jnp.tanh(GELU_C0 * (x + GELU_C1 * (x * x * x)))
    return 0.5 * x * (1.0 + t), t


def _gelu_with_grad(x):
    x2 = x * x
    t = jnp.tanh(GELU_C0 * (x + GELU_C1 * (x2 * x)))
    onep = 1.0 + t
    hx = 0.5 * x
    grad = 0.5 * onep + hx * (1.0 - t * t) * (GELU_C0 + (3.0 * GELU_C0 * GELU_C1) * x2)
    return hx * onep, grad


def _mean(x):
    return jnp.mean(x, axis=-1, keepdims=True)


def _rms(x):
    r = lax.rsqrt(_mean(x * x) + EPS_RMS)
    return x * r, r


def _rms_bwd(dn, xh, r, g):
    dxh = dn * g
    return r * (dxh - xh * _mean(dxh * xh))


def _ln(x):
    xc = x - _mean(x)
    r = lax.rsqrt(_mean(xc * xc) + EPS_LN)
    return xc * r, r


def _ln_bwd(dy, xh, r, g):
    dxh = dy * g
    return r * (dxh - _mean(dxh) - xh * _mean(dxh * xh))


def _colsum(x):
    return jnp.sum(x, axis=0, keepdims=True)


def _const_spec(shape):
    nd = len(shape)
    return pl.BlockSpec(shape, lambda *_: (0,) * nd, pipeline_mode=pl.Buffered(1))


def _row_spec(ts, width):
    return pl.BlockSpec((ts, width), lambda i: (i, 0))


def _acc_spec(shape):
    nd = len(shape)
    return pl.BlockSpec(shape, lambda *_: (0,) * nd)


def _tile(s, want):
    return min(s, want)


HBM_SPEC = pl.BlockSpec(memory_space=pltpu.HBM)


class Rider:
    def __init__(self, ins, outs, aliases, nsem, start, finish):
        self.ins, self.outs, self.aliases, self.nsem = list(ins), list(outs), dict(aliases), nsem
        self.start, self.finish = start, finish


def _pcall(body, *, name, grid, args, in_specs, out_shape, out_specs, scratch=(), rider=None):
    sem = ("arbitrary",) * len(grid)
    n_in, n_out = len(args), len(out_shape)
    if rider is None:
        res = pl.pallas_call(
            body, grid=grid, name=name, out_shape=tuple(out_shape), in_specs=list(in_specs),
            out_specs=tuple(out_specs), scratch_shapes=list(scratch), compiler_params=_cparams(sem))(*args)
        return tuple(res), ()
    r_in, r_out = len(rider.ins), len(rider.outs)

    def wrapped(*refs):
        a, ri = refs[:n_in], refs[n_in:n_in + r_in]
        o = refs[n_in + r_in:n_in + r_in + n_out]
        ro = refs[n_in + r_in + n_out:n_in + r_in + n_out + r_out]
        s, (send, recv) = refs[n_in + r_in + n_out + r_out:-2], refs[-2:]
        first = functools.reduce(jnp.logical_and, [pl.program_id(d) == 0 for d in range(len(grid))])
        last = functools.reduce(jnp.logical_and, [pl.program_id(d) == g - 1 for d, g in enumerate(grid)])

        @pl.when(first)
        def _():
            rider.start(ri, ro, send, recv)

        body(*a, *o, *s)

        @pl.when(last)
        def _():
            rider.finish(ri, ro, send, recv)

    res = pl.pallas_call(
        wrapped, grid=grid, name=name, out_shape=tuple(out_shape) + tuple(rider.outs),
        in_specs=list(in_specs) + [HBM_SPEC] * r_in, out_specs=tuple(out_specs) + (HBM_SPEC,) * r_out,
        scratch_shapes=list(scratch) + [pltpu.SemaphoreType.DMA((rider.nsem,)),
                                        pltpu.SemaphoreType.DMA((rider.nsem,))],
        input_output_aliases={n_in + i: n_out + j for i, j in rider.aliases.items()},
        compiler_params=_cparams(sem, has_side_effects=True))(*args, *rider.ins)
    return tuple(res[:n_out]), tuple(res[n_out:])


def run_rider(rider, name):
    r_in = len(rider.ins)

    def body(*refs):
        ri, ro, (send, recv) = refs[:r_in], refs[r_in:-2], refs[-2:]
        rider.start(ri, ro, send, recv)
        rider.finish(ri, ro, send, recv)

    return pl.pallas_call(
        body, name=name, out_shape=tuple(rider.outs), in_specs=[HBM_SPEC] * r_in,
        out_specs=(HBM_SPEC,) * len(rider.outs),
        scratch_shapes=[pltpu.SemaphoreType.DMA((rider.nsem,)), pltpu.SemaphoreType.DMA((rider.nsem,))],
        input_output_aliases=rider.aliases,
        compiler_params=pltpu.CompilerParams(has_side_effects=True))(*rider.ins)


FFN_BWD_TILE = 256


def _ffn_apply(x, g_ref, wgu_ref, wd_ref, gu_ref):
    xh, _ = _rms(x)
    nb = (xh * g_ref[...]).astype(BF)
    acc = jnp.zeros(x.shape, F32)
    for j in range(2):
        g = _dot(nb, wgu_ref[j])
        u = _dot(nb, wgu_ref[2 + j])
        gu_ref[:, j * HC:(j + 1) * HC] = g.astype(BF)
        gu_ref[:, FF + j * HC:FF + (j + 1) * HC] = u.astype(BF)
        a = (g * _sigmoid(g) * u).astype(BF)
        acc = acc + _dot(a, wd_ref[j * HC:(j + 1) * HC, :])
    return x + 0.5 * acc


def ffn_fwd(h, gain, wgu, wd, rider=None):
    s = h.shape[0]
    ts = _tile(s, 512)

    def body(h_ref, g_ref, wgu_ref, wd_ref, o_ref, gu_ref):
        o_ref[...] = _ffn_apply(h_ref[...], g_ref, wgu_ref, wd_ref, gu_ref)

    return _pcall(
        body, grid=(s // ts,), name="ffn1_fwd", args=(h, gain, wgu, wd),
        out_shape=[jax.ShapeDtypeStruct((s, D), F32), jax.ShapeDtypeStruct((s, 2 * FF), BF)],
        in_specs=[_row_spec(ts, D), _const_spec((1, D)), _const_spec((NSH, D, HC)), _const_spec((FF, D))],
        out_specs=[_row_spec(ts, D), _row_spec(ts, 2 * FF)], rider=rider)


def ffn_fwd_loss(h, gain, wgu, wd, gfin, target):
    s = h.shape[0]
    ts = _tile(s, 512)

    def body(h_ref, g_ref, wgu_ref, wd_ref, gf_ref, t_ref, dh_ref, gu_ref, loss_ref, dgf_ref):
        @pl.when(pl.program_id(0) == 0)
        def _():
            loss_ref[...] = jnp.zeros_like(loss_ref)
            dgf_ref[...] = jnp.zeros_like(dgf_ref)

        h4 = _ffn_apply(h_ref[...], g_ref, wgu_ref, wd_ref, gu_ref)
        yh, r4 = _rms(h4)
        gf = gf_ref[...]
        e = yh * gf - t_ref[...]
        loss_ref[...] += _colsum(e * e)
        dy = e * (1.0 / D)
        dgf_ref[...] += _colsum(dy * yh)
        dh_ref[...] = _rms_bwd(dy, yh, r4, gf)

    return pl.pallas_call(
        body, grid=(s // ts,), name="ffn_fwd_loss",
        out_shape=(jax.ShapeDtypeStruct((s, D), F32), jax.ShapeDtypeStruct((s, 2 * FF), BF),
                   jax.ShapeDtypeStruct((1, D), F32), jax.ShapeDtypeStruct((1, D), F32)),
        in_specs=[_row_spec(ts, D), _const_spec((1, D)), _const_spec((NSH, D, HC)), _const_spec((FF, D)),
                  _const_spec((1, D)), _row_spec(ts, D)],
        out_specs=(_row_spec(ts, D), _row_spec(ts, 2 * FF), _acc_spec((1, D)), _acc_spec((1, D))),
        compiler_params=_cparams(("arbitrary",)),
    )(h, gain, wgu, wd, gfin, target)


def ffn_bwd(h, gu, dh, gain, wgu, wd, name):
    s = h.shape[0]
    ts = _tile(s, FFN_BWD_TILE)

    def body(h_ref, gu_ref, dh_ref, g_ref, wgu_ref, wd_ref, dx_ref, n_ref, a_ref, dgu_ref, dhb_ref, dg_ref):
        @pl.when(pl.program_id(0) == 0)
        def _():
            dg_ref[...] = jnp.zeros_like(dg_ref)

        x = h_ref[...]
        dh = dh_ref[...]
        gain_v = g_ref[...]
        xh, r = _rms(x)
        n_ref[...] = (xh * gain_v).astype(BF)
        dhb = (0.5 * dh).astype(BF)
        dhb_ref[...] = dhb
        dn = jnp.zeros((ts, D), F32)
        for j in range(2):
            g = gu_ref[:, j * HC:(j + 1) * HC].astype(F32)
            u = gu_ref[:, FF + j * HC:FF + (j + 1) * HC].astype(F32)
            sg = _sigmoid(g)
            sl = g * sg
            a_ref[:, j * HC:(j + 1) * HC] = (sl * u).astype(BF)
            da = _dot_nt(dhb, wd_ref[j * HC:(j + 1) * HC, :])
            dgb = (da * u * (sg * (1.0 + g * (1.0 - sg)))).astype(BF)
            dub = (da * sl).astype(BF)
            dgu_ref[:, j * HC:(j + 1) * HC] = dgb
            dgu_ref[:, FF + j * HC:FF + (j + 1) * HC] = dub
            dn = dn + _dot_nt(dgb, wgu_ref[j]) + _dot_nt(dub, wgu_ref[2 + j])
        dg_ref[...] += _colsum(dn * xh)
        dx_ref[...] = dh + _rms_bwd(dn, xh, r, gain_v)

    return pl.pallas_call(
        body, grid=(s // ts,), name=name,
        out_shape=(jax.ShapeDtypeStruct((s, D), F32), jax.ShapeDtypeStruct((s, D), BF),
                   jax.ShapeDtypeStruct((s, FF), BF), jax.ShapeDtypeStruct((s, 2 * FF), BF),
                   jax.ShapeDtypeStruct((s, D), BF), jax.ShapeDtypeStruct((1, D), F32)),
        in_specs=[_row_spec(ts, D), _row_spec(ts, 2 * FF), _row_spec(ts, D), _const_spec((1, D)),
                  _const_spec((NSH, D, HC)), _const_spec((FF, D))],
        out_specs=(_row_spec(ts, D), _row_spec(ts, D), _row_spec(ts, FF), _row_spec(ts, 2 * FF),
                   _row_spec(ts, D), _acc_spec((1, D))),
        compiler_params=_cparams(("arbitrary",)),
    )(h, gu, dh, gain, wgu, wd)


def dw_matmul(x, dy, nsplit, name, rider=None):
    s, k = x.shape
    n = dy.shape[1]
    nb = n // nsplit
    ts = _tile(s, 1024)
    nsteps = s // ts

    def body(x_ref, dy_ref, o_ref, ob_ref):
        @pl.when(pl.program_id(1) == 0)
        def _():
            o_ref[...] = jnp.zeros_like(o_ref)

        o_ref[0] += _dot_tn(x_ref[...], dy_ref[...])

        @pl.when(pl.program_id(1) == nsteps - 1)
        def _():
            ob_ref[...] = o_ref[...].astype(BF)

    spec = pl.BlockSpec((1, k, nb), lambda j, i: (j, 0, 0))
    (out, outb), rode = _pcall(
        body, grid=(nsplit, nsteps), name=name, args=(x, dy),
        out_shape=[jax.ShapeDtypeStruct((nsplit, k, nb), F32), jax.ShapeDtypeStruct((nsplit, k, nb), BF)],
        in_specs=[pl.BlockSpec((ts, k), lambda j, i: (i, 0)), pl.BlockSpec((ts, nb), lambda j, i: (i, j))],
        out_specs=[spec, spec], rider=rider)
    return out, outb, rode


def _split_in_proj(p, b):
    h = INB - D
    a_val = p[0][:, :D] + b[:, 0:D]
    a_gate = jnp.concatenate([p[0][:, D:], p[1][:, :h]], axis=1) + b[:, D:2 * D]
    b_u = p[1][:, h:] + b[:, 2 * D:3 * D]
    b_v = p[2][:, :D] + b[:, 3 * D:4 * D]
    g_a = jnp.concatenate([p[2][:, D:], p[3][:, :h]], axis=1) + b[:, 4 * D:5 * D]
    g_b = p[3][:, h:] + b[:, 5 * D:6 * D]
    return a_val, a_gate, b_u, b_v, g_a, g_b


def _sgu_mix(vnb, ws_ref, sb_ref, mixed_ref, ts):
    for ci in range(ts // CHUNK):
        rows = slice(ci * CHUNK, (ci + 1) * CHUNK)
        for g in range(NG):
            cols = slice(g * GD, (g + 1) * GD)
            mixed_ref[rows, cols] = _dot(ws_ref[g], vnb[rows, cols]) + sb_ref[:, cols]


SUB = 8
CB = 128
SH_ROWS_EXTRA = HALO - SUB


def _shifted_copies(ext_ref, sh_ref, lanes, ts):
    for b in range(1, SUB):
        sh_ref[b - 1] = ext_ref[b:b + ts + SH_ROWS_EXTRA, lanes]


def _window(ext_ref, sh_ref, lanes, first, r0, nrows):
    b = first % SUB
    a = first - b
    if b == 0:
        return ext_ref[a + r0:a + r0 + nrows, lanes]
    return sh_ref[b - 1, a + r0:a + r0 + nrows, :]


def mix_fwd(h, gain, win, b_in, conv_w, conv_b, lna_g, lna_b, wa, lnb_g, lnb_b, ws, sbias, wb, wo, rider=None):
    s = h.shape[0]
    ts = _tile(s, 256)

    def body(h_ref, g_ref, win_ref, bin_ref, cw_ref, cb_ref, lag_ref, lab_ref, wa_ref, lbg_ref, lbb_ref,
             ws_ref, sb_ref, wb_ref, wo_ref, o_ref, p_ref, n_ref, c_ref, ext_ref, mixed_ref, sh_ref):
        @pl.when(pl.program_id(0) == 0)
        def _():
            ext_ref[0:HALO, :] = jnp.zeros((HALO, D), F32)

        x = h_ref[...]
        xh, _ = _rms(x)
        nb = (xh * g_ref[...]).astype(BF)
        n_ref[...] = nb
        b = bin_ref[...]
        p = []
        for k in range(NSH):
            pk = _dot(nb, win_ref[k])
            p_ref[:, k * INB:(k + 1) * INB] = (pk + b[:, k * INB:(k + 1) * INB]).astype(BF)
            p.append(pk)
        a_val, a_gate, b_u, b_v, g_a, g_b = _split_in_proj(p, b)
        ext_ref[HALO:HALO + ts, :] = a_val * _sigmoid(a_gate)
        for l0 in range(0, D, CB):
            lanes = slice(l0, l0 + CB)
            _shifted_copies(ext_ref, sh_ref, lanes, ts)
            for r0 in range(0, ts, CB):
                acc = jnp.zeros((CB, CB), F32) + cb_ref[:, lanes]
                for k in range(CW):
                    acc = acc + cw_ref[k:k + 1, lanes] * _window(ext_ref, sh_ref, lanes,
                                                                 HALO - (CW - 1) + k, r0, CB)
                c_ref[r0:r0 + CB, lanes] = acc
        ext_ref[0:HALO, :] = ext_ref[ts:ts + HALO, :]
        ch, _ = _ln(c_ref[...])
        la = ch * lag_ref[...] + lab_ref[...]
        sa = (la * _sigmoid(la)).astype(BF)
        ya = _dot(sa, wa_ref[...])
        ub, _ = _gelu(b_u)
        gv, _ = _gelu(b_v)
        vh, _ = _ln(gv)
        vnb = (vh * lbg_ref[...] + lbb_ref[...]).astype(BF)
        _sgu_mix(vnb, ws_ref, sb_ref, mixed_ref, ts)
        ob = (ub * mixed_ref[...]).astype(BF)
        yb = _dot(ob, wb_ref[...])
        merged = (_sigmoid(g_a) * ya + _sigmoid(g_b) * yb).astype(BF)
        o_ref[...] = x + _dot(merged, wo_ref[...])

    vec = _const_spec((1, D))
    sq = _const_spec((D, D))
    return _pcall(
        body, grid=(s // ts,), name="mix_fwd",
        args=(h, gain, win, b_in, conv_w, conv_b, lna_g, lna_b, wa, lnb_g, lnb_b, ws, sbias, wb, wo),
        out_shape=(jax.ShapeDtypeStruct((s, D), F32), jax.ShapeDtypeStruct((s, DIN), BF),
                   jax.ShapeDtypeStruct((s, D), BF), jax.ShapeDtypeStruct((s, D), F32)),
        in_specs=[_row_spec(ts, D), vec, _const_spec((NSH, D, INB)), _const_spec((1, DIN)),
                  _const_spec((HALO, D)), vec, vec, vec, sq, vec, vec,
                  _const_spec((NG, CHUNK, CHUNK)), _const_spec((CHUNK, D)), sq, sq],
        out_specs=(_row_spec(ts, D), _row_spec(ts, DIN), _row_spec(ts, D), _row_spec(ts, D)),
        scratch=[pltpu.VMEM((ts + HALO, D), F32), pltpu.VMEM((ts, D), F32),
                 pltpu.VMEM((SUB - 1, ts + SH_ROWS_EXTRA, CB), F32)], rider=rider)


def mix_bwd_branches(p, c, dh, lna_g, lna_b, wa, lnb_g, lnb_b, ws, wst, sbias, wb, wo, rider=None):
    s = dh.shape[0]
    ts = _tile(s, 256)
    nsteps = s // ts

    def body(p_ref, c_ref, dh_ref, lag_ref, lab_ref, wa_ref, lbg_ref, lbb_ref, ws_ref, wst_ref, sb_ref,
             wb_ref, wo_ref, dc_ref, dp_ref, sa_ref, dya_ref, ob_ref, dyb_ref, mg_ref, dhb_ref,
             dws_ref, dsb_ref, dlag_ref, dlab_ref, dlbg_ref, dlbb_ref, mixed_ref, dmix_ref, dvn_ref, dsb_acc):
        step = pl.program_id(0)

        @pl.when(step == 0)
        def _():
            for ref in (dws_ref, dsb_acc, dlag_ref, dlab_ref, dlbg_ref, dlbb_ref):
                ref[...] = jnp.zeros_like(ref)

        b_u = p_ref[:, 2 * D:3 * D].astype(F32)
        b_v = p_ref[:, 3 * D:4 * D].astype(F32)
        sga = _sigmoid(p_ref[:, 4 * D:5 * D].astype(F32))
        sgb = _sigmoid(p_ref[:, 5 * D:6 * D].astype(F32))
        lag = lag_ref[...]
        ch, ra = _ln(c_ref[...])
        la = ch * lag + lab_ref[...]
        sla = _sigmoid(la)
        sa = (la * sla).astype(BF)
        sa_ref[...] = sa
        ya = _dot(sa, wa_ref[...])
        lbg = lbg_ref[...]
        ub, dub = _gelu_with_grad(b_u)
        gv, dgv = _gelu_with_grad(b_v)
        vh, rb = _ln(gv)
        vnb = (vh * lbg + lbb_ref[...]).astype(BF)
        _sgu_mix(vnb, ws_ref, sb_ref, mixed_ref, ts)
        mixed = mixed_ref[...]
        ob = (ub * mixed).astype(BF)
        ob_ref[...] = ob
        yb = _dot(ob, wb_ref[...])
        mg_ref[...] = (sga * ya + sgb * yb).astype(BF)
        dhb = dh_ref[...].astype(BF)
        dhb_ref[...] = dhb
        dm = _dot_nt(dhb, wo_ref[...])
        dp_ref[:, 0:2 * D] = jnp.zeros((ts, 2 * D), BF)
        dp_ref[:, 4 * D:5 * D] = (dm * ya * sga * (1.0 - sga)).astype(BF)
        dp_ref[:, 5 * D:6 * D] = (dm * yb * sgb * (1.0 - sgb)).astype(BF)
        dya = (dm * sga).astype(BF)
        dya_ref[...] = dya
        dyb = (dm * sgb).astype(BF)
        dyb_ref[...] = dyb
        dla = _dot_nt(dya, wa_ref[...]) * (sla * (1.0 + la * (1.0 - sla)))
        dlag_ref[...] += _colsum(dla * ch)
        dlab_ref[...] += _colsum(dla)
        dc_ref[...] = _ln_bwd(dla, ch, ra, lag)
        dob = _dot_nt(dyb, wb_ref[...])
        dp_ref[:, 2 * D:3 * D] = (dob * mixed * dub).astype(BF)
        dmix = dob * ub
        dmix_ref[...] = dmix.astype(BF)
        dsb = jnp.zeros((CHUNK, D), F32)
        for ci in range(ts // CHUNK):
            rows = slice(ci * CHUNK, (ci + 1) * CHUNK)
            dsb = dsb + dmix[rows, :]
            for g in range(NG):
                cols = slice(g * GD, (g + 1) * GD)
                dmb = dmix_ref[rows, cols]
                dws_ref[g] += _dot_nt(dmb, vnb[rows, cols])
                dvn_ref[rows, cols] = _dot(wst_ref[g], dmb)
        dsb_acc[...] += dsb
        dvn = dvn_ref[...]
        dlbg_ref[...] += _colsum(dvn * vh)
        dlbb_ref[...] += _colsum(dvn)
        dp_ref[:, 3 * D:4 * D] = (_ln_bwd(dvn, vh, rb, lbg) * dgv).astype(BF)

        @pl.when(step == nsteps - 1)
        def _():
            row = lax.broadcasted_iota(jnp.int32, (CHUNK, CHUNK), 0)
            col = lax.broadcasted_iota(jnp.int32, (CHUNK, CHUNK), 1)
            for g in range(NG):
                dws_ref[g] = jnp.where(col <= row, dws_ref[g], 0.0)
            acc = jnp.zeros((CHUNK, CHUNK), F32)
            for g in range(NG):
                tot = jnp.sum(dsb_acc[:, g * GD:(g + 1) * GD], axis=-1, keepdims=True)
                acc = acc + jnp.where(col == g, tot, 0.0)
            dsb_ref[...] = acc

    vec = _const_spec((1, D))
    sq = _const_spec((D, D))
    bf_rows = jax.ShapeDtypeStruct((s, D), BF)
    acc_vec = jax.ShapeDtypeStruct((1, D), F32)
    return _pcall(
        body, grid=(nsteps,), name="mix_bwd_branches",
        args=(p, c, dh, lna_g, lna_b, wa, lnb_g, lnb_b, ws, wst, sbias, wb, wo),
        out_shape=(jax.ShapeDtypeStruct((s, D), F32), jax.ShapeDtypeStruct((s, DIN), BF),
                   bf_rows, bf_rows, bf_rows, bf_rows, bf_rows, bf_rows,
                   jax.ShapeDtypeStruct((NG, CHUNK, CHUNK), F32), jax.ShapeDtypeStruct((CHUNK, CHUNK), F32),
                   acc_vec, acc_vec, acc_vec, acc_vec),
        in_specs=[_row_spec(ts, DIN), _row_spec(ts, D), _row_spec(ts, D), vec, vec, sq, vec, vec,
                  _const_spec((NG, CHUNK, CHUNK)), _const_spec((NG, CHUNK, CHUNK)), _const_spec((CHUNK, D)),
                  sq, sq],
        out_specs=(_row_spec(ts, D), _row_spec(ts, DIN)) + (_row_spec(ts, D),) * 6
        + (_acc_spec((NG, CHUNK, CHUNK)), _acc_spec((CHUNK, CHUNK))) + (_acc_spec((1, D)),) * 4,
        scratch=[pltpu.VMEM((ts, D), F32), pltpu.VMEM((ts, D), BF), pltpu.VMEM((ts, D), F32),
                 pltpu.VMEM((CHUNK, D), F32)], rider=rider)


def conv_bwd(p, dc, dp, conv_w):
    s = dc.shape[0]
    ts = _tile(s, 256)
    nsteps = s // ts
    per = ts // HALO

    rb = 16

    def body(pm_ref, pp_ref, dcm_ref, dcn_ref, cw_ref, dpin_ref, dp_ref, dw_ref, db_ref, ext_ref, dext_ref,
             dw8_ref, sh_ref, dsh_ref, dglu_ref):
        del dpin_ref
        step = pl.program_id(0)

        @pl.when(step == 0)
        def _():
            dw8_ref[...] = jnp.zeros_like(dw8_ref)
            db_ref[...] = jnp.zeros_like(db_ref)

        a_val = pm_ref[:, 0:D].astype(F32)
        sg = _sigmoid(pm_ref[:, D:2 * D].astype(F32))
        prev = pp_ref[:, 0:D].astype(F32) * _sigmoid(pp_ref[:, D:2 * D].astype(F32))
        ext_ref[0:HALO, :] = jnp.where(step > 0, prev, 0.0)
        ext_ref[HALO:HALO + ts, :] = a_val * sg
        dcm = dcm_ref[...]
        dext_ref[0:ts, :] = dcm
        dext_ref[ts:ts + HALO, :] = jnp.where(step < nsteps - 1, dcn_ref[...], 0.0)
        db_ref[...] += _colsum(dcm)
        for l0 in range(0, D, CB):
            lanes = slice(l0, l0 + CB)
            _shifted_copies(dext_ref, dsh_ref, lanes, ts)
            for r0 in range(0, ts, CB):
                acc = jnp.zeros((CB, CB), F32)
                for k in range(CW):
                    acc = acc + cw_ref[k:k + 1, lanes] * _window(dext_ref, dsh_ref, lanes, CW - 1 - k, r0, CB)
                dglu_ref[r0:r0 + CB, lanes] = acc
            _shifted_copies(ext_ref, sh_ref, lanes, ts)
            accs = [jnp.zeros((SUB, CB), F32) for _ in range(CW)]
            for r0 in range(0, ts, rb):
                dcb = dext_ref[r0:r0 + rb, lanes]
                for k in range(CW):
                    prod = dcb * _window(ext_ref, sh_ref, lanes, HALO - (CW - 1) + k, r0, rb)
                    accs[k] = accs[k] + jnp.sum(prod.reshape(rb // SUB, SUB, CB), axis=0)
            for k in range(CW):
                dw8_ref[k, :, lanes] += accs[k]
        dglu = dglu_ref[...]
        dp_ref[:, 0:D] = (dglu * sg).astype(BF)
        dp_ref[:, D:2 * D] = (dglu * a_val * sg * (1.0 - sg)).astype(BF)

        @pl.when(step == nsteps - 1)
        def _():
            dw_ref[...] = jnp.zeros_like(dw_ref)
            for k in range(CW):
                dw_ref[k:k + 1, :] = _colsum(dw8_ref[k])

    return pl.pallas_call(
        body, grid=(nsteps,), name="conv_bwd",
        out_shape=(jax.ShapeDtypeStruct((s, DIN), BF), jax.ShapeDtypeStruct((HALO, D), F32),
                   jax.ShapeDtypeStruct((1, D), F32)),
        in_specs=[pl.BlockSpec((ts, 2 * D), lambda i: (i, 0)),
                  pl.BlockSpec((HALO, 2 * D), lambda i: (jnp.maximum(i * per - 1, 0), 0)),
                  _row_spec(ts, D),
                  pl.BlockSpec((HALO, D), lambda i: (jnp.minimum((i + 1) * per, s // HALO - 1), 0)),
                  _const_spec((HALO, D)),
                  pl.BlockSpec(memory_space=pl.ANY)],
        out_specs=(pl.BlockSpec((ts, 2 * D), lambda i: (i, 0)), _acc_spec((HALO, D)), _acc_spec((1, D))),
        scratch_shapes=[pltpu.VMEM((ts + HALO, D), F32), pltpu.VMEM((ts + HALO, D), F32),
                        pltpu.VMEM((HALO, SUB, D), F32),
                        pltpu.VMEM((SUB - 1, ts + SH_ROWS_EXTRA, CB), F32),
                        pltpu.VMEM((SUB - 1, ts + SH_ROWS_EXTRA, CB), F32),
                        pltpu.VMEM((ts, D), F32)],
        input_output_aliases={5: 0},
        compiler_params=_cparams(("arbitrary",)),
    )(p, p, dc, dc, conv_w, dp)


def mix_bwd_in(dp, h, dh, gain, win):
    s = h.shape[0]
    ts = _tile(s, 512)

    def body(dp_ref, h_ref, dh_ref, g_ref, win_ref, dx_ref, dg_ref, db_ref):
        @pl.when(pl.program_id(0) == 0)
        def _():
            dg_ref[...] = jnp.zeros_like(dg_ref)
            db_ref[...] = jnp.zeros_like(db_ref)

        gain_v = g_ref[...]
        xh, r = _rms(h_ref[...])
        dn = jnp.zeros((ts, D), F32)
        for k in range(NSH):
            dpk = dp_ref[:, k * INB:(k + 1) * INB]
            dn = dn + _dot_nt(dpk, win_ref[k])
            db_ref[:, k * INB:(k + 1) * INB] += _colsum(dpk.astype(F32))
        dg_ref[...] += _colsum(dn * xh)
        dx_ref[...] = dh_ref[...] + _rms_bwd(dn, xh, r, gain_v)

    return pl.pallas_call(
        body, grid=(s // ts,), name="mix_bwd_in",
        out_shape=(jax.ShapeDtypeStruct((s, D), F32), jax.ShapeDtypeStruct((1, D), F32),
                   jax.ShapeDtypeStruct((1, DIN), F32)),
        in_specs=[_row_spec(ts, DIN), _row_spec(ts, D), _row_spec(ts, D), _const_spec((1, D)),
                  _const_spec((NSH, D, INB))],
        out_specs=(_row_spec(ts, D), _acc_spec((1, D)), _acc_spec((1, DIN))),
        compiler_params=_cparams(("arbitrary",)),
    )(dp, h, dh, gain, win)


def kv_proj(mem, gain, wkv):
    def body(m_ref, g_ref, w_ref, k_ref, v_ref, n_ref):
        xh, _ = _rms(m_ref[...])
        nb = (xh * g_ref[...]).astype(BF)
        n_ref[...] = nb
        half = D // 2
        for j in range(2):
            k_ref[:, j * half:(j + 1) * half] = _dot(nb, w_ref[j]).astype(BF)
            v_ref[:, j * half:(j + 1) * half] = _dot(nb, w_ref[2 + j]).astype(BF)

    o = jax.ShapeDtypeStruct((NMEM, D), BF)
    return pl.pallas_call(body, name="kv_proj", out_shape=(o, o, o), compiler_params=_cparams())(mem, gain, wkv)


def kv_bwd(mem, gain, memn, wkv, dk, dv):
    def body(m_ref, g_ref, n_ref, w_ref, dk_ref, dv_ref, dw_ref, dwb_ref, dg_ref):
        xh, _ = _rms(m_ref[...])
        nb = n_ref[...]
        half = D // 2
        dn = jnp.zeros((NMEM, D), F32)
        for j in range(2):
            dkb = dk_ref[:, j * half:(j + 1) * half].astype(BF)
            dvb = dv_ref[:, j * half:(j + 1) * half].astype(BF)
            for slot, dyb in ((j, dkb), (2 + j, dvb)):
                dw = _dot_tn(nb, dyb)
                dw_ref[slot] = dw
                dwb_ref[slot] = dw.astype(BF)
            dn = dn + _dot_nt(dkb, w_ref[j]) + _dot_nt(dvb, w_ref[2 + j])
        dg_ref[...] = _colsum(dn * xh)

    return pl.pallas_call(
        body, name="kv_bwd",
        out_shape=(jax.ShapeDtypeStruct((NSH, D, D // 2), F32), jax.ShapeDtypeStruct((NSH, D, D // 2), BF),
                   jax.ShapeDtypeStruct((1, D), F32)),
        compiler_params=_cparams())(mem, gain, memn, wkv, dk, dv)


def _attend(qb, k_ref, v_ref, h):
    cols = slice(h * HD, (h + 1) * HD)
    sc = _dot_nt(qb[:, cols], k_ref[:, cols]) * ATT_SCALE
    e = jnp.exp(sc - jnp.max(sc, axis=-1, keepdims=True))
    pr = e / jnp.sum(e, axis=-1, keepdims=True)
    return pr, _dot(pr.astype(BF), v_ref[:, cols])


def xattn_fwd(h, gain, wq, k, v, wo):
    s = h.shape[0]
    ts = _tile(s, 512)

    def body(h_ref, g_ref, wq_ref, k_ref, v_ref, wo_ref, o_ref, att_ref):
        x = h_ref[...]
        xh, _ = _rms(x)
        nb = (xh * g_ref[...]).astype(BF)
        qb = _dot(nb, wq_ref[...]).astype(BF)
        for hd in range(NH):
            _, oh = _attend(qb, k_ref, v_ref, hd)
            att_ref[:, hd * HD:(hd + 1) * HD] = oh.astype(BF)
        o_ref[...] = x + _dot(att_ref[...], wo_ref[...])

    sq = _const_spec((D, D))
    kvs = _const_spec((NMEM, D))
    return pl.pallas_call(
        body, grid=(s // ts,), name="xattn_fwd",
        out_shape=jax.ShapeDtypeStruct((s, D), F32),
        in_specs=[_row_spec(ts, D), _const_spec((1, D)), sq, kvs, kvs, sq],
        out_specs=_row_spec(ts, D),
        scratch_shapes=[pltpu.VMEM((ts, D), BF)],
        compiler_params=_cparams(("arbitrary",)),
    )(h, gain, wq, k, v, wo)


def xattn_bwd(h, dh, gain, wq, k, v, wo, rider=None):
    s = h.shape[0]
    ts = _tile(s, 512)

    def body(h_ref, dh_ref, g_ref, wq_ref, k_ref, v_ref, wo_ref,
             dx_ref, n_ref, dq_ref, att_ref, dhb_ref, dk_ref, dv_ref, dg_ref):
        @pl.when(pl.program_id(0) == 0)
        def _():
            for ref in (dk_ref, dv_ref, dg_ref):
                ref[...] = jnp.zeros_like(ref)

        x = h_ref[...]
        dh = dh_ref[...]
        gain_v = g_ref[...]
        xh, r = _rms(x)
        nb = (xh * gain_v).astype(BF)
        n_ref[...] = nb
        qb = _dot(nb, wq_ref[...]).astype(BF)
        dhb = dh.astype(BF)
        dhb_ref[...] = dhb
        dob = _dot_nt(dhb, wo_ref[...]).astype(BF)
        for hd in range(NH):
            cols = slice(hd * HD, (hd + 1) * HD)
            pr, oh = _attend(qb, k_ref, v_ref, hd)
            att_ref[:, cols] = oh.astype(BF)
            doh = dob[:, cols]
            dpr = _dot_nt(doh, v_ref[:, cols])
            dv_ref[:, cols] += _dot_tn(pr.astype(BF), doh)
            dsc = (pr * (dpr - jnp.sum(dpr * pr, axis=-1, keepdims=True)) * ATT_SCALE).astype(BF)
            dq_ref[:, cols] = _dot(dsc, k_ref[:, cols]).astype(BF)
            dk_ref[:, cols] += _dot_tn(dsc, qb[:, cols])
        dn = _dot_nt(dq_ref[...], wq_ref[...])
        dg_ref[...] += _colsum(dn * xh)
        dx_ref[...] = dh + _rms_bwd(dn, xh, r, gain_v)

    sq = _const_spec((D, D))
    kvs = _const_spec((NMEM, D))
    bf_rows = jax.ShapeDtypeStruct((s, D), BF)
    kv_acc = jax.ShapeDtypeStruct((NMEM, D), F32)
    return _pcall(
        body, grid=(s // ts,), name="xattn_bwd", args=(h, dh, gain, wq, k, v, wo),
        out_shape=(jax.ShapeDtypeStruct((s, D), F32), bf_rows, bf_rows, bf_rows, bf_rows, kv_acc, kv_acc,
                   jax.ShapeDtypeStruct((1, D), F32)),
        in_specs=[_row_spec(ts, D), _row_spec(ts, D), _const_spec((1, D)), sq, kvs, kvs, sq],
        out_specs=(_row_spec(ts, D),) * 5 + (_acc_spec((NMEM, D)), _acc_spec((NMEM, D)), _acc_spec((1, D))),
        rider=rider)


BLOCK_BYTES = 3 << 19


def _row_block(rows, cols):
    rb = rows
    while rb * cols * 4 > BLOCK_BYTES and rb % 32 == 0:
        rb //= 2
    return rb


def cast_bf16(w, chip, name):
    r, c = w.shape
    rb = _row_block(r, c)

    def body(chip_ref, w_ref, o_ref):
        del chip_ref
        o_ref[0] = w_ref[...].astype(BF)

    return pl.pallas_call(
        body, name=name, out_shape=jax.ShapeDtypeStruct((NSH, r, c), BF),
        grid_spec=pltpu.PrefetchScalarGridSpec(
            num_scalar_prefetch=1, grid=(r // rb,),
            in_specs=[pl.BlockSpec((rb, c), lambda i, chip_ref: (i, 0))],
            out_specs=pl.BlockSpec((1, rb, c), lambda i, chip_ref: (chip_ref[0], i, 0))),
        compiler_params=_cparams(("arbitrary",)))(chip, w)


NDEV = 8


def device_sum(g4, recv, place, name):
    _, _, rh, c = g4.shape
    rb = _row_block(rh, c)

    def body(place_ref, g_ref, r_ref, o_ref):
        del place_ref
        acc = g_ref[0, 0]
        for j in range(NDEV - 1):
            acc = acc + r_ref[j].astype(F32)
        o_ref[0] = acc

    return pl.pallas_call(
        body, name=name, out_shape=jax.ShapeDtypeStruct((2, rh, c), F32),
        grid_spec=pltpu.PrefetchScalarGridSpec(
            num_scalar_prefetch=1, grid=(rh // rb,),
            in_specs=[pl.BlockSpec((1, 1, rb, c), lambda i, place_ref: (place_ref[0], place_ref[1], i, 0)),
                      pl.BlockSpec((NDEV - 1, rb, c), lambda i, place_ref: (0, i, 0))],
            out_specs=pl.BlockSpec((1, rb, c), lambda i, place_ref: (place_ref[1], i, 0))),
        compiler_params=_cparams(("arbitrary",)))(place, g4, recv)


def _adamw_math(w, g, m, v):
    m = ADAM_B1 * m + (1.0 - ADAM_B1) * g
    v = ADAM_B2 * v + (1.0 - ADAM_B2) * (g * g)
    m_hat = m / (1.0 - ADAM_B1 ** ADAM_STEP)
    v_hat = v / (1.0 - ADAM_B2 ** ADAM_STEP)
    delta = -ADAM_LR * (m_hat / (jnp.sqrt(v_hat) + ADAM_EPS) + ADAM_WD * w)
    return delta, m, v


def adamw(w, g, m, v, name):
    r, c = w.shape
    rb = _row_block(r, c)

    def body(w_ref, g_ref, m_ref, v_ref, d_ref, mo_ref, vo_ref):
        d, mn, vn = _adamw_math(w_ref[...], g_ref[...], m_ref[...], v_ref[...])
        d_ref[...] = d
        mo_ref[...] = mn
        vo_ref[...] = vn

    o = jax.ShapeDtypeStruct((r, c), F32)
    spec = _row_spec(rb, c)
    return pl.pallas_call(
        body, grid=(r // rb,), name=name, out_shape=(o, o, o),
        in_specs=[spec] * 4, out_specs=(spec,) * 3,
        compiler_params=_cparams(("arbitrary",)))(w, g, m, v)


def _place():
    return lax.axis_index("x"), lax.axis_index("y"), lax.axis_index("c")


def _other_chips(x, y):
    return [(1 - x, y), (x, 1 - y), (1 - x, 1 - y)]


NOTHER = NSH - 1


def gather_rider(arrays):
    nw = len(arrays)
    nici = nw * NOTHER

    def copies(refs, send_sems, recv_sems):
        x, y, c = _place()
        ici, d2d = [], []
        for w in range(nw):
            for j, (px, py) in enumerate(_other_chips(x, y)):
                n = w * NOTHER + j
                sems = dict(send_sem=send_sems.at[n], recv_sem=recv_sems.at[n],
                            device_id=(px, py, c), device_id_type=MESH)
                mine = refs[w].at[2 * x + y, c]
                theirs = refs[w].at[2 * px + py, c]
                ici.append((pltpu.make_async_remote_copy(src_ref=mine, dst_ref=mine, **sems),
                            pltpu.make_async_remote_copy(src_ref=mine, dst_ref=theirs, **sems)))
                sems = dict(send_sem=send_sems.at[nici + n], recv_sem=recv_sems.at[nici + n],
                            device_id=(x, y, 1 - c), device_id_type=MESH)
                d2d.append((pltpu.make_async_remote_copy(src_ref=theirs, dst_ref=theirs, **sems),
                            pltpu.make_async_remote_copy(src_ref=theirs, dst_ref=refs[w].at[2 * px + py, 1 - c],
                                                         **sems)))
        return ici, d2d

    def start(ins, outs, send_sems, recv_sems):
        ici, _ = copies(outs, send_sems, recv_sems)
        for send, _ in ici:
            send.start()

    def finish(ins, outs, send_sems, recv_sems):
        ici, d2d = copies(outs, send_sems, recv_sems)
        for (_, landed), (forward, _) in zip(ici, d2d):
            landed.wait_recv()
            forward.start()
        for _, landed in d2d:
            landed.wait_recv()
        for send, _ in ici + d2d:
            send.wait_send()

    return Rider(arrays, [jax.ShapeDtypeStruct(a.shape, a.dtype) for a in arrays], {i: i for i in range(nw)},
                 2 * nici, start, finish)


def _peers(x, y, c):
    return [(x ^ (rel >> 2), y ^ ((rel >> 1) & 1), c ^ (rel & 1)) for rel in range(1, NDEV)]


def reduce_rider(grads):
    nw = len(grads)
    npeer = NDEV - 1

    def copies(ins, outs, send_sems, recv_sems):
        x, y, c = _place()
        return [pltpu.make_async_remote_copy(
            src_ref=ins[w].at[2 * px + py, pc], dst_ref=outs[w].at[r],
            send_sem=send_sems.at[w * npeer + r], recv_sem=recv_sems.at[w * npeer + r],
            device_id=(px, py, pc), device_id_type=MESH)
            for w in range(nw) for r, (px, py, pc) in enumerate(_peers(x, y, c))]

    def start(ins, outs, send_sems, recv_sems):
        for cp in copies(ins, outs, send_sems, recv_sems):
            cp.start()

    def finish(ins, outs, send_sems, recv_sems):
        for cp in copies(ins, outs, send_sems, recv_sems):
            cp.wait()

    return Rider(grads, [jax.ShapeDtypeStruct((npeer,) + g.shape[2:], g.dtype) for g in grads], {},
                 nw * npeer, start, finish)


class _Offset:
    def __init__(self, ref, base):
        self.ref, self.base = ref, base

    @property
    def at(self):
        return self

    def __getitem__(self, i):
        return self.ref.at[self.base + i]


def merge_riders(riders):
    ins, outs, aliases, spans, nsem = [], [], {}, [], 0
    for r in riders:
        spans.append((len(ins), len(outs), nsem))
        aliases.update({len(ins) + i: len(outs) + j for i, j in r.aliases.items()})
        ins, outs, nsem = ins + r.ins, outs + r.outs, nsem + r.nsem

    def each(step):
        def run(in_refs, out_refs, send_sems, recv_sems):
            for r, (i0, o0, s0) in zip(riders, spans):
                getattr(r, step)(in_refs[i0:i0 + len(r.ins)], out_refs[o0:o0 + len(r.outs)],
                                 _Offset(send_sems, s0), _Offset(recv_sems, s0))
        return run

    return Rider(ins, outs, aliases, nsem, each("start"), each("finish"))


def split_results(riders, results):
    out, o0 = [], 0
    for r in riders:
        out.append(tuple(results[o0:o0 + len(r.outs)]))
        o0 += len(r.outs)
    return out


def swap_rider(halves):
    nw = len(halves)

    def copies(refs, send_sems, recv_sems):
        x, y, c = _place()
        out = []
        for w in range(nw):
            sems = dict(send_sem=send_sems.at[w], recv_sem=recv_sems.at[w],
                        device_id=(x, y, 1 - c), device_id_type=MESH)
            mine = refs[w].at[c]
            out.append((pltpu.make_async_remote_copy(src_ref=mine, dst_ref=mine, **sems),
                        pltpu.make_async_remote_copy(src_ref=mine, dst_ref=refs[w].at[1 - c], **sems)))
        return out

    def start(ins, outs, send_sems, recv_sems):
        for send, _ in copies(outs, send_sems, recv_sems):
            send.start()

    def finish(ins, outs, send_sems, recv_sems):
        cps = copies(outs, send_sems, recv_sems)
        for _, recv in cps:
            recv.wait_recv()
        for send, _ in cps:
            send.wait_send()

    return Rider(halves, [jax.ShapeDtypeStruct(h.shape, h.dtype) for h in halves], {i: i for i in range(nw)},
                 nw, start, finish)


def allgather_rider(slots):
    def copies(ref, send_sems, recv_sems):
        x, y, c = _place()
        mine = ref.at[4 * x + 2 * y + c]
        out = []
        for r, peer in enumerate(_peers(x, y, c)):
            sems = dict(send_sem=send_sems.at[r], recv_sem=recv_sems.at[r], device_id=peer, device_id_type=MESH)
            out.append((pltpu.make_async_remote_copy(src_ref=mine, dst_ref=mine, **sems),
                        pltpu.make_async_remote_copy(
                            src_ref=mine, dst_ref=ref.at[4 * peer[0] + 2 * peer[1] + peer[2]], **sems)))
        return out

    def start(ins, outs, send_sems, recv_sems):
        for send, _ in copies(outs[0], send_sems, recv_sems):
            send.start()

    def finish(ins, outs, send_sems, recv_sems):
        cps = copies(outs[0], send_sems, recv_sems)
        for _, recv in cps:
            recv.wait_recv()
        for send, _ in cps:
            send.wait_send()

    return Rider([slots], [jax.ShapeDtypeStruct(slots.shape, slots.dtype)], {0: 0}, NDEV - 1, start, finish)


def sum_slots(slots):
    def body(s_ref, o_ref):
        acc = s_ref[0]
        for dev in range(1, NDEV):
            acc = acc + s_ref[dev]
        o_ref[...] = acc

    return pl.pallas_call(body, name="sum_slots", out_shape=jax.ShapeDtypeStruct(slots.shape[1:], F32),
                          compiler_params=_cparams())(slots)


BIG = ("ffn1_w_gu", "ffn1_w_down", "w_in", "w_a_out", "w_b_out", "w_out", "w_q", "w_kv", "w_o",
       "ffn2_w_gu", "ffn2_w_down")
SMALL = {"ffn1_norm": (0, 1), "mix_norm": (8, 1), "xattn_norm": (16, 1), "mem_norm": (24, 1),
         "ffn2_norm": (32, 1), "final_norm": (40, 1), "conv_b": (48, 1), "conv_ln_g": (56, 1),
         "conv_ln_b": (64, 1), "sgu_ln_g": (72, 1), "sgu_ln_b": (80, 1), "b_in": (88, 6),
         "conv_w": (96, CW), "sgu_w": (128, 64), "sgu_b": (192, 1)}
LOSS_ROW = 200
SMALL_ROWS = 208


def _pad_rows(a, rows):
    return jnp.pad(a, ((0, rows - a.shape[0]), (0, D - a.shape[1])))


def _pack_small(parts):
    names = sorted(parts, key=lambda n: SMALL[n][0] if n in SMALL else LOSS_ROW)
    rows = []
    for i, n in enumerate(names):
        start = SMALL[n][0] if n in SMALL else LOSS_ROW
        end = SMALL_ROWS if i + 1 == len(names) else (SMALL[names[i + 1]][0] if names[i + 1] in SMALL else LOSS_ROW)
        rows.append(_pad_rows(parts[n], end - start))
    return jnp.concatenate(rows, axis=0)


def _small_views(w):
    return {
        "ffn1_norm": w["ffn1_norm"], "mix_norm": w["mix_norm"], "xattn_norm": w["xattn_norm"],
        "mem_norm": w["mem_norm"], "ffn2_norm": w["ffn2_norm"], "final_norm": w["final_norm"].reshape(1, D),
        "conv_b": w["conv_b"], "conv_ln_g": w["conv_ln_g"], "conv_ln_b": w["conv_ln_b"],
        "sgu_ln_g": w["sgu_ln_g"], "sgu_ln_b": w["sgu_ln_b"], "b_in": w["b_in"].reshape(6, D),
        "conv_w": w["conv_w"][0], "sgu_w": w["sgu_w"].reshape(64, D), "sgu_b": w["sgu_b"].reshape(1, NG * CHUNK),
    }


def _unpack_small(buf, like, chip):
    out = {}
    for n, (start, rows) in SMALL.items():
        blk = buf[start:start + rows]
        if n == "conv_w":
            blk = blk[:, :like[n].shape[-1]] if chip is None else lax.dynamic_slice_in_dim(
                blk, chip * like[n].shape[-1], like[n].shape[-1], axis=1)
        elif n == "sgu_b":
            blk = blk[:, :NG * CHUNK]
        out[n] = blk.reshape(like[n].shape)
    return out


def kernel(x, mem, ffn1_norm, ffn1_w_gu, ffn1_w_down, mix_norm, w_in, b_in, conv_w, conv_b, conv_ln_g, conv_ln_b, w_a_out, sgu_ln_g, sgu_ln_b, sgu_w, sgu_b, w_b_out, w_out, xattn_norm, mem_norm, w_q, w_kv, w_o, ffn2_norm, ffn2_w_gu, ffn2_w_down, final_norm, loss_target, m_ffn1_norm, m_ffn1_w_gu, m_ffn1_w_down, m_mix_norm, m_w_in, m_b_in, m_conv_w, m_conv_b, m_conv_ln_g, m_conv_ln_b, m_w_a_out, m_sgu_ln_g, m_sgu_ln_b, m_sgu_w, m_sgu_b, m_w_b_out, m_w_out, m_xattn_norm, m_mem_norm, m_w_q, m_w_kv, m_w_o, m_ffn2_norm, m_ffn2_w_gu, m_ffn2_w_down, m_final_norm, v_ffn1_norm, v_ffn1_w_gu, v_ffn1_w_down, v_mix_norm, v_w_in, v_b_in, v_conv_w, v_conv_b, v_conv_ln_g, v_conv_ln_b, v_w_a_out, v_sgu_ln_g, v_sgu_ln_b, v_sgu_w, v_sgu_b, v_w_b_out, v_w_out, v_xattn_norm, v_mem_norm, v_w_q, v_w_kv, v_w_o, v_ffn2_norm, v_ffn2_w_gu, v_ffn2_w_down, v_final_norm):
    names = ("ffn1_norm", "ffn1_w_gu", "ffn1_w_down", "mix_norm", "w_in", "b_in", "conv_w", "conv_b",
             "conv_ln_g", "conv_ln_b", "w_a_out", "sgu_ln_g", "sgu_ln_b", "sgu_w", "sgu_b", "w_b_out", "w_out",
             "xattn_norm", "mem_norm", "w_q", "w_kv", "w_o", "ffn2_norm", "ffn2_w_gu", "ffn2_w_down",
             "final_norm")
    wts = dict(zip(names, (ffn1_norm, ffn1_w_gu, ffn1_w_down, mix_norm, w_in, b_in, conv_w, conv_b, conv_ln_g,
                           conv_ln_b, w_a_out, sgu_ln_g, sgu_ln_b, sgu_w, sgu_b, w_b_out, w_out, xattn_norm,
                           mem_norm, w_q, w_kv, w_o, ffn2_norm, ffn2_w_gu, ffn2_w_down, final_norm)))
    mom1 = dict(zip(names, (m_ffn1_norm, m_ffn1_w_gu, m_ffn1_w_down, m_mix_norm, m_w_in, m_b_in, m_conv_w,
                            m_conv_b, m_conv_ln_g, m_conv_ln_b, m_w_a_out, m_sgu_ln_g, m_sgu_ln_b, m_sgu_w,
                            m_sgu_b, m_w_b_out, m_w_out, m_xattn_norm, m_mem_norm, m_w_q, m_w_kv, m_w_o,
                            m_ffn2_norm, m_ffn2_w_gu, m_ffn2_w_down, m_final_norm)))
    mom2 = dict(zip(names, (v_ffn1_norm, v_ffn1_w_gu, v_ffn1_w_down, v_mix_norm, v_w_in, v_b_in, v_conv_w,
                            v_conv_b, v_conv_ln_g, v_conv_ln_b, v_w_a_out, v_sgu_ln_g, v_sgu_ln_b, v_sgu_w,
                            v_sgu_b, v_w_b_out, v_w_out, v_xattn_norm, v_mem_norm, v_w_q, v_w_kv, v_w_o,
                            v_ffn2_norm, v_ffn2_w_gu, v_ffn2_w_down, v_final_norm)))
    xi, yi, ci = _place()
    chip = (2 * xi + yi).astype(jnp.int32)
    core = ci.astype(jnp.int32)
    chip_arr = chip.reshape(1)
    place_arr = jnp.stack([chip, core])
    x2, mem2, tgt = x[0], mem[0], loss_target[0]

    slot = {n: cast_bf16(wts[n][0], chip_arr, "cast_" + n) for n in BIG}
    cw_pad = jnp.pad(conv_w[0], ((0, HALO - CW), (0, 0)))
    slot["conv_w"] = lax.dynamic_update_slice(jnp.zeros((NSH,) + cw_pad.shape, F32), cw_pad[None], (chip, 0, 0))
    g_first = ("ffn1_w_gu", "ffn1_w_down")
    g_mix = ("w_in", "w_a_out", "w_b_out", "w_out", "conv_w")
    g_rest = ("w_q", "w_kv", "w_o", "ffn2_w_gu", "ffn2_w_down")
    def gather(group):
        return gather_rider([slot[n].reshape(NSH, 2, slot[n].shape[1] // 2, slot[n].shape[2]) for n in group])

    def gathered(group, res):
        return {n: r.reshape(slot[n].shape) for n, r in zip(group, res)}

    full = gathered(g_first, run_rider(gather(g_first), "gather_ffn1"))
    wgu1, wd1 = full["ffn1_w_gu"], full["ffn1_w_down"].reshape(FF, D)
    tril = jnp.tril(jnp.ones((CHUNK, CHUNK), dtype=bool))
    ws = jnp.where(tril[None], sgu_w[0], 0.0).astype(BF)
    wst = jnp.transpose(ws, (0, 2, 1))
    sbias = jnp.repeat(jnp.transpose(sgu_b[0]), GD, axis=1)
    gfin = final_norm.reshape(1, D)

    (h1, gu1), rode = ffn_fwd(x2, ffn1_norm, wgu1, wd1, rider=gather(g_mix))
    full.update(gathered(g_mix, rode))
    win = full["w_in"]
    wa, wb, wout = (full[n].reshape(D, D) for n in ("w_a_out", "w_b_out", "w_out"))
    cw_full = jnp.transpose(full["conv_w"], (1, 0, 2)).reshape(HALO, D)
    (h2, proj, n2b, conv_out), rode = mix_fwd(
        h1, mix_norm, win, b_in, cw_full, conv_b, conv_ln_g, conv_ln_b, wa, sgu_ln_g, sgu_ln_b, ws, sbias, wb,
        wout, rider=gather(g_rest))
    full.update(gathered(g_rest, rode))
    wgu2, wd2, wkv = full["ffn2_w_gu"], full["ffn2_w_down"].reshape(FF, D), full["w_kv"]
    wq, wo = full["w_q"].reshape(D, D), full["w_o"].reshape(D, D)
    kb, vb, memn = kv_proj(mem2, mem_norm, wkv)
    h3 = xattn_fwd(h2, xattn_norm, wq, kb, vb, wo)
    dh4, gu2, loss_lanes, d_final = ffn_fwd_loss(h3, ffn2_norm, wgu2, wd2, gfin, tgt)

    own, halves = {}, {}

    def exchange(group, grads):
        views = []
        for n, (g, gb) in zip(group, grads):
            rs, cs = wts[n].shape[1:]
            own[n] = g.reshape(NSH, 2, rs // 2, cs)
            views.append(gb.reshape(NSH, 2, rs // 2, cs))
        return reduce_rider(views)

    def reduce(group, recv):
        for n, r in zip(group, recv):
            halves[n] = device_sum(own[n], r, place_arr, "device_sum_" + n)

    dh3, n4, a4, dgu4, dhb4, d_ffn2n = ffn_bwd(h3, gu2, dh4, ffn2_norm, wgu2, wd2, "ffn2_bwd")
    g_ffn2 = ("ffn2_w_gu", "ffn2_w_down")
    ride = exchange(g_ffn2, [dw_matmul(n4, dgu4, NSH, "dw_ffn2_gu")[:2], dw_matmul(a4, dhb4, 1, "dw_ffn2_down")[:2]])
    (dh2, n3, dq, att, dhb3, dk, dv, d_xn), rode = xattn_bwd(h2, dh3, xattn_norm, wq, kb, vb, wo, rider=ride)
    reduce(g_ffn2, rode)
    g_att = ("w_q", "w_o", "w_kv")
    d_wkv, d_wkv_b, d_memn = kv_bwd(mem2, mem_norm, memn, wkv, dk, dv)
    ride = exchange(g_att, [dw_matmul(n3, dq, 1, "dw_q")[:2], dw_matmul(att, dhb3, 1, "dw_o")[:2],
                            (d_wkv, d_wkv_b)])
    ((dconv, dproj, sa, dya, ob, dyb, mg, dhb2, d_sgu_w, d_sgu_b, d_lna_g, d_lna_b, d_lnb_g, d_lnb_b),
     rode) = mix_bwd_branches(proj, conv_out, dh2, conv_ln_g, conv_ln_b, wa, sgu_ln_g, sgu_ln_b, ws, wst,
                              sbias, wb, wout, rider=ride)
    reduce(g_att, rode)
    dproj, d_conv_w, d_conv_b = conv_bwd(proj, dconv, dproj, cw_full)
    dh1, d_mixn, d_b_in = mix_bwd_in(dproj, h1, dh2, mix_norm, win)
    g_mixw = ("w_in", "w_a_out", "w_b_out")
    ride = exchange(g_mixw, [dw_matmul(n2b, dproj, NSH, "dw_in")[:2], dw_matmul(sa, dya, 1, "dw_a_out")[:2],
                             dw_matmul(ob, dyb, 1, "dw_b_out")[:2]])
    dx, n1, a1, dgu1, dhb1, d_ffn1n = ffn_bwd(x2, gu1, dh1, ffn1_norm, wgu1, wd1, "ffn1_bwd")
    small_grads = {
        "ffn1_norm": d_ffn1n, "mix_norm": d_mixn, "xattn_norm": d_xn, "mem_norm": d_memn, "ffn2_norm": d_ffn2n,
        "final_norm": d_final, "conv_b": d_conv_b, "conv_ln_g": d_lna_g, "conv_ln_b": d_lna_b,
        "sgu_ln_g": d_lnb_g, "sgu_ln_b": d_lnb_b, "b_in": d_b_in.reshape(6, D), "conv_w": d_conv_w[:CW],
        "sgu_w": d_sgu_w.reshape(64, D), "sgu_b": jnp.transpose(d_sgu_b[:, :NG]).reshape(1, NG * CHUNK),
        "loss": loss_lanes}
    slots = lax.dynamic_update_slice(jnp.zeros((NDEV, SMALL_ROWS, D), F32), _pack_small(small_grads)[None],
                                     (2 * chip + core, 0, 0))
    riders = [ride, allgather_rider(slots)]
    d_wgu1, d_wgu1_b, rode = dw_matmul(n1, dgu1, NSH, "dw_ffn1_gu", rider=merge_riders(riders))
    recv_mix, all_slots = split_results(riders, rode)
    reduce(g_mixw, recv_mix)
    ride = exchange(("ffn1_w_gu",), [(d_wgu1, d_wgu1_b)])
    d_down, d_down_b, rode = dw_matmul(a1, dhb1, 1, "dw_ffn1_down", rider=ride)
    reduce(("ffn1_w_gu",), rode)
    ride = exchange(("ffn1_w_down",), [(d_down, d_down_b)])
    d_out, d_out_b, rode = dw_matmul(mg, dhb2, 1, "dw_out", rider=ride)
    reduce(("ffn1_w_down",), rode)
    reduce(("w_out",), run_rider(exchange(("w_out",), [(d_out, d_out_b)]), "exchange_w_out"))
    swapped = run_rider(swap_rider([halves[n] for n in BIG]), "pair_swap")
    gshard = {n: g.reshape(wts[n].shape[1:]) for n, g in zip(BIG, swapped)}

    small = sum_slots(all_slots[0])
    loss = (0.5 / D) * jnp.sum(small[LOSS_ROW])
    gsmall = _unpack_small(small, wts, chip)

    out_g, out_d, out_m, out_v = dict(gsmall), {}, {}, {}
    sw, sm, sv = (_pack_small(_small_views(t))[:LOSS_ROW] for t in (wts, mom1, mom2))
    sg = _pack_small(_small_views({n: gsmall[n] for n in SMALL}))[:LOSS_ROW]
    for dst, packed in zip((out_d, out_m, out_v), adamw(sw, sg, sm, sv, "adamw_small")):
        dst.update(_unpack_small(packed, wts, None))
    for n in BIG:
        shape = wts[n].shape
        out_g[n] = gshard[n].reshape(shape)
        d, mn, vn = adamw(wts[n][0], gshard[n], mom1[n][0], mom2[n][0], "adamw_" + n)
        out_d[n], out_m[n], out_v[n] = d.reshape(shape), mn.reshape(shape), vn.reshape(shape)
    return (loss, dx[None], *[out_g[n] for n in names], *[out_d[n] for n in names],
            *[out_m[n] for n in names], *[out_v[n] for n in names])
```

```python
import functools
import math

import jax
import jax.numpy as jnp
from jax import lax
from jax.experimental import pallas as pl
from jax.experimental.pallas import tpu as pltpu

F32 = jnp.float32
BF = jnp.bfloat16
MESH = pl.DeviceIdType.MESH

D = 1024
FF = 2816
HC = FF // 2
NSH = 4
DIN = 6 * D
INB = DIN // NSH
CW = 31
HALO = 32
CHUNK = 128
NG = 4
GD = D // NG
NH = 4
HD = D // NH
NMEM = 256
EPS_RMS = 1e-6
EPS_LN = 1e-5
GELU_C0 = math.sqrt(2.0 / math.pi)
GELU_C1 = 0.044715
ATT_SCALE = 1.0 / math.sqrt(HD)

ADAM_LR = 0.001
ADAM_B1 = 0.9
ADAM_B2 = 0.999
ADAM_EPS = 1e-08
ADAM_WD = 0.01
ADAM_STEP = 10

VMEM_LIMIT = 56 * 1024 * 1024


def _cparams(sem=None, **kw):
    if sem is not None:
        kw["dimension_semantics"] = sem
    return pltpu.CompilerParams(vmem_limit_bytes=VMEM_LIMIT, **kw)


def _dot(a, b):
    return jnp.dot(a, b, preferred_element_type=F32)


def _dot_nt(a, b):
    return lax.dot_general(a, b, (((1,), (1,)), ((), ())), preferred_element_type=F32)


def _dot_tn(a, b):
    return lax.dot_general(a, b, (((0,), (0,)), ((), ())), preferred_element_type=F32)


def _sigmoid(x):
    return 1.0 / (1.0 + jnp.exp(-x))


def _gelu(x):
    t = jnp.tanh(GELU_C0 * (x + GELU_C1 * (x * x * x)))
    return 0.5 * x * (1.0 + t), t


def _gelu_with_grad(x):
    x2 = x * x
    t = jnp.tanh(GELU_C0 * (x + GELU_C1 * (x2 * x)))
    onep = 1.0 + t
    hx = 0.5 * x
    grad = 0.5 * onep + hx * (1.0 - t * t) * (GELU_C0 + (3.0 * GELU_C0 * GELU_C1) * x2)
    return hx * onep, grad


def _mean(x):
    return jnp.mean(x, axis=-1, keepdims=True)


def _rms(x):
    r = lax.rsqrt(_mean(x * x) + EPS_RMS)
    return x * r, r


def _rms_bwd(dn, xh, r, g):
    dxh = dn * g
    return r * (dxh - xh * _mean(dxh * xh))


def _ln(x):
    xc = x - _mean(x)
    r = lax.rsqrt(_mean(xc * xc) + EPS_LN)
    return xc * r, r


def _ln_bwd(dy, xh, r, g):
    dxh = dy * g
    return r * (dxh - _mean(dxh) - xh * _mean(dxh * xh))


def _colsum(x):
    return jnp.sum(x, axis=0, keepdims=True)


def _const_spec(shape):
    nd = len(shape)
    return pl.BlockSpec(shape, lambda *_: (0,) * nd, pipeline_mode=pl.Buffered(1))


def _row_spec(ts, width):
    return pl.BlockSpec((ts, width), lambda i: (i, 0))


def _acc_spec(shape):
    nd = len(shape)
    return pl.BlockSpec(shape, lambda *_: (0,) * nd)


def _tile(s, want):
    return min(s, want)


HBM_SPEC = pl.BlockSpec(memory_space=pltpu.HBM)


class Rider:
    def __init__(self, ins, outs, aliases, nsem, start, finish):
        self.ins, self.outs, self.aliases, self.nsem = list(ins), list(outs), dict(aliases), nsem
        self.start, self.finish = start, finish


def _pcall(body, *, name, grid, args, in_specs, out_shape, out_specs, scratch=(), rider=None):
    sem = ("arbitrary",) * len(grid)
    n_in, n_out = len(args), len(out_shape)
    if rider is None:
        res = pl.pallas_call(
            body, grid=grid, name=name, out_shape=tuple(out_shape), in_specs=list(in_specs),
            out_specs=tuple(out_specs), scratch_shapes=list(scratch), compiler_params=_cparams(sem))(*args)
        return tuple(res), ()
    r_in, r_out = len(rider.ins), len(rider.outs)

    def wrapped(*refs):
        a, ri = refs[:n_in], refs[n_in:n_in + r_in]
        o = refs[n_in + r_in:n_in + r_in + n_out]
        ro = refs[n_in + r_in + n_out:n_in + r_in + n_out + r_out]
        s, (send, recv) = refs[n_in + r_in + n_out + r_out:-2], refs[-2:]
        first = functools.reduce(jnp.logical_and, [pl.program_id(d) == 0 for d in range(len(grid))])
        last = functools.reduce(jnp.logical_and, [pl.program_id(d) == g - 1 for d, g in enumerate(grid)])

        @pl.when(first)
        def _():
            rider.start(ri, ro, send, recv)

        body(*a, *o, *s)

        @pl.when(last)
        def _():
            rider.finish(ri, ro, send, recv)

    res = pl.pallas_call(
        wrapped, grid=grid, name=name, out_shape=tuple(out_shape) + tuple(rider.outs),
        in_specs=list(in_specs) + [HBM_SPEC] * r_in, out_specs=tuple(out_specs) + (HBM_SPEC,) * r_out,
        scratch_shapes=list(scratch) + [pltpu.SemaphoreType.DMA((rider.nsem,)),
                                        pltpu.SemaphoreType.DMA((rider.nsem,))],
        input_output_aliases={n_in + i: n_out + j for i, j in rider.aliases.items()},
        compiler_params=_cparams(sem, has_side_effects=True))(*args, *rider.ins)
    return tuple(res[:n_out]), tuple(res[n_out:])


def run_rider(rider, name):
    r_in = len(rider.ins)

    def body(*refs):
        ri, ro, (send, recv) = refs[:r_in], refs[r_in:-2], refs[-2:]
        rider.start(ri, ro, send, recv)
        rider.finish(ri, ro, send, recv)

    return pl.pallas_call(
        body, name=name, out_shape=tuple(rider.outs), in_specs=[HBM_SPEC] * r_in,
        out_specs=(HBM_SPEC,) * len(rider.outs),
        scratch_shapes=[pltpu.SemaphoreType.DMA((rider.nsem,)), pltpu.SemaphoreType.DMA((rider.nsem,))],
        input_output_aliases=rider.aliases,
        compiler_params=pltpu.CompilerParams(has_side_effects=True))(*rider.ins)


FFN_BWD_TILE = 256


def _ffn_apply(x, g_ref, wgu_ref, wd_ref, gu_ref):
    xh, _ = _rms(x)
    nb = (xh * g_ref[...]).astype(BF)
    acc = jnp.zeros(x.shape, F32)
    for j in range(2):
        g = _dot(nb, wgu_ref[j])
        u = _dot(nb, wgu_ref[2 + j])
        gu_ref[:, j * HC:(j + 1) * HC] = g.astype(BF)
        gu_ref[:, FF + j * HC:FF + (j + 1) * HC] = u.astype(BF)
        a = (g * _sigmoid(g) * u).astype(BF)
        acc = acc + _dot(a, wd_ref[j * HC:(j + 1) * HC, :])
    return x + 0.5 * acc


def ffn_hidden(h, gain, wgu, rider=None):
    s = h.shape[0]
    ts = _tile(s, 512)

    def body(h_ref, g_ref, wgu_ref, gu_ref, a_ref):
        xh, _ = _rms(h_ref[...])
        nb = (xh * g_ref[...]).astype(BF)
        for j in range(2):
            g = _dot(nb, wgu_ref[j])
            u = _dot(nb, wgu_ref[2 + j])
            gu_ref[:, j * HC:(j + 1) * HC] = g.astype(BF)
            gu_ref[:, FF + j * HC:FF + (j + 1) * HC] = u.astype(BF)
            a_ref[:, j * HC:(j + 1) * HC] = (g * _sigmoid(g) * u).astype(BF)

    return _pcall(
        body, grid=(s // ts,), name="ffn1_hidden", args=(h, gain, wgu),
        out_shape=[jax.ShapeDtypeStruct((s, 2 * FF), BF), jax.ShapeDtypeStruct((s, FF), BF)],
        in_specs=[_row_spec(ts, D), _const_spec((1, D)), _const_spec((NSH, D, HC))],
        out_specs=[_row_spec(ts, 2 * FF), _row_spec(ts, FF)], rider=rider)


def ffn_down(h, a, wd):
    s = h.shape[0]
    ts = _tile(s, 512)

    def body(h_ref, a_ref, wd_ref, o_ref):
        o_ref[...] = h_ref[...] + 0.5 * _dot(a_ref[...], wd_ref[...])

    return pl.pallas_call(
        body, grid=(s // ts,), name="ffn1_down", out_shape=jax.ShapeDtypeStruct((s, D), F32),
        in_specs=[_row_spec(ts, D), _row_spec(ts, FF), _const_spec((FF, D))], out_specs=_row_spec(ts, D),
        compiler_params=_cparams(("arbitrary",)))(h, a, wd)


def ffn_fwd_loss(h, gain, wgu, wd, gfin, target):
    s = h.shape[0]
    ts = _tile(s, 512)

    def body(h_ref, g_ref, wgu_ref, wd_ref, gf_ref, t_ref, dh_ref, gu_ref, loss_ref, dgf_ref):
        @pl.when(pl.program_id(0) == 0)
        def _():
            loss_ref[...] = jnp.zeros_like(loss_ref)
            dgf_ref[...] = jnp.zeros_like(dgf_ref)

        h4 = _ffn_apply(h_ref[...], g_ref, wgu_ref, wd_ref, gu_ref)
        yh, r4 = _rms(h4)
        gf = gf_ref[...]
        e = yh * gf - t_ref[...]
        loss_ref[...] += _colsum(e * e)
        dy = e * (1.0 / D)
        dgf_ref[...] += _colsum(dy * yh)
        dh_ref[...] = _rms_bwd(dy, yh, r4, gf)

    return pl.pallas_call(
        body, grid=(s // ts,), name="ffn_fwd_loss",
        out_shape=(jax.ShapeDtypeStruct((s, D), F32), jax.ShapeDtypeStruct((s, 2 * FF), BF),
                   jax.ShapeDtypeStruct((1, D), F32), jax.ShapeDtypeStruct((1, D), F32)),
        in_specs=[_row_spec(ts, D), _const_spec((1, D)), _const_spec((NSH, D, HC)), _const_spec((FF, D)),
                  _const_spec((1, D)), _row_spec(ts, D)],
        out_specs=(_row_spec(ts, D), _row_spec(ts, 2 * FF), _acc_spec((1, D)), _acc_spec((1, D))),
        compiler_params=_cparams(("arbitrary",)),
    )(h, gain, wgu, wd, gfin, target)


def ffn_bwd(h, gu, dh, gain, wgu, wd, name):
    s = h.shape[0]
    ts = _tile(s, FFN_BWD_TILE)

    def body(h_ref, gu_ref, dh_ref, g_ref, wgu_ref, wd_ref, dx_ref, n_ref, a_ref, dgu_ref, dhb_ref, dg_ref):
        @pl.when(pl.program_id(0) == 0)
        def _():
            dg_ref[...] = jnp.zeros_like(dg_ref)

        x = h_ref[...]
        dh = dh_ref[...]
        gain_v = g_ref[...]
        xh, r = _rms(x)
        n_ref[...] = (xh * gain_v).astype(BF)
        dhb = (0.5 * dh).astype(BF)
        dhb_ref[...] = dhb
        dn = jnp.zeros((ts, D), F32)
        for j in range(2):
            g = gu_ref[:, j * HC:(j + 1) * HC].astype(F32)
            u = gu_ref[:, FF + j * HC:FF + (j + 1) * HC].astype(F32)
            sg = _sigmoid(g)
            sl = g * sg
            a_ref[:, j * HC:(j + 1) * HC] = (sl * u).astype(BF)
            da = _dot_nt(dhb, wd_ref[j * HC:(j + 1) * HC, :])
            dgb = (da * u * (sg * (1.0 + g * (1.0 - sg)))).astype(BF)
            dub = (da * sl).astype(BF)
            dgu_ref[:, j * HC:(j + 1) * HC] = dgb
            dgu_ref[:, FF + j * HC:FF + (j + 1) * HC] = dub
            dn = dn + _dot_nt(dgb, wgu_ref[j]) + _dot_nt(dub, wgu_ref[2 + j])
        dg_ref[...] += _colsum(dn * xh)
        dx_ref[...] = dh + _rms_bwd(dn, xh, r, gain_v)

    return pl.pallas_call(
        body, grid=(s // ts,), name=name,
        out_shape=(jax.ShapeDtypeStruct((s, D), F32), jax.ShapeDtypeStruct((s, D), BF),
                   jax.ShapeDtypeStruct((s, FF), BF), jax.ShapeDtypeStruct((s, 2 * FF), BF),
                   jax.ShapeDtypeStruct((s, D), BF), jax.ShapeDtypeStruct((1, D), F32)),
        in_specs=[_row_spec(ts, D), _row_spec(ts, 2 * FF), _row_spec(ts, D), _const_spec((1, D)),
                  _const_spec((NSH, D, HC)), _const_spec((FF, D))],
        out_specs=(_row_spec(ts, D), _row_spec(ts, D), _row_spec(ts, FF), _row_spec(ts, 2 * FF),
                   _row_spec(ts, D), _acc_spec((1, D))),
        compiler_params=_cparams(("arbitrary",)),
    )(h, gu, dh, gain, wgu, wd)


def dw_matmul(x, dy, nsplit, name, rider=None):
    s, k = x.shape
    n = dy.shape[1]
    nb = n // nsplit
    ts = _tile(s, 1024)
    nsteps = s // ts

    def body(x_ref, dy_ref, o_ref, ob_ref):
        @pl.when(pl.program_id(1) == 0)
        def _():
            o_ref[...] = jnp.zeros_like(o_ref)

        o_ref[0] += _dot_tn(x_ref[...], dy_ref[...])

        @pl.when(pl.program_id(1) == nsteps - 1)
        def _():
            ob_ref[...] = o_ref[...].astype(BF)

    spec = pl.BlockSpec((1, k, nb), lambda j, i: (j, 0, 0))
    (out, outb), rode = _pcall(
        body, grid=(nsplit, nsteps), name=name, args=(x, dy),
        out_shape=[jax.ShapeDtypeStruct((nsplit, k, nb), F32), jax.ShapeDtypeStruct((nsplit, k, nb), BF)],
        in_specs=[pl.BlockSpec((ts, k), lambda j, i: (i, 0)), pl.BlockSpec((ts, nb), lambda j, i: (i, j))],
        out_specs=[spec, spec], rider=rider)
    return out, outb, rode


def _split_in_proj(p, b):
    h = INB - D
    a_val = p[0][:, :D] + b[:, 0:D]
    a_gate = jnp.concatenate([p[0][:, D:], p[1][:, :h]], axis=1) + b[:, D:2 * D]
    b_u = p[1][:, h:] + b[:, 2 * D:3 * D]
    b_v = p[2][:, :D] + b[:, 3 * D:4 * D]
    g_a = jnp.concatenate([p[2][:, D:], p[3][:, :h]], axis=1) + b[:, 4 * D:5 * D]
    g_b = p[3][:, h:] + b[:, 5 * D:6 * D]
    return a_val, a_gate, b_u, b_v, g_a, g_b


def _sgu_mix(vnb, ws_ref, sb_ref, mixed_ref, ts):
    for ci in range(ts // CHUNK):
        rows = slice(ci * CHUNK, (ci + 1) * CHUNK)
        for g in range(NG):
            cols = slice(g * GD, (g + 1) * GD)
            mixed_ref[rows, cols] = _dot(ws_ref[g], vnb[rows, cols]) + sb_ref[:, cols]


SUB = 8
CB = 128
SH_ROWS_EXTRA = HALO - SUB


def _shifted_copies(ext_ref, sh_ref, lanes, ts):
    for b in range(1, SUB):
        sh_ref[b - 1] = ext_ref[b:b + ts + SH_ROWS_EXTRA, lanes]


def _window(ext_ref, sh_ref, lanes, first, r0, nrows):
    b = first % SUB
    a = first - b
    if b == 0:
        return ext_ref[a + r0:a + r0 + nrows, lanes]
    return sh_ref[b - 1, a + r0:a + r0 + nrows, :]


def mix_fwd(h, gain, win, b_in, conv_w, conv_b, lna_g, lna_b, wa, lnb_g, lnb_b, ws, sbias, wb, wo, rider=None):
    s = h.shape[0]
    ts = _tile(s, 256)

    def body(h_ref, g_ref, win_ref, bin_ref, cw_ref, cb_ref, lag_ref, lab_ref, wa_ref, lbg_ref, lbb_ref,
             ws_ref, sb_ref, wb_ref, wo_ref, o_ref, p_ref, n_ref, c_ref, ext_ref, mixed_ref, sh_ref):
        @pl.when(pl.program_id(0) == 0)
        def _():
            ext_ref[0:HALO, :] = jnp.zeros((HALO, D), F32)

        x = h_ref[...]
        xh, _ = _rms(x)
        nb = (xh * g_ref[...]).astype(BF)
        n_ref[...] = nb
        b = bin_ref[...]
        p = []
        for k in range(NSH):
            pk = _dot(nb, win_ref[k])
            p_ref[:, k * INB:(k + 1) * INB] = (pk + b[:, k * INB:(k + 1) * INB]).astype(BF)
            p.append(pk)
        a_val, a_gate, b_u, b_v, g_a, g_b = _split_in_proj(p, b)
        ext_ref[HALO:HALO + ts, :] = a_val * _sigmoid(a_gate)
        for l0 in range(0, D, CB):
            lanes = slice(l0, l0 + CB)
            _shifted_copies(ext_ref, sh_ref, lanes, ts)
            for r0 in range(0, ts, CB):
                acc = jnp.zeros((CB, CB), F32) + cb_ref[:, lanes]
                for k in range(CW):
                    acc = acc + cw_ref[k:k + 1, lanes] * _window(ext_ref, sh_ref, lanes,
                                                                 HALO - (CW - 1) + k, r0, CB)
                c_ref[r0:r0 + CB, lanes] = acc
        ext_ref[0:HALO, :] = ext_ref[ts:ts + HALO, :]
        ch, _ = _ln(c_ref[...])
        la = ch * lag_ref[...] + lab_ref[...]
        sa = (la * _sigmoid(la)).astype(BF)
        ya = _dot(sa, wa_ref[...])
        ub, _ = _gelu(b_u)
        gv, _ = _gelu(b_v)
        vh, _ = _ln(gv)
        vnb = (vh * lbg_ref[...] + lbb_ref[...]).astype(BF)
        _sgu_mix(vnb, ws_ref, sb_ref, mixed_ref, ts)
        ob = (ub * mixed_ref[...]).astype(BF)
        yb = _dot(ob, wb_ref[...])
        merged = (_sigmoid(g_a) * ya + _sigmoid(g_b) * yb).astype(BF)
        o_ref[...] = x + _dot(merged, wo_ref[...])

    vec = _const_spec((1, D))
    sq = _const_spec((D, D))
    return _pcall(
        body, grid=(s // ts,), name="mix_fwd",
        args=(h, gain, win, b_in, conv_w, conv_b, lna_g, lna_b, wa, lnb_g, lnb_b, ws, sbias, wb, wo),
        out_shape=(jax.ShapeDtypeStruct((s, D), F32), jax.ShapeDtypeStruct((s, DIN), BF),
                   jax.ShapeDtypeStruct((s, D), BF), jax.ShapeDtypeStruct((s, D), F32)),
        in_specs=[_row_spec(ts, D), vec, _const_spec((NSH, D, INB)), _const_spec((1, DIN)),
                  _const_spec((HALO, D)), vec, vec, vec, sq, vec, vec,
                  _const_spec((NG, CHUNK, CHUNK)), _const_spec((CHUNK, D)), sq, sq],
        out_specs=(_row_spec(ts, D), _row_spec(ts, DIN), _row_spec(ts, D), _row_spec(ts, D)),
        scratch=[pltpu.VMEM((ts + HALO, D), F32), pltpu.VMEM((ts, D), F32),
                 pltpu.VMEM((SUB - 1, ts + SH_ROWS_EXTRA, CB), F32)], rider=rider)


def mix_bwd_branches(p, c, dh, lna_g, lna_b, wa, lnb_g, lnb_b, ws, wst, sbias, wb, wo, rider=None):
    s = dh.shape[0]
    ts = _tile(s, 256)
    nsteps = s // ts

    def body(p_ref, c_ref, dh_ref, lag_ref, lab_ref, wa_ref, lbg_ref, lbb_ref, ws_ref, wst_ref, sb_ref,
             wb_ref, wo_ref, dc_ref, dp_ref, sa_ref, dya_ref, ob_ref, dyb_ref, mg_ref, dhb_ref,
             dws_ref, dsb_ref, dlag_ref, dlab_ref, dlbg_ref, dlbb_ref, mixed_ref, dmix_ref, dvn_ref, dsb_acc):
        step = pl.program_id(0)

        @pl.when(step == 0)
        def _():
            for ref in (dws_ref, dsb_acc, dlag_ref, dlab_ref, dlbg_ref, dlbb_ref):
                ref[...] = jnp.zeros_like(ref)

        b_u = p_ref[:, 2 * D:3 * D].astype(F32)
        b_v = p_ref[:, 3 * D:4 * D].astype(F32)
        sga = _sigmoid(p_ref[:, 4 * D:5 * D].astype(F32))
        sgb = _sigmoid(p_ref[:, 5 * D:6 * D].astype(F32))
        lag = lag_ref[...]
        ch, ra = _ln(c_ref[...])
        la = ch * lag + lab_ref[...]
        sla = _sigmoid(la)
        sa = (la * sla).astype(BF)
        sa_ref[...] = sa
        ya = _dot(sa, wa_ref[...])
        lbg = lbg_ref[...]
        ub, dub = _gelu_with_grad(b_u)
        gv, dgv = _gelu_with_grad(b_v)
        vh, rb = _ln(gv)
        vnb = (vh * lbg + lbb_ref[...]).astype(BF)
        _sgu_mix(vnb, ws_ref, sb_ref, mixed_ref, ts)
        mixed = mixed_ref[...]
        ob = (ub * mixed).astype(BF)
        ob_ref[...] = ob
        yb = _dot(ob, wb_ref[...])
        mg_ref[...] = (sga * ya + sgb * yb).astype(BF)
        dhb = dh_ref[...].astype(BF)
        dhb_ref[...] = dhb
        dm = _dot_nt(dhb, wo_ref[...])
        dp_ref[:, 0:2 * D] = jnp.zeros((ts, 2 * D), BF)
        dp_ref[:, 4 * D:5 * D] = (dm * ya * sga * (1.0 - sga)).astype(BF)
        dp_ref[:, 5 * D:6 * D] = (dm * yb * sgb * (1.0 - sgb)).astype(BF)
        dya = (dm * sga).astype(BF)
        dya_ref[...] = dya
        dyb = (dm * sgb).astype(BF)
        dyb_ref[...] = dyb
        dla = _dot_nt(dya, wa_ref[...]) * (sla * (1.0 + la * (1.0 - sla)))
        dlag_ref[...] += _colsum(dla * ch)
        dlab_ref[...] += _colsum(dla)
        dc_ref[...] = _ln_bwd(dla, ch, ra, lag)
        dob = _dot_nt(dyb, wb_ref[...])
        dp_ref[:, 2 * D:3 * D] = (dob * mixed * dub).astype(BF)
        dmix = dob * ub
        dmix_ref[...] = dmix.astype(BF)
        dsb = jnp.zeros((CHUNK, D), F32)
        for ci in range(ts // CHUNK):
            rows = slice(ci * CHUNK, (ci + 1) * CHUNK)
            dsb = dsb + dmix[rows, :]
            for g in range(NG):
                cols = slice(g * GD, (g + 1) * GD)
                dmb = dmix_ref[rows, cols]
                dws_ref[g] += _dot_nt(dmb, vnb[rows, cols])
                dvn_ref[rows, cols] = _dot(wst_ref[g], dmb)
        dsb_acc[...] += dsb
        dvn = dvn_ref[...]
        dlbg_ref[...] += _colsum(dvn * vh)
        dlbb_ref[...] += _colsum(dvn)
        dp_ref[:, 3 * D:4 * D] = (_ln_bwd(dvn, vh, rb, lbg) * dgv).astype(BF)

        @pl.when(step == nsteps - 1)
        def _():
            row = lax.broadcasted_iota(jnp.int32, (CHUNK, CHUNK), 0)
            col = lax.broadcasted_iota(jnp.int32, (CHUNK, CHUNK), 1)
            for g in range(NG):
                dws_ref[g] = jnp.where(col <= row, dws_ref[g], 0.0)
            acc = jnp.zeros((CHUNK, CHUNK), F32)
            for g in range(NG):
                tot = jnp.sum(dsb_acc[:, g * GD:(g + 1) * GD], axis=-1, keepdims=True)
                acc = acc + jnp.where(col == g, tot, 0.0)
            dsb_ref[...] = acc

    vec = _const_spec((1, D))
    sq = _const_spec((D, D))
    bf_rows = jax.ShapeDtypeStruct((s, D), BF)
    acc_vec = jax.ShapeDtypeStruct((1, D), F32)
    return _pcall(
        body, grid=(nsteps,), name="mix_bwd_branches",
        args=(p, c, dh, lna_g, lna_b, wa, lnb_g, lnb_b, ws, wst, sbias, wb, wo),
        out_shape=(jax.ShapeDtypeStruct((s, D), F32), jax.ShapeDtypeStruct((s, DIN), BF),
                   bf_rows, bf_rows, bf_rows, bf_rows, bf_rows, bf_rows,
                   jax.ShapeDtypeStruct((NG, CHUNK, CHUNK), F32), jax.ShapeDtypeStruct((CHUNK, CHUNK), F32),
                   acc_vec, acc_vec, acc_vec, acc_vec),
        in_specs=[_row_spec(ts, DIN), _row_spec(ts, D), _row_spec(ts, D), vec, vec, sq, vec, vec,
                  _const_spec((NG, CHUNK, CHUNK)), _const_spec((NG, CHUNK, CHUNK)), _const_spec((CHUNK, D)),
                  sq, sq],
        out_specs=(_row_spec(ts, D), _row_spec(ts, DIN)) + (_row_spec(ts, D),) * 6
        + (_acc_spec((NG, CHUNK, CHUNK)), _acc_spec((CHUNK, CHUNK))) + (_acc_spec((1, D)),) * 4,
        scratch=[pltpu.VMEM((ts, D), F32), pltpu.VMEM((ts, D), BF), pltpu.VMEM((ts, D), F32),
                 pltpu.VMEM((CHUNK, D), F32)], rider=rider)


def conv_bwd(p, dc, dp, conv_w):
    s = dc.shape[0]
    ts = _tile(s, 256)
    nsteps = s // ts
    per = ts // HALO

    rb = 16

    def body(pm_ref, pp_ref, dcm_ref, dcn_ref, cw_ref, dpin_ref, dp_ref, dw_ref, db_ref, ext_ref, dext_ref,
             dw8_ref, sh_ref, dsh_ref, dglu_ref):
        del dpin_ref
        step = pl.program_id(0)

        @pl.when(step == 0)
        def _():
            dw8_ref[...] = jnp.zeros_like(dw8_ref)
            db_ref[...] = jnp.zeros_like(db_ref)

        a_val = pm_ref[:, 0:D].astype(F32)
        sg = _sigmoid(pm_ref[:, D:2 * D].astype(F32))
        prev = pp_ref[:, 0:D].astype(F32) * _sigmoid(pp_ref[:, D:2 * D].astype(F32))
        ext_ref[0:HALO, :] = jnp.where(step > 0, prev, 0.0)
        ext_ref[HALO:HALO + ts, :] = a_val * sg
        dcm = dcm_ref[...]
        dext_ref[0:ts, :] = dcm
        dext_ref[ts:ts + HALO, :] = jnp.where(step < nsteps - 1, dcn_ref[...], 0.0)
        db_ref[...] += _colsum(dcm)
        for l0 in range(0, D, CB):
            lanes = slice(l0, l0 + CB)
            _shifted_copies(dext_ref, dsh_ref, lanes, ts)
            for r0 in range(0, ts, CB):
                acc = jnp.zeros((CB, CB), F32)
                for k in range(CW):
                    acc = acc + cw_ref[k:k + 1, lanes] * _window(dext_ref, dsh_ref, lanes, CW - 1 - k, r0, CB)
                dglu_ref[r0:r0 + CB, lanes] = acc
            _shifted_copies(ext_ref, sh_ref, lanes, ts)
            accs = [jnp.zeros((SUB, CB), F32) for _ in range(CW)]
            for r0 in range(0, ts, rb):
                dcb = dext_ref[r0:r0 + rb, lanes]
                for k in range(CW):
                    prod = dcb * _window(ext_ref, sh_ref, lanes, HALO - (CW - 1) + k, r0, rb)
                    accs[k] = accs[k] + jnp.sum(prod.reshape(rb // SUB, SUB, CB), axis=0)
            for k in range(CW):
                dw8_ref[k, :, lanes] += accs[k]
        dglu = dglu_ref[...]
        dp_ref[:, 0:D] = (dglu * sg).astype(BF)
        dp_ref[:, D:2 * D] = (dglu * a_val * sg * (1.0 - sg)).astype(BF)

        @pl.when(step == nsteps - 1)
        def _():
            dw_ref[...] = jnp.zeros_like(dw_ref)
            for k in range(CW):
                dw_ref[k:k + 1, :] = _colsum(dw8_ref[k])

    return pl.pallas_call(
        body, grid=(nsteps,), name="conv_bwd",
        out_shape=(jax.ShapeDtypeStruct((s, DIN), BF), jax.ShapeDtypeStruct((HALO, D), F32),
                   jax.ShapeDtypeStruct((1, D), F32)),
        in_specs=[pl.BlockSpec((ts, 2 * D), lambda i: (i, 0)),
                  pl.BlockSpec((HALO, 2 * D), lambda i: (jnp.maximum(i * per - 1, 0), 0)),
                  _row_spec(ts, D),
                  pl.BlockSpec((HALO, D), lambda i: (jnp.minimum((i + 1) * per, s // HALO - 1), 0)),
                  _const_spec((HALO, D)),
                  pl.BlockSpec(memory_space=pl.ANY)],
        out_specs=(pl.BlockSpec((ts, 2 * D), lambda i: (i, 0)), _acc_spec((HALO, D)), _acc_spec((1, D))),
        scratch_shapes=[pltpu.VMEM((ts + HALO, D), F32), pltpu.VMEM((ts + HALO, D), F32),
                        pltpu.VMEM((HALO, SUB, D), F32),
                        pltpu.VMEM((SUB - 1, ts + SH_ROWS_EXTRA, CB), F32),
                        pltpu.VMEM((SUB - 1, ts + SH_ROWS_EXTRA, CB), F32),
                        pltpu.VMEM((ts, D), F32)],
        input_output_aliases={5: 0},
        compiler_params=_cparams(("arbitrary",)),
    )(p, p, dc, dc, conv_w, dp)


def mix_bwd_in(dp, h, dh, gain, win):
    s = h.shape[0]
    ts = _tile(s, 512)

    def body(dp_ref, h_ref, dh_ref, g_ref, win_ref, dx_ref, dg_ref, db_ref):
        @pl.when(pl.program_id(0) == 0)
        def _():
            dg_ref[...] = jnp.zeros_like(dg_ref)
            db_ref[...] = jnp.zeros_like(db_ref)

        gain_v = g_ref[...]
        xh, r = _rms(h_ref[...])
        dn = jnp.zeros((ts, D), F32)
        for k in range(NSH):
            dpk = dp_ref[:, k * INB:(k + 1) * INB]
            dn = dn + _dot_nt(dpk, win_ref[k])
            db_ref[:, k * INB:(k + 1) * INB] += _colsum(dpk.astype(F32))
        dg_ref[...] += _colsum(dn * xh)
        dx_ref[...] = dh_ref[...] + _rms_bwd(dn, xh, r, gain_v)

    return pl.pallas_call(
        body, grid=(s // ts,), name="mix_bwd_in",
        out_shape=(jax.ShapeDtypeStruct((s, D), F32), jax.ShapeDtypeStruct((1, D), F32),
                   jax.ShapeDtypeStruct((1, DIN), F32)),
        in_specs=[_row_spec(ts, DIN), _row_spec(ts, D), _row_spec(ts, D), _const_spec((1, D)),
                  _const_spec((NSH, D, INB))],
        out_specs=(_row_spec(ts, D), _acc_spec((1, D)), _acc_spec((1, DIN))),
        compiler_params=_cparams(("arbitrary",)),
    )(dp, h, dh, gain, win)


def kv_proj(mem, gain, wkv):
    def body(m_ref, g_ref, w_ref, k_ref, v_ref, n_ref):
        xh, _ = _rms(m_ref[...])
        nb = (xh * g_ref[...]).astype(BF)
        n_ref[...] = nb
        half = D // 2
        for j in range(2):
            k_ref[:, j * half:(j + 1) * half] = _dot(nb, w_ref[j]).astype(BF)
            v_ref[:, j * half:(j + 1) * half] = _dot(nb, w_ref[2 + j]).astype(BF)

    o = jax.ShapeDtypeStruct((NMEM, D), BF)
    return pl.pallas_call(body, name="kv_proj", out_shape=(o, o, o), compiler_params=_cparams())(mem, gain, wkv)


def kv_bwd(mem, gain, memn, wkv, dk, dv):
    def body(m_ref, g_ref, n_ref, w_ref, dk_ref, dv_ref, dw_ref, dwb_ref, dg_ref):
        xh, _ = _rms(m_ref[...])
        nb = n_ref[...]
        half = D // 2
        dn = jnp.zeros((NMEM, D), F32)
        for j in range(2):
            dkb = dk_ref[:, j * half:(j + 1) * half].astype(BF)
            dvb = dv_ref[:, j * half:(j + 1) * half].astype(BF)
            for slot, dyb in ((j, dkb), (2 + j, dvb)):
                dw = _dot_tn(nb, dyb)
                dw_ref[slot] = dw
                dwb_ref[slot] = dw.astype(BF)
            dn = dn + _dot_nt(dkb, w_ref[j]) + _dot_nt(dvb, w_ref[2 + j])
        dg_ref[...] = _colsum(dn * xh)

    return pl.pallas_call(
        body, name="kv_bwd",
        out_shape=(jax.ShapeDtypeStruct((NSH, D, D // 2), F32), jax.ShapeDtypeStruct((NSH, D, D // 2), BF),
                   jax.ShapeDtypeStruct((1, D), F32)),
        compiler_params=_cparams())(mem, gain, memn, wkv, dk, dv)


def _attend(qb, k_ref, v_ref, h):
    cols = slice(h * HD, (h + 1) * HD)
    sc = _dot_nt(qb[:, cols], k_ref[:, cols]) * ATT_SCALE
    e = jnp.exp(sc - jnp.max(sc, axis=-1, keepdims=True))
    pr = e / jnp.sum(e, axis=-1, keepdims=True)
    return pr, _dot(pr.astype(BF), v_ref[:, cols])


def xattn_fwd(h, gain, wq, k, v, wo):
    s = h.shape[0]
    ts = _tile(s, 1024)

    def body(h_ref, g_ref, wq_ref, k_ref, v_ref, wo_ref, o_ref, att_ref):
        x = h_ref[...]
        xh, _ = _rms(x)
        nb = (xh * g_ref[...]).astype(BF)
        qb = _dot(nb, wq_ref[...]).astype(BF)
        for hd in range(NH):
            _, oh = _attend(qb, k_ref, v_ref, hd)
            att_ref[:, hd * HD:(hd + 1) * HD] = oh.astype(BF)
        o_ref[...] = x + _dot(att_ref[...], wo_ref[...])

    sq = _const_spec((D, D))
    kvs = _const_spec((NMEM, D))
    return pl.pallas_call(
        body, grid=(s // ts,), name="xattn_fwd",
        out_shape=jax.ShapeDtypeStruct((s, D), F32),
        in_specs=[_row_spec(ts, D), _const_spec((1, D)), sq, kvs, kvs, sq],
        out_specs=_row_spec(ts, D),
        scratch_shapes=[pltpu.VMEM((ts, D), BF)],
        compiler_params=_cparams(("arbitrary",)),
    )(h, gain, wq, k, v, wo)


def xattn_bwd(h, dh, gain, wq, k, v, wo, rider=None):
    s = h.shape[0]
    ts = _tile(s, 512)

    def body(h_ref, dh_ref, g_ref, wq_ref, k_ref, v_ref, wo_ref,
             dx_ref, n_ref, dq_ref, att_ref, dhb_ref, dk_ref, dv_ref, dg_ref):
        @pl.when(pl.program_id(0) == 0)
        def _():
            for ref in (dk_ref, dv_ref, dg_ref):
                ref[...] = jnp.zeros_like(ref)

        x = h_ref[...]
        dh = dh_ref[...]
        gain_v = g_ref[...]
        xh, r = _rms(x)
        nb = (xh * gain_v).astype(BF)
        n_ref[...] = nb
        qb = _dot(nb, wq_ref[...]).astype(BF)
        dhb = dh.astype(BF)
        dhb_ref[...] = dhb
        dob = _dot_nt(dhb, wo_ref[...]).astype(BF)
        for hd in range(NH):
            cols = slice(hd * HD, (hd + 1) * HD)
            pr, oh = _attend(qb, k_ref, v_ref, hd)
            att_ref[:, cols] = oh.astype(BF)
            doh = dob[:, cols]
            dpr = _dot_nt(doh, v_ref[:, cols])
            dv_ref[:, cols] += _dot_tn(pr.astype(BF), doh)
            dsc = (pr * (dpr - jnp.sum(dpr * pr, axis=-1, keepdims=True)) * ATT_SCALE).astype(BF)
            dq_ref[:, cols] = _dot(dsc, k_ref[:, cols]).astype(BF)
            dk_ref[:, cols] += _dot_tn(dsc, qb[:, cols])
        dn = _dot_nt(dq_ref[...], wq_ref[...])
        dg_ref[...] += _colsum(dn * xh)
        dx_ref[...] = dh + _rms_bwd(dn, xh, r, gain_v)

    sq = _const_spec((D, D))
    kvs = _const_spec((NMEM, D))
    bf_rows = jax.ShapeDtypeStruct((s, D), BF)
    kv_acc = jax.ShapeDtypeStruct((NMEM, D), F32)
    return _pcall(
        body, grid=(s // ts,), name="xattn_bwd", args=(h, dh, gain, wq, k, v, wo),
        out_shape=(jax.ShapeDtypeStruct((s, D), F32), bf_rows, bf_rows, bf_rows, bf_rows, kv_acc, kv_acc,
                   jax.ShapeDtypeStruct((1, D), F32)),
        in_specs=[_row_spec(ts, D), _row_spec(ts, D), _const_spec((1, D)), sq, kvs, kvs, sq],
        out_specs=(_row_spec(ts, D),) * 5 + (_acc_spec((NMEM, D)), _acc_spec((NMEM, D)), _acc_spec((1, D))),
        rider=rider)


BLOCK_BYTES = 3 << 19


def _row_block(rows, cols):
    rb = rows
    while rb * cols * 4 > BLOCK_BYTES and rb % 32 == 0:
        rb //= 2
    return rb


def cast_bf16(w, chip, name):
    r, c = w.shape
    rb = _row_block(r, c)

    def body(chip_ref, w_ref, o_ref):
        del chip_ref
        o_ref[0] = w_ref[...].astype(BF)

    return pl.pallas_call(
        body, name=name, out_shape=jax.ShapeDtypeStruct((NSH, r, c), BF),
        grid_spec=pltpu.PrefetchScalarGridSpec(
            num_scalar_prefetch=1, grid=(r // rb,),
            in_specs=[pl.BlockSpec((rb, c), lambda i, chip_ref: (i, 0))],
            out_specs=pl.BlockSpec((1, rb, c), lambda i, chip_ref: (chip_ref[0], i, 0))),
        compiler_params=_cparams(("arbitrary",)))(chip, w)


NDEV = 8


def device_sum(g4, recv, place, name):
    _, _, rh, c = g4.shape
    rb = _row_block(rh, c)

    def body(place_ref, g_ref, r_ref, o_ref):
        del place_ref
        acc = g_ref[0, 0]
        for j in range(NDEV - 1):
            acc = acc + r_ref[j].astype(F32)
        o_ref[0] = acc

    return pl.pallas_call(
        body, name=name, out_shape=jax.ShapeDtypeStruct((2, rh, c), F32),
        grid_spec=pltpu.PrefetchScalarGridSpec(
            num_scalar_prefetch=1, grid=(rh // rb,),
            in_specs=[pl.BlockSpec((1, 1, rb, c), lambda i, place_ref: (place_ref[0], place_ref[1], i, 0)),
                      pl.BlockSpec((NDEV - 1, rb, c), lambda i, place_ref: (0, i, 0))],
            out_specs=pl.BlockSpec((1, rb, c), lambda i, place_ref: (place_ref[1], i, 0))),
        compiler_params=_cparams(("arbitrary",)))(place, g4, recv)


def _adamw_math(w, g, m, v):
    m = ADAM_B1 * m + (1.0 - ADAM_B1) * g
    v = ADAM_B2 * v + (1.0 - ADAM_B2) * (g * g)
    m_hat = m / (1.0 - ADAM_B1 ** ADAM_STEP)
    v_hat = v / (1.0 - ADAM_B2 ** ADAM_STEP)
    delta = -ADAM_LR * (m_hat / (jnp.sqrt(v_hat) + ADAM_EPS) + ADAM_WD * w)
    return delta, m, v


def adamw(w, g, m, v, name):
    r, c = w.shape
    rb = _row_block(r, c)

    def body(w_ref, g_ref, m_ref, v_ref, d_ref, mo_ref, vo_ref):
        d, mn, vn = _adamw_math(w_ref[...], g_ref[...], m_ref[...], v_ref[...])
        d_ref[...] = d
        mo_ref[...] = mn
        vo_ref[...] = vn

    o = jax.ShapeDtypeStruct((r, c), F32)
    spec = _row_spec(rb, c)
    return pl.pallas_call(
        body, grid=(r // rb,), name=name, out_shape=(o, o, o),
        in_specs=[spec] * 4, out_specs=(spec,) * 3,
        compiler_params=_cparams(("arbitrary",)))(w, g, m, v)


def _place():
    return lax.axis_index("x"), lax.axis_index("y"), lax.axis_index("c")


def _other_chips(x, y):
    return [(1 - x, y), (x, 1 - y), (1 - x, 1 - y)]


NOTHER = NSH - 1


def gather_rider(arrays):
    nw = len(arrays)
    nici = nw * NOTHER

    def copies(refs, send_sems, recv_sems):
        x, y, c = _place()
        ici, d2d = [], []
        for w in range(nw):
            for j, (px, py) in enumerate(_other_chips(x, y)):
                n = w * NOTHER + j
                sems = dict(send_sem=send_sems.at[n], recv_sem=recv_sems.at[n],
                            device_id=(px, py, c), device_id_type=MESH)
                mine = refs[w].at[2 * x + y, c]
                theirs = refs[w].at[2 * px + py, c]
                ici.append((pltpu.make_async_remote_copy(src_ref=mine, dst_ref=mine, **sems),
                            pltpu.make_async_remote_copy(src_ref=mine, dst_ref=theirs, **sems)))
                sems = dict(send_sem=send_sems.at[nici + n], recv_sem=recv_sems.at[nici + n],
                            device_id=(x, y, 1 - c), device_id_type=MESH)
                d2d.append((pltpu.make_async_remote_copy(src_ref=theirs, dst_ref=theirs, **sems),
                            pltpu.make_async_remote_copy(src_ref=theirs, dst_ref=refs[w].at[2 * px + py, 1 - c],
                                                         **sems)))
        return ici, d2d

    def start(ins, outs, send_sems, recv_sems):
        ici, _ = copies(outs, send_sems, recv_sems)
        for send, _ in ici:
            send.start()

    def finish(ins, outs, send_sems, recv_sems):
        ici, d2d = copies(outs, send_sems, recv_sems)
        for (_, landed), (forward, _) in zip(ici, d2d):
            landed.wait_recv()
            forward.start()
        for _, landed in d2d:
            landed.wait_recv()
        for send, _ in ici + d2d:
            send.wait_send()

    return Rider(arrays, [jax.ShapeDtypeStruct(a.shape, a.dtype) for a in arrays], {i: i for i in range(nw)},
                 2 * nici, start, finish)


def _peers(x, y, c):
    return [(x ^ (rel >> 2), y ^ ((rel >> 1) & 1), c ^ (rel & 1)) for rel in range(1, NDEV)]


def reduce_rider(grads):
    nw = len(grads)
    npeer = NDEV - 1

    def copies(ins, outs, send_sems, recv_sems):
        x, y, c = _place()
        return [pltpu.make_async_remote_copy(
            src_ref=ins[w].at[2 * px + py, pc], dst_ref=outs[w].at[r],
            send_sem=send_sems.at[w * npeer + r], recv_sem=recv_sems.at[w * npeer + r],
            device_id=(px, py, pc), device_id_type=MESH)
            for w in range(nw) for r, (px, py, pc) in enumerate(_peers(x, y, c))]

    def start(ins, outs, send_sems, recv_sems):
        for cp in copies(ins, outs, send_sems, recv_sems):
            cp.start()

    def finish(ins, outs, send_sems, recv_sems):
        for cp in copies(ins, outs, send_sems, recv_sems):
            cp.wait()

    return Rider(grads, [jax.ShapeDtypeStruct((npeer,) + g.shape[2:], g.dtype) for g in grads], {},
                 nw * npeer, start, finish)


class _Offset:
    def __init__(self, ref, base):
        self.ref, self.base = ref, base

    @property
    def at(self):
        return self

    def __getitem__(self, i):
        return self.ref.at[self.base + i]


def merge_riders(riders):
    ins, outs, aliases, spans, nsem = [], [], {}, [], 0
    for r in riders:
        spans.append((len(ins), len(outs), nsem))
        aliases.update({len(ins) + i: len(outs) + j for i, j in r.aliases.items()})
        ins, outs, nsem = ins + r.ins, outs + r.outs, nsem + r.nsem

    def each(step):
        def run(in_refs, out_refs, send_sems, recv_sems):
            for r, (i0, o0, s0) in zip(riders, spans):
                getattr(r, step)(in_refs[i0:i0 + len(r.ins)], out_refs[o0:o0 + len(r.outs)],
                                 _Offset(send_sems, s0), _Offset(recv_sems, s0))
        return run

    return Rider(ins, outs, aliases, nsem, each("start"), each("finish"))


def split_results(riders, results):
    out, o0 = [], 0
    for r in riders:
        out.append(tuple(results[o0:o0 + len(r.outs)]))
        o0 += len(r.outs)
    return out


def swap_rider(halves):
    nw = len(halves)

    def copies(refs, send_sems, recv_sems):
        x, y, c = _place()
        out = []
        for w in range(nw):
            sems = dict(send_sem=send_sems.at[w], recv_sem=recv_sems.at[w],
                        device_id=(x, y, 1 - c), device_id_type=MESH)
            mine = refs[w].at[c]
            out.append((pltpu.make_async_remote_copy(src_ref=mine, dst_ref=mine, **sems),
                        pltpu.make_async_remote_copy(src_ref=mine, dst_ref=refs[w].at[1 - c], **sems)))
        return out

    def start(ins, outs, send_sems, recv_sems):
        for send, _ in copies(outs, send_sems, recv_sems):
            send.start()

    def finish(ins, outs, send_sems, recv_sems):
        cps = copies(outs, send_sems, recv_sems)
        for _, recv in cps:
            recv.wait_recv()
        for send, _ in cps:
            send.wait_send()

    return Rider(halves, [jax.ShapeDtypeStruct(h.shape, h.dtype) for h in halves], {i: i for i in range(nw)},
                 nw, start, finish)


def allgather_rider(slots):
    def copies(ref, send_sems, recv_sems):
        x, y, c = _place()
        mine = ref.at[4 * x + 2 * y + c]
        out = []
        for r, peer in enumerate(_peers(x, y, c)):
            sems = dict(send_sem=send_sems.at[r], recv_sem=recv_sems.at[r], device_id=peer, device_id_type=MESH)
            out.append((pltpu.make_async_remote_copy(src_ref=mine, dst_ref=mine, **sems),
                        pltpu.make_async_remote_copy(
                            src_ref=mine, dst_ref=ref.at[4 * peer[0] + 2 * peer[1] + peer[2]], **sems)))
        return out

    def start(ins, outs, send_sems, recv_sems):
        for send, _ in copies(outs[0], send_sems, recv_sems):
            send.start()

    def finish(ins, outs, send_sems, recv_sems):
        cps = copies(outs[0], send_sems, recv_sems)
        for _, recv in cps:
            recv.wait_recv()
        for send, _ in cps:
            send.wait_send()

    return Rider([slots], [jax.ShapeDtypeStruct(slots.shape, slots.dtype)], {0: 0}, NDEV - 1, start, finish)


def sum_slots(slots):
    def body(s_ref, o_ref):
        acc = s_ref[0]
        for dev in range(1, NDEV):
            acc = acc + s_ref[dev]
        o_ref[...] = acc

    return pl.pallas_call(body, name="sum_slots", out_shape=jax.ShapeDtypeStruct(slots.shape[1:], F32),
                          compiler_params=_cparams())(slots)


BIG = ("ffn1_w_gu", "ffn1_w_down", "w_in", "w_a_out", "w_b_out", "w_out", "w_q", "w_kv", "w_o",
       "ffn2_w_gu", "ffn2_w_down")
SMALL = {"ffn1_norm": (0, 1), "mix_norm": (8, 1), "xattn_norm": (16, 1), "mem_norm": (24, 1),
         "ffn2_norm": (32, 1), "final_norm": (40, 1), "conv_b": (48, 1), "conv_ln_g": (56, 1),
         "conv_ln_b": (64, 1), "sgu_ln_g": (72, 1), "sgu_ln_b": (80, 1), "b_in": (88, 6),
         "conv_w": (96, CW), "sgu_w": (128, 64), "sgu_b": (192, 1)}
LOSS_ROW = 200
SMALL_ROWS = 208


def _pad_rows(a, rows):
    return jnp.pad(a, ((0, rows - a.shape[0]), (0, D - a.shape[1])))


def _pack_small(parts):
    names = sorted(parts, key=lambda n: SMALL[n][0] if n in SMALL else LOSS_ROW)
    rows = []
    for i, n in enumerate(names):
        start = SMALL[n][0] if n in SMALL else LOSS_ROW
        end = SMALL_ROWS if i + 1 == len(names) else (SMALL[names[i + 1]][0] if names[i + 1] in SMALL else LOSS_ROW)
        rows.append(_pad_rows(parts[n], end - start))
    return jnp.concatenate(rows, axis=0)


def _small_views(w):
    return {
        "ffn1_norm": w["ffn1_norm"], "mix_norm": w["mix_norm"], "xattn_norm": w["xattn_norm"],
        "mem_norm": w["mem_norm"], "ffn2_norm": w["ffn2_norm"], "final_norm": w["final_norm"].reshape(1, D),
        "conv_b": w["conv_b"], "conv_ln_g": w["conv_ln_g"], "conv_ln_b": w["conv_ln_b"],
        "sgu_ln_g": w["sgu_ln_g"], "sgu_ln_b": w["sgu_ln_b"], "b_in": w["b_in"].reshape(6, D),
        "conv_w": w["conv_w"][0], "sgu_w": w["sgu_w"].reshape(64, D), "sgu_b": w["sgu_b"].reshape(1, NG * CHUNK),
    }


def _unpack_small(buf, like, chip):
    out = {}
    for n, (start, rows) in SMALL.items():
        blk = buf[start:start + rows]
        if n == "conv_w":
            blk = blk[:, :like[n].shape[-1]] if chip is None else lax.dynamic_slice_in_dim(
                blk, chip * like[n].shape[-1], like[n].shape[-1], axis=1)
        elif n == "sgu_b":
            blk = blk[:, :NG * CHUNK]
        out[n] = blk.reshape(like[n].shape)
    return out


def kernel(x, mem, ffn1_norm, ffn1_w_gu, ffn1_w_down, mix_norm, w_in, b_in, conv_w, conv_b, conv_ln_g, conv_ln_b, w_a_out, sgu_ln_g, sgu_ln_b, sgu_w, sgu_b, w_b_out, w_out, xattn_norm, mem_norm, w_q, w_kv, w_o, ffn2_norm, ffn2_w_gu, ffn2_w_down, final_norm, loss_target, m_ffn1_norm, m_ffn1_w_gu, m_ffn1_w_down, m_mix_norm, m_w_in, m_b_in, m_conv_w, m_conv_b, m_conv_ln_g, m_conv_ln_b, m_w_a_out, m_sgu_ln_g, m_sgu_ln_b, m_sgu_w, m_sgu_b, m_w_b_out, m_w_out, m_xattn_norm, m_mem_norm, m_w_q, m_w_kv, m_w_o, m_ffn2_norm, m_ffn2_w_gu, m_ffn2_w_down, m_final_norm, v_ffn1_norm, v_ffn1_w_gu, v_ffn1_w_down, v_mix_norm, v_w_in, v_b_in, v_conv_w, v_conv_b, v_conv_ln_g, v_conv_ln_b, v_w_a_out, v_sgu_ln_g, v_sgu_ln_b, v_sgu_w, v_sgu_b, v_w_b_out, v_w_out, v_xattn_norm, v_mem_norm, v_w_q, v_w_kv, v_w_o, v_ffn2_norm, v_ffn2_w_gu, v_ffn2_w_down, v_final_norm):
    names = ("ffn1_norm", "ffn1_w_gu", "ffn1_w_down", "mix_norm", "w_in", "b_in", "conv_w", "conv_b",
             "conv_ln_g", "conv_ln_b", "w_a_out", "sgu_ln_g", "sgu_ln_b", "sgu_w", "sgu_b", "w_b_out", "w_out",
             "xattn_norm", "mem_norm", "w_q", "w_kv", "w_o", "ffn2_norm", "ffn2_w_gu", "ffn2_w_down",
             "final_norm")
    wts = dict(zip(names, (ffn1_norm, ffn1_w_gu, ffn1_w_down, mix_norm, w_in, b_in, conv_w, conv_b, conv_ln_g,
                           conv_ln_b, w_a_out, sgu_ln_g, sgu_ln_b, sgu_w, sgu_b, w_b_out, w_out, xattn_norm,
                           mem_norm, w_q, w_kv, w_o, ffn2_norm, ffn2_w_gu, ffn2_w_down, final_norm)))
    mom1 = dict(zip(names, (m_ffn1_norm, m_ffn1_w_gu, m_ffn1_w_down, m_mix_norm, m_w_in, m_b_in, m_conv_w,
                            m_conv_b, m_conv_ln_g, m_conv_ln_b, m_w_a_out, m_sgu_ln_g, m_sgu_ln_b, m_sgu_w,
                            m_sgu_b, m_w_b_out, m_w_out, m_xattn_norm, m_mem_norm, m_w_q, m_w_kv, m_w_o,
                            m_ffn2_norm, m_ffn2_w_gu, m_ffn2_w_down, m_final_norm)))
    mom2 = dict(zip(names, (v_ffn1_norm, v_ffn1_w_gu, v_ffn1_w_down, v_mix_norm, v_w_in, v_b_in, v_conv_w,
                            v_conv_b, v_conv_ln_g, v_conv_ln_b, v_w_a_out, v_sgu_ln_g, v_sgu_ln_b, v_sgu_w,
                            v_sgu_b, v_w_b_out, v_w_out, v_xattn_norm, v_mem_norm, v_w_q, v_w_kv, v_w_o,
                            v_ffn2_norm, v_ffn2_w_gu, v_ffn2_w_down, v_final_norm)))
    xi, yi, ci = _place()
    chip = (2 * xi + yi).astype(jnp.int32)
    core = ci.astype(jnp.int32)
    chip_arr = chip.reshape(1)
    place_arr = jnp.stack([chip, core])
    x2, mem2, tgt = x[0], mem[0], loss_target[0]

    slot = {n: cast_bf16(wts[n][0], chip_arr, "cast_" + n) for n in BIG}
    cw_pad = jnp.pad(conv_w[0], ((0, HALO - CW), (0, 0)))
    slot["conv_w"] = lax.dynamic_update_slice(jnp.zeros((NSH,) + cw_pad.shape, F32), cw_pad[None], (chip, 0, 0))
    g_first = ("ffn1_w_gu",)
    g_mix = ("ffn1_w_down", "w_in", "w_a_out", "w_b_out", "w_out", "conv_w")
    g_rest = ("w_q", "w_kv", "w_o", "ffn2_w_gu", "ffn2_w_down")

    def gather(group):
        return gather_rider([slot[n].reshape(NSH, 2, slot[n].shape[1] // 2, slot[n].shape[2]) for n in group])

    def gathered(group, res):
        return {n: r.reshape(slot[n].shape) for n, r in zip(group, res)}

    full = gathered(g_first, run_rider(gather(g_first), "gather_ffn1"))
    wgu1 = full["ffn1_w_gu"]
    tril = jnp.tril(jnp.ones((CHUNK, CHUNK), dtype=bool))
    ws = jnp.where(tril[None], sgu_w[0], 0.0).astype(BF)
    wst = jnp.transpose(ws, (0, 2, 1))
    sbias = jnp.repeat(jnp.transpose(sgu_b[0]), GD, axis=1)
    gfin = final_norm.reshape(1, D)

    (gu1, act1), rode = ffn_hidden(x2, ffn1_norm, wgu1, rider=gather(g_mix))
    full.update(gathered(g_mix, rode))
    wd1 = full["ffn1_w_down"].reshape(FF, D)
    h1 = ffn_down(x2, act1, wd1)
    win = full["w_in"]
    wa, wb, wout = (full[n].reshape(D, D) for n in ("w_a_out", "w_b_out", "w_out"))
    cw_full = jnp.transpose(full["conv_w"], (1, 0, 2)).reshape(HALO, D)
    (h2, proj, n2b, conv_out), rode = mix_fwd(
        h1, mix_norm, win, b_in, cw_full, conv_b, conv_ln_g, conv_ln_b, wa, sgu_ln_g, sgu_ln_b, ws, sbias, wb,
        wout, rider=gather(g_rest))
    full.update(gathered(g_rest, rode))
    wgu2, wd2, wkv = full["ffn2_w_gu"], full["ffn2_w_down"].reshape(FF, D), full["w_kv"]
    wq, wo = full["w_q"].reshape(D, D), full["w_o"].reshape(D, D)
    kb, vb, memn = kv_proj(mem2, mem_norm, wkv)
    h3 = xattn_fwd(h2, xattn_norm, wq, kb, vb, wo)
    dh4, gu2, loss_lanes, d_final = ffn_fwd_loss(h3, ffn2_norm, wgu2, wd2, gfin, tgt)

    own, halves = {}, {}

    def exchange(group, grads):
        views = []
        for n, (g, gb) in zip(group, grads):
            rs, cs = wts[n].shape[1:]
            own[n] = g.reshape(NSH, 2, rs // 2, cs)
            views.append(gb.reshape(NSH, 2, rs // 2, cs))
        return reduce_rider(views)

    def reduce(group, recv):
        for n, r in zip(group, recv):
            halves[n] = device_sum(own[n], r, place_arr, "device_sum_" + n)

    dh3, n4, a4, dgu4, dhb4, d_ffn2n = ffn_bwd(h3, gu2, dh4, ffn2_norm, wgu2, wd2, "ffn2_bwd")
    g_ffn2 = ("ffn2_w_gu", "ffn2_w_down")
    ride = exchange(g_ffn2, [dw_matmul(n4, dgu4, NSH, "dw_ffn2_gu")[:2], dw_matmul(a4, dhb4, 1, "dw_ffn2_down")[:2]])
    (dh2, n3, dq, att, dhb3, dk, dv, d_xn), rode = xattn_bwd(h2, dh3, xattn_norm, wq, kb, vb, wo, rider=ride)
    reduce(g_ffn2, rode)
    g_att = ("w_q", "w_o", "w_kv")
    d_wkv, d_wkv_b, d_memn = kv_bwd(mem2, mem_norm, memn, wkv, dk, dv)
    ride = exchange(g_att, [dw_matmul(n3, dq, 1, "dw_q")[:2], dw_matmul(att, dhb3, 1, "dw_o")[:2],
                            (d_wkv, d_wkv_b)])
    ((dconv, dproj, sa, dya, ob, dyb, mg, dhb2, d_sgu_w, d_sgu_b, d_lna_g, d_lna_b, d_lnb_g, d_lnb_b),
     rode) = mix_bwd_branches(proj, conv_out, dh2, conv_ln_g, conv_ln_b, wa, sgu_ln_g, sgu_ln_b, ws, wst,
                              sbias, wb, wout, rider=ride)
    reduce(g_att, rode)
    dproj, d_conv_w, d_conv_b = conv_bwd(proj, dconv, dproj, cw_full)
    dh1, d_mixn, d_b_in = mix_bwd_in(dproj, h1, dh2, mix_norm, win)
    g_mixw = ("w_in", "w_a_out", "w_b_out")
    ride = exchange(g_mixw, [dw_matmul(n2b, dproj, NSH, "dw_in")[:2], dw_matmul(sa, dya, 1, "dw_a_out")[:2],
                             dw_matmul(ob, dyb, 1, "dw_b_out")[:2]])
    dx, n1, a1, dgu1, dhb1, d_ffn1n = ffn_bwd(x2, gu1, dh1, ffn1_norm, wgu1, wd1, "ffn1_bwd")
    small_grads = {
        "ffn1_norm": d_ffn1n, "mix_norm": d_mixn, "xattn_norm": d_xn, "mem_norm": d_memn, "ffn2_norm": d_ffn2n,
        "final_norm": d_final, "conv_b": d_conv_b, "conv_ln_g": d_lna_g, "conv_ln_b": d_lna_b,
        "sgu_ln_g": d_lnb_g, "sgu_ln_b": d_lnb_b, "b_in": d_b_in.reshape(6, D), "conv_w": d_conv_w[:CW],
        "sgu_w": d_sgu_w.reshape(64, D), "sgu_b": jnp.transpose(d_sgu_b[:, :NG]).reshape(1, NG * CHUNK),
        "loss": loss_lanes}
    slots = lax.dynamic_update_slice(jnp.zeros((NDEV, SMALL_ROWS, D), F32), _pack_small(small_grads)[None],
                                     (2 * chip + core, 0, 0))
    riders = [ride, allgather_rider(slots)]
    d_wgu1, d_wgu1_b, rode = dw_matmul(n1, dgu1, NSH, "dw_ffn1_gu", rider=merge_riders(riders))
    recv_mix, all_slots = split_results(riders, rode)
    reduce(g_mixw, recv_mix)
    ride = exchange(("ffn1_w_gu",), [(d_wgu1, d_wgu1_b)])
    d_down, d_down_b, rode = dw_matmul(a1, dhb1, 1, "dw_ffn1_down", rider=ride)
    reduce(("ffn1_w_gu",), rode)
    ride = exchange(("ffn1_w_down",), [(d_down, d_down_b)])
    d_out, d_out_b, rode = dw_matmul(mg, dhb2, 1, "dw_out", rider=ride)
    reduce(("ffn1_w_down",), rode)
    reduce(("w_out",), run_rider(exchange(("w_out",), [(d_out, d_out_b)]), "exchange_w_out"))
    swapped = run_rider(swap_rider([halves[n] for n in BIG]), "pair_swap")
    gshard = {n: g.reshape(wts[n].shape[1:]) for n, g in zip(BIG, swapped)}

    small = sum_slots(all_slots[0])
    loss = (0.5 / D) * jnp.sum(small[LOSS_ROW])
    gsmall = _unpack_small(small, wts, chip)

    out_g, out_d, out_m, out_v = dict(gsmall), {}, {}, {}
    sw, sm, sv = (_pack_small(_small_views(t))[:LOSS_ROW] for t in (wts, mom1, mom2))
    sg = _pack_small(_small_views({n: gsmall[n] for n in SMALL}))[:LOSS_ROW]
    for dst, packed in zip((out_d, out_m, out_v), adamw(sw, sg, sm, sv, "adamw_small")):
        dst.update(_unpack_small(packed, wts, None))
    for n in BIG:
        shape = wts[n].shape
        out_g[n] = gshard[n].reshape(shape)
        d, mn, vn = adamw(wts[n][0], gshard[n], mom1[n][0], mom2[n][0], "adamw_" + n)
        out_d[n], out_m[n], out_v[n] = d.reshape(shape), mn.reshape(shape), vn.reshape(shape)
    return (loss, dx[None], *[out_g[n] for n in names], *[out_d[n] for n in names],
            *[out_m[n] for n in names], *[out_v[n] for n in names])
```

```python
import functools
import math

import jax
import jax.numpy as jnp
from jax import lax
from jax.experimental import pallas as pl
from jax.experimental.pallas import tpu as pltpu

F32 = jnp.float32
BF = jnp.bfloat16
MESH = pl.DeviceIdType.MESH

D = 1024
FF = 2816
HC = FF // 2
NSH = 4
DIN = 6 * D
INB = DIN // NSH
CW = 31
HALO = 32
CHUNK = 128
NG = 4
GD = D // NG
NH = 4
HD = D // NH
NMEM = 256
EPS_RMS = 1e-6
EPS_LN = 1e-5
GELU_C0 = math.sqrt(2.0 / math.pi)
GELU_C1 = 0.044715
ATT_SCALE = 1.0 / math.sqrt(HD)

ADAM_LR = 0.001
ADAM_B1 = 0.9
ADAM_B2 = 0.999
ADAM_EPS = 1e-08
ADAM_WD = 0.01
ADAM_STEP = 10

VMEM_LIMIT = 56 * 1024 * 1024


def _cparams(sem=None, **kw):
    if sem is not None:
        kw["dimension_semantics"] = sem
    return pltpu.CompilerParams(vmem_limit_bytes=VMEM_LIMIT, **kw)


def _dot(a, b):
    return jnp.dot(a, b, preferred_element_type=F32)


def _dot_nt(a, b):
    return lax.dot_general(a, b, (((1,), (1,)), ((), ())), preferred_element_type=F32)


def _dot_tn(a, b):
    return lax.dot_general(a, b, (((0,), (0,)), ((), ())), preferred_element_type=F32)


def _sigmoid(x):
    return 1.0 / (1.0 + jnp.exp(-x))


def _gelu(x):
    t = jnp.tanh(GELU_C0 * (x + GELU_C1 * (x * x * x)))
    return 0.5 * x * (1.0 + t), t


def _gelu_with_grad(x):
    x2 = x * x
    t = jnp.tanh(GELU_C0 * (x + GELU_C1 * (x2 * x)))
    onep = 1.0 + t
    hx = 0.5 * x
    grad = 0.5 * onep + hx * (1.0 - t * t) * (GELU_C0 + (3.0 * GELU_C0 * GELU_C1) * x2)
    return hx * onep, grad


def _mean(x):
    return jnp.mean(x, axis=-1, keepdims=True)


def _rms(x):
    r = lax.rsqrt(_mean(x * x) + EPS_RMS)
    return x * r, r


def _rms_bwd(dn, xh, r, g):
    dxh = dn * g
    return r * (dxh - xh * _mean(dxh * xh))


def _ln(x):
    xc = x - _mean(x)
    r = lax.rsqrt(_mean(xc * xc) + EPS_LN)
    return xc * r, r


def _ln_bwd(dy, xh, r, g):
    dxh = dy * g
    return r * (dxh - _mean(dxh) - xh * _mean(dxh * xh))


def _colsum(x):
    return jnp.sum(x, axis=0, keepdims=True)


def _const_spec(shape):
    nd = len(shape)
    return pl.BlockSpec(shape, lambda *_: (0,) * nd, pipeline_mode=pl.Buffered(1))


def _row_spec(ts, width):
    return pl.BlockSpec((ts, width), lambda i: (i, 0))


def _acc_spec(shape):
    nd = len(shape)
    return pl.BlockSpec(shape, lambda *_: (0,) * nd)


def _tile(s, want):
    return min(s, want)


HBM_SPEC = pl.BlockSpec(memory_space=pltpu.HBM)


class Rider:
    def __init__(self, ins, outs, aliases, nsem, start, finish, relay=None):
        self.ins, self.outs, self.aliases, self.nsem = list(ins), list(outs), dict(aliases), nsem
        self.start, self.finish, self.relay = start, finish, relay


def _pcall(body, *, name, grid, args, in_specs, out_shape, out_specs, scratch=(), rider=None):
    sem = ("arbitrary",) * len(grid)
    n_in, n_out = len(args), len(out_shape)
    if rider is None:
        res = pl.pallas_call(
            body, grid=grid, name=name, out_shape=tuple(out_shape), in_specs=list(in_specs),
            out_specs=tuple(out_specs), scratch_shapes=list(scratch), compiler_params=_cparams(sem))(*args)
        return tuple(res), ()
    r_in, r_out = len(rider.ins), len(rider.outs)

    def wrapped(*refs):
        a, ri = refs[:n_in], refs[n_in:n_in + r_in]
        o = refs[n_in + r_in:n_in + r_in + n_out]
        ro = refs[n_in + r_in + n_out:n_in + r_in + n_out + r_out]
        s, (send, recv) = refs[n_in + r_in + n_out + r_out:-2], refs[-2:]
        first = functools.reduce(jnp.logical_and, [pl.program_id(d) == 0 for d in range(len(grid))])
        last = functools.reduce(jnp.logical_and, [pl.program_id(d) == g - 1 for d, g in enumerate(grid)])

        @pl.when(first)
        def _():
            rider.start(ri, ro, send, recv)

        body(*a, *o, *s)

        if rider.relay is not None:
            step = functools.reduce(lambda acc, d: acc * grid[d] + pl.program_id(d), range(len(grid)), 0)

            @pl.when(step == max((3 * math.prod(grid)) // 4 - 1, 0))
            def _():
                rider.relay(ri, ro, send, recv)

        @pl.when(last)
        def _():
            rider.finish(ri, ro, send, recv)

    res = pl.pallas_call(
        wrapped, grid=grid, name=name, out_shape=tuple(out_shape) + tuple(rider.outs),
        in_specs=list(in_specs) + [HBM_SPEC] * r_in, out_specs=tuple(out_specs) + (HBM_SPEC,) * r_out,
        scratch_shapes=list(scratch) + [pltpu.SemaphoreType.DMA((rider.nsem,)),
                                        pltpu.SemaphoreType.DMA((rider.nsem,))],
        input_output_aliases={n_in + i: n_out + j for i, j in rider.aliases.items()},
        compiler_params=_cparams(sem, has_side_effects=True))(*args, *rider.ins)
    return tuple(res[:n_out]), tuple(res[n_out:])


def run_rider(rider, name):
    r_in = len(rider.ins)

    def body(*refs):
        ri, ro, (send, recv) = refs[:r_in], refs[r_in:-2], refs[-2:]
        rider.start(ri, ro, send, recv)
        if rider.relay is not None:
            rider.relay(ri, ro, send, recv)
        rider.finish(ri, ro, send, recv)

    return pl.pallas_call(
        body, name=name, out_shape=tuple(rider.outs), in_specs=[HBM_SPEC] * r_in,
        out_specs=(HBM_SPEC,) * len(rider.outs),
        scratch_shapes=[pltpu.SemaphoreType.DMA((rider.nsem,)), pltpu.SemaphoreType.DMA((rider.nsem,))],
        input_output_aliases=rider.aliases,
        compiler_params=pltpu.CompilerParams(has_side_effects=True))(*rider.ins)


FFN_BWD_TILE = 256


def _ffn_apply(x, g_ref, wgu_ref, wd_ref, gu_ref):
    xh, _ = _rms(x)
    nb = (xh * g_ref[...]).astype(BF)
    acc = jnp.zeros(x.shape, F32)
    for j in range(2):
        g = _dot(nb, wgu_ref[j])
        u = _dot(nb, wgu_ref[2 + j])
        gu_ref[:, j * HC:(j + 1) * HC] = g.astype(BF)
        gu_ref[:, FF + j * HC:FF + (j + 1) * HC] = u.astype(BF)
        a = (g * _sigmoid(g) * u).astype(BF)
        acc = acc + _dot(a, wd_ref[j * HC:(j + 1) * HC, :])
    return x + 0.5 * acc


def ffn_hidden(h, gain, wgu, rider=None):
    s = h.shape[0]
    ts = _tile(s, 512)

    def body(h_ref, g_ref, wgu_ref, gu_ref, a_ref):
        xh, _ = _rms(h_ref[...])
        nb = (xh * g_ref[...]).astype(BF)
        for j in range(2):
            g = _dot(nb, wgu_ref[j])
            u = _dot(nb, wgu_ref[2 + j])
            gu_ref[:, j * HC:(j + 1) * HC] = g.astype(BF)
            gu_ref[:, FF + j * HC:FF + (j + 1) * HC] = u.astype(BF)
            a_ref[:, j * HC:(j + 1) * HC] = (g * _sigmoid(g) * u).astype(BF)

    return _pcall(
        body, grid=(s // ts,), name="ffn1_hidden", args=(h, gain, wgu),
        out_shape=[jax.ShapeDtypeStruct((s, 2 * FF), BF), jax.ShapeDtypeStruct((s, FF), BF)],
        in_specs=[_row_spec(ts, D), _const_spec((1, D)), _const_spec((NSH, D, HC))],
        out_specs=[_row_spec(ts, 2 * FF), _row_spec(ts, FF)], rider=rider)


def ffn_down(h, a, wd):
    s = h.shape[0]
    ts = _tile(s, 512)

    def body(h_ref, a_ref, wd_ref, o_ref):
        o_ref[...] = h_ref[...] + 0.5 * _dot(a_ref[...], wd_ref[...])

    return pl.pallas_call(
        body, grid=(s // ts,), name="ffn1_down", out_shape=jax.ShapeDtypeStruct((s, D), F32),
        in_specs=[_row_spec(ts, D), _row_spec(ts, FF), _const_spec((FF, D))], out_specs=_row_spec(ts, D),
        compiler_params=_cparams(("arbitrary",)))(h, a, wd)


def ffn_fwd_loss(h, gain, wgu, wd, gfin, target):
    s = h.shape[0]
    ts = _tile(s, 512)

    def body(h_ref, g_ref, wgu_ref, wd_ref, gf_ref, t_ref, dh_ref, gu_ref, loss_ref, dgf_ref):
        @pl.when(pl.program_id(0) == 0)
        def _():
            loss_ref[...] = jnp.zeros_like(loss_ref)
            dgf_ref[...] = jnp.zeros_like(dgf_ref)

        h4 = _ffn_apply(h_ref[...], g_ref, wgu_ref, wd_ref, gu_ref)
        yh, r4 = _rms(h4)
        gf = gf_ref[...]
        e = yh * gf - t_ref[...]
        loss_ref[...] += _colsum(e * e)
        dy = e * (1.0 / D)
        dgf_ref[...] += _colsum(dy * yh)
        dh_ref[...] = _rms_bwd(dy, yh, r4, gf)

    return pl.pallas_call(
        body, grid=(s // ts,), name="ffn_fwd_loss",
        out_shape=(jax.ShapeDtypeStruct((s, D), F32), jax.ShapeDtypeStruct((s, 2 * FF), BF),
                   jax.ShapeDtypeStruct((1, D), F32), jax.ShapeDtypeStruct((1, D), F32)),
        in_specs=[_row_spec(ts, D), _const_spec((1, D)), _const_spec((NSH, D, HC)), _const_spec((FF, D)),
                  _const_spec((1, D)), _row_spec(ts, D)],
        out_specs=(_row_spec(ts, D), _row_spec(ts, 2 * FF), _acc_spec((1, D)), _acc_spec((1, D))),
        compiler_params=_cparams(("arbitrary",)),
    )(h, gain, wgu, wd, gfin, target)


def ffn_bwd(h, gu, dh, gain, wgu, wd, name):
    s = h.shape[0]
    ts = _tile(s, FFN_BWD_TILE)

    def body(h_ref, gu_ref, dh_ref, g_ref, wgu_ref, wd_ref, dx_ref, n_ref, a_ref, dgu_ref, dhb_ref, dg_ref):
        @pl.when(pl.program_id(0) == 0)
        def _():
            dg_ref[...] = jnp.zeros_like(dg_ref)

        x = h_ref[...]
        dh = dh_ref[...]
        gain_v = g_ref[...]
        xh, r = _rms(x)
        n_ref[...] = (xh * gain_v).astype(BF)
        dhb = (0.5 * dh).astype(BF)
        dhb_ref[...] = dhb
        dn = jnp.zeros((ts, D), F32)
        for j in range(2):
            g = gu_ref[:, j * HC:(j + 1) * HC].astype(F32)
            u = gu_ref[:, FF + j * HC:FF + (j + 1) * HC].astype(F32)
            sg = _sigmoid(g)
            sl = g * sg
            a_ref[:, j * HC:(j + 1) * HC] = (sl * u).astype(BF)
            da = _dot_nt(dhb, wd_ref[j * HC:(j + 1) * HC, :])
            dgb = (da * u * (sg * (1.0 + g * (1.0 - sg)))).astype(BF)
            dub = (da * sl).astype(BF)
            dgu_ref[:, j * HC:(j + 1) * HC] = dgb
            dgu_ref[:, FF + j * HC:FF + (j + 1) * HC] = dub
            dn = dn + _dot_nt(dgb, wgu_ref[j]) + _dot_nt(dub, wgu_ref[2 + j])
        dg_ref[...] += _colsum(dn * xh)
        dx_ref[...] = dh + _rms_bwd(dn, xh, r, gain_v)

    return pl.pallas_call(
        body, grid=(s // ts,), name=name,
        out_shape=(jax.ShapeDtypeStruct((s, D), F32), jax.ShapeDtypeStruct((s, D), BF),
                   jax.ShapeDtypeStruct((s, FF), BF), jax.ShapeDtypeStruct((s, 2 * FF), BF),
                   jax.ShapeDtypeStruct((s, D), BF), jax.ShapeDtypeStruct((1, D), F32)),
        in_specs=[_row_spec(ts, D), _row_spec(ts, 2 * FF), _row_spec(ts, D), _const_spec((1, D)),
                  _const_spec((NSH, D, HC)), _const_spec((FF, D))],
        out_specs=(_row_spec(ts, D), _row_spec(ts, D), _row_spec(ts, FF), _row_spec(ts, 2 * FF),
                   _row_spec(ts, D), _acc_spec((1, D))),
        compiler_params=_cparams(("arbitrary",)),
    )(h, gu, dh, gain, wgu, wd)


def dw_matmul(x, dy, nsplit, name, rider=None):
    s, k = x.shape
    n = dy.shape[1]
    nb = n // nsplit
    ts = _tile(s, 1024)
    nsteps = s // ts

    def body(x_ref, dy_ref, o_ref, ob_ref):
        @pl.when(pl.program_id(1) == 0)
        def _():
            o_ref[...] = jnp.zeros_like(o_ref)

        o_ref[0] += _dot_tn(x_ref[...], dy_ref[...])

        @pl.when(pl.program_id(1) == nsteps - 1)
        def _():
            ob_ref[...] = o_ref[...].astype(BF)

    spec = pl.BlockSpec((1, k, nb), lambda j, i: (j, 0, 0))
    (out, outb), rode = _pcall(
        body, grid=(nsplit, nsteps), name=name, args=(x, dy),
        out_shape=[jax.ShapeDtypeStruct((nsplit, k, nb), F32), jax.ShapeDtypeStruct((nsplit, k, nb), BF)],
        in_specs=[pl.BlockSpec((ts, k), lambda j, i: (i, 0)), pl.BlockSpec((ts, nb), lambda j, i: (i, j))],
        out_specs=[spec, spec], rider=rider)
    return out, outb, rode


def _split_in_proj(p, b):
    h = INB - D
    a_val = p[0][:, :D] + b[:, 0:D]
    a_gate = jnp.concatenate([p[0][:, D:], p[1][:, :h]], axis=1) + b[:, D:2 * D]
    b_u = p[1][:, h:] + b[:, 2 * D:3 * D]
    b_v = p[2][:, :D] + b[:, 3 * D:4 * D]
    g_a = jnp.concatenate([p[2][:, D:], p[3][:, :h]], axis=1) + b[:, 4 * D:5 * D]
    g_b = p[3][:, h:] + b[:, 5 * D:6 * D]
    return a_val, a_gate, b_u, b_v, g_a, g_b


def _sgu_mix(vnb, ws_ref, sb_ref, mixed_ref, ts):
    for ci in range(ts // CHUNK):
        rows = slice(ci * CHUNK, (ci + 1) * CHUNK)
        for g in range(NG):
            cols = slice(g * GD, (g + 1) * GD)
            mixed_ref[rows, cols] = _dot(ws_ref[g], vnb[rows, cols]) + sb_ref[:, cols]


SUB = 8
CB = 128
SH_ROWS_EXTRA = HALO - SUB


def _shifted_copies(ext_ref, sh_ref, lanes, ts):
    for b in range(1, SUB):
        sh_ref[b - 1] = ext_ref[b:b + ts + SH_ROWS_EXTRA, lanes]


def _window(ext_ref, sh_ref, lanes, first, r0, nrows):
    b = first % SUB
    a = first - b
    if b == 0:
        return ext_ref[a + r0:a + r0 + nrows, lanes]
    return sh_ref[b - 1, a + r0:a + r0 + nrows, :]


def mix_fwd(h, gain, win, b_in, conv_w, conv_b, lna_g, lna_b, wa, lnb_g, lnb_b, ws, sbias, wb, wo, rider=None):
    s = h.shape[0]
    ts = _tile(s, 256)

    def body(h_ref, g_ref, win_ref, bin_ref, cw_ref, cb_ref, lag_ref, lab_ref, wa_ref, lbg_ref, lbb_ref,
             ws_ref, sb_ref, wb_ref, wo_ref, o_ref, p_ref, n_ref, c_ref, ext_ref, mixed_ref, sh_ref):
        @pl.when(pl.program_id(0) == 0)
        def _():
            ext_ref[0:HALO, :] = jnp.zeros((HALO, D), F32)

        x = h_ref[...]
        xh, _ = _rms(x)
        nb = (xh * g_ref[...]).astype(BF)
        n_ref[...] = nb
        b = bin_ref[...]
        p = []
        for k in range(NSH):
            pk = _dot(nb, win_ref[k])
            p_ref[:, k * INB:(k + 1) * INB] = (pk + b[:, k * INB:(k + 1) * INB]).astype(BF)
            p.append(pk)
        a_val, a_gate, b_u, b_v, g_a, g_b = _split_in_proj(p, b)
        ext_ref[HALO:HALO + ts, :] = a_val * _sigmoid(a_gate)
        for l0 in range(0, D, CB):
            lanes = slice(l0, l0 + CB)
            _shifted_copies(ext_ref, sh_ref, lanes, ts)
            for r0 in range(0, ts, CB):
                acc = jnp.zeros((CB, CB), F32) + cb_ref[:, lanes]
                for k in range(CW):
                    acc = acc + cw_ref[k:k + 1, lanes] * _window(ext_ref, sh_ref, lanes,
                                                                 HALO - (CW - 1) + k, r0, CB)
                c_ref[r0:r0 + CB, lanes] = acc
        ext_ref[0:HALO, :] = ext_ref[ts:ts + HALO, :]
        ch, _ = _ln(c_ref[...])
        la = ch * lag_ref[...] + lab_ref[...]
        sa = (la * _sigmoid(la)).astype(BF)
        ya = _dot(sa, wa_ref[...])
        ub, _ = _gelu(b_u)
        gv, _ = _gelu(b_v)
        vh, _ = _ln(gv)
        vnb = (vh * lbg_ref[...] + lbb_ref[...]).astype(BF)
        _sgu_mix(vnb, ws_ref, sb_ref, mixed_ref, ts)
        ob = (ub * mixed_ref[...]).astype(BF)
        yb = _dot(ob, wb_ref[...])
        merged = (_sigmoid(g_a) * ya + _sigmoid(g_b) * yb).astype(BF)
        o_ref[...] = x + _dot(merged, wo_ref[...])

    vec = _const_spec((1, D))
    sq = _const_spec((D, D))
    return _pcall(
        body, grid=(s // ts,), name="mix_fwd",
        args=(h, gain, win, b_in, conv_w, conv_b, lna_g, lna_b, wa, lnb_g, lnb_b, ws, sbias, wb, wo),
        out_shape=(jax.ShapeDtypeStruct((s, D), F32), jax.ShapeDtypeStruct((s, DIN), BF),
                   jax.ShapeDtypeStruct((s, D), BF), jax.ShapeDtypeStruct((s, D), F32)),
        in_specs=[_row_spec(ts, D), vec, _const_spec((NSH, D, INB)), _const_spec((1, DIN)),
                  _const_spec((HALO, D)), vec, vec, vec, sq, vec, vec,
                  _const_spec((NG, CHUNK, CHUNK)), _const_spec((CHUNK, D)), sq, sq],
        out_specs=(_row_spec(ts, D), _row_spec(ts, DIN), _row_spec(ts, D), _row_spec(ts, D)),
        scratch=[pltpu.VMEM((ts + HALO, D), F32), pltpu.VMEM((ts, D), F32),
                 pltpu.VMEM((SUB - 1, ts + SH_ROWS_EXTRA, CB), F32)], rider=rider)


def mix_bwd_branches(p, c, dh, lna_g, lna_b, wa, lnb_g, lnb_b, ws, wst, sbias, wb, wo, rider=None):
    s = dh.shape[0]
    ts = _tile(s, 256)
    nsteps = s // ts

    def body(p_ref, c_ref, dh_ref, lag_ref, lab_ref, wa_ref, lbg_ref, lbb_ref, ws_ref, wst_ref, sb_ref,
             wb_ref, wo_ref, dc_ref, dp_ref, sa_ref, dya_ref, ob_ref, dyb_ref, mg_ref, dhb_ref,
             dws_ref, dsb_ref, dlag_ref, dlab_ref, dlbg_ref, dlbb_ref, mixed_ref, dmix_ref, dvn_ref, dsb_acc):
        step = pl.program_id(0)

        @pl.when(step == 0)
        def _():
            for ref in (dws_ref, dsb_acc, dlag_ref, dlab_ref, dlbg_ref, dlbb_ref):
                ref[...] = jnp.zeros_like(ref)

        b_u = p_ref[:, 2 * D:3 * D].astype(F32)
        b_v = p_ref[:, 3 * D:4 * D].astype(F32)
        sga = _sigmoid(p_ref[:, 4 * D:5 * D].astype(F32))
        sgb = _sigmoid(p_ref[:, 5 * D:6 * D].astype(F32))
        lag = lag_ref[...]
        ch, ra = _ln(c_ref[...])
        la = ch * lag + lab_ref[...]
        sla = _sigmoid(la)
        sa = (la * sla).astype(BF)
        sa_ref[...] = sa
        ya = _dot(sa, wa_ref[...])
        lbg = lbg_ref[...]
        ub, dub = _gelu_with_grad(b_u)
        gv, dgv = _gelu_with_grad(b_v)
        vh, rb = _ln(gv)
        vnb = (vh * lbg + lbb_ref[...]).astype(BF)
        _sgu_mix(vnb, ws_ref, sb_ref, mixed_ref, ts)
        mixed = mixed_ref[...]
        ob = (ub * mixed).astype(BF)
        ob_ref[...] = ob
        yb = _dot(ob, wb_ref[...])
        mg_ref[...] = (sga * ya + sgb * yb).astype(BF)
        dhb = dh_ref[...].astype(BF)
        dhb_ref[...] = dhb
        dm = _dot_nt(dhb, wo_ref[...])
        dp_ref[:, 0:2 * D] = jnp.zeros((ts, 2 * D), BF)
        dp_ref[:, 4 * D:5 * D] = (dm * ya * sga * (1.0 - sga)).astype(BF)
        dp_ref[:, 5 * D:6 * D] = (dm * yb * sgb * (1.0 - sgb)).astype(BF)
        dya = (dm * sga).astype(BF)
        dya_ref[...] = dya
        dyb = (dm * sgb).astype(BF)
        dyb_ref[...] = dyb
        dla = _dot_nt(dya, wa_ref[...]) * (sla * (1.0 + la * (1.0 - sla)))
        dlag_ref[...] += _colsum(dla * ch)
        dlab_ref[...] += _colsum(dla)
        dc_ref[...] = _ln_bwd(dla, ch, ra, lag)
        dob = _dot_nt(dyb, wb_ref[...])
        dp_ref[:, 2 * D:3 * D] = (dob * mixed * dub).astype(BF)
        dmix = dob * ub
        dmix_ref[...] = dmix.astype(BF)
        dsb = jnp.zeros((CHUNK, D), F32)
        for ci in range(ts // CHUNK):
            rows = slice(ci * CHUNK, (ci + 1) * CHUNK)
            dsb = dsb + dmix[rows, :]
            for g in range(NG):
                cols = slice(g * GD, (g + 1) * GD)
                dmb = dmix_ref[rows, cols]
                dws_ref[g] += _dot_nt(dmb, vnb[rows, cols])
                dvn_ref[rows, cols] = _dot(wst_ref[g], dmb)
        dsb_acc[...] += dsb
        dvn = dvn_ref[...]
        dlbg_ref[...] += _colsum(dvn * vh)
        dlbb_ref[...] += _colsum(dvn)
        dp_ref[:, 3 * D:4 * D] = (_ln_bwd(dvn, vh, rb, lbg) * dgv).astype(BF)

        @pl.when(step == nsteps - 1)
        def _():
            row = lax.broadcasted_iota(jnp.int32, (CHUNK, CHUNK), 0)
            col = lax.broadcasted_iota(jnp.int32, (CHUNK, CHUNK), 1)
            for g in range(NG):
                dws_ref[g] = jnp.where(col <= row, dws_ref[g], 0.0)
            acc = jnp.zeros((CHUNK, CHUNK), F32)
            for g in range(NG):
                tot = jnp.sum(dsb_acc[:, g * GD:(g + 1) * GD], axis=-1, keepdims=True)
                acc = acc + jnp.where(col == g, tot, 0.0)
            dsb_ref[...] = acc

    vec = _const_spec((1, D))
    sq = _const_spec((D, D))
    bf_rows = jax.ShapeDtypeStruct((s, D), BF)
    acc_vec = jax.ShapeDtypeStruct((1, D), F32)
    return _pcall(
        body, grid=(nsteps,), name="mix_bwd_branches",
        args=(p, c, dh, lna_g, lna_b, wa, lnb_g, lnb_b, ws, wst, sbias, wb, wo),
        out_shape=(jax.ShapeDtypeStruct((s, D), F32), jax.ShapeDtypeStruct((s, DIN), BF),
                   bf_rows, bf_rows, bf_rows, bf_rows, bf_rows, bf_rows,
                   jax.ShapeDtypeStruct((NG, CHUNK, CHUNK), F32), jax.ShapeDtypeStruct((CHUNK, CHUNK), F32),
                   acc_vec, acc_vec, acc_vec, acc_vec),
        in_specs=[_row_spec(ts, DIN), _row_spec(ts, D), _row_spec(ts, D), vec, vec, sq, vec, vec,
                  _const_spec((NG, CHUNK, CHUNK)), _const_spec((NG, CHUNK, CHUNK)), _const_spec((CHUNK, D)),
                  sq, sq],
        out_specs=(_row_spec(ts, D), _row_spec(ts, DIN)) + (_row_spec(ts, D),) * 6
        + (_acc_spec((NG, CHUNK, CHUNK)), _acc_spec((CHUNK, CHUNK))) + (_acc_spec((1, D)),) * 4,
        scratch=[pltpu.VMEM((ts, D), F32), pltpu.VMEM((ts, D), BF), pltpu.VMEM((ts, D), F32),
                 pltpu.VMEM((CHUNK, D), F32)], rider=rider)


def conv_bwd(p, dc, dp, conv_w):
    s = dc.shape[0]
    ts = _tile(s, 256)
    nsteps = s // ts
    per = ts // HALO

    rb = 16

    def body(pm_ref, pp_ref, dcm_ref, dcn_ref, cw_ref, dpin_ref, dp_ref, dw_ref, db_ref, ext_ref, dext_ref,
             dw8_ref, sh_ref, dsh_ref, dglu_ref):
        del dpin_ref
        step = pl.program_id(0)

        @pl.when(step == 0)
        def _():
            dw8_ref[...] = jnp.zeros_like(dw8_ref)
            db_ref[...] = jnp.zeros_like(db_ref)

        a_val = pm_ref[:, 0:D].astype(F32)
        sg = _sigmoid(pm_ref[:, D:2 * D].astype(F32))
        prev = pp_ref[:, 0:D].astype(F32) * _sigmoid(pp_ref[:, D:2 * D].astype(F32))
        ext_ref[0:HALO, :] = jnp.where(step > 0, prev, 0.0)
        ext_ref[HALO:HALO + ts, :] = a_val * sg
        dcm = dcm_ref[...]
        dext_ref[0:ts, :] = dcm
        dext_ref[ts:ts + HALO, :] = jnp.where(step < nsteps - 1, dcn_ref[...], 0.0)
        db_ref[...] += _colsum(dcm)
        for l0 in range(0, D, CB):
            lanes = slice(l0, l0 + CB)
            _shifted_copies(dext_ref, dsh_ref, lanes, ts)
            for r0 in range(0, ts, CB):
                acc = jnp.zeros((CB, CB), F32)
                for k in range(CW):
                    acc = acc + cw_ref[k:k + 1, lanes] * _window(dext_ref, dsh_ref, lanes, CW - 1 - k, r0, CB)
                dglu_ref[r0:r0 + CB, lanes] = acc
            _shifted_copies(ext_ref, sh_ref, lanes, ts)
            accs = [jnp.zeros((SUB, CB), F32) for _ in range(CW)]
            for r0 in range(0, ts, rb):
                dcb = dext_ref[r0:r0 + rb, lanes]
                for k in range(CW):
                    prod = dcb * _window(ext_ref, sh_ref, lanes, HALO - (CW - 1) + k, r0, rb)
                    accs[k] = accs[k] + jnp.sum(prod.reshape(rb // SUB, SUB, CB), axis=0)
            for k in range(CW):
                dw8_ref[k, :, lanes] += accs[k]
        dglu = dglu_ref[...]
        dp_ref[:, 0:D] = (dglu * sg).astype(BF)
        dp_ref[:, D:2 * D] = (dglu * a_val * sg * (1.0 - sg)).astype(BF)

        @pl.when(step == nsteps - 1)
        def _():
            dw_ref[...] = jnp.zeros_like(dw_ref)
            for k in range(CW):
                dw_ref[k:k + 1, :] = _colsum(dw8_ref[k])

    return pl.pallas_call(
        body, grid=(nsteps,), name="conv_bwd",
        out_shape=(jax.ShapeDtypeStruct((s, DIN), BF), jax.ShapeDtypeStruct((HALO, D), F32),
                   jax.ShapeDtypeStruct((1, D), F32)),
        in_specs=[pl.BlockSpec((ts, 2 * D), lambda i: (i, 0)),
                  pl.BlockSpec((HALO, 2 * D), lambda i: (jnp.maximum(i * per - 1, 0), 0)),
                  _row_spec(ts, D),
                  pl.BlockSpec((HALO, D), lambda i: (jnp.minimum((i + 1) * per, s // HALO - 1), 0)),
                  _const_spec((HALO, D)),
                  pl.BlockSpec(memory_space=pl.ANY)],
        out_specs=(pl.BlockSpec((ts, 2 * D), lambda i: (i, 0)), _acc_spec((HALO, D)), _acc_spec((1, D))),
        scratch_shapes=[pltpu.VMEM((ts + HALO, D), F32), pltpu.VMEM((ts + HALO, D), F32),
                        pltpu.VMEM((HALO, SUB, D), F32),
                        pltpu.VMEM((SUB - 1, ts + SH_ROWS_EXTRA, CB), F32),
                        pltpu.VMEM((SUB - 1, ts + SH_ROWS_EXTRA, CB), F32),
                        pltpu.VMEM((ts, D), F32)],
        input_output_aliases={5: 0},
        compiler_params=_cparams(("arbitrary",)),
    )(p, p, dc, dc, conv_w, dp)


def mix_bwd_in(dp, h, dh, gain, win):
    s = h.shape[0]
    ts = _tile(s, 512)

    def body(dp_ref, h_ref, dh_ref, g_ref, win_ref, dx_ref, dg_ref, db_ref):
        @pl.when(pl.program_id(0) == 0)
        def _():
            dg_ref[...] = jnp.zeros_like(dg_ref)
            db_ref[...] = jnp.zeros_like(db_ref)

        gain_v = g_ref[...]
        xh, r = _rms(h_ref[...])
        dn = jnp.zeros((ts, D), F32)
        for k in range(NSH):
            dpk = dp_ref[:, k * INB:(k + 1) * INB]
            dn = dn + _dot_nt(dpk, win_ref[k])
            db_ref[:, k * INB:(k + 1) * INB] += _colsum(dpk.astype(F32))
        dg_ref[...] += _colsum(dn * xh)
        dx_ref[...] = dh_ref[...] + _rms_bwd(dn, xh, r, gain_v)

    return pl.pallas_call(
        body, grid=(s // ts,), name="mix_bwd_in",
        out_shape=(jax.ShapeDtypeStruct((s, D), F32), jax.ShapeDtypeStruct((1, D), F32),
                   jax.ShapeDtypeStruct((1, DIN), F32)),
        in_specs=[_row_spec(ts, DIN), _row_spec(ts, D), _row_spec(ts, D), _const_spec((1, D)),
                  _const_spec((NSH, D, INB))],
        out_specs=(_row_spec(ts, D), _acc_spec((1, D)), _acc_spec((1, DIN))),
        compiler_params=_cparams(("arbitrary",)),
    )(dp, h, dh, gain, win)


def kv_proj(mem, gain, wkv):
    def body(m_ref, g_ref, w_ref, k_ref, v_ref, n_ref):
        xh, _ = _rms(m_ref[...])
        nb = (xh * g_ref[...]).astype(BF)
        n_ref[...] = nb
        half = D // 2
        for j in range(2):
            k_ref[:, j * half:(j + 1) * half] = _dot(nb, w_ref[j]).astype(BF)
            v_ref[:, j * half:(j + 1) * half] = _dot(nb, w_ref[2 + j]).astype(BF)

    o = jax.ShapeDtypeStruct((NMEM, D), BF)
    return pl.pallas_call(body, name="kv_proj", out_shape=(o, o, o), compiler_params=_cparams())(mem, gain, wkv)


def kv_bwd(mem, gain, memn, wkv, dk, dv):
    def body(m_ref, g_ref, n_ref, w_ref, dk_ref, dv_ref, dw_ref, dwb_ref, dg_ref):
        xh, _ = _rms(m_ref[...])
        nb = n_ref[...]
        half = D // 2
        dn = jnp.zeros((NMEM, D), F32)
        for j in range(2):
            dkb = dk_ref[:, j * half:(j + 1) * half].astype(BF)
            dvb = dv_ref[:, j * half:(j + 1) * half].astype(BF)
            for slot, dyb in ((j, dkb), (2 + j, dvb)):
                dw = _dot_tn(nb, dyb)
                dw_ref[slot] = dw
                dwb_ref[slot] = dw.astype(BF)
            dn = dn + _dot_nt(dkb, w_ref[j]) + _dot_nt(dvb, w_ref[2 + j])
        dg_ref[...] = _colsum(dn * xh)

    return pl.pallas_call(
        body, name="kv_bwd",
        out_shape=(jax.ShapeDtypeStruct((NSH, D, D // 2), F32), jax.ShapeDtypeStruct((NSH, D, D // 2), BF),
                   jax.ShapeDtypeStruct((1, D), F32)),
        compiler_params=_cparams())(mem, gain, memn, wkv, dk, dv)


def _attend(qb, k_ref, v_ref, h):
    cols = slice(h * HD, (h + 1) * HD)
    sc = _dot_nt(qb[:, cols], k_ref[:, cols]) * ATT_SCALE
    e = jnp.exp(sc - jnp.max(sc, axis=-1, keepdims=True))
    pr = e / jnp.sum(e, axis=-1, keepdims=True)
    return pr, _dot(pr.astype(BF), v_ref[:, cols])


def xattn_fwd(h, gain, wq, k, v, wo):
    s = h.shape[0]
    ts = _tile(s, 1024)

    def body(h_ref, g_ref, wq_ref, k_ref, v_ref, wo_ref, o_ref, att_ref):
        x = h_ref[...]
        xh, _ = _rms(x)
        nb = (xh * g_ref[...]).astype(BF)
        qb = _dot(nb, wq_ref[...]).astype(BF)
        for hd in range(NH):
            _, oh = _attend(qb, k_ref, v_ref, hd)
            att_ref[:, hd * HD:(hd + 1) * HD] = oh.astype(BF)
        o_ref[...] = x + _dot(att_ref[...], wo_ref[...])

    sq = _const_spec((D, D))
    kvs = _const_spec((NMEM, D))
    return pl.pallas_call(
        body, grid=(s // ts,), name="xattn_fwd",
        out_shape=jax.ShapeDtypeStruct((s, D), F32),
        in_specs=[_row_spec(ts, D), _const_spec((1, D)), sq, kvs, kvs, sq],
        out_specs=_row_spec(ts, D),
        scratch_shapes=[pltpu.VMEM((ts, D), BF)],
        compiler_params=_cparams(("arbitrary",)),
    )(h, gain, wq, k, v, wo)


def xattn_bwd(h, dh, gain, wq, k, v, wo, rider=None):
    s = h.shape[0]
    ts = _tile(s, 512)

    def body(h_ref, dh_ref, g_ref, wq_ref, k_ref, v_ref, wo_ref,
             dx_ref, n_ref, dq_ref, att_ref, dhb_ref, dk_ref, dv_ref, dg_ref):
        @pl.when(pl.program_id(0) == 0)
        def _():
            for ref in (dk_ref, dv_ref, dg_ref):
                ref[...] = jnp.zeros_like(ref)

        x = h_ref[...]
        dh = dh_ref[...]
        gain_v = g_ref[...]
        xh, r = _rms(x)
        nb = (xh * gain_v).astype(BF)
        n_ref[...] = nb
        qb = _dot(nb, wq_ref[...]).astype(BF)
        dhb = dh.astype(BF)
        dhb_ref[...] = dhb
        dob = _dot_nt(dhb, wo_ref[...]).astype(BF)
        for hd in range(NH):
            cols = slice(hd * HD, (hd + 1) * HD)
            pr, oh = _attend(qb, k_ref, v_ref, hd)
            att_ref[:, cols] = oh.astype(BF)
            doh = dob[:, cols]
            dpr = _dot_nt(doh, v_ref[:, cols])
            dv_ref[:, cols] += _dot_tn(pr.astype(BF), doh)
            dsc = (pr * (dpr - jnp.sum(dpr * pr, axis=-1, keepdims=True)) * ATT_SCALE).astype(BF)
            dq_ref[:, cols] = _dot(dsc, k_ref[:, cols]).astype(BF)
            dk_ref[:, cols] += _dot_tn(dsc, qb[:, cols])
        dn = _dot_nt(dq_ref[...], wq_ref[...])
        dg_ref[...] += _colsum(dn * xh)
        dx_ref[...] = dh + _rms_bwd(dn, xh, r, gain_v)

    sq = _const_spec((D, D))
    kvs = _const_spec((NMEM, D))
    bf_rows = jax.ShapeDtypeStruct((s, D), BF)
    kv_acc = jax.ShapeDtypeStruct((NMEM, D), F32)
    return _pcall(
        body, grid=(s // ts,), name="xattn_bwd", args=(h, dh, gain, wq, k, v, wo),
        out_shape=(jax.ShapeDtypeStruct((s, D), F32), bf_rows, bf_rows, bf_rows, bf_rows, kv_acc, kv_acc,
                   jax.ShapeDtypeStruct((1, D), F32)),
        in_specs=[_row_spec(ts, D), _row_spec(ts, D), _const_spec((1, D)), sq, kvs, kvs, sq],
        out_specs=(_row_spec(ts, D),) * 5 + (_acc_spec((NMEM, D)), _acc_spec((NMEM, D)), _acc_spec((1, D))),
        rider=rider)


BLOCK_BYTES = 3 << 19


def _row_block(rows, cols):
    rb = rows
    while rb * cols * 4 > BLOCK_BYTES and rb % 32 == 0:
        rb //= 2
    return rb


def cast_bf16(w, chip, name):
    r, c = w.shape
    rb = _row_block(r, c)

    def body(chip_ref, w_ref, o_ref):
        del chip_ref
        o_ref[0] = w_ref[...].astype(BF)

    return pl.pallas_call(
        body, name=name, out_shape=jax.ShapeDtypeStruct((NSH, r, c), BF),
        grid_spec=pltpu.PrefetchScalarGridSpec(
            num_scalar_prefetch=1, grid=(r // rb,),
            in_specs=[pl.BlockSpec((rb, c), lambda i, chip_ref: (i, 0))],
            out_specs=pl.BlockSpec((1, rb, c), lambda i, chip_ref: (chip_ref[0], i, 0))),
        compiler_params=_cparams(("arbitrary",)))(chip, w)


NDEV = 8


def device_sum(g4, recv, place, name):
    _, _, rh, c = g4.shape
    rb = _row_block(rh, c)

    def body(place_ref, g_ref, r_ref, o_ref):
        del place_ref
        acc = g_ref[0, 0]
        for j in range(NDEV - 1):
            acc = acc + r_ref[j].astype(F32)
        o_ref[0] = acc

    return pl.pallas_call(
        body, name=name, out_shape=jax.ShapeDtypeStruct((2, rh, c), F32),
        grid_spec=pltpu.PrefetchScalarGridSpec(
            num_scalar_prefetch=1, grid=(rh // rb,),
            in_specs=[pl.BlockSpec((1, 1, rb, c), lambda i, place_ref: (place_ref[0], place_ref[1], i, 0)),
                      pl.BlockSpec((NDEV - 1, rb, c), lambda i, place_ref: (0, i, 0))],
            out_specs=pl.BlockSpec((1, rb, c), lambda i, place_ref: (place_ref[1], i, 0))),
        compiler_params=_cparams(("arbitrary",)))(place, g4, recv)


def _adamw_math(w, g, m, v):
    m = ADAM_B1 * m + (1.0 - ADAM_B1) * g
    v = ADAM_B2 * v + (1.0 - ADAM_B2) * (g * g)
    m_hat = m / (1.0 - ADAM_B1 ** ADAM_STEP)
    v_hat = v / (1.0 - ADAM_B2 ** ADAM_STEP)
    delta = -ADAM_LR * (m_hat / (jnp.sqrt(v_hat) + ADAM_EPS) + ADAM_WD * w)
    return delta, m, v


def adamw(w, g, m, v, name):
    r, c = w.shape
    rb = _row_block(r, c)

    def body(w_ref, g_ref, m_ref, v_ref, d_ref, mo_ref, vo_ref):
        d, mn, vn = _adamw_math(w_ref[...], g_ref[...], m_ref[...], v_ref[...])
        d_ref[...] = d
        mo_ref[...] = mn
        vo_ref[...] = vn

    o = jax.ShapeDtypeStruct((r, c), F32)
    spec = _row_spec(rb, c)
    return pl.pallas_call(
        body, grid=(r // rb,), name=name, out_shape=(o, o, o),
        in_specs=[spec] * 4, out_specs=(spec,) * 3,
        compiler_params=_cparams(("arbitrary",)))(w, g, m, v)


def _place():
    return lax.axis_index("x"), lax.axis_index("y"), lax.axis_index("c")


def _other_chips(x, y):
    return [(1 - x, y), (x, 1 - y), (1 - x, 1 - y)]


NOTHER = NSH - 1


def gather_rider(arrays):
    nw = len(arrays)
    nici = nw * NOTHER

    def copies(refs, send_sems, recv_sems):
        x, y, c = _place()
        ici, d2d = [], []
        for w in range(nw):
            for j, (px, py) in enumerate(_other_chips(x, y)):
                n = w * NOTHER + j
                sems = dict(send_sem=send_sems.at[n], recv_sem=recv_sems.at[n],
                            device_id=(px, py, c), device_id_type=MESH)
                mine = refs[w].at[2 * x + y, c]
                theirs = refs[w].at[2 * px + py, c]
                ici.append((pltpu.make_async_remote_copy(src_ref=mine, dst_ref=mine, **sems),
                            pltpu.make_async_remote_copy(src_ref=mine, dst_ref=theirs, **sems)))
                sems = dict(send_sem=send_sems.at[nici + n], recv_sem=recv_sems.at[nici + n],
                            device_id=(x, y, 1 - c), device_id_type=MESH)
                d2d.append((pltpu.make_async_remote_copy(src_ref=theirs, dst_ref=theirs, **sems),
                            pltpu.make_async_remote_copy(src_ref=theirs, dst_ref=refs[w].at[2 * px + py, 1 - c],
                                                         **sems)))
        return ici, d2d

    def start(ins, outs, send_sems, recv_sems):
        ici, _ = copies(outs, send_sems, recv_sems)
        for send, _ in ici:
            send.start()

    def relay(ins, outs, send_sems, recv_sems):
        ici, d2d = copies(outs, send_sems, recv_sems)
        for (_, landed), (forward, _) in zip(ici, d2d):
            landed.wait_recv()
            forward.start()

    def finish(ins, outs, send_sems, recv_sems):
        ici, d2d = copies(outs, send_sems, recv_sems)
        for _, landed in d2d:
            landed.wait_recv()
        for send, _ in ici + d2d:
            send.wait_send()

    return Rider(arrays, [jax.ShapeDtypeStruct(a.shape, a.dtype) for a in arrays], {i: i for i in range(nw)},
                 2 * nici, start, finish, relay)


def _peers(x, y, c):
    return [(x ^ (rel >> 2), y ^ ((rel >> 1) & 1), c ^ (rel & 1)) for rel in range(1, NDEV)]


def reduce_rider(grads):
    nw = len(grads)
    npeer = NDEV - 1

    def copies(ins, outs, send_sems, recv_sems):
        x, y, c = _place()
        return [pltpu.make_async_remote_copy(
            src_ref=ins[w].at[2 * px + py, pc], dst_ref=outs[w].at[r],
            send_sem=send_sems.at[w * npeer + r], recv_sem=recv_sems.at[w * npeer + r],
            device_id=(px, py, pc), device_id_type=MESH)
            for w in range(nw) for r, (px, py, pc) in enumerate(_peers(x, y, c))]

    def start(ins, outs, send_sems, recv_sems):
        for cp in copies(ins, outs, send_sems, recv_sems):
            cp.start()

    def finish(ins, outs, send_sems, recv_sems):
        for cp in copies(ins, outs, send_sems, recv_sems):
            cp.wait()

    return Rider(grads, [jax.ShapeDtypeStruct((npeer,) + g.shape[2:], g.dtype) for g in grads], {},
                 nw * npeer, start, finish)


class _Offset:
    def __init__(self, ref, base):
        self.ref, self.base = ref, base

    @property
    def at(self):
        return self

    def __getitem__(self, i):
        return self.ref.at[self.base + i]


def merge_riders(riders):
    ins, outs, aliases, spans, nsem = [], [], {}, [], 0
    for r in riders:
        spans.append((len(ins), len(outs), nsem))
        aliases.update({len(ins) + i: len(outs) + j for i, j in r.aliases.items()})
        ins, outs, nsem = ins + r.ins, outs + r.outs, nsem + r.nsem

    def each(step):
        def run(in_refs, out_refs, send_sems, recv_sems):
            for r, (i0, o0, s0) in zip(riders, spans):
                if getattr(r, step) is not None:
                    getattr(r, step)(in_refs[i0:i0 + len(r.ins)], out_refs[o0:o0 + len(r.outs)],
                                     _Offset(send_sems, s0), _Offset(recv_sems, s0))
        return run

    relay = each("relay") if any(r.relay is not None for r in riders) else None
    return Rider(ins, outs, aliases, nsem, each("start"), each("finish"), relay)


def split_results(riders, results):
    out, o0 = [], 0
    for r in riders:
        out.append(tuple(results[o0:o0 + len(r.outs)]))
        o0 += len(r.outs)
    return out


def swap_rider(halves):
    nw = len(halves)

    def copies(refs, send_sems, recv_sems):
        x, y, c = _place()
        out = []
        for w in range(nw):
            sems = dict(send_sem=send_sems.at[w], recv_sem=recv_sems.at[w],
                        device_id=(x, y, 1 - c), device_id_type=MESH)
            mine = refs[w].at[c]
            out.append((pltpu.make_async_remote_copy(src_ref=mine, dst_ref=mine, **sems),
                        pltpu.make_async_remote_copy(src_ref=mine, dst_ref=refs[w].at[1 - c], **sems)))
        return out

    def start(ins, outs, send_sems, recv_sems):
        for send, _ in copies(outs, send_sems, recv_sems):
            send.start()

    def finish(ins, outs, send_sems, recv_sems):
        cps = copies(outs, send_sems, recv_sems)
        for _, recv in cps:
            recv.wait_recv()
        for send, _ in cps:
            send.wait_send()

    return Rider(halves, [jax.ShapeDtypeStruct(h.shape, h.dtype) for h in halves], {i: i for i in range(nw)},
                 nw, start, finish)


def allgather_rider(slots):
    def copies(ref, send_sems, recv_sems):
        x, y, c = _place()
        mine = ref.at[4 * x + 2 * y + c]
        out = []
        for r, peer in enumerate(_peers(x, y, c)):
            sems = dict(send_sem=send_sems.at[r], recv_sem=recv_sems.at[r], device_id=peer, device_id_type=MESH)
            out.append((pltpu.make_async_remote_copy(src_ref=mine, dst_ref=mine, **sems),
                        pltpu.make_async_remote_copy(
                            src_ref=mine, dst_ref=ref.at[4 * peer[0] + 2 * peer[1] + peer[2]], **sems)))
        return out

    def start(ins, outs, send_sems, recv_sems):
        for send, _ in copies(outs[0], send_sems, recv_sems):
            send.start()

    def finish(ins, outs, send_sems, recv_sems):
        cps = copies(outs[0], send_sems, recv_sems)
        for _, recv in cps:
            recv.wait_recv()
        for send, _ in cps:
            send.wait_send()

    return Rider([slots], [jax.ShapeDtypeStruct(slots.shape, slots.dtype)], {0: 0}, NDEV - 1, start, finish)


def sum_slots(slots):
    def body(s_ref, o_ref):
        acc = s_ref[0]
        for dev in range(1, NDEV):
            acc = acc + s_ref[dev]
        o_ref[...] = acc

    return pl.pallas_call(body, name="sum_slots", out_shape=jax.ShapeDtypeStruct(slots.shape[1:], F32),
                          compiler_params=_cparams())(slots)


BIG = ("ffn1_w_gu", "ffn1_w_down", "w_in", "w_a_out", "w_b_out", "w_out", "w_q", "w_kv", "w_o",
       "ffn2_w_gu", "ffn2_w_down")
SMALL = {"ffn1_norm": (0, 1), "mix_norm": (8, 1), "xattn_norm": (16, 1), "mem_norm": (24, 1),
         "ffn2_norm": (32, 1), "final_norm": (40, 1), "conv_b": (48, 1), "conv_ln_g": (56, 1),
         "conv_ln_b": (64, 1), "sgu_ln_g": (72, 1), "sgu_ln_b": (80, 1), "b_in": (88, 6),
         "conv_w": (96, CW), "sgu_w": (128, 64), "sgu_b": (192, 1)}
LOSS_ROW = 200
SMALL_ROWS = 208


def _pad_rows(a, rows):
    return jnp.pad(a, ((0, rows - a.shape[0]), (0, D - a.shape[1])))


def _pack_small(parts):
    names = sorted(parts, key=lambda n: SMALL[n][0] if n in SMALL else LOSS_ROW)
    rows = []
    for i, n in enumerate(names):
        start = SMALL[n][0] if n in SMALL else LOSS_ROW
        end = SMALL_ROWS if i + 1 == len(names) else (SMALL[names[i + 1]][0] if names[i + 1] in SMALL else LOSS_ROW)
        rows.append(_pad_rows(parts[n], end - start))
    return jnp.concatenate(rows, axis=0)


def _small_views(w):
    return {
        "ffn1_norm": w["ffn1_norm"], "mix_norm": w["mix_norm"], "xattn_norm": w["xattn_norm"],
        "mem_norm": w["mem_norm"], "ffn2_norm": w["ffn2_norm"], "final_norm": w["final_norm"].reshape(1, D),
        "conv_b": w["conv_b"], "conv_ln_g": w["conv_ln_g"], "conv_ln_b": w["conv_ln_b"],
        "sgu_ln_g": w["sgu_ln_g"], "sgu_ln_b": w["sgu_ln_b"], "b_in": w["b_in"].reshape(6, D),
        "conv_w": w["conv_w"][0], "sgu_w": w["sgu_w"].reshape(64, D), "sgu_b": w["sgu_b"].reshape(1, NG * CHUNK),
    }


def _unpack_small(buf, like, chip):
    out = {}
    for n, (start, rows) in SMALL.items():
        blk = buf[start:start + rows]
        if n == "conv_w":
            blk = blk[:, :like[n].shape[-1]] if chip is None else lax.dynamic_slice_in_dim(
                blk, chip * like[n].shape[-1], like[n].shape[-1], axis=1)
        elif n == "sgu_b":
            blk = blk[:, :NG * CHUNK]
        out[n] = blk.reshape(like[n].shape)
    return out


def kernel(x, mem, ffn1_norm, ffn1_w_gu, ffn1_w_down, mix_norm, w_in, b_in, conv_w, conv_b, conv_ln_g, conv_ln_b, w_a_out, sgu_ln_g, sgu_ln_b, sgu_w, sgu_b, w_b_out, w_out, xattn_norm, mem_norm, w_q, w_kv, w_o, ffn2_norm, ffn2_w_gu, ffn2_w_down, final_norm, loss_target, m_ffn1_norm, m_ffn1_w_gu, m_ffn1_w_down, m_mix_norm, m_w_in, m_b_in, m_conv_w, m_conv_b, m_conv_ln_g, m_conv_ln_b, m_w_a_out, m_sgu_ln_g, m_sgu_ln_b, m_sgu_w, m_sgu_b, m_w_b_out, m_w_out, m_xattn_norm, m_mem_norm, m_w_q, m_w_kv, m_w_o, m_ffn2_norm, m_ffn2_w_gu, m_ffn2_w_down, m_final_norm, v_ffn1_norm, v_ffn1_w_gu, v_ffn1_w_down, v_mix_norm, v_w_in, v_b_in, v_conv_w, v_conv_b, v_conv_ln_g, v_conv_ln_b, v_w_a_out, v_sgu_ln_g, v_sgu_ln_b, v_sgu_w, v_sgu_b, v_w_b_out, v_w_out, v_xattn_norm, v_mem_norm, v_w_q, v_w_kv, v_w_o, v_ffn2_norm, v_ffn2_w_gu, v_ffn2_w_down, v_final_norm):
    names = ("ffn1_norm", "ffn1_w_gu", "ffn1_w_down", "mix_norm", "w_in", "b_in", "conv_w", "conv_b",
             "conv_ln_g", "conv_ln_b", "w_a_out", "sgu_ln_g", "sgu_ln_b", "sgu_w", "sgu_b", "w_b_out", "w_out",
             "xattn_norm", "mem_norm", "w_q", "w_kv", "w_o", "ffn2_norm", "ffn2_w_gu", "ffn2_w_down",
             "final_norm")
    wts = dict(zip(names, (ffn1_norm, ffn1_w_gu, ffn1_w_down, mix_norm, w_in, b_in, conv_w, conv_b, conv_ln_g,
                           conv_ln_b, w_a_out, sgu_ln_g, sgu_ln_b, sgu_w, sgu_b, w_b_out, w_out, xattn_norm,
                           mem_norm, w_q, w_kv, w_o, ffn2_norm, ffn2_w_gu, ffn2_w_down, final_norm)))
    mom1 = dict(zip(names, (m_ffn1_norm, m_ffn1_w_gu, m_ffn1_w_down, m_mix_norm, m_w_in, m_b_in, m_conv_w,
                            m_conv_b, m_conv_ln_g, m_conv_ln_b, m_w_a_out, m_sgu_ln_g, m_sgu_ln_b, m_sgu_w,
                            m_sgu_b, m_w_b_out, m_w_out, m_xattn_norm, m_mem_norm, m_w_q, m_w_kv, m_w_o,
                            m_ffn2_norm, m_ffn2_w_gu, m_ffn2_w_down, m_final_norm)))
    mom2 = dict(zip(names, (v_ffn1_norm, v_ffn1_w_gu, v_ffn1_w_down, v_mix_norm, v_w_in, v_b_in, v_conv_w,
                            v_conv_b, v_conv_ln_g, v_conv_ln_b, v_w_a_out, v_sgu_ln_g, v_sgu_ln_b, v_sgu_w,
                            v_sgu_b, v_w_b_out, v_w_out, v_xattn_norm, v_mem_norm, v_w_q, v_w_kv, v_w_o,
                            v_ffn2_norm, v_ffn2_w_gu, v_ffn2_w_down, v_final_norm)))
    xi, yi, ci = _place()
    chip = (2 * xi + yi).astype(jnp.int32)
    core = ci.astype(jnp.int32)
    chip_arr = chip.reshape(1)
    place_arr = jnp.stack([chip, core])
    x2, mem2, tgt = x[0], mem[0], loss_target[0]

    slot = {n: cast_bf16(wts[n][0], chip_arr, "cast_" + n) for n in BIG}
    cw_pad = jnp.pad(conv_w[0], ((0, HALO - CW), (0, 0)))
    slot["conv_w"] = lax.dynamic_update_slice(jnp.zeros((NSH,) + cw_pad.shape, F32), cw_pad[None], (chip, 0, 0))
    g_first = ("ffn1_w_gu",)
    g_mix = ("ffn1_w_down", "w_in", "w_a_out", "w_b_out", "w_out", "conv_w")
    g_rest = ("w_q", "w_kv", "w_o", "ffn2_w_gu", "ffn2_w_down")

    def gather(group):
        return gather_rider([slot[n].reshape(NSH, 2, slot[n].shape[1] // 2, slot[n].shape[2]) for n in group])

    def gathered(group, res):
        return {n: r.reshape(slot[n].shape) for n, r in zip(group, res)}

    full = gathered(g_first, run_rider(gather(g_first), "gather_ffn1"))
    wgu1 = full["ffn1_w_gu"]
    tril = jnp.tril(jnp.ones((CHUNK, CHUNK), dtype=bool))
    ws = jnp.where(tril[None], sgu_w[0], 0.0).astype(BF)
    wst = jnp.transpose(ws, (0, 2, 1))
    sbias = jnp.repeat(jnp.transpose(sgu_b[0]), GD, axis=1)
    gfin = final_norm.reshape(1, D)

    (gu1, act1), rode = ffn_hidden(x2, ffn1_norm, wgu1, rider=gather(g_mix))
    full.update(gathered(g_mix, rode))
    wd1 = full["ffn1_w_down"].reshape(FF, D)
    h1 = ffn_down(x2, act1, wd1)
    win = full["w_in"]
    wa, wb, wout = (full[n].reshape(D, D) for n in ("w_a_out", "w_b_out", "w_out"))
    cw_full = jnp.transpose(full["conv_w"], (1, 0, 2)).reshape(HALO, D)
    (h2, proj, n2b, conv_out), rode = mix_fwd(
        h1, mix_norm, win, b_in, cw_full, conv_b, conv_ln_g, conv_ln_b, wa, sgu_ln_g, sgu_ln_b, ws, sbias, wb,
        wout, rider=gather(g_rest))
    full.update(gathered(g_rest, rode))
    wgu2, wd2, wkv = full["ffn2_w_gu"], full["ffn2_w_down"].reshape(FF, D), full["w_kv"]
    wq, wo = full["w_q"].reshape(D, D), full["w_o"].reshape(D, D)
    kb, vb, memn = kv_proj(mem2, mem_norm, wkv)
    h3 = xattn_fwd(h2, xattn_norm, wq, kb, vb, wo)
    dh4, gu2, loss_lanes, d_final = ffn_fwd_loss(h3, ffn2_norm, wgu2, wd2, gfin, tgt)

    own, halves = {}, {}

    def exchange(group, grads):
        views = []
        for n, (g, gb) in zip(group, grads):
            rs, cs = wts[n].shape[1:]
            own[n] = g.reshape(NSH, 2, rs // 2, cs)
            views.append(gb.reshape(NSH, 2, rs // 2, cs))
        return reduce_rider(views)

    def reduce(group, recv):
        for n, r in zip(group, recv):
            halves[n] = device_sum(own[n], r, place_arr, "device_sum_" + n)

    dh3, n4, a4, dgu4, dhb4, d_ffn2n = ffn_bwd(h3, gu2, dh4, ffn2_norm, wgu2, wd2, "ffn2_bwd")
    g_ffn2 = ("ffn2_w_gu", "ffn2_w_down")
    ride = exchange(g_ffn2, [dw_matmul(n4, dgu4, NSH, "dw_ffn2_gu")[:2], dw_matmul(a4, dhb4, 1, "dw_ffn2_down")[:2]])
    (dh2, n3, dq, att, dhb3, dk, dv, d_xn), rode = xattn_bwd(h2, dh3, xattn_norm, wq, kb, vb, wo, rider=ride)
    reduce(g_ffn2, rode)
    g_att = ("w_q", "w_o", "w_kv")
    d_wkv, d_wkv_b, d_memn = kv_bwd(mem2, mem_norm, memn, wkv, dk, dv)
    ride = exchange(g_att, [dw_matmul(n3, dq, 1, "dw_q")[:2], dw_matmul(att, dhb3, 1, "dw_o")[:2],
                            (d_wkv, d_wkv_b)])
    ((dconv, dproj, sa, dya, ob, dyb, mg, dhb2, d_sgu_w, d_sgu_b, d_lna_g, d_lna_b, d_lnb_g, d_lnb_b),
     rode) = mix_bwd_branches(proj, conv_out, dh2, conv_ln_g, conv_ln_b, wa, sgu_ln_g, sgu_ln_b, ws, wst,
                              sbias, wb, wout, rider=ride)
    reduce(g_att, rode)
    dproj, d_conv_w, d_conv_b = conv_bwd(proj, dconv, dproj, cw_full)
    dh1, d_mixn, d_b_in = mix_bwd_in(dproj, h1, dh2, mix_norm, win)
    g_mixw = ("w_in", "w_a_out", "w_b_out")
    ride = exchange(g_mixw, [dw_matmul(n2b, dproj, NSH, "dw_in")[:2], dw_matmul(sa, dya, 1, "dw_a_out")[:2],
                             dw_matmul(ob, dyb, 1, "dw_b_out")[:2]])
    dx, n1, a1, dgu1, dhb1, d_ffn1n = ffn_bwd(x2, gu1, dh1, ffn1_norm, wgu1, wd1, "ffn1_bwd")
    small_grads = {
        "ffn1_norm": d_ffn1n, "mix_norm": d_mixn, "xattn_norm": d_xn, "mem_norm": d_memn, "ffn2_norm": d_ffn2n,
        "final_norm": d_final, "conv_b": d_conv_b, "conv_ln_g": d_lna_g, "conv_ln_b": d_lna_b,
        "sgu_ln_g": d_lnb_g, "sgu_ln_b": d_lnb_b, "b_in": d_b_in.reshape(6, D), "conv_w": d_conv_w[:CW],
        "sgu_w": d_sgu_w.reshape(64, D), "sgu_b": jnp.transpose(d_sgu_b[:, :NG]).reshape(1, NG * CHUNK),
        "loss": loss_lanes}
    slots = lax.dynamic_update_slice(jnp.zeros((NDEV, SMALL_ROWS, D), F32), _pack_small(small_grads)[None],
                                     (2 * chip + core, 0, 0))
    riders = [ride, allgather_rider(slots)]
    d_wgu1, d_wgu1_b, rode = dw_matmul(n1, dgu1, NSH, "dw_ffn1_gu", rider=merge_riders(riders))
    recv_mix, all_slots = split_results(riders, rode)
    reduce(g_mixw, recv_mix)
    ride = exchange(("ffn1_w_gu",), [(d_wgu1, d_wgu1_b)])
    d_down, d_down_b, rode = dw_matmul(a1, dhb1, 1, "dw_ffn1_down", rider=ride)
    reduce(("ffn1_w_gu",), rode)
    ride = exchange(("ffn1_w_down",), [(d_down, d_down_b)])
    d_out, d_out_b, rode = dw_matmul(mg, dhb2, 1, "dw_out", rider=ride)
    reduce(("ffn1_w_down",), rode)
    reduce(("w_out",), run_rider(exchange(("w_out",), [(d_out, d_out_b)]), "exchange_w_out"))
    swapped = run_rider(swap_rider([halves[n] for n in BIG]), "pair_swap")
    gshard = {n: g.reshape(wts[n].shape[1:]) for n, g in zip(BIG, swapped)}

    small = sum_slots(all_slots[0])
    loss = (0.5 / D) * jnp.sum(small[LOSS_ROW])
    gsmall = _unpack_small(small, wts, chip)

    out_g, out_d, out_m, out_v = dict(gsmall), {}, {}, {}
    sw, sm, sv = (_pack_small(_small_views(t))[:LOSS_ROW] for t in (wts, mom1, mom2))
    sg = _pack_small(_small_views({n: gsmall[n] for n in SMALL}))[:LOSS_ROW]
    for dst, packed in zip((out_d, out_m, out_v), adamw(sw, sg, sm, sv, "adamw_small")):
        dst.update(_unpack_small(packed, wts, None))
    for n in BIG:
        shape = wts[n].shape
        out_g[n] = gshard[n].reshape(shape)
        d, mn, vn = adamw(wts[n][0], gshard[n], mom1[n][0], mom2[n][0], "adamw_" + n)
        out_d[n], out_m[n], out_v[n] = d.reshape(shape), mn.reshape(shape), vn.reshape(shape)
    return (loss, dx[None], *[out_g[n] for n in names], *[out_d[n] for n in names],
            *[out_m[n] for n in names], *[out_v[n] for n in names])
```

```python
import functools
import math

import jax
import jax.numpy as jnp
from jax import lax
from jax.experimental import pallas as pl
from jax.experimental.pallas import tpu as pltpu

F32 = jnp.float32
BF = jnp.bfloat16
MESH = pl.DeviceIdType.MESH

D = 1024
FF = 2816
HC = FF // 2
NSH = 4
DIN = 6 * D
INB = DIN // NSH
CW = 31
HALO = 32
CHUNK = 128
NG = 4
GD = D // NG
NH = 4
HD = D // NH
NMEM = 256
EPS_RMS = 1e-6
EPS_LN = 1e-5
GELU_C0 = math.sqrt(2.0 / math.pi)
GELU_C1 = 0.044715
ATT_SCALE = 1.0 / math.sqrt(HD)

ADAM_LR = 0.001
ADAM_B1 = 0.9
ADAM_B2 = 0.999
ADAM_EPS = 1e-08
ADAM_WD = 0.01
ADAM_STEP = 10

VMEM_LIMIT = 56 * 1024 * 1024


def _cparams(sem=None, **kw):
    if sem is not None:
        kw["dimension_semantics"] = sem
    return pltpu.CompilerParams(vmem_limit_bytes=VMEM_LIMIT, **kw)


def _dot(a, b):
    return jnp.dot(a, b, preferred_element_type=F32)


def _dot_nt(a, b):
    return lax.dot_general(a, b, (((1,), (1,)), ((), ())), preferred_element_type=F32)


def _dot_tn(a, b):
    return lax.dot_general(a, b, (((0,), (0,)), ((), ())), preferred_element_type=F32)


def _sigmoid(x):
    return 1.0 / (1.0 + jnp.exp(-x))


def _gelu(x):
    t = jnp.tanh(GELU_C0 * (x + GELU_C1 * (x * x * x)))
    return 0.5 * x * (1.0 + t), t


def _gelu_with_grad(x):
    x2 = x * x
    t = jnp.tanh(GELU_C0 * (x + GELU_C1 * (x2 * x)))
    onep = 1.0 + t
    hx = 0.5 * x
    grad = 0.5 * onep + hx * (1.0 - t * t) * (GELU_C0 + (3.0 * GELU_C0 * GELU_C1) * x2)
    return hx * onep, grad


def _mean(x):
    return jnp.mean(x, axis=-1, keepdims=True)


def _rms(x):
    r = lax.rsqrt(_mean(x * x) + EPS_RMS)
    return x * r, r


def _rms_bwd(dn, xh, r, g):
    dxh = dn * g
    return r * (dxh - xh * _mean(dxh * xh))


def _ln(x):
    xc = x - _mean(x)
    r = lax.rsqrt(_mean(xc * xc) + EPS_LN)
    return xc * r, r


def _ln_bwd(dy, xh, r, g):
    dxh = dy * g
    return r * (dxh - _mean(dxh) - xh * _mean(dxh * xh))


def _colsum(x):
    return jnp.sum(x, axis=0, keepdims=True)


def _const_spec(shape):
    nd = len(shape)
    return pl.BlockSpec(shape, lambda *_: (0,) * nd, pipeline_mode=pl.Buffered(1))


def _row_spec(ts, width):
    return pl.BlockSpec((ts, width), lambda i: (i, 0))


def _acc_spec(shape):
    nd = len(shape)
    return pl.BlockSpec(shape, lambda *_: (0,) * nd)


def _tile(s, want):
    return min(s, want)


HBM_SPEC = pl.BlockSpec(memory_space=pltpu.HBM)


class Rider:
    def __init__(self, ins, outs, aliases, nsem, start, finish, relay=None):
        self.ins, self.outs, self.aliases, self.nsem = list(ins), list(outs), dict(aliases), nsem
        self.start, self.finish, self.relay = start, finish, relay


def _pcall(body, *, name, grid, args, in_specs, out_shape, out_specs, scratch=(), rider=None):
    sem = ("arbitrary",) * len(grid)
    n_in, n_out = len(args), len(out_shape)
    if rider is None:
        res = pl.pallas_call(
            body, grid=grid, name=name, out_shape=tuple(out_shape), in_specs=list(in_specs),
            out_specs=tuple(out_specs), scratch_shapes=list(scratch), compiler_params=_cparams(sem))(*args)
        return tuple(res), ()
    r_in, r_out = len(rider.ins), len(rider.outs)

    def wrapped(*refs):
        a, ri = refs[:n_in], refs[n_in:n_in + r_in]
        o = refs[n_in + r_in:n_in + r_in + n_out]
        ro = refs[n_in + r_in + n_out:n_in + r_in + n_out + r_out]
        s, (send, recv) = refs[n_in + r_in + n_out + r_out:-2], refs[-2:]
        first = functools.reduce(jnp.logical_and, [pl.program_id(d) == 0 for d in range(len(grid))])
        last = functools.reduce(jnp.logical_and, [pl.program_id(d) == g - 1 for d, g in enumerate(grid)])

        @pl.when(first)
        def _():
            rider.start(ri, ro, send, recv)

        body(*a, *o, *s)

        if rider.relay is not None:
            step = functools.reduce(lambda acc, d: acc * grid[d] + pl.program_id(d), range(len(grid)), 0)

            @pl.when(step == max((3 * math.prod(grid)) // 4 - 1, 0))
            def _():
                rider.relay(ri, ro, send, recv)

        @pl.when(last)
        def _():
            rider.finish(ri, ro, send, recv)

    res = pl.pallas_call(
        wrapped, grid=grid, name=name, out_shape=tuple(out_shape) + tuple(rider.outs),
        in_specs=list(in_specs) + [HBM_SPEC] * r_in, out_specs=tuple(out_specs) + (HBM_SPEC,) * r_out,
        scratch_shapes=list(scratch) + [pltpu.SemaphoreType.DMA((rider.nsem,)),
                                        pltpu.SemaphoreType.DMA((rider.nsem,))],
        input_output_aliases={n_in + i: n_out + j for i, j in rider.aliases.items()},
        compiler_params=_cparams(sem, has_side_effects=True))(*args, *rider.ins)
    return tuple(res[:n_out]), tuple(res[n_out:])


def run_rider(rider, name):
    r_in = len(rider.ins)

    def body(*refs):
        ri, ro, (send, recv) = refs[:r_in], refs[r_in:-2], refs[-2:]
        rider.start(ri, ro, send, recv)
        if rider.relay is not None:
            rider.relay(ri, ro, send, recv)
        rider.finish(ri, ro, send, recv)

    return pl.pallas_call(
        body, name=name, out_shape=tuple(rider.outs), in_specs=[HBM_SPEC] * r_in,
        out_specs=(HBM_SPEC,) * len(rider.outs),
        scratch_shapes=[pltpu.SemaphoreType.DMA((rider.nsem,)), pltpu.SemaphoreType.DMA((rider.nsem,))],
        input_output_aliases=rider.aliases,
        compiler_params=pltpu.CompilerParams(has_side_effects=True))(*rider.ins)


FFN_BWD_TILE = 256


def _ffn_apply(x, g_ref, wgu_ref, wd_ref, gu_ref):
    xh, _ = _rms(x)
    nb = (xh * g_ref[...]).astype(BF)
    acc = jnp.zeros(x.shape, F32)
    for j in range(2):
        g = _dot(nb, wgu_ref[j])
        u = _dot(nb, wgu_ref[2 + j])
        gu_ref[:, j * HC:(j + 1) * HC] = g.astype(BF)
        gu_ref[:, FF + j * HC:FF + (j + 1) * HC] = u.astype(BF)
        a = (g * _sigmoid(g) * u).astype(BF)
        acc = acc + _dot(a, wd_ref[j * HC:(j + 1) * HC, :])
    return x + 0.5 * acc


def ffn_hidden(h, gain, wgu, rider=None):
    s = h.shape[0]
    ts = _tile(s, 512)

    def body(h_ref, g_ref, wgu_ref, gu_ref, a_ref):
        xh, _ = _rms(h_ref[...])
        nb = (xh * g_ref[...]).astype(BF)
        for j in range(2):
            g = _dot(nb, wgu_ref[j])
            u = _dot(nb, wgu_ref[2 + j])
            gu_ref[:, j * HC:(j + 1) * HC] = g.astype(BF)
            gu_ref[:, FF + j * HC:FF + (j + 1) * HC] = u.astype(BF)
            a_ref[:, j * HC:(j + 1) * HC] = (g * _sigmoid(g) * u).astype(BF)

    return _pcall(
        body, grid=(s // ts,), name="ffn1_hidden", args=(h, gain, wgu),
        out_shape=[jax.ShapeDtypeStruct((s, 2 * FF), BF), jax.ShapeDtypeStruct((s, FF), BF)],
        in_specs=[_row_spec(ts, D), _const_spec((1, D)), _const_spec((NSH, D, HC))],
        out_specs=[_row_spec(ts, 2 * FF), _row_spec(ts, FF)], rider=rider)


def ffn_down(h, a, wd):
    s = h.shape[0]
    ts = _tile(s, 512)

    def body(h_ref, a_ref, wd_ref, o_ref):
        o_ref[...] = h_ref[...] + 0.5 * _dot(a_ref[...], wd_ref[...])

    return pl.pallas_call(
        body, grid=(s // ts,), name="ffn1_down", out_shape=jax.ShapeDtypeStruct((s, D), F32),
        in_specs=[_row_spec(ts, D), _row_spec(ts, FF), _const_spec((FF, D))], out_specs=_row_spec(ts, D),
        compiler_params=_cparams(("arbitrary",)))(h, a, wd)


def ffn_fwd_loss(h, gain, wgu, wd, gfin, target):
    s = h.shape[0]
    ts = _tile(s, 512)

    def body(h_ref, g_ref, wgu_ref, wd_ref, gf_ref, t_ref, dh_ref, gu_ref, loss_ref, dgf_ref):
        @pl.when(pl.program_id(0) == 0)
        def _():
            loss_ref[...] = jnp.zeros_like(loss_ref)
            dgf_ref[...] = jnp.zeros_like(dgf_ref)

        h4 = _ffn_apply(h_ref[...], g_ref, wgu_ref, wd_ref, gu_ref)
        yh, r4 = _rms(h4)
        gf = gf_ref[...]
        e = yh * gf - t_ref[...]
        loss_ref[...] += _colsum(e * e)
        dy = e * (1.0 / D)
        dgf_ref[...] += _colsum(dy * yh)
        dh_ref[...] = _rms_bwd(dy, yh, r4, gf)

    return pl.pallas_call(
        body, grid=(s // ts,), name="ffn_fwd_loss",
        out_shape=(jax.ShapeDtypeStruct((s, D), F32), jax.ShapeDtypeStruct((s, 2 * FF), BF),
                   jax.ShapeDtypeStruct((1, D), F32), jax.ShapeDtypeStruct((1, D), F32)),
        in_specs=[_row_spec(ts, D), _const_spec((1, D)), _const_spec((NSH, D, HC)), _const_spec((FF, D)),
                  _const_spec((1, D)), _row_spec(ts, D)],
        out_specs=(_row_spec(ts, D), _row_spec(ts, 2 * FF), _acc_spec((1, D)), _acc_spec((1, D))),
        compiler_params=_cparams(("arbitrary",)),
    )(h, gain, wgu, wd, gfin, target)


def ffn_bwd(h, gu, dh, gain, wgu, wd, name):
    s = h.shape[0]
    ts = _tile(s, FFN_BWD_TILE)

    def body(h_ref, gu_ref, dh_ref, g_ref, wgu_ref, wd_ref, dx_ref, n_ref, a_ref, dgu_ref, dhb_ref, dg_ref):
        @pl.when(pl.program_id(0) == 0)
        def _():
            dg_ref[...] = jnp.zeros_like(dg_ref)

        x = h_ref[...]
        dh = dh_ref[...]
        gain_v = g_ref[...]
        xh, r = _rms(x)
        n_ref[...] = (xh * gain_v).astype(BF)
        dhb = (0.5 * dh).astype(BF)
        dhb_ref[...] = dhb
        dn = jnp.zeros((ts, D), F32)
        for j in range(2):
            g = gu_ref[:, j * HC:(j + 1) * HC].astype(F32)
            u = gu_ref[:, FF + j * HC:FF + (j + 1) * HC].astype(F32)
            sg = _sigmoid(g)
            sl = g * sg
            a_ref[:, j * HC:(j + 1) * HC] = (sl * u).astype(BF)
            da = _dot_nt(dhb, wd_ref[j * HC:(j + 1) * HC, :])
            dgb = (da * u * (sg * (1.0 + g * (1.0 - sg)))).astype(BF)
            dub = (da * sl).astype(BF)
            dgu_ref[:, j * HC:(j + 1) * HC] = dgb
            dgu_ref[:, FF + j * HC:FF + (j + 1) * HC] = dub
            dn = dn + _dot_nt(dgb, wgu_ref[j]) + _dot_nt(dub, wgu_ref[2 + j])
        dg_ref[...] += _colsum(dn * xh)
        dx_ref[...] = dh + _rms_bwd(dn, xh, r, gain_v)

    return pl.pallas_call(
        body, grid=(s // ts,), name=name,
        out_shape=(jax.ShapeDtypeStruct((s, D), F32), jax.ShapeDtypeStruct((s, D), BF),
                   jax.ShapeDtypeStruct((s, FF), BF), jax.ShapeDtypeStruct((s, 2 * FF), BF),
                   jax.ShapeDtypeStruct((s, D), BF), jax.ShapeDtypeStruct((1, D), F32)),
        in_specs=[_row_spec(ts, D), _row_spec(ts, 2 * FF), _row_spec(ts, D), _const_spec((1, D)),
                  _const_spec((NSH, D, HC)), _const_spec((FF, D))],
        out_specs=(_row_spec(ts, D), _row_spec(ts, D), _row_spec(ts, FF), _row_spec(ts, 2 * FF),
                   _row_spec(ts, D), _acc_spec((1, D))),
        compiler_params=_cparams(("arbitrary",)),
    )(h, gu, dh, gain, wgu, wd)


def dw_matmul(x, dy, nsplit, name, rider=None):
    s, k = x.shape
    n = dy.shape[1]
    nb = n // nsplit
    ts = _tile(s, 2048 if k <= D else 1024)
    nsteps = s // ts

    def body(x_ref, dy_ref, o_ref, ob_ref):
        @pl.when(pl.program_id(1) == 0)
        def _():
            o_ref[...] = jnp.zeros_like(o_ref)

        o_ref[0] += _dot_tn(x_ref[...], dy_ref[...])

        @pl.when(pl.program_id(1) == nsteps - 1)
        def _():
            ob_ref[...] = o_ref[...].astype(BF)

    spec = pl.BlockSpec((1, k, nb), lambda j, i: (j, 0, 0))
    (out, outb), rode = _pcall(
        body, grid=(nsplit, nsteps), name=name, args=(x, dy),
        out_shape=[jax.ShapeDtypeStruct((nsplit, k, nb), F32), jax.ShapeDtypeStruct((nsplit, k, nb), BF)],
        in_specs=[pl.BlockSpec((ts, k), lambda j, i: (i, 0)), pl.BlockSpec((ts, nb), lambda j, i: (i, j))],
        out_specs=[spec, spec], rider=rider)
    return out, outb, rode


def _split_in_proj(p, b):
    h = INB - D
    a_val = p[0][:, :D] + b[:, 0:D]
    a_gate = jnp.concatenate([p[0][:, D:], p[1][:, :h]], axis=1) + b[:, D:2 * D]
    b_u = p[1][:, h:] + b[:, 2 * D:3 * D]
    b_v = p[2][:, :D] + b[:, 3 * D:4 * D]
    g_a = jnp.concatenate([p[2][:, D:], p[3][:, :h]], axis=1) + b[:, 4 * D:5 * D]
    g_b = p[3][:, h:] + b[:, 5 * D:6 * D]
    return a_val, a_gate, b_u, b_v, g_a, g_b


def _sgu_mix(vnb, ws_ref, sb_ref, mixed_ref, ts):
    for ci in range(ts // CHUNK):
        rows = slice(ci * CHUNK, (ci + 1) * CHUNK)
        for g in range(NG):
            cols = slice(g * GD, (g + 1) * GD)
            mixed_ref[rows, cols] = _dot(ws_ref[g], vnb[rows, cols]) + sb_ref[:, cols]


SUB = 8
CB = 128
SH_ROWS_EXTRA = HALO - SUB


def _shifted_copies(ext_ref, sh_ref, lanes, ts):
    for b in range(1, SUB):
        sh_ref[b - 1] = ext_ref[b:b + ts + SH_ROWS_EXTRA, lanes]


def _window(ext_ref, sh_ref, lanes, first, r0, nrows):
    b = first % SUB
    a = first - b
    if b == 0:
        return ext_ref[a + r0:a + r0 + nrows, lanes]
    return sh_ref[b - 1, a + r0:a + r0 + nrows, :]


def mix_fwd(h, gain, win, b_in, conv_w, conv_b, lna_g, lna_b, wa, lnb_g, lnb_b, ws, sbias, wb, wo, rider=None):
    s = h.shape[0]
    ts = _tile(s, 256)

    def body(h_ref, g_ref, win_ref, bin_ref, cw_ref, cb_ref, lag_ref, lab_ref, wa_ref, lbg_ref, lbb_ref,
             ws_ref, sb_ref, wb_ref, wo_ref, o_ref, p_ref, n_ref, c_ref, ext_ref, mixed_ref, sh_ref):
        @pl.when(pl.program_id(0) == 0)
        def _():
            ext_ref[0:HALO, :] = jnp.zeros((HALO, D), F32)

        x = h_ref[...]
        xh, _ = _rms(x)
        nb = (xh * g_ref[...]).astype(BF)
        n_ref[...] = nb
        b = bin_ref[...]
        p = []
        for k in range(NSH):
            pk = _dot(nb, win_ref[k])
            p_ref[:, k * INB:(k + 1) * INB] = (pk + b[:, k * INB:(k + 1) * INB]).astype(BF)
            p.append(pk)
        a_val, a_gate, b_u, b_v, g_a, g_b = _split_in_proj(p, b)
        ext_ref[HALO:HALO + ts, :] = a_val * _sigmoid(a_gate)
        for l0 in range(0, D, CB):
            lanes = slice(l0, l0 + CB)
            _shifted_copies(ext_ref, sh_ref, lanes, ts)
            for r0 in range(0, ts, CB):
                acc = jnp.zeros((CB, CB), F32) + cb_ref[:, lanes]
                for k in range(CW):
                    acc = acc + cw_ref[k:k + 1, lanes] * _window(ext_ref, sh_ref, lanes,
                                                                 HALO - (CW - 1) + k, r0, CB)
                c_ref[r0:r0 + CB, lanes] = acc
        ext_ref[0:HALO, :] = ext_ref[ts:ts + HALO, :]
        ch, _ = _ln(c_ref[...])
        la = ch * lag_ref[...] + lab_ref[...]
        sa = (la * _sigmoid(la)).astype(BF)
        ya = _dot(sa, wa_ref[...])
        ub, _ = _gelu(b_u)
        gv, _ = _gelu(b_v)
        vh, _ = _ln(gv)
        vnb = (vh * lbg_ref[...] + lbb_ref[...]).astype(BF)
        _sgu_mix(vnb, ws_ref, sb_ref, mixed_ref, ts)
        ob = (ub * mixed_ref[...]).astype(BF)
        yb = _dot(ob, wb_ref[...])
        merged = (_sigmoid(g_a) * ya + _sigmoid(g_b) * yb).astype(BF)
        o_ref[...] = x + _dot(merged, wo_ref[...])

    vec = _const_spec((1, D))
    sq = _const_spec((D, D))
    return _pcall(
        body, grid=(s // ts,), name="mix_fwd",
        args=(h, gain, win, b_in, conv_w, conv_b, lna_g, lna_b, wa, lnb_g, lnb_b, ws, sbias, wb, wo),
        out_shape=(jax.ShapeDtypeStruct((s, D), F32), jax.ShapeDtypeStruct((s, DIN), BF),
                   jax.ShapeDtypeStruct((s, D), BF), jax.ShapeDtypeStruct((s, D), F32)),
        in_specs=[_row_spec(ts, D), vec, _const_spec((NSH, D, INB)), _const_spec((1, DIN)),
                  _const_spec((HALO, D)), vec, vec, vec, sq, vec, vec,
                  _const_spec((NG, CHUNK, CHUNK)), _const_spec((CHUNK, D)), sq, sq],
        out_specs=(_row_spec(ts, D), _row_spec(ts, DIN), _row_spec(ts, D), _row_spec(ts, D)),
        scratch=[pltpu.VMEM((ts + HALO, D), F32), pltpu.VMEM((ts, D), F32),
                 pltpu.VMEM((SUB - 1, ts + SH_ROWS_EXTRA, CB), F32)], rider=rider)


def mix_bwd_branches(p, c, dh, lna_g, lna_b, wa, lnb_g, lnb_b, ws, wst, sbias, wb, wo, rider=None):
    s = dh.shape[0]
    ts = _tile(s, 256)
    nsteps = s // ts

    def body(p_ref, c_ref, dh_ref, lag_ref, lab_ref, wa_ref, lbg_ref, lbb_ref, ws_ref, wst_ref, sb_ref,
             wb_ref, wo_ref, dc_ref, dp_ref, sa_ref, dya_ref, ob_ref, dyb_ref, mg_ref, dhb_ref,
             dws_ref, dsb_ref, dlag_ref, dlab_ref, dlbg_ref, dlbb_ref, mixed_ref, dmix_ref, dvn_ref, dsb_acc):
        step = pl.program_id(0)

        @pl.when(step == 0)
        def _():
            for ref in (dws_ref, dsb_acc, dlag_ref, dlab_ref, dlbg_ref, dlbb_ref):
                ref[...] = jnp.zeros_like(ref)

        b_u = p_ref[:, 2 * D:3 * D].astype(F32)
        b_v = p_ref[:, 3 * D:4 * D].astype(F32)
        sga = _sigmoid(p_ref[:, 4 * D:5 * D].astype(F32))
        sgb = _sigmoid(p_ref[:, 5 * D:6 * D].astype(F32))
        lag = lag_ref[...]
        ch, ra = _ln(c_ref[...])
        la = ch * lag + lab_ref[...]
        sla = _sigmoid(la)
        sa = (la * sla).astype(BF)
        sa_ref[...] = sa
        ya = _dot(sa, wa_ref[...])
        lbg = lbg_ref[...]
        ub, dub = _gelu_with_grad(b_u)
        gv, dgv = _gelu_with_grad(b_v)
        vh, rb = _ln(gv)
        vnb = (vh * lbg + lbb_ref[...]).astype(BF)
        _sgu_mix(vnb, ws_ref, sb_ref, mixed_ref, ts)
        mixed = mixed_ref[...]
        ob = (ub * mixed).astype(BF)
        ob_ref[...] = ob
        yb = _dot(ob, wb_ref[...])
        mg_ref[...] = (sga * ya + sgb * yb).astype(BF)
        dhb = dh_ref[...].astype(BF)
        dhb_ref[...] = dhb
        dm = _dot_nt(dhb, wo_ref[...])
        dp_ref[:, 0:2 * D] = jnp.zeros((ts, 2 * D), BF)
        dp_ref[:, 4 * D:5 * D] = (dm * ya * sga * (1.0 - sga)).astype(BF)
        dp_ref[:, 5 * D:6 * D] = (dm * yb * sgb * (1.0 - sgb)).astype(BF)
        dya = (dm * sga).astype(BF)
        dya_ref[...] = dya
        dyb = (dm * sgb).astype(BF)
        dyb_ref[...] = dyb
        dla = _dot_nt(dya, wa_ref[...]) * (sla * (1.0 + la * (1.0 - sla)))
        dlag_ref[...] += _colsum(dla * ch)
        dlab_ref[...] += _colsum(dla)
        dc_ref[...] = _ln_bwd(dla, ch, ra, lag)
        dob = _dot_nt(dyb, wb_ref[...])
        dp_ref[:, 2 * D:3 * D] = (dob * mixed * dub).astype(BF)
        dmix = dob * ub
        dmix_ref[...] = dmix.astype(BF)
        dsb = jnp.zeros((CHUNK, D), F32)
        for ci in range(ts // CHUNK):
            rows = slice(ci * CHUNK, (ci + 1) * CHUNK)
            dsb = dsb + dmix[rows, :]
            for g in range(NG):
                cols = slice(g * GD, (g + 1) * GD)
                dmb = dmix_ref[rows, cols]
                dws_ref[g] += _dot_nt(dmb, vnb[rows, cols])
                dvn_ref[rows, cols] = _dot(wst_ref[g], dmb)
        dsb_acc[...] += dsb
        dvn = dvn_ref[...]
        dlbg_ref[...] += _colsum(dvn * vh)
        dlbb_ref[...] += _colsum(dvn)
        dp_ref[:, 3 * D:4 * D] = (_ln_bwd(dvn, vh, rb, lbg) * dgv).astype(BF)

        @pl.when(step == nsteps - 1)
        def _():
            row = lax.broadcasted_iota(jnp.int32, (CHUNK, CHUNK), 0)
            col = lax.broadcasted_iota(jnp.int32, (CHUNK, CHUNK), 1)
            for g in range(NG):
                dws_ref[g] = jnp.where(col <= row, dws_ref[g], 0.0)
            acc = jnp.zeros((CHUNK, CHUNK), F32)
            for g in range(NG):
                tot = jnp.sum(dsb_acc[:, g * GD:(g + 1) * GD], axis=-1, keepdims=True)
                acc = acc + jnp.where(col == g, tot, 0.0)
            dsb_ref[...] = acc

    vec = _const_spec((1, D))
    sq = _const_spec((D, D))
    bf_rows = jax.ShapeDtypeStruct((s, D), BF)
    acc_vec = jax.ShapeDtypeStruct((1, D), F32)
    return _pcall(
        body, grid=(nsteps,), name="mix_bwd_branches",
        args=(p, c, dh, lna_g, lna_b, wa, lnb_g, lnb_b, ws, wst, sbias, wb, wo),
        out_shape=(jax.ShapeDtypeStruct((s, D), F32), jax.ShapeDtypeStruct((s, DIN), BF),
                   bf_rows, bf_rows, bf_rows, bf_rows, bf_rows, bf_rows,
                   jax.ShapeDtypeStruct((NG, CHUNK, CHUNK), F32), jax.ShapeDtypeStruct((CHUNK, CHUNK), F32),
                   acc_vec, acc_vec, acc_vec, acc_vec),
        in_specs=[_row_spec(ts, DIN), _row_spec(ts, D), _row_spec(ts, D), vec, vec, sq, vec, vec,
                  _const_spec((NG, CHUNK, CHUNK)), _const_spec((NG, CHUNK, CHUNK)), _const_spec((CHUNK, D)),
                  sq, sq],
        out_specs=(_row_spec(ts, D), _row_spec(ts, DIN)) + (_row_spec(ts, D),) * 6
        + (_acc_spec((NG, CHUNK, CHUNK)), _acc_spec((CHUNK, CHUNK))) + (_acc_spec((1, D)),) * 4,
        scratch=[pltpu.VMEM((ts, D), F32), pltpu.VMEM((ts, D), BF), pltpu.VMEM((ts, D), F32),
                 pltpu.VMEM((CHUNK, D), F32)], rider=rider)


def conv_bwd(p, dc, dp, conv_w):
    s = dc.shape[0]
    ts = _tile(s, 256)
    nsteps = s // ts
    per = ts // HALO

    rb = 16

    def body(pm_ref, pp_ref, dcm_ref, dcn_ref, cw_ref, dpin_ref, dp_ref, dw_ref, db_ref, ext_ref, dext_ref,
             dw8_ref, sh_ref, dsh_ref, dglu_ref):
        del dpin_ref
        step = pl.program_id(0)

        @pl.when(step == 0)
        def _():
            dw8_ref[...] = jnp.zeros_like(dw8_ref)
            db_ref[...] = jnp.zeros_like(db_ref)

        a_val = pm_ref[:, 0:D].astype(F32)
        sg = _sigmoid(pm_ref[:, D:2 * D].astype(F32))
        prev = pp_ref[:, 0:D].astype(F32) * _sigmoid(pp_ref[:, D:2 * D].astype(F32))
        ext_ref[0:HALO, :] = jnp.where(step > 0, prev, 0.0)
        ext_ref[HALO:HALO + ts, :] = a_val * sg
        dcm = dcm_ref[...]
        dext_ref[0:ts, :] = dcm
        dext_ref[ts:ts + HALO, :] = jnp.where(step < nsteps - 1, dcn_ref[...], 0.0)
        db_ref[...] += _colsum(dcm)
        for l0 in range(0, D, CB):
            lanes = slice(l0, l0 + CB)
            _shifted_copies(dext_ref, dsh_ref, lanes, ts)
            for r0 in range(0, ts, CB):
                acc = jnp.zeros((CB, CB), F32)
                for k in range(CW):
                    acc = acc + cw_ref[k:k + 1, lanes] * _window(dext_ref, dsh_ref, lanes, CW - 1 - k, r0, CB)
                dglu_ref[r0:r0 + CB, lanes] = acc
            _shifted_copies(ext_ref, sh_ref, lanes, ts)
            accs = [jnp.zeros((SUB, CB), F32) for _ in range(CW)]
            for r0 in range(0, ts, rb):
                dcb = dext_ref[r0:r0 + rb, lanes]
                for k in range(CW):
                    prod = dcb * _window(ext_ref, sh_ref, lanes, HALO - (CW - 1) + k, r0, rb)
                    accs[k] = accs[k] + jnp.sum(prod.reshape(rb // SUB, SUB, CB), axis=0)
            for k in range(CW):
                dw8_ref[k, :, lanes] += accs[k]
        dglu = dglu_ref[...]
        dp_ref[:, 0:D] = (dglu * sg).astype(BF)
        dp_ref[:, D:2 * D] = (dglu * a_val * sg * (1.0 - sg)).astype(BF)

        @pl.when(step == nsteps - 1)
        def _():
            dw_ref[...] = jnp.zeros_like(dw_ref)
            for k in range(CW):
                dw_ref[k:k + 1, :] = _colsum(dw8_ref[k])

    return pl.pallas_call(
        body, grid=(nsteps,), name="conv_bwd",
        out_shape=(jax.ShapeDtypeStruct((s, DIN), BF), jax.ShapeDtypeStruct((HALO, D), F32),
                   jax.ShapeDtypeStruct((1, D), F32)),
        in_specs=[pl.BlockSpec((ts, 2 * D), lambda i: (i, 0)),
                  pl.BlockSpec((HALO, 2 * D), lambda i: (jnp.maximum(i * per - 1, 0), 0)),
                  _row_spec(ts, D),
                  pl.BlockSpec((HALO, D), lambda i: (jnp.minimum((i + 1) * per, s // HALO - 1), 0)),
                  _const_spec((HALO, D)),
                  pl.BlockSpec(memory_space=pl.ANY)],
        out_specs=(pl.BlockSpec((ts, 2 * D), lambda i: (i, 0)), _acc_spec((HALO, D)), _acc_spec((1, D))),
        scratch_shapes=[pltpu.VMEM((ts + HALO, D), F32), pltpu.VMEM((ts + HALO, D), F32),
                        pltpu.VMEM((HALO, SUB, D), F32),
                        pltpu.VMEM((SUB - 1, ts + SH_ROWS_EXTRA, CB), F32),
                        pltpu.VMEM((SUB - 1, ts + SH_ROWS_EXTRA, CB), F32),
                        pltpu.VMEM((ts, D), F32)],
        input_output_aliases={5: 0},
        compiler_params=_cparams(("arbitrary",)),
    )(p, p, dc, dc, conv_w, dp)


def mix_bwd_in(dp, h, dh, gain, win):
    s = h.shape[0]
    ts = _tile(s, 512)

    def body(dp_ref, h_ref, dh_ref, g_ref, win_ref, dx_ref, dg_ref, db_ref):
        @pl.when(pl.program_id(0) == 0)
        def _():
            dg_ref[...] = jnp.zeros_like(dg_ref)
            db_ref[...] = jnp.zeros_like(db_ref)

        gain_v = g_ref[...]
        xh, r = _rms(h_ref[...])
        dn = jnp.zeros((ts, D), F32)
        for k in range(NSH):
            dpk = dp_ref[:, k * INB:(k + 1) * INB]
            dn = dn + _dot_nt(dpk, win_ref[k])
            db_ref[:, k * INB:(k + 1) * INB] += _colsum(dpk.astype(F32))
        dg_ref[...] += _colsum(dn * xh)
        dx_ref[...] = dh_ref[...] + _rms_bwd(dn, xh, r, gain_v)

    return pl.pallas_call(
        body, grid=(s // ts,), name="mix_bwd_in",
        out_shape=(jax.ShapeDtypeStruct((s, D), F32), jax.ShapeDtypeStruct((1, D), F32),
                   jax.ShapeDtypeStruct((1, DIN), F32)),
        in_specs=[_row_spec(ts, DIN), _row_spec(ts, D), _row_spec(ts, D), _const_spec((1, D)),
                  _const_spec((NSH, D, INB))],
        out_specs=(_row_spec(ts, D), _acc_spec((1, D)), _acc_spec((1, DIN))),
        compiler_params=_cparams(("arbitrary",)),
    )(dp, h, dh, gain, win)


def kv_proj(mem, gain, wkv):
    def body(m_ref, g_ref, w_ref, k_ref, v_ref, n_ref):
        xh, _ = _rms(m_ref[...])
        nb = (xh * g_ref[...]).astype(BF)
        n_ref[...] = nb
        half = D // 2
        for j in range(2):
            k_ref[:, j * half:(j + 1) * half] = _dot(nb, w_ref[j]).astype(BF)
            v_ref[:, j * half:(j + 1) * half] = _dot(nb, w_ref[2 + j]).astype(BF)

    o = jax.ShapeDtypeStruct((NMEM, D), BF)
    return pl.pallas_call(body, name="kv_proj", out_shape=(o, o, o), compiler_params=_cparams())(mem, gain, wkv)


def kv_bwd(mem, gain, memn, wkv, dk, dv):
    def body(m_ref, g_ref, n_ref, w_ref, dk_ref, dv_ref, dw_ref, dwb_ref, dg_ref):
        xh, _ = _rms(m_ref[...])
        nb = n_ref[...]
        half = D // 2
        dn = jnp.zeros((NMEM, D), F32)
        for j in range(2):
            dkb = dk_ref[:, j * half:(j + 1) * half].astype(BF)
            dvb = dv_ref[:, j * half:(j + 1) * half].astype(BF)
            for slot, dyb in ((j, dkb), (2 + j, dvb)):
                dw = _dot_tn(nb, dyb)
                dw_ref[slot] = dw
                dwb_ref[slot] = dw.astype(BF)
            dn = dn + _dot_nt(dkb, w_ref[j]) + _dot_nt(dvb, w_ref[2 + j])
        dg_ref[...] = _colsum(dn * xh)

    return pl.pallas_call(
        body, name="kv_bwd",
        out_shape=(jax.ShapeDtypeStruct((NSH, D, D // 2), F32), jax.ShapeDtypeStruct((NSH, D, D // 2), BF),
                   jax.ShapeDtypeStruct((1, D), F32)),
        compiler_params=_cparams())(mem, gain, memn, wkv, dk, dv)


def _attend(qb, k_ref, v_ref, h):
    cols = slice(h * HD, (h + 1) * HD)
    sc = _dot_nt(qb[:, cols], k_ref[:, cols]) * ATT_SCALE
    e = jnp.exp(sc - jnp.max(sc, axis=-1, keepdims=True))
    pr = e / jnp.sum(e, axis=-1, keepdims=True)
    return pr, _dot(pr.astype(BF), v_ref[:, cols])


def xattn_fwd(h, gain, wq, k, v, wo):
    s = h.shape[0]
    ts = _tile(s, 1024)

    def body(h_ref, g_ref, wq_ref, k_ref, v_ref, wo_ref, o_ref, att_ref):
        x = h_ref[...]
        xh, _ = _rms(x)
        nb = (xh * g_ref[...]).astype(BF)
        qb = _dot(nb, wq_ref[...]).astype(BF)
        for hd in range(NH):
            _, oh = _attend(qb, k_ref, v_ref, hd)
            att_ref[:, hd * HD:(hd + 1) * HD] = oh.astype(BF)
        o_ref[...] = x + _dot(att_ref[...], wo_ref[...])

    sq = _const_spec((D, D))
    kvs = _const_spec((NMEM, D))
    return pl.pallas_call(
        body, grid=(s // ts,), name="xattn_fwd",
        out_shape=jax.ShapeDtypeStruct((s, D), F32),
        in_specs=[_row_spec(ts, D), _const_spec((1, D)), sq, kvs, kvs, sq],
        out_specs=_row_spec(ts, D),
        scratch_shapes=[pltpu.VMEM((ts, D), BF)],
        compiler_params=_cparams(("arbitrary",)),
    )(h, gain, wq, k, v, wo)


def xattn_bwd(h, dh, gain, wq, k, v, wo, rider=None):
    s = h.shape[0]
    ts = _tile(s, 512)

    def body(h_ref, dh_ref, g_ref, wq_ref, k_ref, v_ref, wo_ref,
             dx_ref, n_ref, dq_ref, att_ref, dhb_ref, dk_ref, dv_ref, dg_ref):
        @pl.when(pl.program_id(0) == 0)
        def _():
            for ref in (dk_ref, dv_ref, dg_ref):
                ref[...] = jnp.zeros_like(ref)

        x = h_ref[...]
        dh = dh_ref[...]
        gain_v = g_ref[...]
        xh, r = _rms(x)
        nb = (xh * gain_v).astype(BF)
        n_ref[...] = nb
        qb = _dot(nb, wq_ref[...]).astype(BF)
        dhb = dh.astype(BF)
        dhb_ref[...] = dhb
        dob = _dot_nt(dhb, wo_ref[...]).astype(BF)
        for hd in range(NH):
            cols = slice(hd * HD, (hd + 1) * HD)
            pr, oh = _attend(qb, k_ref, v_ref, hd)
            att_ref[:, cols] = oh.astype(BF)
            doh = dob[:, cols]
            dpr = _dot_nt(doh, v_ref[:, cols])
            dv_ref[:, cols] += _dot_tn(pr.astype(BF), doh)
            dsc = (pr * (dpr - jnp.sum(dpr * pr, axis=-1, keepdims=True)) * ATT_SCALE).astype(BF)
            dq_ref[:, cols] = _dot(dsc, k_ref[:, cols]).astype(BF)
            dk_ref[:, cols] += _dot_tn(dsc, qb[:, cols])
        dn = _dot_nt(dq_ref[...], wq_ref[...])
        dg_ref[...] += _colsum(dn * xh)
        dx_ref[...] = dh + _rms_bwd(dn, xh, r, gain_v)

    sq = _const_spec((D, D))
    kvs = _const_spec((NMEM, D))
    bf_rows = jax.ShapeDtypeStruct((s, D), BF)
    kv_acc = jax.ShapeDtypeStruct((NMEM, D), F32)
    return _pcall(
        body, grid=(s // ts,), name="xattn_bwd", args=(h, dh, gain, wq, k, v, wo),
        out_shape=(jax.ShapeDtypeStruct((s, D), F32), bf_rows, bf_rows, bf_rows, bf_rows, kv_acc, kv_acc,
                   jax.ShapeDtypeStruct((1, D), F32)),
        in_specs=[_row_spec(ts, D), _row_spec(ts, D), _const_spec((1, D)), sq, kvs, kvs, sq],
        out_specs=(_row_spec(ts, D),) * 5 + (_acc_spec((NMEM, D)), _acc_spec((NMEM, D)), _acc_spec((1, D))),
        rider=rider)


BLOCK_BYTES = 3 << 19


def _row_block(rows, cols):
    rb = rows
    while rb * cols * 4 > BLOCK_BYTES and rb % 32 == 0:
        rb //= 2
    return rb


def cast_bf16(w, chip, name):
    r, c = w.shape
    rb = _row_block(r, c)

    def body(chip_ref, w_ref, o_ref):
        del chip_ref
        o_ref[0] = w_ref[...].astype(BF)

    return pl.pallas_call(
        body, name=name, out_shape=jax.ShapeDtypeStruct((NSH, r, c), BF),
        grid_spec=pltpu.PrefetchScalarGridSpec(
            num_scalar_prefetch=1, grid=(r // rb,),
            in_specs=[pl.BlockSpec((rb, c), lambda i, chip_ref: (i, 0))],
            out_specs=pl.BlockSpec((1, rb, c), lambda i, chip_ref: (chip_ref[0], i, 0))),
        compiler_params=_cparams(("arbitrary",)))(chip, w)


NDEV = 8


def device_sum(g4, recv, place, name):
    _, _, rh, c = g4.shape
    rb = _row_block(rh, c)

    def body(place_ref, g_ref, r_ref, o_ref):
        del place_ref
        acc = g_ref[0, 0]
        for j in range(NDEV - 1):
            acc = acc + r_ref[j].astype(F32)
        o_ref[0] = acc

    return pl.pallas_call(
        body, name=name, out_shape=jax.ShapeDtypeStruct((2, rh, c), F32),
        grid_spec=pltpu.PrefetchScalarGridSpec(
            num_scalar_prefetch=1, grid=(rh // rb,),
            in_specs=[pl.BlockSpec((1, 1, rb, c), lambda i, place_ref: (place_ref[0], place_ref[1], i, 0)),
                      pl.BlockSpec((NDEV - 1, rb, c), lambda i, place_ref: (0, i, 0))],
            out_specs=pl.BlockSpec((1, rb, c), lambda i, place_ref: (place_ref[1], i, 0))),
        compiler_params=_cparams(("arbitrary",)))(place, g4, recv)


def _adamw_math(w, g, m, v):
    m = ADAM_B1 * m + (1.0 - ADAM_B1) * g
    v = ADAM_B2 * v + (1.0 - ADAM_B2) * (g * g)
    m_hat = m / (1.0 - ADAM_B1 ** ADAM_STEP)
    v_hat = v / (1.0 - ADAM_B2 ** ADAM_STEP)
    delta = -ADAM_LR * (m_hat / (jnp.sqrt(v_hat) + ADAM_EPS) + ADAM_WD * w)
    return delta, m, v


def adamw(w, g, m, v, name):
    r, c = w.shape
    rb = _row_block(r, c)

    def body(w_ref, g_ref, m_ref, v_ref, d_ref, mo_ref, vo_ref):
        d, mn, vn = _adamw_math(w_ref[...], g_ref[...], m_ref[...], v_ref[...])
        d_ref[...] = d
        mo_ref[...] = mn
        vo_ref[...] = vn

    o = jax.ShapeDtypeStruct((r, c), F32)
    spec = _row_spec(rb, c)
    return pl.pallas_call(
        body, grid=(r // rb,), name=name, out_shape=(o, o, o),
        in_specs=[spec] * 4, out_specs=(spec,) * 3,
        compiler_params=_cparams(("arbitrary",)))(w, g, m, v)


def _place():
    return lax.axis_index("x"), lax.axis_index("y"), lax.axis_index("c")


def _other_chips(x, y):
    return [(1 - x, y), (x, 1 - y), (1 - x, 1 - y)]


NOTHER = NSH - 1


def gather_rider(arrays):
    nw = len(arrays)
    nici = nw * NOTHER

    def copies(refs, send_sems, recv_sems):
        x, y, c = _place()
        ici, d2d = [], []
        for w in range(nw):
            for j, (px, py) in enumerate(_other_chips(x, y)):
                n = w * NOTHER + j
                sems = dict(send_sem=send_sems.at[n], recv_sem=recv_sems.at[n],
                            device_id=(px, py, c), device_id_type=MESH)
                mine = refs[w].at[2 * x + y, c]
                theirs = refs[w].at[2 * px + py, c]
                ici.append((pltpu.make_async_remote_copy(src_ref=mine, dst_ref=mine, **sems),
                            pltpu.make_async_remote_copy(src_ref=mine, dst_ref=theirs, **sems)))
                sems = dict(send_sem=send_sems.at[nici + n], recv_sem=recv_sems.at[nici + n],
                            device_id=(x, y, 1 - c), device_id_type=MESH)
                d2d.append((pltpu.make_async_remote_copy(src_ref=theirs, dst_ref=theirs, **sems),
                            pltpu.make_async_remote_copy(src_ref=theirs, dst_ref=refs[w].at[2 * px + py, 1 - c],
                                                         **sems)))
        return ici, d2d

    def start(ins, outs, send_sems, recv_sems):
        ici, _ = copies(outs, send_sems, recv_sems)
        for send, _ in ici:
            send.start()

    def relay(ins, outs, send_sems, recv_sems):
        ici, d2d = copies(outs, send_sems, recv_sems)
        for (_, landed), (forward, _) in zip(ici, d2d):
            landed.wait_recv()
            forward.start()

    def finish(ins, outs, send_sems, recv_sems):
        ici, d2d = copies(outs, send_sems, recv_sems)
        for _, landed in d2d:
            landed.wait_recv()
        for send, _ in ici + d2d:
            send.wait_send()

    return Rider(arrays, [jax.ShapeDtypeStruct(a.shape, a.dtype) for a in arrays], {i: i for i in range(nw)},
                 2 * nici, start, finish, relay)


def _peers(x, y, c):
    return [(x ^ (rel >> 2), y ^ ((rel >> 1) & 1), c ^ (rel & 1)) for rel in range(1, NDEV)]


def reduce_rider(grads):
    nw = len(grads)
    npeer = NDEV - 1

    def copies(ins, outs, send_sems, recv_sems):
        x, y, c = _place()
        return [pltpu.make_async_remote_copy(
            src_ref=ins[w].at[2 * px + py, pc], dst_ref=outs[w].at[r],
            send_sem=send_sems.at[w * npeer + r], recv_sem=recv_sems.at[w * npeer + r],
            device_id=(px, py, pc), device_id_type=MESH)
            for w in range(nw) for r, (px, py, pc) in enumerate(_peers(x, y, c))]

    def start(ins, outs, send_sems, recv_sems):
        for cp in copies(ins, outs, send_sems, recv_sems):
            cp.start()

    def finish(ins, outs, send_sems, recv_sems):
        for cp in copies(ins, outs, send_sems, recv_sems):
            cp.wait()

    return Rider(grads, [jax.ShapeDtypeStruct((npeer,) + g.shape[2:], g.dtype) for g in grads], {},
                 nw * npeer, start, finish)


class _Offset:
    def __init__(self, ref, base):
        self.ref, self.base = ref, base

    @property
    def at(self):
        return self

    def __getitem__(self, i):
        return self.ref.at[self.base + i]


def merge_riders(riders):
    ins, outs, aliases, spans, nsem = [], [], {}, [], 0
    for r in riders:
        spans.append((len(ins), len(outs), nsem))
        aliases.update({len(ins) + i: len(outs) + j for i, j in r.aliases.items()})
        ins, outs, nsem = ins + r.ins, outs + r.outs, nsem + r.nsem

    def each(step):
        def run(in_refs, out_refs, send_sems, recv_sems):
            for r, (i0, o0, s0) in zip(riders, spans):
                if getattr(r, step) is not None:
                    getattr(r, step)(in_refs[i0:i0 + len(r.ins)], out_refs[o0:o0 + len(r.outs)],
                                     _Offset(send_sems, s0), _Offset(recv_sems, s0))
        return run

    relay = each("relay") if any(r.relay is not None for r in riders) else None
    return Rider(ins, outs, aliases, nsem, each("start"), each("finish"), relay)


def split_results(riders, results):
    out, o0 = [], 0
    for r in riders:
        out.append(tuple(results[o0:o0 + len(r.outs)]))
        o0 += len(r.outs)
    return out


def swap_rider(halves):
    nw = len(halves)

    def copies(refs, send_sems, recv_sems):
        x, y, c = _place()
        out = []
        for w in range(nw):
            sems = dict(send_sem=send_sems.at[w], recv_sem=recv_sems.at[w],
                        device_id=(x, y, 1 - c), device_id_type=MESH)
            mine = refs[w].at[c]
            out.append((pltpu.make_async_remote_copy(src_ref=mine, dst_ref=mine, **sems),
                        pltpu.make_async_remote_copy(src_ref=mine, dst_ref=refs[w].at[1 - c], **sems)))
        return out

    def start(ins, outs, send_sems, recv_sems):
        for send, _ in copies(outs, send_sems, recv_sems):
            send.start()

    def finish(ins, outs, send_sems, recv_sems):
        cps = copies(outs, send_sems, recv_sems)
        for _, recv in cps:
            recv.wait_recv()
        for send, _ in cps:
            send.wait_send()

    return Rider(halves, [jax.ShapeDtypeStruct(h.shape, h.dtype) for h in halves], {i: i for i in range(nw)},
                 nw, start, finish)


def allgather_rider(slots):
    def copies(ref, send_sems, recv_sems):
        x, y, c = _place()
        mine = ref.at[4 * x + 2 * y + c]
        out = []
        for r, peer in enumerate(_peers(x, y, c)):
            sems = dict(send_sem=send_sems.at[r], recv_sem=recv_sems.at[r], device_id=peer, device_id_type=MESH)
            out.append((pltpu.make_async_remote_copy(src_ref=mine, dst_ref=mine, **sems),
                        pltpu.make_async_remote_copy(
                            src_ref=mine, dst_ref=ref.at[4 * peer[0] + 2 * peer[1] + peer[2]], **sems)))
        return out

    def start(ins, outs, send_sems, recv_sems):
        for send, _ in copies(outs[0], send_sems, recv_sems):
            send.start()

    def finish(ins, outs, send_sems, recv_sems):
        cps = copies(outs[0], send_sems, recv_sems)
        for _, recv in cps:
            recv.wait_recv()
        for send, _ in cps:
            send.wait_send()

    return Rider([slots], [jax.ShapeDtypeStruct(slots.shape, slots.dtype)], {0: 0}, NDEV - 1, start, finish)


def sum_slots(slots):
    def body(s_ref, o_ref):
        acc = s_ref[0]
        for dev in range(1, NDEV):
            acc = acc + s_ref[dev]
        o_ref[...] = acc

    return pl.pallas_call(body, name="sum_slots", out_shape=jax.ShapeDtypeStruct(slots.shape[1:], F32),
                          compiler_params=_cparams())(slots)


BIG = ("ffn1_w_gu", "ffn1_w_down", "w_in", "w_a_out", "w_b_out", "w_out", "w_q", "w_kv", "w_o",
       "ffn2_w_gu", "ffn2_w_down")
SMALL = {"ffn1_norm": (0, 1), "mix_norm": (8, 1), "xattn_norm": (16, 1), "mem_norm": (24, 1),
         "ffn2_norm": (32, 1), "final_norm": (40, 1), "conv_b": (48, 1), "conv_ln_g": (56, 1),
         "conv_ln_b": (64, 1), "sgu_ln_g": (72, 1), "sgu_ln_b": (80, 1), "b_in": (88, 6),
         "conv_w": (96, CW), "sgu_w": (128, 64), "sgu_b": (192, 1)}
LOSS_ROW = 200
SMALL_ROWS = 208


def _pad_rows(a, rows):
    return jnp.pad(a, ((0, rows - a.shape[0]), (0, D - a.shape[1])))


def _pack_small(parts):
    names = sorted(parts, key=lambda n: SMALL[n][0] if n in SMALL else LOSS_ROW)
    rows = []
    for i, n in enumerate(names):
        start = SMALL[n][0] if n in SMALL else LOSS_ROW
        end = SMALL_ROWS if i + 1 == len(names) else (SMALL[names[i + 1]][0] if names[i + 1] in SMALL else LOSS_ROW)
        rows.append(_pad_rows(parts[n], end - start))
    return jnp.concatenate(rows, axis=0)


def _small_views(w):
    return {
        "ffn1_norm": w["ffn1_norm"], "mix_norm": w["mix_norm"], "xattn_norm": w["xattn_norm"],
        "mem_norm": w["mem_norm"], "ffn2_norm": w["ffn2_norm"], "final_norm": w["final_norm"].reshape(1, D),
        "conv_b": w["conv_b"], "conv_ln_g": w["conv_ln_g"], "conv_ln_b": w["conv_ln_b"],
        "sgu_ln_g": w["sgu_ln_g"], "sgu_ln_b": w["sgu_ln_b"], "b_in": w["b_in"].reshape(6, D),
        "conv_w": w["conv_w"][0], "sgu_w": w["sgu_w"].reshape(64, D), "sgu_b": w["sgu_b"].reshape(1, NG * CHUNK),
    }


def _unpack_small(buf, like, chip):
    out = {}
    for n, (start, rows) in SMALL.items():
        blk = buf[start:start + rows]
        if n == "conv_w":
            blk = blk[:, :like[n].shape[-1]] if chip is None else lax.dynamic_slice_in_dim(
                blk, chip * like[n].shape[-1], like[n].shape[-1], axis=1)
        elif n == "sgu_b":
            blk = blk[:, :NG * CHUNK]
        out[n] = blk.reshape(like[n].shape)
    return out


def kernel(x, mem, ffn1_norm, ffn1_w_gu, ffn1_w_down, mix_norm, w_in, b_in, conv_w, conv_b, conv_ln_g, conv_ln_b, w_a_out, sgu_ln_g, sgu_ln_b, sgu_w, sgu_b, w_b_out, w_out, xattn_norm, mem_norm, w_q, w_kv, w_o, ffn2_norm, ffn2_w_gu, ffn2_w_down, final_norm, loss_target, m_ffn1_norm, m_ffn1_w_gu, m_ffn1_w_down, m_mix_norm, m_w_in, m_b_in, m_conv_w, m_conv_b, m_conv_ln_g, m_conv_ln_b, m_w_a_out, m_sgu_ln_g, m_sgu_ln_b, m_sgu_w, m_sgu_b, m_w_b_out, m_w_out, m_xattn_norm, m_mem_norm, m_w_q, m_w_kv, m_w_o, m_ffn2_norm, m_ffn2_w_gu, m_ffn2_w_down, m_final_norm, v_ffn1_norm, v_ffn1_w_gu, v_ffn1_w_down, v_mix_norm, v_w_in, v_b_in, v_conv_w, v_conv_b, v_conv_ln_g, v_conv_ln_b, v_w_a_out, v_sgu_ln_g, v_sgu_ln_b, v_sgu_w, v_sgu_b, v_w_b_out, v_w_out, v_xattn_norm, v_mem_norm, v_w_q, v_w_kv, v_w_o, v_ffn2_norm, v_ffn2_w_gu, v_ffn2_w_down, v_final_norm):
    names = ("ffn1_norm", "ffn1_w_gu", "ffn1_w_down", "mix_norm", "w_in", "b_in", "conv_w", "conv_b",
             "conv_ln_g", "conv_ln_b", "w_a_out", "sgu_ln_g", "sgu_ln_b", "sgu_w", "sgu_b", "w_b_out", "w_out",
             "xattn_norm", "mem_norm", "w_q", "w_kv", "w_o", "ffn2_norm", "ffn2_w_gu", "ffn2_w_down",
             "final_norm")
    wts = dict(zip(names, (ffn1_norm, ffn1_w_gu, ffn1_w_down, mix_norm, w_in, b_in, conv_w, conv_b, conv_ln_g,
                           conv_ln_b, w_a_out, sgu_ln_g, sgu_ln_b, sgu_w, sgu_b, w_b_out, w_out, xattn_norm,
                           mem_norm, w_q, w_kv, w_o, ffn2_norm, ffn2_w_gu, ffn2_w_down, final_norm)))
    mom1 = dict(zip(names, (m_ffn1_norm, m_ffn1_w_gu, m_ffn1_w_down, m_mix_norm, m_w_in, m_b_in, m_conv_w,
                            m_conv_b, m_conv_ln_g, m_conv_ln_b, m_w_a_out, m_sgu_ln_g, m_sgu_ln_b, m_sgu_w,
                            m_sgu_b, m_w_b_out, m_w_out, m_xattn_norm, m_mem_norm, m_w_q, m_w_kv, m_w_o,
                            m_ffn2_norm, m_ffn2_w_gu, m_ffn2_w_down, m_final_norm)))
    mom2 = dict(zip(names, (v_ffn1_norm, v_ffn1_w_gu, v_ffn1_w_down, v_mix_norm, v_w_in, v_b_in, v_conv_w,
                            v_conv_b, v_conv_ln_g, v_conv_ln_b, v_w_a_out, v_sgu_ln_g, v_sgu_ln_b, v_sgu_w,
                            v_sgu_b, v_w_b_out, v_w_out, v_xattn_norm, v_mem_norm, v_w_q, v_w_kv, v_w_o,
                            v_ffn2_norm, v_ffn2_w_gu, v_ffn2_w_down, v_final_norm)))
    xi, yi, ci = _place()
    chip = (2 * xi + yi).astype(jnp.int32)
    core = ci.astype(jnp.int32)
    chip_arr = chip.reshape(1)
    place_arr = jnp.stack([chip, core])
    x2, mem2, tgt = x[0], mem[0], loss_target[0]

    slot = {n: cast_bf16(wts[n][0], chip_arr, "cast_" + n) for n in BIG}
    cw_pad = jnp.pad(conv_w[0], ((0, HALO - CW), (0, 0)))
    slot["conv_w"] = lax.dynamic_update_slice(jnp.zeros((NSH,) + cw_pad.shape, F32), cw_pad[None], (chip, 0, 0))
    g_first = ("ffn1_w_gu",)
    g_mix = ("ffn1_w_down", "w_in", "w_a_out", "w_b_out", "w_out", "conv_w")
    g_rest = ("w_q", "w_kv", "w_o", "ffn2_w_gu", "ffn2_w_down")

    def gather(group):
        return gather_rider([slot[n].reshape(NSH, 2, slot[n].shape[1] // 2, slot[n].shape[2]) for n in group])

    def gathered(group, res):
        return {n: r.reshape(slot[n].shape) for n, r in zip(group, res)}

    full = gathered(g_first, run_rider(gather(g_first), "gather_ffn1"))
    wgu1 = full["ffn1_w_gu"]
    tril = jnp.tril(jnp.ones((CHUNK, CHUNK), dtype=bool))
    ws = jnp.where(tril[None], sgu_w[0], 0.0).astype(BF)
    wst = jnp.transpose(ws, (0, 2, 1))
    sbias = jnp.repeat(jnp.transpose(sgu_b[0]), GD, axis=1)
    gfin = final_norm.reshape(1, D)

    (gu1, act1), rode = ffn_hidden(x2, ffn1_norm, wgu1, rider=gather(g_mix))
    full.update(gathered(g_mix, rode))
    wd1 = full["ffn1_w_down"].reshape(FF, D)
    h1 = ffn_down(x2, act1, wd1)
    win = full["w_in"]
    wa, wb, wout = (full[n].reshape(D, D) for n in ("w_a_out", "w_b_out", "w_out"))
    cw_full = jnp.transpose(full["conv_w"], (1, 0, 2)).reshape(HALO, D)
    (h2, proj, n2b, conv_out), rode = mix_fwd(
        h1, mix_norm, win, b_in, cw_full, conv_b, conv_ln_g, conv_ln_b, wa, sgu_ln_g, sgu_ln_b, ws, sbias, wb,
        wout, rider=gather(g_rest))
    full.update(gathered(g_rest, rode))
    wgu2, wd2, wkv = full["ffn2_w_gu"], full["ffn2_w_down"].reshape(FF, D), full["w_kv"]
    wq, wo = full["w_q"].reshape(D, D), full["w_o"].reshape(D, D)
    kb, vb, memn = kv_proj(mem2, mem_norm, wkv)
    h3 = xattn_fwd(h2, xattn_norm, wq, kb, vb, wo)
    dh4, gu2, loss_lanes, d_final = ffn_fwd_loss(h3, ffn2_norm, wgu2, wd2, gfin, tgt)

    own, halves = {}, {}

    def exchange(group, grads):
        views = []
        for n, (g, gb) in zip(group, grads):
            rs, cs = wts[n].shape[1:]
            own[n] = g.reshape(NSH, 2, rs // 2, cs)
            views.append(gb.reshape(NSH, 2, rs // 2, cs))
        return reduce_rider(views)

    def reduce(group, recv):
        for n, r in zip(group, recv):
            halves[n] = device_sum(own[n], r, place_arr, "device_sum_" + n)

    dh3, n4, a4, dgu4, dhb4, d_ffn2n = ffn_bwd(h3, gu2, dh4, ffn2_norm, wgu2, wd2, "ffn2_bwd")
    g_ffn2 = ("ffn2_w_gu", "ffn2_w_down")
    ride = exchange(g_ffn2, [dw_matmul(n4, dgu4, NSH, "dw_ffn2_gu")[:2], dw_matmul(a4, dhb4, 1, "dw_ffn2_down")[:2]])
    (dh2, n3, dq, att, dhb3, dk, dv, d_xn), rode = xattn_bwd(h2, dh3, xattn_norm, wq, kb, vb, wo, rider=ride)
    reduce(g_ffn2, rode)
    g_att = ("w_q", "w_o", "w_kv")
    d_wkv, d_wkv_b, d_memn = kv_bwd(mem2, mem_norm, memn, wkv, dk, dv)
    ride = exchange(g_att, [dw_matmul(n3, dq, 1, "dw_q")[:2], dw_matmul(att, dhb3, 1, "dw_o")[:2],
                            (d_wkv, d_wkv_b)])
    ((dconv, dproj, sa, dya, ob, dyb, mg, dhb2, d_sgu_w, d_sgu_b, d_lna_g, d_lna_b, d_lnb_g, d_lnb_b),
     rode) = mix_bwd_branches(proj, conv_out, dh2, conv_ln_g, conv_ln_b, wa, sgu_ln_g, sgu_ln_b, ws, wst,
                              sbias, wb, wout, rider=ride)
    reduce(g_att, rode)
    dproj, d_conv_w, d_conv_b = conv_bwd(proj, dconv, dproj, cw_full)
    dh1, d_mixn, d_b_in = mix_bwd_in(dproj, h1, dh2, mix_norm, win)
    g_mixw = ("w_in", "w_a_out", "w_b_out")
    ride = exchange(g_mixw, [dw_matmul(n2b, dproj, NSH, "dw_in")[:2], dw_matmul(sa, dya, 1, "dw_a_out")[:2],
                             dw_matmul(ob, dyb, 1, "dw_b_out")[:2]])
    dx, n1, a1, dgu1, dhb1, d_ffn1n = ffn_bwd(x2, gu1, dh1, ffn1_norm, wgu1, wd1, "ffn1_bwd")
    small_grads = {
        "ffn1_norm": d_ffn1n, "mix_norm": d_mixn, "xattn_norm": d_xn, "mem_norm": d_memn, "ffn2_norm": d_ffn2n,
        "final_norm": d_final, "conv_b": d_conv_b, "conv_ln_g": d_lna_g, "conv_ln_b": d_lna_b,
        "sgu_ln_g": d_lnb_g, "sgu_ln_b": d_lnb_b, "b_in": d_b_in.reshape(6, D), "conv_w": d_conv_w[:CW],
        "sgu_w": d_sgu_w.reshape(64, D), "sgu_b": jnp.transpose(d_sgu_b[:, :NG]).reshape(1, NG * CHUNK),
        "loss": loss_lanes}
    slots = lax.dynamic_update_slice(jnp.zeros((NDEV, SMALL_ROWS, D), F32), _pack_small(small_grads)[None],
                                     (2 * chip + core, 0, 0))
    riders = [ride, allgather_rider(slots)]
    d_wgu1, d_wgu1_b, rode = dw_matmul(n1, dgu1, NSH, "dw_ffn1_gu", rider=merge_riders(riders))
    recv_mix, all_slots = split_results(riders, rode)
    reduce(g_mixw, recv_mix)
    ride = exchange(("ffn1_w_gu",), [(d_wgu1, d_wgu1_b)])
    d_down, d_down_b, rode = dw_matmul(a1, dhb1, 1, "dw_ffn1_down", rider=ride)
    reduce(("ffn1_w_gu",), rode)
    ride = exchange(("ffn1_w_down",), [(d_down, d_down_b)])
    d_out, d_out_b, rode = dw_matmul(mg, dhb2, 1, "dw_out", rider=ride)
    reduce(("ffn1_w_down",), rode)
    reduce(("w_out",), run_rider(exchange(("w_out",), [(d_out, d_out_b)]), "exchange_w_out"))
    swapped = run_rider(swap_rider([halves[n] for n in BIG]), "pair_swap")
    gshard = {n: g.reshape(wts[n].shape[1:]) for n, g in zip(BIG, swapped)}

    small = sum_slots(all_slots[0])
    loss = (0.5 / D) * jnp.sum(small[LOSS_ROW])
    gsmall = _unpack_small(small, wts, chip)

    out_g, out_d, out_m, out_v = dict(gsmall), {}, {}, {}
    sw, sm, sv = (_pack_small(_small_views(t))[:LOSS_ROW] for t in (wts, mom1, mom2))
    sg = _pack_small(_small_views({n: gsmall[n] for n in SMALL}))[:LOSS_ROW]
    for dst, packed in zip((out_d, out_m, out_v), adamw(sw, sg, sm, sv, "adamw_small")):
        dst.update(_unpack_small(packed, wts, None))
    for n in BIG:
        shape = wts[n].shape
        out_g[n] = gshard[n].reshape(shape)
        d, mn, vn = adamw(wts[n][0], gshard[n], mom1[n][0], mom2[n][0], "adamw_" + n)
        out_d[n], out_m[n], out_v[n] = d.reshape(shape), mn.reshape(shape), vn.reshape(shape)
    return (loss, dx[None], *[out_g[n] for n in names], *[out_d[n] for n in names],
            *[out_m[n] for n in names], *[out_v[n] for n in names])
```

```python
import functools
import math

import jax
import jax.numpy as jnp
from jax import lax
from jax.experimental import pallas as pl
from jax.experimental.pallas import tpu as pltpu

F32 = jnp.float32
BF = jnp.bfloat16
MESH = pl.DeviceIdType.MESH

D = 1024
FF = 2816
HC = FF // 2
NSH = 4
DIN = 6 * D
INB = DIN // NSH
CW = 31
HALO = 32
CHUNK = 128
NG = 4
GD = D // NG
NH = 4
HD = D // NH
NMEM = 256
EPS_RMS = 1e-6
EPS_LN = 1e-5
GELU_C0 = math.sqrt(2.0 / math.pi)
GELU_C1 = 0.044715
ATT_SCALE = 1.0 / math.sqrt(HD)

ADAM_LR = 0.001
ADAM_B1 = 0.9
ADAM_B2 = 0.999
ADAM_EPS = 1e-08
ADAM_WD = 0.01
ADAM_STEP = 10

VMEM_LIMIT = 56 * 1024 * 1024


def _cparams(sem=None, **kw):
    if sem is not None:
        kw["dimension_semantics"] = sem
    return pltpu.CompilerParams(vmem_limit_bytes=VMEM_LIMIT, **kw)


def _dot(a, b):
    return jnp.dot(a, b, preferred_element_type=F32)


def _dot_nt(a, b):
    return lax.dot_general(a, b, (((1,), (1,)), ((), ())), preferred_element_type=F32)


def _dot_tn(a, b):
    return lax.dot_general(a, b, (((0,), (0,)), ((), ())), preferred_element_type=F32)


def _sigmoid(x):
    return 1.0 / (1.0 + jnp.exp(-x))


def _gelu(x):
    t = jnp.tanh(GELU_C0 * (x + GELU_C1 * (x * x * x)))
    return 0.5 * x * (1.0 + t), t


def _gelu_with_grad(x):
    x2 = x * x
    t = jnp.tanh(GELU_C0 * (x + GELU_C1 * (x2 * x)))
    onep = 1.0 + t
    hx = 0.5 * x
    grad = 0.5 * onep + hx * (1.0 - t * t) * (GELU_C0 + (3.0 * GELU_C0 * GELU_C1) * x2)
    return hx * onep, grad


def _mean(x):
    return jnp.mean(x, axis=-1, keepdims=True)


def _rms(x):
    r = lax.rsqrt(_mean(x * x) + EPS_RMS)
    return x * r, r


def _rms_bwd(dn, xh, r, g):
    dxh = dn * g
    return r * (dxh - xh * _mean(dxh * xh))


def _ln(x):
    xc = x - _mean(x)
    r = lax.rsqrt(_mean(xc * xc) + EPS_LN)
    return xc * r, r


def _ln_bwd(dy, xh, r, g):
    dxh = dy * g
    return r * (dxh - _mean(dxh) - xh * _mean(dxh * xh))


def _colsum(x):
    return jnp.sum(x, axis=0, keepdims=True)


def _const_spec(shape):
    nd = len(shape)
    return pl.BlockSpec(shape, lambda *_: (0,) * nd, pipeline_mode=pl.Buffered(1))


def _row_spec(ts, width):
    return pl.BlockSpec((ts, width), lambda i: (i, 0))


def _acc_spec(shape):
    nd = len(shape)
    return pl.BlockSpec(shape, lambda *_: (0,) * nd)


def _tile(s, want):
    return min(s, want)


HBM_SPEC = pl.BlockSpec(memory_space=pltpu.HBM)


class Rider:
    def __init__(self, ins, outs, aliases, nsem, start, finish, relay=None):
        self.ins, self.outs, self.aliases, self.nsem = list(ins), list(outs), dict(aliases), nsem
        self.start, self.finish, self.relay = start, finish, relay


def _pcall(body, *, name, grid, args, in_specs, out_shape, out_specs, scratch=(), rider=None):
    sem = ("arbitrary",) * len(grid)
    n_in, n_out = len(args), len(out_shape)
    if rider is None:
        res = pl.pallas_call(
            body, grid=grid, name=name, out_shape=tuple(out_shape), in_specs=list(in_specs),
            out_specs=tuple(out_specs), scratch_shapes=list(scratch), compiler_params=_cparams(sem))(*args)
        return tuple(res), ()
    r_in, r_out = len(rider.ins), len(rider.outs)

    def wrapped(*refs):
        a, ri = refs[:n_in], refs[n_in:n_in + r_in]
        o = refs[n_in + r_in:n_in + r_in + n_out]
        ro = refs[n_in + r_in + n_out:n_in + r_in + n_out + r_out]
        s, (send, recv) = refs[n_in + r_in + n_out + r_out:-2], refs[-2:]
        first = functools.reduce(jnp.logical_and, [pl.program_id(d) == 0 for d in range(len(grid))])
        last = functools.reduce(jnp.logical_and, [pl.program_id(d) == g - 1 for d, g in enumerate(grid)])

        @pl.when(first)
        def _():
            rider.start(ri, ro, send, recv)

        body(*a, *o, *s)

        if rider.relay is not None:
            step = functools.reduce(lambda acc, d: acc * grid[d] + pl.program_id(d), range(len(grid)), 0)

            @pl.when(step == max((3 * math.prod(grid)) // 4 - 1, 0))
            def _():
                rider.relay(ri, ro, send, recv)

        @pl.when(last)
        def _():
            rider.finish(ri, ro, send, recv)

    res = pl.pallas_call(
        wrapped, grid=grid, name=name, out_shape=tuple(out_shape) + tuple(rider.outs),
        in_specs=list(in_specs) + [HBM_SPEC] * r_in, out_specs=tuple(out_specs) + (HBM_SPEC,) * r_out,
        scratch_shapes=list(scratch) + [pltpu.SemaphoreType.DMA((rider.nsem,)),
                                        pltpu.SemaphoreType.DMA((rider.nsem,))],
        input_output_aliases={n_in + i: n_out + j for i, j in rider.aliases.items()},
        compiler_params=_cparams(sem, has_side_effects=True))(*args, *rider.ins)
    return tuple(res[:n_out]), tuple(res[n_out:])


def run_rider(rider, name):
    r_in = len(rider.ins)

    def body(*refs):
        ri, ro, (send, recv) = refs[:r_in], refs[r_in:-2], refs[-2:]
        rider.start(ri, ro, send, recv)
        if rider.relay is not None:
            rider.relay(ri, ro, send, recv)
        rider.finish(ri, ro, send, recv)

    return pl.pallas_call(
        body, name=name, out_shape=tuple(rider.outs), in_specs=[HBM_SPEC] * r_in,
        out_specs=(HBM_SPEC,) * len(rider.outs),
        scratch_shapes=[pltpu.SemaphoreType.DMA((rider.nsem,)), pltpu.SemaphoreType.DMA((rider.nsem,))],
        input_output_aliases=rider.aliases,
        compiler_params=pltpu.CompilerParams(has_side_effects=True))(*rider.ins)


FFN_BWD_TILE = 256


def _ffn_apply(x, g_ref, wgu_ref, wd_ref, gu_ref):
    xh, _ = _rms(x)
    nb = (xh * g_ref[...]).astype(BF)
    acc = jnp.zeros(x.shape, F32)
    for j in range(2):
        g = _dot(nb, wgu_ref[j])
        u = _dot(nb, wgu_ref[2 + j])
        gu_ref[:, j * HC:(j + 1) * HC] = g.astype(BF)
        gu_ref[:, FF + j * HC:FF + (j + 1) * HC] = u.astype(BF)
        a = (g * _sigmoid(g) * u).astype(BF)
        acc = acc + _dot(a, wd_ref[j * HC:(j + 1) * HC, :])
    return x + 0.5 * acc


def ffn_hidden(h, gain, wgu, rider=None):
    s = h.shape[0]
    ts = _tile(s, 512)

    def body(h_ref, g_ref, wgu_ref, gu_ref, a_ref):
        xh, _ = _rms(h_ref[...])
        nb = (xh * g_ref[...]).astype(BF)
        for j in range(2):
            g = _dot(nb, wgu_ref[j])
            u = _dot(nb, wgu_ref[2 + j])
            gu_ref[:, j * HC:(j + 1) * HC] = g.astype(BF)
            gu_ref[:, FF + j * HC:FF + (j + 1) * HC] = u.astype(BF)
            a_ref[:, j * HC:(j + 1) * HC] = (g * _sigmoid(g) * u).astype(BF)

    return _pcall(
        body, grid=(s // ts,), name="ffn1_hidden", args=(h, gain, wgu),
        out_shape=[jax.ShapeDtypeStruct((s, 2 * FF), BF), jax.ShapeDtypeStruct((s, FF), BF)],
        in_specs=[_row_spec(ts, D), _const_spec((1, D)), _const_spec((NSH, D, HC))],
        out_specs=[_row_spec(ts, 2 * FF), _row_spec(ts, FF)], rider=rider)


def ffn_down(h, a, wd):
    s = h.shape[0]
    ts = _tile(s, 512)

    def body(h_ref, a_ref, wd_ref, o_ref):
        o_ref[...] = h_ref[...] + 0.5 * _dot(a_ref[...], wd_ref[...])

    return pl.pallas_call(
        body, grid=(s // ts,), name="ffn1_down", out_shape=jax.ShapeDtypeStruct((s, D), F32),
        in_specs=[_row_spec(ts, D), _row_spec(ts, FF), _const_spec((FF, D))], out_specs=_row_spec(ts, D),
        compiler_params=_cparams(("arbitrary",)))(h, a, wd)


def ffn_fwd_loss(h, gain, wgu, wd, gfin, target):
    s = h.shape[0]
    ts = _tile(s, 512)

    def body(h_ref, g_ref, wgu_ref, wd_ref, gf_ref, t_ref, dh_ref, gu_ref, loss_ref, dgf_ref):
        @pl.when(pl.program_id(0) == 0)
        def _():
            loss_ref[...] = jnp.zeros_like(loss_ref)
            dgf_ref[...] = jnp.zeros_like(dgf_ref)

        h4 = _ffn_apply(h_ref[...], g_ref, wgu_ref, wd_ref, gu_ref)
        yh, r4 = _rms(h4)
        gf = gf_ref[...]
        e = yh * gf - t_ref[...]
        loss_ref[...] += _colsum(e * e)
        dy = e * (1.0 / D)
        dgf_ref[...] += _colsum(dy * yh)
        dh_ref[...] = _rms_bwd(dy, yh, r4, gf)

    return pl.pallas_call(
        body, grid=(s // ts,), name="ffn_fwd_loss",
        out_shape=(jax.ShapeDtypeStruct((s, D), F32), jax.ShapeDtypeStruct((s, 2 * FF), BF),
                   jax.ShapeDtypeStruct((1, D), F32), jax.ShapeDtypeStruct((1, D), F32)),
        in_specs=[_row_spec(ts, D), _const_spec((1, D)), _const_spec((NSH, D, HC)), _const_spec((FF, D)),
                  _const_spec((1, D)), _row_spec(ts, D)],
        out_specs=(_row_spec(ts, D), _row_spec(ts, 2 * FF), _acc_spec((1, D)), _acc_spec((1, D))),
        compiler_params=_cparams(("arbitrary",)),
    )(h, gain, wgu, wd, gfin, target)


def ffn_bwd(h, gu, dh, gain, wgu, wd, name):
    s = h.shape[0]
    ts = _tile(s, FFN_BWD_TILE)

    def body(h_ref, gu_ref, dh_ref, g_ref, wgu_ref, wd_ref, dx_ref, n_ref, a_ref, dgu_ref, dhb_ref, dg_ref):
        @pl.when(pl.program_id(0) == 0)
        def _():
            dg_ref[...] = jnp.zeros_like(dg_ref)

        x = h_ref[...]
        dh = dh_ref[...]
        gain_v = g_ref[...]
        xh, r = _rms(x)
        n_ref[...] = (xh * gain_v).astype(BF)
        dhb = (0.5 * dh).astype(BF)
        dhb_ref[...] = dhb
        dn = jnp.zeros((ts, D), F32)
        for j in range(2):
            g = gu_ref[:, j * HC:(j + 1) * HC].astype(F32)
            u = gu_ref[:, FF + j * HC:FF + (j + 1) * HC].astype(F32)
            sg = _sigmoid(g)
            sl = g * sg
            a_ref[:, j * HC:(j + 1) * HC] = (sl * u).astype(BF)
            da = _dot_nt(dhb, wd_ref[j * HC:(j + 1) * HC, :])
            dgb = (da * u * (sg * (1.0 + g * (1.0 - sg)))).astype(BF)
            dub = (da * sl).astype(BF)
            dgu_ref[:, j * HC:(j + 1) * HC] = dgb
            dgu_ref[:, FF + j * HC:FF + (j + 1) * HC] = dub
            dn = dn + _dot_nt(dgb, wgu_ref[j]) + _dot_nt(dub, wgu_ref[2 + j])
        dg_ref[...] += _colsum(dn * xh)
        dx_ref[...] = dh + _rms_bwd(dn, xh, r, gain_v)

    return pl.pallas_call(
        body, grid=(s // ts,), name=name,
        out_shape=(jax.ShapeDtypeStruct((s, D), F32), jax.ShapeDtypeStruct((s, D), BF),
                   jax.ShapeDtypeStruct((s, FF), BF), jax.ShapeDtypeStruct((s, 2 * FF), BF),
                   jax.ShapeDtypeStruct((s, D), BF), jax.ShapeDtypeStruct((1, D), F32)),
        in_specs=[_row_spec(ts, D), _row_spec(ts, 2 * FF), _row_spec(ts, D), _const_spec((1, D)),
                  _const_spec((NSH, D, HC)), _const_spec((FF, D))],
        out_specs=(_row_spec(ts, D), _row_spec(ts, D), _row_spec(ts, FF), _row_spec(ts, 2 * FF),
                   _row_spec(ts, D), _acc_spec((1, D))),
        compiler_params=_cparams(("arbitrary",)),
    )(h, gu, dh, gain, wgu, wd)


def dw_matmul(x, dy, nsplit, name, rider=None):
    s, k = x.shape
    n = dy.shape[1]
    nb = n // nsplit
    ts = _tile(s, 2048 if k <= D else 1024)
    nsteps = s // ts

    def body(x_ref, dy_ref, o_ref, ob_ref):
        @pl.when(pl.program_id(1) == 0)
        def _():
            o_ref[...] = jnp.zeros_like(o_ref)

        o_ref[0] += _dot_tn(x_ref[...], dy_ref[...])

        @pl.when(pl.program_id(1) == nsteps - 1)
        def _():
            ob_ref[...] = o_ref[...].astype(BF)

    spec = pl.BlockSpec((1, k, nb), lambda j, i: (j, 0, 0))
    (out, outb), rode = _pcall(
        body, grid=(nsplit, nsteps), name=name, args=(x, dy),
        out_shape=[jax.ShapeDtypeStruct((nsplit, k, nb), F32), jax.ShapeDtypeStruct((nsplit, k, nb), BF)],
        in_specs=[pl.BlockSpec((ts, k), lambda j, i: (i, 0)), pl.BlockSpec((ts, nb), lambda j, i: (i, j))],
        out_specs=[spec, spec], rider=rider)
    return out, outb, rode


def _split_in_proj(p, b):
    h = INB - D
    a_val = p[0][:, :D] + b[:, 0:D]
    a_gate = jnp.concatenate([p[0][:, D:], p[1][:, :h]], axis=1) + b[:, D:2 * D]
    b_u = p[1][:, h:] + b[:, 2 * D:3 * D]
    b_v = p[2][:, :D] + b[:, 3 * D:4 * D]
    g_a = jnp.concatenate([p[2][:, D:], p[3][:, :h]], axis=1) + b[:, 4 * D:5 * D]
    g_b = p[3][:, h:] + b[:, 5 * D:6 * D]
    return a_val, a_gate, b_u, b_v, g_a, g_b


def _sgu_mix(vnb, ws_ref, sb_ref, mixed_ref, ts):
    for ci in range(ts // CHUNK):
        rows = slice(ci * CHUNK, (ci + 1) * CHUNK)
        for g in range(NG):
            cols = slice(g * GD, (g + 1) * GD)
            mixed_ref[rows, cols] = _dot(ws_ref[g], vnb[rows, cols]) + sb_ref[:, cols]


SUB = 8
CB = 128
SH_ROWS_EXTRA = HALO - SUB


def _shifted_copies(ext_ref, sh_ref, lanes, ts):
    for b in range(1, SUB):
        sh_ref[b - 1] = ext_ref[b:b + ts + SH_ROWS_EXTRA, lanes]


def _window(ext_ref, sh_ref, lanes, first, r0, nrows):
    b = first % SUB
    a = first - b
    if b == 0:
        return ext_ref[a + r0:a + r0 + nrows, lanes]
    return sh_ref[b - 1, a + r0:a + r0 + nrows, :]


def mix_fwd(h, gain, win, b_in, conv_w, conv_b, lna_g, lna_b, wa, lnb_g, lnb_b, ws, sbias, wb, wo, rider=None):
    s = h.shape[0]
    ts = _tile(s, 256)

    def body(h_ref, g_ref, win_ref, bin_ref, cw_ref, cb_ref, lag_ref, lab_ref, wa_ref, lbg_ref, lbb_ref,
             ws_ref, sb_ref, wb_ref, wo_ref, o_ref, p_ref, n_ref, c_ref, ext_ref, mixed_ref, sh_ref):
        @pl.when(pl.program_id(0) == 0)
        def _():
            ext_ref[0:HALO, :] = jnp.zeros((HALO, D), F32)

        x = h_ref[...]
        xh, _ = _rms(x)
        nb = (xh * g_ref[...]).astype(BF)
        n_ref[...] = nb
        b = bin_ref[...]
        p = []
        for k in range(NSH):
            pk = _dot(nb, win_ref[k])
            p_ref[:, k * INB:(k + 1) * INB] = (pk + b[:, k * INB:(k + 1) * INB]).astype(BF)
            p.append(pk)
        a_val, a_gate, b_u, b_v, g_a, g_b = _split_in_proj(p, b)
        ext_ref[HALO:HALO + ts, :] = a_val * _sigmoid(a_gate)
        for l0 in range(0, D, CB):
            lanes = slice(l0, l0 + CB)
            _shifted_copies(ext_ref, sh_ref, lanes, ts)
            for r0 in range(0, ts, CB):
                acc = jnp.zeros((CB, CB), F32) + cb_ref[:, lanes]
                for k in range(CW):
                    acc = acc + cw_ref[k:k + 1, lanes] * _window(ext_ref, sh_ref, lanes,
                                                                 HALO - (CW - 1) + k, r0, CB)
                c_ref[r0:r0 + CB, lanes] = acc
        ext_ref[0:HALO, :] = ext_ref[ts:ts + HALO, :]
        ch, _ = _ln(c_ref[...])
        la = ch * lag_ref[...] + lab_ref[...]
        sa = (la * _sigmoid(la)).astype(BF)
        ya = _dot(sa, wa_ref[...])
        ub, _ = _gelu(b_u)
        gv, _ = _gelu(b_v)
        vh, _ = _ln(gv)
        vnb = (vh * lbg_ref[...] + lbb_ref[...]).astype(BF)
        _sgu_mix(vnb, ws_ref, sb_ref, mixed_ref, ts)
        ob = (ub * mixed_ref[...]).astype(BF)
        yb = _dot(ob, wb_ref[...])
        merged = (_sigmoid(g_a) * ya + _sigmoid(g_b) * yb).astype(BF)
        o_ref[...] = x + _dot(merged, wo_ref[...])

    vec = _const_spec((1, D))
    sq = _const_spec((D, D))
    return _pcall(
        body, grid=(s // ts,), name="mix_fwd",
        args=(h, gain, win, b_in, conv_w, conv_b, lna_g, lna_b, wa, lnb_g, lnb_b, ws, sbias, wb, wo),
        out_shape=(jax.ShapeDtypeStruct((s, D), F32), jax.ShapeDtypeStruct((s, DIN), BF),
                   jax.ShapeDtypeStruct((s, D), BF), jax.ShapeDtypeStruct((s, D), F32)),
        in_specs=[_row_spec(ts, D), vec, _const_spec((NSH, D, INB)), _const_spec((1, DIN)),
                  _const_spec((HALO, D)), vec, vec, vec, sq, vec, vec,
                  _const_spec((NG, CHUNK, CHUNK)), _const_spec((CHUNK, D)), sq, sq],
        out_specs=(_row_spec(ts, D), _row_spec(ts, DIN), _row_spec(ts, D), _row_spec(ts, D)),
        scratch=[pltpu.VMEM((ts + HALO, D), F32), pltpu.VMEM((ts, D), F32),
                 pltpu.VMEM((SUB - 1, ts + SH_ROWS_EXTRA, CB), F32)], rider=rider)


def mix_bwd_branches(p, c, dh, lna_g, lna_b, wa, lnb_g, lnb_b, ws, wst, sbias, wb, wo, rider=None):
    s = dh.shape[0]
    ts = _tile(s, 256)
    nsteps = s // ts

    def body(p_ref, c_ref, dh_ref, lag_ref, lab_ref, wa_ref, lbg_ref, lbb_ref, ws_ref, wst_ref, sb_ref,
             wb_ref, wo_ref, dc_ref, dp_ref, sa_ref, dya_ref, ob_ref, dyb_ref, mg_ref, dhb_ref,
             dws_ref, dsb_ref, dlag_ref, dlab_ref, dlbg_ref, dlbb_ref, mixed_ref, dmix_ref, dvn_ref, dsb_acc):
        step = pl.program_id(0)

        @pl.when(step == 0)
        def _():
            for ref in (dws_ref, dsb_acc, dlag_ref, dlab_ref, dlbg_ref, dlbb_ref):
                ref[...] = jnp.zeros_like(ref)

        b_u = p_ref[:, 2 * D:3 * D].astype(F32)
        b_v = p_ref[:, 3 * D:4 * D].astype(F32)
        sga = _sigmoid(p_ref[:, 4 * D:5 * D].astype(F32))
        sgb = _sigmoid(p_ref[:, 5 * D:6 * D].astype(F32))
        lag = lag_ref[...]
        ch, ra = _ln(c_ref[...])
        la = ch * lag + lab_ref[...]
        sla = _sigmoid(la)
        sa = (la * sla).astype(BF)
        sa_ref[...] = sa
        ya = _dot(sa, wa_ref[...])
        lbg = lbg_ref[...]
        ub, dub = _gelu_with_grad(b_u)
        gv, dgv = _gelu_with_grad(b_v)
        vh, rb = _ln(gv)
        vnb = (vh * lbg + lbb_ref[...]).astype(BF)
        _sgu_mix(vnb, ws_ref, sb_ref, mixed_ref, ts)
        mixed = mixed_ref[...]
        ob = (ub * mixed).astype(BF)
        ob_ref[...] = ob
        yb = _dot(ob, wb_ref[...])
        mg_ref[...] = (sga * ya + sgb * yb).astype(BF)
        dhb = dh_ref[...].astype(BF)
        dhb_ref[...] = dhb
        dm = _dot_nt(dhb, wo_ref[...])
        dp_ref[:, 0:2 * D] = jnp.zeros((ts, 2 * D), BF)
        dp_ref[:, 4 * D:5 * D] = (dm * ya * sga * (1.0 - sga)).astype(BF)
        dp_ref[:, 5 * D:6 * D] = (dm * yb * sgb * (1.0 - sgb)).astype(BF)
        dya = (dm * sga).astype(BF)
        dya_ref[...] = dya
        dyb = (dm * sgb).astype(BF)
        dyb_ref[...] = dyb
        dla = _dot_nt(dya, wa_ref[...]) * (sla * (1.0 + la * (1.0 - sla)))
        dlag_ref[...] += _colsum(dla * ch)
        dlab_ref[...] += _colsum(dla)
        dc_ref[...] = _ln_bwd(dla, ch, ra, lag)
        dob = _dot_nt(dyb, wb_ref[...])
        dp_ref[:, 2 * D:3 * D] = (dob * mixed * dub).astype(BF)
        dmix = dob * ub
        dmix_ref[...] = dmix.astype(BF)
        dsb = jnp.zeros((CHUNK, D), F32)
        for ci in range(ts // CHUNK):
            rows = slice(ci * CHUNK, (ci + 1) * CHUNK)
            dsb = dsb + dmix[rows, :]
            for g in range(NG):
                cols = slice(g * GD, (g + 1) * GD)
                dmb = dmix_ref[rows, cols]
                dws_ref[g] += _dot_nt(dmb, vnb[rows, cols])
                dvn_ref[rows, cols] = _dot(wst_ref[g], dmb)
        dsb_acc[...] += dsb
        dvn = dvn_ref[...]
        dlbg_ref[...] += _colsum(dvn * vh)
        dlbb_ref[...] += _colsum(dvn)
        dp_ref[:, 3 * D:4 * D] = (_ln_bwd(dvn, vh, rb, lbg) * dgv).astype(BF)

        @pl.when(step == nsteps - 1)
        def _():
            row = lax.broadcasted_iota(jnp.int32, (CHUNK, CHUNK), 0)
            col = lax.broadcasted_iota(jnp.int32, (CHUNK, CHUNK), 1)
            for g in range(NG):
                dws_ref[g] = jnp.where(col <= row, dws_ref[g], 0.0)
            acc = jnp.zeros((CHUNK, CHUNK), F32)
            for g in range(NG):
                tot = jnp.sum(dsb_acc[:, g * GD:(g + 1) * GD], axis=-1, keepdims=True)
                acc = acc + jnp.where(col == g, tot, 0.0)
            dsb_ref[...] = acc

    vec = _const_spec((1, D))
    sq = _const_spec((D, D))
    bf_rows = jax.ShapeDtypeStruct((s, D), BF)
    acc_vec = jax.ShapeDtypeStruct((1, D), F32)
    return _pcall(
        body, grid=(nsteps,), name="mix_bwd_branches",
        args=(p, c, dh, lna_g, lna_b, wa, lnb_g, lnb_b, ws, wst, sbias, wb, wo),
        out_shape=(jax.ShapeDtypeStruct((s, D), F32), jax.ShapeDtypeStruct((s, DIN), BF),
                   bf_rows, bf_rows, bf_rows, bf_rows, bf_rows, bf_rows,
                   jax.ShapeDtypeStruct((NG, CHUNK, CHUNK), F32), jax.ShapeDtypeStruct((CHUNK, CHUNK), F32),
                   acc_vec, acc_vec, acc_vec, acc_vec),
        in_specs=[_row_spec(ts, DIN), _row_spec(ts, D), _row_spec(ts, D), vec, vec, sq, vec, vec,
                  _const_spec((NG, CHUNK, CHUNK)), _const_spec((NG, CHUNK, CHUNK)), _const_spec((CHUNK, D)),
                  sq, sq],
        out_specs=(_row_spec(ts, D), _row_spec(ts, DIN)) + (_row_spec(ts, D),) * 6
        + (_acc_spec((NG, CHUNK, CHUNK)), _acc_spec((CHUNK, CHUNK))) + (_acc_spec((1, D)),) * 4,
        scratch=[pltpu.VMEM((ts, D), F32), pltpu.VMEM((ts, D), BF), pltpu.VMEM((ts, D), F32),
                 pltpu.VMEM((CHUNK, D), F32)], rider=rider)


def conv_bwd(p, dc, dp, conv_w):
    s = dc.shape[0]
    ts = _tile(s, 256)
    nsteps = s // ts
    per = ts // HALO

    rb = 16

    def body(pm_ref, pp_ref, dcm_ref, dcn_ref, cw_ref, dpin_ref, dp_ref, dw_ref, db_ref, ext_ref, dext_ref,
             dw8_ref, sh_ref, dsh_ref, dglu_ref):
        del dpin_ref
        step = pl.program_id(0)

        @pl.when(step == 0)
        def _():
            dw8_ref[...] = jnp.zeros_like(dw8_ref)
            db_ref[...] = jnp.zeros_like(db_ref)

        a_val = pm_ref[:, 0:D].astype(F32)
        sg = _sigmoid(pm_ref[:, D:2 * D].astype(F32))
        prev = pp_ref[:, 0:D].astype(F32) * _sigmoid(pp_ref[:, D:2 * D].astype(F32))
        ext_ref[0:HALO, :] = jnp.where(step > 0, prev, 0.0)
        ext_ref[HALO:HALO + ts, :] = a_val * sg
        dcm = dcm_ref[...]
        dext_ref[0:ts, :] = dcm
        dext_ref[ts:ts + HALO, :] = jnp.where(step < nsteps - 1, dcn_ref[...], 0.0)
        db_ref[...] += _colsum(dcm)
        for l0 in range(0, D, CB):
            lanes = slice(l0, l0 + CB)
            _shifted_copies(dext_ref, dsh_ref, lanes, ts)
            for r0 in range(0, ts, CB):
                acc = jnp.zeros((CB, CB), F32)
                for k in range(CW):
                    acc = acc + cw_ref[k:k + 1, lanes] * _window(dext_ref, dsh_ref, lanes, CW - 1 - k, r0, CB)
                dglu_ref[r0:r0 + CB, lanes] = acc
            _shifted_copies(ext_ref, sh_ref, lanes, ts)
            accs = [jnp.zeros((SUB, CB), F32) for _ in range(CW)]
            for r0 in range(0, ts, rb):
                dcb = dext_ref[r0:r0 + rb, lanes]
                for k in range(CW):
                    prod = dcb * _window(ext_ref, sh_ref, lanes, HALO - (CW - 1) + k, r0, rb)
                    accs[k] = accs[k] + jnp.sum(prod.reshape(rb // SUB, SUB, CB), axis=0)
            for k in range(CW):
                dw8_ref[k, :, lanes] += accs[k]
        dglu = dglu_ref[...]
        dp_ref[:, 0:D] = (dglu * sg).astype(BF)
        dp_ref[:, D:2 * D] = (dglu * a_val * sg * (1.0 - sg)).astype(BF)

        @pl.when(step == nsteps - 1)
        def _():
            dw_ref[...] = jnp.zeros_like(dw_ref)
            for k in range(CW):
                dw_ref[k:k + 1, :] = _colsum(dw8_ref[k])

    return pl.pallas_call(
        body, grid=(nsteps,), name="conv_bwd",
        out_shape=(jax.ShapeDtypeStruct((s, DIN), BF), jax.ShapeDtypeStruct((HALO, D), F32),
                   jax.ShapeDtypeStruct((1, D), F32)),
        in_specs=[pl.BlockSpec((ts, 2 * D), lambda i: (i, 0)),
                  pl.BlockSpec((HALO, 2 * D), lambda i: (jnp.maximum(i * per - 1, 0), 0)),
                  _row_spec(ts, D),
                  pl.BlockSpec((HALO, D), lambda i: (jnp.minimum((i + 1) * per, s // HALO - 1), 0)),
                  _const_spec((HALO, D)),
                  pl.BlockSpec(memory_space=pl.ANY)],
        out_specs=(pl.BlockSpec((ts, 2 * D), lambda i: (i, 0)), _acc_spec((HALO, D)), _acc_spec((1, D))),
        scratch_shapes=[pltpu.VMEM((ts + HALO, D), F32), pltpu.VMEM((ts + HALO, D), F32),
                        pltpu.VMEM((HALO, SUB, D), F32),
                        pltpu.VMEM((SUB - 1, ts + SH_ROWS_EXTRA, CB), F32),
                        pltpu.VMEM((SUB - 1, ts + SH_ROWS_EXTRA, CB), F32),
                        pltpu.VMEM((ts, D), F32)],
        input_output_aliases={5: 0},
        compiler_params=_cparams(("arbitrary",)),
    )(p, p, dc, dc, conv_w, dp)


def mix_bwd_in(dp, h, dh, gain, win, rider=None):
    s = h.shape[0]
    ts = _tile(s, 512)

    def body(dp_ref, h_ref, dh_ref, g_ref, win_ref, dx_ref, dg_ref, db_ref):
        @pl.when(pl.program_id(0) == 0)
        def _():
            dg_ref[...] = jnp.zeros_like(dg_ref)
            db_ref[...] = jnp.zeros_like(db_ref)

        gain_v = g_ref[...]
        xh, r = _rms(h_ref[...])
        dn = jnp.zeros((ts, D), F32)
        for k in range(NSH):
            dpk = dp_ref[:, k * INB:(k + 1) * INB]
            dn = dn + _dot_nt(dpk, win_ref[k])
            db_ref[:, k * INB:(k + 1) * INB] += _colsum(dpk.astype(F32))
        dg_ref[...] += _colsum(dn * xh)
        dx_ref[...] = dh_ref[...] + _rms_bwd(dn, xh, r, gain_v)

    return _pcall(
        body, grid=(s // ts,), name="mix_bwd_in", args=(dp, h, dh, gain, win),
        out_shape=(jax.ShapeDtypeStruct((s, D), F32), jax.ShapeDtypeStruct((1, D), F32),
                   jax.ShapeDtypeStruct((1, DIN), F32)),
        in_specs=[_row_spec(ts, DIN), _row_spec(ts, D), _row_spec(ts, D), _const_spec((1, D)),
                  _const_spec((NSH, D, INB))],
        out_specs=(_row_spec(ts, D), _acc_spec((1, D)), _acc_spec((1, DIN))), rider=rider)


def kv_proj(mem, gain, wkv):
    def body(m_ref, g_ref, w_ref, k_ref, v_ref, n_ref):
        xh, _ = _rms(m_ref[...])
        nb = (xh * g_ref[...]).astype(BF)
        n_ref[...] = nb
        half = D // 2
        for j in range(2):
            k_ref[:, j * half:(j + 1) * half] = _dot(nb, w_ref[j]).astype(BF)
            v_ref[:, j * half:(j + 1) * half] = _dot(nb, w_ref[2 + j]).astype(BF)

    o = jax.ShapeDtypeStruct((NMEM, D), BF)
    return pl.pallas_call(body, name="kv_proj", out_shape=(o, o, o), compiler_params=_cparams())(mem, gain, wkv)


def kv_bwd(mem, gain, memn, wkv, dk, dv):
    def body(m_ref, g_ref, n_ref, w_ref, dk_ref, dv_ref, dw_ref, dwb_ref, dg_ref):
        xh, _ = _rms(m_ref[...])
        nb = n_ref[...]
        half = D // 2
        dn = jnp.zeros((NMEM, D), F32)
        for j in range(2):
            dkb = dk_ref[:, j * half:(j + 1) * half].astype(BF)
            dvb = dv_ref[:, j * half:(j + 1) * half].astype(BF)
            for slot, dyb in ((j, dkb), (2 + j, dvb)):
                dw = _dot_tn(nb, dyb)
                dw_ref[slot] = dw
                dwb_ref[slot] = dw.astype(BF)
            dn = dn + _dot_nt(dkb, w_ref[j]) + _dot_nt(dvb, w_ref[2 + j])
        dg_ref[...] = _colsum(dn * xh)

    return pl.pallas_call(
        body, name="kv_bwd",
        out_shape=(jax.ShapeDtypeStruct((NSH, D, D // 2), F32), jax.ShapeDtypeStruct((NSH, D, D // 2), BF),
                   jax.ShapeDtypeStruct((1, D), F32)),
        compiler_params=_cparams())(mem, gain, memn, wkv, dk, dv)


def _attend(qb, k_ref, v_ref, h):
    cols = slice(h * HD, (h + 1) * HD)
    sc = _dot_nt(qb[:, cols], k_ref[:, cols]) * ATT_SCALE
    e = jnp.exp(sc - jnp.max(sc, axis=-1, keepdims=True))
    pr = e / jnp.sum(e, axis=-1, keepdims=True)
    return pr, _dot(pr.astype(BF), v_ref[:, cols])


def xattn_fwd(h, gain, wq, k, v, wo):
    s = h.shape[0]
    ts = _tile(s, 1024)

    def body(h_ref, g_ref, wq_ref, k_ref, v_ref, wo_ref, o_ref, att_ref):
        x = h_ref[...]
        xh, _ = _rms(x)
        nb = (xh * g_ref[...]).astype(BF)
        qb = _dot(nb, wq_ref[...]).astype(BF)
        for hd in range(NH):
            _, oh = _attend(qb, k_ref, v_ref, hd)
            att_ref[:, hd * HD:(hd + 1) * HD] = oh.astype(BF)
        o_ref[...] = x + _dot(att_ref[...], wo_ref[...])

    sq = _const_spec((D, D))
    kvs = _const_spec((NMEM, D))
    return pl.pallas_call(
        body, grid=(s // ts,), name="xattn_fwd",
        out_shape=jax.ShapeDtypeStruct((s, D), F32),
        in_specs=[_row_spec(ts, D), _const_spec((1, D)), sq, kvs, kvs, sq],
        out_specs=_row_spec(ts, D),
        scratch_shapes=[pltpu.VMEM((ts, D), BF)],
        compiler_params=_cparams(("arbitrary",)),
    )(h, gain, wq, k, v, wo)


def xattn_bwd(h, dh, gain, wq, k, v, wo, rider=None):
    s = h.shape[0]
    ts = _tile(s, 512)

    def body(h_ref, dh_ref, g_ref, wq_ref, k_ref, v_ref, wo_ref,
             dx_ref, n_ref, dq_ref, att_ref, dhb_ref, dk_ref, dv_ref, dg_ref):
        @pl.when(pl.program_id(0) == 0)
        def _():
            for ref in (dk_ref, dv_ref, dg_ref):
                ref[...] = jnp.zeros_like(ref)

        x = h_ref[...]
        dh = dh_ref[...]
        gain_v = g_ref[...]
        xh, r = _rms(x)
        nb = (xh * gain_v).astype(BF)
        n_ref[...] = nb
        qb = _dot(nb, wq_ref[...]).astype(BF)
        dhb = dh.astype(BF)
        dhb_ref[...] = dhb
        dob = _dot_nt(dhb, wo_ref[...]).astype(BF)
        for hd in range(NH):
            cols = slice(hd * HD, (hd + 1) * HD)
            pr, oh = _attend(qb, k_ref, v_ref, hd)
            att_ref[:, cols] = oh.astype(BF)
            doh = dob[:, cols]
            dpr = _dot_nt(doh, v_ref[:, cols])
            dv_ref[:, cols] += _dot_tn(pr.astype(BF), doh)
            dsc = (pr * (dpr - jnp.sum(dpr * pr, axis=-1, keepdims=True)) * ATT_SCALE).astype(BF)
            dq_ref[:, cols] = _dot(dsc, k_ref[:, cols]).astype(BF)
            dk_ref[:, cols] += _dot_tn(dsc, qb[:, cols])
        dn = _dot_nt(dq_ref[...], wq_ref[...])
        dg_ref[...] += _colsum(dn * xh)
        dx_ref[...] = dh + _rms_bwd(dn, xh, r, gain_v)

    sq = _const_spec((D, D))
    kvs = _const_spec((NMEM, D))
    bf_rows = jax.ShapeDtypeStruct((s, D), BF)
    kv_acc = jax.ShapeDtypeStruct((NMEM, D), F32)
    return _pcall(
        body, grid=(s // ts,), name="xattn_bwd", args=(h, dh, gain, wq, k, v, wo),
        out_shape=(jax.ShapeDtypeStruct((s, D), F32), bf_rows, bf_rows, bf_rows, bf_rows, kv_acc, kv_acc,
                   jax.ShapeDtypeStruct((1, D), F32)),
        in_specs=[_row_spec(ts, D), _row_spec(ts, D), _const_spec((1, D)), sq, kvs, kvs, sq],
        out_specs=(_row_spec(ts, D),) * 5 + (_acc_spec((NMEM, D)), _acc_spec((NMEM, D)), _acc_spec((1, D))),
        rider=rider)


BLOCK_BYTES = 3 << 19


def _row_block(rows, cols):
    rb = rows
    while rb * cols * 4 > BLOCK_BYTES and rb % 32 == 0:
        rb //= 2
    return rb


def cast_bf16(w, chip, name):
    r, c = w.shape
    rb = _row_block(r, c)

    def body(chip_ref, w_ref, o_ref):
        del chip_ref
        o_ref[0] = w_ref[...].astype(BF)

    return pl.pallas_call(
        body, name=name, out_shape=jax.ShapeDtypeStruct((NSH, r, c), BF),
        grid_spec=pltpu.PrefetchScalarGridSpec(
            num_scalar_prefetch=1, grid=(r // rb,),
            in_specs=[pl.BlockSpec((rb, c), lambda i, chip_ref: (i, 0))],
            out_specs=pl.BlockSpec((1, rb, c), lambda i, chip_ref: (chip_ref[0], i, 0))),
        compiler_params=_cparams(("arbitrary",)))(chip, w)


NDEV = 8


def device_sum(g4, recv, place, name):
    _, _, rh, c = g4.shape
    rb = _row_block(rh, c)

    def body(place_ref, g_ref, r_ref, o_ref):
        del place_ref
        acc = g_ref[0, 0]
        for j in range(NDEV - 1):
            acc = acc + r_ref[j].astype(F32)
        o_ref[0] = acc

    return pl.pallas_call(
        body, name=name, out_shape=jax.ShapeDtypeStruct((2, rh, c), F32),
        grid_spec=pltpu.PrefetchScalarGridSpec(
            num_scalar_prefetch=1, grid=(rh // rb,),
            in_specs=[pl.BlockSpec((1, 1, rb, c), lambda i, place_ref: (place_ref[0], place_ref[1], i, 0)),
                      pl.BlockSpec((NDEV - 1, rb, c), lambda i, place_ref: (0, i, 0))],
            out_specs=pl.BlockSpec((1, rb, c), lambda i, place_ref: (place_ref[1], i, 0))),
        compiler_params=_cparams(("arbitrary",)))(place, g4, recv)


def _adamw_math(w, g, m, v):
    m = ADAM_B1 * m + (1.0 - ADAM_B1) * g
    v = ADAM_B2 * v + (1.0 - ADAM_B2) * (g * g)
    m_hat = m / (1.0 - ADAM_B1 ** ADAM_STEP)
    v_hat = v / (1.0 - ADAM_B2 ** ADAM_STEP)
    delta = -ADAM_LR * (m_hat / (jnp.sqrt(v_hat) + ADAM_EPS) + ADAM_WD * w)
    return delta, m, v


def adamw(w, g, m, v, name):
    r, c = w.shape
    rb = _row_block(r, c)

    def body(w_ref, g_ref, m_ref, v_ref, d_ref, mo_ref, vo_ref):
        d, mn, vn = _adamw_math(w_ref[...], g_ref[...], m_ref[...], v_ref[...])
        d_ref[...] = d
        mo_ref[...] = mn
        vo_ref[...] = vn

    o = jax.ShapeDtypeStruct((r, c), F32)
    spec = _row_spec(rb, c)
    return pl.pallas_call(
        body, grid=(r // rb,), name=name, out_shape=(o, o, o),
        in_specs=[spec] * 4, out_specs=(spec,) * 3,
        compiler_params=_cparams(("arbitrary",)))(w, g, m, v)


def _place():
    return lax.axis_index("x"), lax.axis_index("y"), lax.axis_index("c")


def _other_chips(x, y):
    return [(1 - x, y), (x, 1 - y), (1 - x, 1 - y)]


NOTHER = NSH - 1


def gather_rider(arrays):
    nw = len(arrays)
    nici = nw * NOTHER

    def copies(refs, send_sems, recv_sems):
        x, y, c = _place()
        ici, d2d = [], []
        for w in range(nw):
            for j, (px, py) in enumerate(_other_chips(x, y)):
                n = w * NOTHER + j
                sems = dict(send_sem=send_sems.at[n], recv_sem=recv_sems.at[n],
                            device_id=(px, py, c), device_id_type=MESH)
                mine = refs[w].at[2 * x + y, c]
                theirs = refs[w].at[2 * px + py, c]
                ici.append((pltpu.make_async_remote_copy(src_ref=mine, dst_ref=mine, **sems),
                            pltpu.make_async_remote_copy(src_ref=mine, dst_ref=theirs, **sems)))
                sems = dict(send_sem=send_sems.at[nici + n], recv_sem=recv_sems.at[nici + n],
                            device_id=(x, y, 1 - c), device_id_type=MESH)
                d2d.append((pltpu.make_async_remote_copy(src_ref=theirs, dst_ref=theirs, **sems),
                            pltpu.make_async_remote_copy(src_ref=theirs, dst_ref=refs[w].at[2 * px + py, 1 - c],
                                                         **sems)))
        return ici, d2d

    def start(ins, outs, send_sems, recv_sems):
        ici, _ = copies(outs, send_sems, recv_sems)
        for send, _ in ici:
            send.start()

    def relay(ins, outs, send_sems, recv_sems):
        ici, d2d = copies(outs, send_sems, recv_sems)
        for (_, landed), (forward, _) in zip(ici, d2d):
            landed.wait_recv()
            forward.start()

    def finish(ins, outs, send_sems, recv_sems):
        ici, d2d = copies(outs, send_sems, recv_sems)
        for _, landed in d2d:
            landed.wait_recv()
        for send, _ in ici + d2d:
            send.wait_send()

    return Rider(arrays, [jax.ShapeDtypeStruct(a.shape, a.dtype) for a in arrays], {i: i for i in range(nw)},
                 2 * nici, start, finish, relay)


def _peers(x, y, c):
    return [(x ^ (rel >> 2), y ^ ((rel >> 1) & 1), c ^ (rel & 1)) for rel in range(1, NDEV)]


def reduce_rider(grads):
    nw = len(grads)
    npeer = NDEV - 1

    def copies(ins, outs, send_sems, recv_sems):
        x, y, c = _place()
        return [pltpu.make_async_remote_copy(
            src_ref=ins[w].at[2 * px + py, pc], dst_ref=outs[w].at[r],
            send_sem=send_sems.at[w * npeer + r], recv_sem=recv_sems.at[w * npeer + r],
            device_id=(px, py, pc), device_id_type=MESH)
            for w in range(nw) for r, (px, py, pc) in enumerate(_peers(x, y, c))]

    def start(ins, outs, send_sems, recv_sems):
        for cp in copies(ins, outs, send_sems, recv_sems):
            cp.start()

    def finish(ins, outs, send_sems, recv_sems):
        for cp in copies(ins, outs, send_sems, recv_sems):
            cp.wait()

    return Rider(grads, [jax.ShapeDtypeStruct((npeer,) + g.shape[2:], g.dtype) for g in grads], {},
                 nw * npeer, start, finish)


class _Offset:
    def __init__(self, ref, base):
        self.ref, self.base = ref, base

    @property
    def at(self):
        return self

    def __getitem__(self, i):
        return self.ref.at[self.base + i]


def merge_riders(riders):
    ins, outs, aliases, spans, nsem = [], [], {}, [], 0
    for r in riders:
        spans.append((len(ins), len(outs), nsem))
        aliases.update({len(ins) + i: len(outs) + j for i, j in r.aliases.items()})
        ins, outs, nsem = ins + r.ins, outs + r.outs, nsem + r.nsem

    def each(step):
        def run(in_refs, out_refs, send_sems, recv_sems):
            for r, (i0, o0, s0) in zip(riders, spans):
                if getattr(r, step) is not None:
                    getattr(r, step)(in_refs[i0:i0 + len(r.ins)], out_refs[o0:o0 + len(r.outs)],
                                     _Offset(send_sems, s0), _Offset(recv_sems, s0))
        return run

    relay = each("relay") if any(r.relay is not None for r in riders) else None
    return Rider(ins, outs, aliases, nsem, each("start"), each("finish"), relay)


def split_results(riders, results):
    out, o0 = [], 0
    for r in riders:
        out.append(tuple(results[o0:o0 + len(r.outs)]))
        o0 += len(r.outs)
    return out


def swap_rider(halves):
    nw = len(halves)

    def copies(refs, send_sems, recv_sems):
        x, y, c = _place()
        out = []
        for w in range(nw):
            sems = dict(send_sem=send_sems.at[w], recv_sem=recv_sems.at[w],
                        device_id=(x, y, 1 - c), device_id_type=MESH)
            mine = refs[w].at[c]
            out.append((pltpu.make_async_remote_copy(src_ref=mine, dst_ref=mine, **sems),
                        pltpu.make_async_remote_copy(src_ref=mine, dst_ref=refs[w].at[1 - c], **sems)))
        return out

    def start(ins, outs, send_sems, recv_sems):
        for send, _ in copies(outs, send_sems, recv_sems):
            send.start()

    def finish(ins, outs, send_sems, recv_sems):
        cps = copies(outs, send_sems, recv_sems)
        for _, recv in cps:
            recv.wait_recv()
        for send, _ in cps:
            send.wait_send()

    return Rider(halves, [jax.ShapeDtypeStruct(h.shape, h.dtype) for h in halves], {i: i for i in range(nw)},
                 nw, start, finish)


def allgather_rider(slots):
    def copies(ref, send_sems, recv_sems):
        x, y, c = _place()
        mine = ref.at[4 * x + 2 * y + c]
        out = []
        for r, peer in enumerate(_peers(x, y, c)):
            sems = dict(send_sem=send_sems.at[r], recv_sem=recv_sems.at[r], device_id=peer, device_id_type=MESH)
            out.append((pltpu.make_async_remote_copy(src_ref=mine, dst_ref=mine, **sems),
                        pltpu.make_async_remote_copy(
                            src_ref=mine, dst_ref=ref.at[4 * peer[0] + 2 * peer[1] + peer[2]], **sems)))
        return out

    def start(ins, outs, send_sems, recv_sems):
        for send, _ in copies(outs[0], send_sems, recv_sems):
            send.start()

    def finish(ins, outs, send_sems, recv_sems):
        cps = copies(outs[0], send_sems, recv_sems)
        for _, recv in cps:
            recv.wait_recv()
        for send, _ in cps:
            send.wait_send()

    return Rider([slots], [jax.ShapeDtypeStruct(slots.shape, slots.dtype)], {0: 0}, NDEV - 1, start, finish)


def sum_slots(slots):
    def body(s_ref, o_ref):
        acc = s_ref[0]
        for dev in range(1, NDEV):
            acc = acc + s_ref[dev]
        o_ref[...] = acc

    return pl.pallas_call(body, name="sum_slots", out_shape=jax.ShapeDtypeStruct(slots.shape[1:], F32),
                          compiler_params=_cparams())(slots)


BIG = ("ffn1_w_gu", "ffn1_w_down", "w_in", "w_a_out", "w_b_out", "w_out", "w_q", "w_kv", "w_o",
       "ffn2_w_gu", "ffn2_w_down")
SMALL = {"ffn1_norm": (0, 1), "mix_norm": (8, 1), "xattn_norm": (16, 1), "mem_norm": (24, 1),
         "ffn2_norm": (32, 1), "final_norm": (40, 1), "conv_b": (48, 1), "conv_ln_g": (56, 1),
         "conv_ln_b": (64, 1), "sgu_ln_g": (72, 1), "sgu_ln_b": (80, 1), "b_in": (88, 6),
         "conv_w": (96, CW), "sgu_w": (128, 64), "sgu_b": (192, 1)}
LOSS_ROW = 200
SMALL_ROWS = 208


def _pad_rows(a, rows):
    return jnp.pad(a, ((0, rows - a.shape[0]), (0, D - a.shape[1])))


def _pack_small(parts):
    names = sorted(parts, key=lambda n: SMALL[n][0] if n in SMALL else LOSS_ROW)
    rows = []
    for i, n in enumerate(names):
        start = SMALL[n][0] if n in SMALL else LOSS_ROW
        end = SMALL_ROWS if i + 1 == len(names) else (SMALL[names[i + 1]][0] if names[i + 1] in SMALL else LOSS_ROW)
        rows.append(_pad_rows(parts[n], end - start))
    return jnp.concatenate(rows, axis=0)


def _small_views(w):
    return {
        "ffn1_norm": w["ffn1_norm"], "mix_norm": w["mix_norm"], "xattn_norm": w["xattn_norm"],
        "mem_norm": w["mem_norm"], "ffn2_norm": w["ffn2_norm"], "final_norm": w["final_norm"].reshape(1, D),
        "conv_b": w["conv_b"], "conv_ln_g": w["conv_ln_g"], "conv_ln_b": w["conv_ln_b"],
        "sgu_ln_g": w["sgu_ln_g"], "sgu_ln_b": w["sgu_ln_b"], "b_in": w["b_in"].reshape(6, D),
        "conv_w": w["conv_w"][0], "sgu_w": w["sgu_w"].reshape(64, D), "sgu_b": w["sgu_b"].reshape(1, NG * CHUNK),
    }


def _unpack_small(buf, like, chip):
    out = {}
    for n, (start, rows) in SMALL.items():
        blk = buf[start:start + rows]
        if n == "conv_w":
            blk = blk[:, :like[n].shape[-1]] if chip is None else lax.dynamic_slice_in_dim(
                blk, chip * like[n].shape[-1], like[n].shape[-1], axis=1)
        elif n == "sgu_b":
            blk = blk[:, :NG * CHUNK]
        out[n] = blk.reshape(like[n].shape)
    return out


def kernel(x, mem, ffn1_norm, ffn1_w_gu, ffn1_w_down, mix_norm, w_in, b_in, conv_w, conv_b, conv_ln_g, conv_ln_b, w_a_out, sgu_ln_g, sgu_ln_b, sgu_w, sgu_b, w_b_out, w_out, xattn_norm, mem_norm, w_q, w_kv, w_o, ffn2_norm, ffn2_w_gu, ffn2_w_down, final_norm, loss_target, m_ffn1_norm, m_ffn1_w_gu, m_ffn1_w_down, m_mix_norm, m_w_in, m_b_in, m_conv_w, m_conv_b, m_conv_ln_g, m_conv_ln_b, m_w_a_out, m_sgu_ln_g, m_sgu_ln_b, m_sgu_w, m_sgu_b, m_w_b_out, m_w_out, m_xattn_norm, m_mem_norm, m_w_q, m_w_kv, m_w_o, m_ffn2_norm, m_ffn2_w_gu, m_ffn2_w_down, m_final_norm, v_ffn1_norm, v_ffn1_w_gu, v_ffn1_w_down, v_mix_norm, v_w_in, v_b_in, v_conv_w, v_conv_b, v_conv_ln_g, v_conv_ln_b, v_w_a_out, v_sgu_ln_g, v_sgu_ln_b, v_sgu_w, v_sgu_b, v_w_b_out, v_w_out, v_xattn_norm, v_mem_norm, v_w_q, v_w_kv, v_w_o, v_ffn2_norm, v_ffn2_w_gu, v_ffn2_w_down, v_final_norm):
    names = ("ffn1_norm", "ffn1_w_gu", "ffn1_w_down", "mix_norm", "w_in", "b_in", "conv_w", "conv_b",
             "conv_ln_g", "conv_ln_b", "w_a_out", "sgu_ln_g", "sgu_ln_b", "sgu_w", "sgu_b", "w_b_out", "w_out",
             "xattn_norm", "mem_norm", "w_q", "w_kv", "w_o", "ffn2_norm", "ffn2_w_gu", "ffn2_w_down",
             "final_norm")
    wts = dict(zip(names, (ffn1_norm, ffn1_w_gu, ffn1_w_down, mix_norm, w_in, b_in, conv_w, conv_b, conv_ln_g,
                           conv_ln_b, w_a_out, sgu_ln_g, sgu_ln_b, sgu_w, sgu_b, w_b_out, w_out, xattn_norm,
                           mem_norm, w_q, w_kv, w_o, ffn2_norm, ffn2_w_gu, ffn2_w_down, final_norm)))
    mom1 = dict(zip(names, (m_ffn1_norm, m_ffn1_w_gu, m_ffn1_w_down, m_mix_norm, m_w_in, m_b_in, m_conv_w,
                            m_conv_b, m_conv_ln_g, m_conv_ln_b, m_w_a_out, m_sgu_ln_g, m_sgu_ln_b, m_sgu_w,
                            m_sgu_b, m_w_b_out, m_w_out, m_xattn_norm, m_mem_norm, m_w_q, m_w_kv, m_w_o,
                            m_ffn2_norm, m_ffn2_w_gu, m_ffn2_w_down, m_final_norm)))
    mom2 = dict(zip(names, (v_ffn1_norm, v_ffn1_w_gu, v_ffn1_w_down, v_mix_norm, v_w_in, v_b_in, v_conv_w,
                            v_conv_b, v_conv_ln_g, v_conv_ln_b, v_w_a_out, v_sgu_ln_g, v_sgu_ln_b, v_sgu_w,
                            v_sgu_b, v_w_b_out, v_w_out, v_xattn_norm, v_mem_norm, v_w_q, v_w_kv, v_w_o,
                            v_ffn2_norm, v_ffn2_w_gu, v_ffn2_w_down, v_final_norm)))
    xi, yi, ci = _place()
    chip = (2 * xi + yi).astype(jnp.int32)
    core = ci.astype(jnp.int32)
    chip_arr = chip.reshape(1)
    place_arr = jnp.stack([chip, core])
    x2, mem2, tgt = x[0], mem[0], loss_target[0]

    slot = {n: cast_bf16(wts[n][0], chip_arr, "cast_" + n) for n in BIG}
    cw_pad = jnp.pad(conv_w[0], ((0, HALO - CW), (0, 0)))
    slot["conv_w"] = lax.dynamic_update_slice(jnp.zeros((NSH,) + cw_pad.shape, F32), cw_pad[None], (chip, 0, 0))
    g_first = ("ffn1_w_gu",)
    g_mix = ("ffn1_w_down", "w_in", "w_a_out", "w_b_out", "w_out", "conv_w")
    g_rest = ("w_q", "w_kv", "w_o", "ffn2_w_gu", "ffn2_w_down")

    def gather(group):
        return gather_rider([slot[n].reshape(NSH, 2, slot[n].shape[1] // 2, slot[n].shape[2]) for n in group])

    def gathered(group, res):
        return {n: r.reshape(slot[n].shape) for n, r in zip(group, res)}

    full = gathered(g_first, run_rider(gather(g_first), "gather_ffn1"))
    wgu1 = full["ffn1_w_gu"]
    tril = jnp.tril(jnp.ones((CHUNK, CHUNK), dtype=bool))
    ws = jnp.where(tril[None], sgu_w[0], 0.0).astype(BF)
    wst = jnp.transpose(ws, (0, 2, 1))
    sbias = jnp.repeat(jnp.transpose(sgu_b[0]), GD, axis=1)
    gfin = final_norm.reshape(1, D)

    (gu1, act1), rode = ffn_hidden(x2, ffn1_norm, wgu1, rider=gather(g_mix))
    full.update(gathered(g_mix, rode))
    wd1 = full["ffn1_w_down"].reshape(FF, D)
    h1 = ffn_down(x2, act1, wd1)
    win = full["w_in"]
    wa, wb, wout = (full[n].reshape(D, D) for n in ("w_a_out", "w_b_out", "w_out"))
    cw_full = jnp.transpose(full["conv_w"], (1, 0, 2)).reshape(HALO, D)
    (h2, proj, n2b, conv_out), rode = mix_fwd(
        h1, mix_norm, win, b_in, cw_full, conv_b, conv_ln_g, conv_ln_b, wa, sgu_ln_g, sgu_ln_b, ws, sbias, wb,
        wout, rider=gather(g_rest))
    full.update(gathered(g_rest, rode))
    wgu2, wd2, wkv = full["ffn2_w_gu"], full["ffn2_w_down"].reshape(FF, D), full["w_kv"]
    wq, wo = full["w_q"].reshape(D, D), full["w_o"].reshape(D, D)
    kb, vb, memn = kv_proj(mem2, mem_norm, wkv)
    h3 = xattn_fwd(h2, xattn_norm, wq, kb, vb, wo)
    dh4, gu2, loss_lanes, d_final = ffn_fwd_loss(h3, ffn2_norm, wgu2, wd2, gfin, tgt)

    own, halves = {}, {}

    def exchange(group, grads):
        views = []
        for n, (g, gb) in zip(group, grads):
            rs, cs = wts[n].shape[1:]
            own[n] = g.reshape(NSH, 2, rs // 2, cs)
            views.append(gb.reshape(NSH, 2, rs // 2, cs))
        return reduce_rider(views)

    def reduce(group, recv):
        for n, r in zip(group, recv):
            halves[n] = device_sum(own[n], r, place_arr, "device_sum_" + n)

    dh3, n4, a4, dgu4, dhb4, d_ffn2n = ffn_bwd(h3, gu2, dh4, ffn2_norm, wgu2, wd2, "ffn2_bwd")
    g_ffn2 = ("ffn2_w_gu", "ffn2_w_down")
    ride = exchange(g_ffn2, [dw_matmul(n4, dgu4, NSH, "dw_ffn2_gu")[:2], dw_matmul(a4, dhb4, 1, "dw_ffn2_down")[:2]])
    (dh2, n3, dq, att, dhb3, dk, dv, d_xn), rode = xattn_bwd(h2, dh3, xattn_norm, wq, kb, vb, wo, rider=ride)
    reduce(g_ffn2, rode)
    g_att = ("w_q", "w_o", "w_kv")
    d_wkv, d_wkv_b, d_memn = kv_bwd(mem2, mem_norm, memn, wkv, dk, dv)
    ride = exchange(g_att, [dw_matmul(n3, dq, 1, "dw_q")[:2], dw_matmul(att, dhb3, 1, "dw_o")[:2],
                            (d_wkv, d_wkv_b)])
    ((dconv, dproj, sa, dya, ob, dyb, mg, dhb2, d_sgu_w, d_sgu_b, d_lna_g, d_lna_b, d_lnb_g, d_lnb_b),
     rode) = mix_bwd_branches(proj, conv_out, dh2, conv_ln_g, conv_ln_b, wa, sgu_ln_g, sgu_ln_b, ws, wst,
                              sbias, wb, wout, rider=ride)
    reduce(g_att, rode)
    dproj, d_conv_w, d_conv_b = conv_bwd(proj, dconv, dproj, cw_full)
    g_mixw = ("w_in", "w_a_out", "w_b_out")
    ride = exchange(g_mixw, [dw_matmul(n2b, dproj, NSH, "dw_in")[:2], dw_matmul(sa, dya, 1, "dw_a_out")[:2],
                             dw_matmul(ob, dyb, 1, "dw_b_out")[:2]])
    (dh1, d_mixn, d_b_in), rode = mix_bwd_in(dproj, h1, dh2, mix_norm, win, rider=ride)
    reduce(g_mixw, rode)
    dx, n1, a1, dgu1, dhb1, d_ffn1n = ffn_bwd(x2, gu1, dh1, ffn1_norm, wgu1, wd1, "ffn1_bwd")
    small_grads = {
        "ffn1_norm": d_ffn1n, "mix_norm": d_mixn, "xattn_norm": d_xn, "mem_norm": d_memn, "ffn2_norm": d_ffn2n,
        "final_norm": d_final, "conv_b": d_conv_b, "conv_ln_g": d_lna_g, "conv_ln_b": d_lna_b,
        "sgu_ln_g": d_lnb_g, "sgu_ln_b": d_lnb_b, "b_in": d_b_in.reshape(6, D), "conv_w": d_conv_w[:CW],
        "sgu_w": d_sgu_w.reshape(64, D), "sgu_b": jnp.transpose(d_sgu_b[:, :NG]).reshape(1, NG * CHUNK),
        "loss": loss_lanes}
    slots = lax.dynamic_update_slice(jnp.zeros((NDEV, SMALL_ROWS, D), F32), _pack_small(small_grads)[None],
                                     (2 * chip + core, 0, 0))
    d_wgu1, d_wgu1_b, all_slots = dw_matmul(n1, dgu1, NSH, "dw_ffn1_gu", rider=allgather_rider(slots))
    ride = exchange(("ffn1_w_gu",), [(d_wgu1, d_wgu1_b)])
    d_down, d_down_b, rode = dw_matmul(a1, dhb1, 1, "dw_ffn1_down", rider=ride)
    reduce(("ffn1_w_gu",), rode)
    ride = exchange(("ffn1_w_down",), [(d_down, d_down_b)])
    d_out, d_out_b, rode = dw_matmul(mg, dhb2, 1, "dw_out", rider=ride)
    reduce(("ffn1_w_down",), rode)
    reduce(("w_out",), run_rider(exchange(("w_out",), [(d_out, d_out_b)]), "exchange_w_out"))
    swapped = run_rider(swap_rider([halves[n] for n in BIG]), "pair_swap")
    gshard = {n: g.reshape(wts[n].shape[1:]) for n, g in zip(BIG, swapped)}

    small = sum_slots(all_slots[0])
    loss = (0.5 / D) * jnp.sum(small[LOSS_ROW])
    gsmall = _unpack_small(small, wts, chip)

    out_g, out_d, out_m, out_v = dict(gsmall), {}, {}, {}
    sw, sm, sv = (_pack_small(_small_views(t))[:LOSS_ROW] for t in (wts, mom1, mom2))
    sg = _pack_small(_small_views({n: gsmall[n] for n in SMALL}))[:LOSS_ROW]
    for dst, packed in zip((out_d, out_m, out_v), adamw(sw, sg, sm, sv, "adamw_small")):
        dst.update(_unpack_small(packed, wts, None))
    for n in BIG:
        shape = wts[n].shape
        out_g[n] = gshard[n].reshape(shape)
        d, mn, vn = adamw(wts[n][0], gshard[n], mom1[n][0], mom2[n][0], "adamw_" + n)
        out_d[n], out_m[n], out_v[n] = d.reshape(shape), mn.reshape(shape), vn.reshape(shape)
    return (loss, dx[None], *[out_g[n] for n in names], *[out_d[n] for n in names],
            *[out_m[n] for n in names], *[out_v[n] for n in names])
```

```python
import functools
import math

import jax
import jax.numpy as jnp
from jax import lax
from jax.experimental import pallas as pl
from jax.experimental.pallas import tpu as pltpu

F32 = jnp.float32
BF = jnp.bfloat16
MESH = pl.DeviceIdType.MESH

D = 1024
FF = 2816
HC = FF // 2
NSH = 4
DIN = 6 * D
INB = DIN // NSH
CW = 31
HALO = 32
CHUNK = 128
NG = 4
GD = D // NG
NH = 4
HD = D // NH
NMEM = 256
EPS_RMS = 1e-6
EPS_LN = 1e-5
GELU_C0 = math.sqrt(2.0 / math.pi)
GELU_C1 = 0.044715
ATT_SCALE = 1.0 / math.sqrt(HD)

ADAM_LR = 0.001
ADAM_B1 = 0.9
ADAM_B2 = 0.999
ADAM_EPS = 1e-08
ADAM_WD = 0.01
ADAM_STEP = 10

VMEM_LIMIT = 56 * 1024 * 1024


def _cparams(sem=None, **kw):
    if sem is not None:
        kw["dimension_semantics"] = sem
    return pltpu.CompilerParams(vmem_limit_bytes=VMEM_LIMIT, **kw)


def _dot(a, b):
    return jnp.dot(a, b, preferred_element_type=F32)


def _dot_nt(a, b):
    return lax.dot_general(a, b, (((1,), (1,)), ((), ())), preferred_element_type=F32)


def _dot_tn(a, b):
    return lax.dot_general(a, b, (((0,), (0,)), ((), ())), preferred_element_type=F32)


def _sigmoid(x):
    return 1.0 / (1.0 + jnp.exp(-x))


def _gelu(x):
    t = jnp.tanh(GELU_C0 * (x + GELU_C1 * (x * x * x)))
    return 0.5 * x * (1.0 + t), t


def _gelu_with_grad(x):
    x2 = x * x
    t = jnp.tanh(GELU_C0 * (x + GELU_C1 * (x2 * x)))
    onep = 1.0 + t
    hx = 0.5 * x
    grad = 0.5 * onep + hx * (1.0 - t * t) * (GELU_C0 + (3.0 * GELU_C0 * GELU_C1) * x2)
    return hx * onep, grad


def _mean(x):
    return jnp.mean(x, axis=-1, keepdims=True)


def _rms(x):
    r = lax.rsqrt(_mean(x * x) + EPS_RMS)
    return x * r, r


def _rms_bwd(dn, xh, r, g):
    dxh = dn * g
    return r * (dxh - xh * _mean(dxh * xh))


def _ln(x):
    xc = x - _mean(x)
    r = lax.rsqrt(_mean(xc * xc) + EPS_LN)
    return xc * r, r


def _ln_bwd(dy, xh, r, g):
    dxh = dy * g
    return r * (dxh - _mean(dxh) - xh * _mean(dxh * xh))


def _colsum(x):
    return jnp.sum(x, axis=0, keepdims=True)


def _const_spec(shape):
    nd = len(shape)
    return pl.BlockSpec(shape, lambda *_: (0,) * nd, pipeline_mode=pl.Buffered(1))


def _row_spec(ts, width):
    return pl.BlockSpec((ts, width), lambda i: (i, 0))


def _acc_spec(shape):
    nd = len(shape)
    return pl.BlockSpec(shape, lambda *_: (0,) * nd)


def _tile(s, want):
    return min(s, want)


HBM_SPEC = pl.BlockSpec(memory_space=pltpu.HBM)


class Rider:
    def __init__(self, ins, outs, aliases, nsem, start, finish, relay=None):
        self.ins, self.outs, self.aliases, self.nsem = list(ins), list(outs), dict(aliases), nsem
        self.start, self.finish, self.relay = start, finish, relay


def _pcall(body, *, name, grid, args, in_specs, out_shape, out_specs, scratch=(), rider=None):
    sem = ("arbitrary",) * len(grid)
    n_in, n_out = len(args), len(out_shape)
    if rider is None:
        res = pl.pallas_call(
            body, grid=grid, name=name, out_shape=tuple(out_shape), in_specs=list(in_specs),
            out_specs=tuple(out_specs), scratch_shapes=list(scratch), compiler_params=_cparams(sem))(*args)
        return tuple(res), ()
    r_in, r_out = len(rider.ins), len(rider.outs)

    def wrapped(*refs):
        a, ri = refs[:n_in], refs[n_in:n_in + r_in]
        o = refs[n_in + r_in:n_in + r_in + n_out]
        ro = refs[n_in + r_in + n_out:n_in + r_in + n_out + r_out]
        s, (send, recv) = refs[n_in + r_in + n_out + r_out:-2], refs[-2:]
        first = functools.reduce(jnp.logical_and, [pl.program_id(d) == 0 for d in range(len(grid))])
        last = functools.reduce(jnp.logical_and, [pl.program_id(d) == g - 1 for d, g in enumerate(grid)])

        @pl.when(first)
        def _():
            rider.start(ri, ro, send, recv)

        body(*a, *o, *s)

        if rider.relay is not None:
            step = functools.reduce(lambda acc, d: acc * grid[d] + pl.program_id(d), range(len(grid)), 0)

            @pl.when(step == max((3 * math.prod(grid)) // 4 - 1, 0))
            def _():
                rider.relay(ri, ro, send, recv)

        @pl.when(last)
        def _():
            rider.finish(ri, ro, send, recv)

    res = pl.pallas_call(
        wrapped, grid=grid, name=name, out_shape=tuple(out_shape) + tuple(rider.outs),
        in_specs=list(in_specs) + [HBM_SPEC] * r_in, out_specs=tuple(out_specs) + (HBM_SPEC,) * r_out,
        scratch_shapes=list(scratch) + [pltpu.SemaphoreType.DMA((rider.nsem,)),
                                        pltpu.SemaphoreType.DMA((rider.nsem,))],
        input_output_aliases={n_in + i: n_out + j for i, j in rider.aliases.items()},
        compiler_params=_cparams(sem, has_side_effects=True))(*args, *rider.ins)
    return tuple(res[:n_out]), tuple(res[n_out:])


def run_rider(rider, name):
    r_in = len(rider.ins)

    def body(*refs):
        ri, ro, (send, recv) = refs[:r_in], refs[r_in:-2], refs[-2:]
        rider.start(ri, ro, send, recv)
        if rider.relay is not None:
            rider.relay(ri, ro, send, recv)
        rider.finish(ri, ro, send, recv)

    return pl.pallas_call(
        body, name=name, out_shape=tuple(rider.outs), in_specs=[HBM_SPEC] * r_in,
        out_specs=(HBM_SPEC,) * len(rider.outs),
        scratch_shapes=[pltpu.SemaphoreType.DMA((rider.nsem,)), pltpu.SemaphoreType.DMA((rider.nsem,))],
        input_output_aliases=rider.aliases,
        compiler_params=pltpu.CompilerParams(has_side_effects=True))(*rider.ins)


FFN_BWD_TILE = 256


def _ffn_apply(x, g_ref, wgu_ref, wd_ref, gu_ref):
    xh, _ = _rms(x)
    nb = (xh * g_ref[...]).astype(BF)
    acc = jnp.zeros(x.shape, F32)
    for j in range(2):
        g = _dot(nb, wgu_ref[j])
        u = _dot(nb, wgu_ref[2 + j])
        gu_ref[:, j * HC:(j + 1) * HC] = g.astype(BF)
        gu_ref[:, FF + j * HC:FF + (j + 1) * HC] = u.astype(BF)
        a = (g * _sigmoid(g) * u).astype(BF)
        acc = acc + _dot(a, wd_ref[j * HC:(j + 1) * HC, :])
    return x + 0.5 * acc


def ffn_hidden(h, gain, wgu, rider=None):
    s = h.shape[0]
    ts = _tile(s, 512)

    def body(h_ref, g_ref, wgu_ref, gu_ref, a_ref):
        xh, _ = _rms(h_ref[...])
        nb = (xh * g_ref[...]).astype(BF)
        for j in range(2):
            g = _dot(nb, wgu_ref[j])
            u = _dot(nb, wgu_ref[2 + j])
            gu_ref[:, j * HC:(j + 1) * HC] = g.astype(BF)
            gu_ref[:, FF + j * HC:FF + (j + 1) * HC] = u.astype(BF)
            a_ref[:, j * HC:(j + 1) * HC] = (g * _sigmoid(g) * u).astype(BF)

    return _pcall(
        body, grid=(s // ts,), name="ffn1_hidden", args=(h, gain, wgu),
        out_shape=[jax.ShapeDtypeStruct((s, 2 * FF), BF), jax.ShapeDtypeStruct((s, FF), BF)],
        in_specs=[_row_spec(ts, D), _const_spec((1, D)), _const_spec((NSH, D, HC))],
        out_specs=[_row_spec(ts, 2 * FF), _row_spec(ts, FF)], rider=rider)


def ffn_down(h, a, wd):
    s = h.shape[0]
    ts = _tile(s, 512)

    def body(h_ref, a_ref, wd_ref, o_ref):
        o_ref[...] = h_ref[...] + 0.5 * _dot(a_ref[...], wd_ref[...])

    return pl.pallas_call(
        body, grid=(s // ts,), name="ffn1_down", out_shape=jax.ShapeDtypeStruct((s, D), F32),
        in_specs=[_row_spec(ts, D), _row_spec(ts, FF), _const_spec((FF, D))], out_specs=_row_spec(ts, D),
        compiler_params=_cparams(("arbitrary",)))(h, a, wd)


def ffn_fwd_loss(h, gain, wgu, wd, gfin, target):
    s = h.shape[0]
    ts = _tile(s, 512)

    def body(h_ref, g_ref, wgu_ref, wd_ref, gf_ref, t_ref, dh_ref, gu_ref, loss_ref, dgf_ref):
        @pl.when(pl.program_id(0) == 0)
        def _():
            loss_ref[...] = jnp.zeros_like(loss_ref)
            dgf_ref[...] = jnp.zeros_like(dgf_ref)

        h4 = _ffn_apply(h_ref[...], g_ref, wgu_ref, wd_ref, gu_ref)
        yh, r4 = _rms(h4)
        gf = gf_ref[...]
        e = yh * gf - t_ref[...]
        loss_ref[...] += _colsum(e * e)
        dy = e * (1.0 / D)
        dgf_ref[...] += _colsum(dy * yh)
        dh_ref[...] = _rms_bwd(dy, yh, r4, gf)

    return pl.pallas_call(
        body, grid=(s // ts,), name="ffn_fwd_loss",
        out_shape=(jax.ShapeDtypeStruct((s, D), F32), jax.ShapeDtypeStruct((s, 2 * FF), BF),
                   jax.ShapeDtypeStruct((1, D), F32), jax.ShapeDtypeStruct((1, D), F32)),
        in_specs=[_row_spec(ts, D), _const_spec((1, D)), _const_spec((NSH, D, HC)), _const_spec((FF, D)),
                  _const_spec((1, D)), _row_spec(ts, D)],
        out_specs=(_row_spec(ts, D), _row_spec(ts, 2 * FF), _acc_spec((1, D)), _acc_spec((1, D))),
        compiler_params=_cparams(("arbitrary",)),
    )(h, gain, wgu, wd, gfin, target)


def ffn_bwd(h, gu, dh, gain, wgu, wd, name):
    s = h.shape[0]
    ts = _tile(s, FFN_BWD_TILE)

    def body(h_ref, gu_ref, dh_ref, g_ref, wgu_ref, wd_ref, dx_ref, n_ref, a_ref, dgu_ref, dhb_ref, dg_ref):
        @pl.when(pl.program_id(0) == 0)
        def _():
            dg_ref[...] = jnp.zeros_like(dg_ref)

        x = h_ref[...]
        dh = dh_ref[...]
        gain_v = g_ref[...]
        xh, r = _rms(x)
        n_ref[...] = (xh * gain_v).astype(BF)
        dhb = (0.5 * dh).astype(BF)
        dhb_ref[...] = dhb
        dn = jnp.zeros((ts, D), F32)
        for j in range(2):
            g = gu_ref[:, j * HC:(j + 1) * HC].astype(F32)
            u = gu_ref[:, FF + j * HC:FF + (j + 1) * HC].astype(F32)
            sg = _sigmoid(g)
            sl = g * sg
            a_ref[:, j * HC:(j + 1) * HC] = (sl * u).astype(BF)
            da = _dot_nt(dhb, wd_ref[j * HC:(j + 1) * HC, :])
            dgb = (da * u * (sg * (1.0 + g * (1.0 - sg)))).astype(BF)
            dub = (da * sl).astype(BF)
            dgu_ref[:, j * HC:(j + 1) * HC] = dgb
            dgu_ref[:, FF + j * HC:FF + (j + 1) * HC] = dub
            dn = dn + _dot_nt(dgb, wgu_ref[j]) + _dot_nt(dub, wgu_ref[2 + j])
        dg_ref[...] += _colsum(dn * xh)
        dx_ref[...] = dh + _rms_bwd(dn, xh, r, gain_v)

    return pl.pallas_call(
        body, grid=(s // ts,), name=name,
        out_shape=(jax.ShapeDtypeStruct((s, D), F32), jax.ShapeDtypeStruct((s, D), BF),
                   jax.ShapeDtypeStruct((s, FF), BF), jax.ShapeDtypeStruct((s, 2 * FF), BF),
                   jax.ShapeDtypeStruct((s, D), BF), jax.ShapeDtypeStruct((1, D), F32)),
        in_specs=[_row_spec(ts, D), _row_spec(ts, 2 * FF), _row_spec(ts, D), _const_spec((1, D)),
                  _const_spec((NSH, D, HC)), _const_spec((FF, D))],
        out_specs=(_row_spec(ts, D), _row_spec(ts, D), _row_spec(ts, FF), _row_spec(ts, 2 * FF),
                   _row_spec(ts, D), _acc_spec((1, D))),
        compiler_params=_cparams(("arbitrary",)),
    )(h, gu, dh, gain, wgu, wd)


def dw_matmul(x, dy, nsplit, name, rider=None):
    s, k = x.shape
    n = dy.shape[1]
    nb = n // nsplit
    ts = _tile(s, 2048 if k <= D else 1024)
    nsteps = s // ts

    def body(x_ref, dy_ref, o_ref, ob_ref):
        @pl.when(pl.program_id(1) == 0)
        def _():
            o_ref[...] = jnp.zeros_like(o_ref)

        o_ref[0] += _dot_tn(x_ref[...], dy_ref[...])

        @pl.when(pl.program_id(1) == nsteps - 1)
        def _():
            ob_ref[...] = o_ref[...].astype(BF)

    spec = pl.BlockSpec((1, k, nb), lambda j, i: (j, 0, 0))
    (out, outb), rode = _pcall(
        body, grid=(nsplit, nsteps), name=name, args=(x, dy),
        out_shape=[jax.ShapeDtypeStruct((nsplit, k, nb), F32), jax.ShapeDtypeStruct((nsplit, k, nb), BF)],
        in_specs=[pl.BlockSpec((ts, k), lambda j, i: (i, 0)), pl.BlockSpec((ts, nb), lambda j, i: (i, j))],
        out_specs=[spec, spec], rider=rider)
    return out, outb, rode


def _split_in_proj(p, b):
    h = INB - D
    a_val = p[0][:, :D] + b[:, 0:D]
    a_gate = jnp.concatenate([p[0][:, D:], p[1][:, :h]], axis=1) + b[:, D:2 * D]
    b_u = p[1][:, h:] + b[:, 2 * D:3 * D]
    b_v = p[2][:, :D] + b[:, 3 * D:4 * D]
    g_a = jnp.concatenate([p[2][:, D:], p[3][:, :h]], axis=1) + b[:, 4 * D:5 * D]
    g_b = p[3][:, h:] + b[:, 5 * D:6 * D]
    return a_val, a_gate, b_u, b_v, g_a, g_b


def _sgu_mix(vnb, ws_ref, sb_ref, mixed_ref, ts):
    for ci in range(ts // CHUNK):
        rows = slice(ci * CHUNK, (ci + 1) * CHUNK)
        for g in range(NG):
            cols = slice(g * GD, (g + 1) * GD)
            mixed_ref[rows, cols] = _dot(ws_ref[g], vnb[rows, cols]) + sb_ref[:, cols]


SUB = 8
CB = 128
SH_ROWS_EXTRA = HALO - SUB


def _shifted_copies(ext_ref, sh_ref, lanes, ts):
    for b in range(1, SUB):
        sh_ref[b - 1] = ext_ref[b:b + ts + SH_ROWS_EXTRA, lanes]


def _window(ext_ref, sh_ref, lanes, first, r0, nrows):
    b = first % SUB
    a = first - b
    if b == 0:
        return ext_ref[a + r0:a + r0 + nrows, lanes]
    return sh_ref[b - 1, a + r0:a + r0 + nrows, :]


def mix_fwd(h, gain, win, b_in, conv_w, conv_b, lna_g, lna_b, wa, lnb_g, lnb_b, ws, sbias, wb, wo, rider=None):
    s = h.shape[0]
    ts = _tile(s, 256)

    def body(h_ref, g_ref, win_ref, bin_ref, cw_ref, cb_ref, lag_ref, lab_ref, wa_ref, lbg_ref, lbb_ref,
             ws_ref, sb_ref, wb_ref, wo_ref, o_ref, p_ref, n_ref, c_ref, ext_ref, mixed_ref, sh_ref):
        @pl.when(pl.program_id(0) == 0)
        def _():
            ext_ref[0:HALO, :] = jnp.zeros((HALO, D), F32)

        x = h_ref[...]
        xh, _ = _rms(x)
        nb = (xh * g_ref[...]).astype(BF)
        n_ref[...] = nb
        b = bin_ref[...]
        p = []
        for k in range(NSH):
            pk = _dot(nb, win_ref[k])
            p_ref[:, k * INB:(k + 1) * INB] = (pk + b[:, k * INB:(k + 1) * INB]).astype(BF)
            p.append(pk)
        a_val, a_gate, b_u, b_v, g_a, g_b = _split_in_proj(p, b)
        ext_ref[HALO:HALO + ts, :] = a_val * _sigmoid(a_gate)
        for l0 in range(0, D, CB):
            lanes = slice(l0, l0 + CB)
            _shifted_copies(ext_ref, sh_ref, lanes, ts)
            for r0 in range(0, ts, CB):
                acc = jnp.zeros((CB, CB), F32) + cb_ref[:, lanes]
                for k in range(CW):
                    acc = acc + cw_ref[k:k + 1, lanes] * _window(ext_ref, sh_ref, lanes,
                                                                 HALO - (CW - 1) + k, r0, CB)
                c_ref[r0:r0 + CB, lanes] = acc
        ext_ref[0:HALO, :] = ext_ref[ts:ts + HALO, :]
        ch, _ = _ln(c_ref[...])
        la = ch * lag_ref[...] + lab_ref[...]
        sa = (la * _sigmoid(la)).astype(BF)
        ya = _dot(sa, wa_ref[...])
        ub, _ = _gelu(b_u)
        gv, _ = _gelu(b_v)
        vh, _ = _ln(gv)
        vnb = (vh * lbg_ref[...] + lbb_ref[...]).astype(BF)
        _sgu_mix(vnb, ws_ref, sb_ref, mixed_ref, ts)
        ob = (ub * mixed_ref[...]).astype(BF)
        yb = _dot(ob, wb_ref[...])
        merged = (_sigmoid(g_a) * ya + _sigmoid(g_b) * yb).astype(BF)
        o_ref[...] = x + _dot(merged, wo_ref[...])

    vec = _const_spec((1, D))
    sq = _const_spec((D, D))
    return _pcall(
        body, grid=(s // ts,), name="mix_fwd",
        args=(h, gain, win, b_in, conv_w, conv_b, lna_g, lna_b, wa, lnb_g, lnb_b, ws, sbias, wb, wo),
        out_shape=(jax.ShapeDtypeStruct((s, D), F32), jax.ShapeDtypeStruct((s, DIN), BF),
                   jax.ShapeDtypeStruct((s, D), BF), jax.ShapeDtypeStruct((s, D), F32)),
        in_specs=[_row_spec(ts, D), vec, _const_spec((NSH, D, INB)), _const_spec((1, DIN)),
                  _const_spec((HALO, D)), vec, vec, vec, sq, vec, vec,
                  _const_spec((NG, CHUNK, CHUNK)), _const_spec((CHUNK, D)), sq, sq],
        out_specs=(_row_spec(ts, D), _row_spec(ts, DIN), _row_spec(ts, D), _row_spec(ts, D)),
        scratch=[pltpu.VMEM((ts + HALO, D), F32), pltpu.VMEM((ts, D), F32),
                 pltpu.VMEM((SUB - 1, ts + SH_ROWS_EXTRA, CB), F32)], rider=rider)


def mix_bwd_branches(p, c, dh, lna_g, lna_b, wa, lnb_g, lnb_b, ws, wst, sbias, wb, wo, rider=None):
    s = dh.shape[0]
    ts = _tile(s, 256)
    nsteps = s // ts

    def body(p_ref, c_ref, dh_ref, lag_ref, lab_ref, wa_ref, lbg_ref, lbb_ref, ws_ref, wst_ref, sb_ref,
             wb_ref, wo_ref, dc_ref, dp_ref, sa_ref, dya_ref, ob_ref, dyb_ref, mg_ref, dhb_ref,
             dws_ref, dsb_ref, dlag_ref, dlab_ref, dlbg_ref, dlbb_ref, mixed_ref, dmix_ref, dvn_ref, dsb_acc):
        step = pl.program_id(0)

        @pl.when(step == 0)
        def _():
            for ref in (dws_ref, dsb_acc, dlag_ref, dlab_ref, dlbg_ref, dlbb_ref):
                ref[...] = jnp.zeros_like(ref)

        b_u = p_ref[:, 2 * D:3 * D].astype(F32)
        b_v = p_ref[:, 3 * D:4 * D].astype(F32)
        sga = _sigmoid(p_ref[:, 4 * D:5 * D].astype(F32))
        sgb = _sigmoid(p_ref[:, 5 * D:6 * D].astype(F32))
        lag = lag_ref[...]
        ch, ra = _ln(c_ref[...])
        la = ch * lag + lab_ref[...]
        sla = _sigmoid(la)
        sa = (la * sla).astype(BF)
        sa_ref[...] = sa
        ya = _dot(sa, wa_ref[...])
        lbg = lbg_ref[...]
        ub, dub = _gelu_with_grad(b_u)
        gv, dgv = _gelu_with_grad(b_v)
        vh, rb = _ln(gv)
        vnb = (vh * lbg + lbb_ref[...]).astype(BF)
        _sgu_mix(vnb, ws_ref, sb_ref, mixed_ref, ts)
        mixed = mixed_ref[...]
        ob = (ub * mixed).astype(BF)
        ob_ref[...] = ob
        yb = _dot(ob, wb_ref[...])
        mg_ref[...] = (sga * ya + sgb * yb).astype(BF)
        dhb = dh_ref[...].astype(BF)
        dhb_ref[...] = dhb
        dm = _dot_nt(dhb, wo_ref[...])
        dp_ref[:, 0:2 * D] = jnp.zeros((ts, 2 * D), BF)
        dp_ref[:, 4 * D:5 * D] = (dm * ya * sga * (1.0 - sga)).astype(BF)
        dp_ref[:, 5 * D:6 * D] = (dm * yb * sgb * (1.0 - sgb)).astype(BF)
        dya = (dm * sga).astype(BF)
        dya_ref[...] = dya
        dyb = (dm * sgb).astype(BF)
        dyb_ref[...] = dyb
        dla = _dot_nt(dya, wa_ref[...]) * (sla * (1.0 + la * (1.0 - sla)))
        dlag_ref[...] += _colsum(dla * ch)
        dlab_ref[...] += _colsum(dla)
        dc_ref[...] = _ln_bwd(dla, ch, ra, lag)
        dob = _dot_nt(dyb, wb_ref[...])
        dp_ref[:, 2 * D:3 * D] = (dob * mixed * dub).astype(BF)
        dmix = dob * ub
        dmix_ref[...] = dmix.astype(BF)
        dsb = jnp.zeros((CHUNK, D), F32)
        for ci in range(ts // CHUNK):
            rows = slice(ci * CHUNK, (ci + 1) * CHUNK)
            dsb = dsb + dmix[rows, :]
            for g in range(NG):
                cols = slice(g * GD, (g + 1) * GD)
                dmb = dmix_ref[rows, cols]
                dws_ref[g] += _dot_nt(dmb, vnb[rows, cols])
                dvn_ref[rows, cols] = _dot(wst_ref[g], dmb)
        dsb_acc[...] += dsb
        dvn = dvn_ref[...]
        dlbg_ref[...] += _colsum(dvn * vh)
        dlbb_ref[...] += _colsum(dvn)
        dp_ref[:, 3 * D:4 * D] = (_ln_bwd(dvn, vh, rb, lbg) * dgv).astype(BF)

        @pl.when(step == nsteps - 1)
        def _():
            row = lax.broadcasted_iota(jnp.int32, (CHUNK, CHUNK), 0)
            col = lax.broadcasted_iota(jnp.int32, (CHUNK, CHUNK), 1)
            for g in range(NG):
                dws_ref[g] = jnp.where(col <= row, dws_ref[g], 0.0)
            acc = jnp.zeros((CHUNK, CHUNK), F32)
            for g in range(NG):
                tot = jnp.sum(dsb_acc[:, g * GD:(g + 1) * GD], axis=-1, keepdims=True)
                acc = acc + jnp.where(col == g, tot, 0.0)
            dsb_ref[...] = acc

    vec = _const_spec((1, D))
    sq = _const_spec((D, D))
    bf_rows = jax.ShapeDtypeStruct((s, D), BF)
    acc_vec = jax.ShapeDtypeStruct((1, D), F32)
    return _pcall(
        body, grid=(nsteps,), name="mix_bwd_branches",
        args=(p, c, dh, lna_g, lna_b, wa, lnb_g, lnb_b, ws, wst, sbias, wb, wo),
        out_shape=(jax.ShapeDtypeStruct((s, D), F32), jax.ShapeDtypeStruct((s, DIN), BF),
                   bf_rows, bf_rows, bf_rows, bf_rows, bf_rows, bf_rows,
                   jax.ShapeDtypeStruct((NG, CHUNK, CHUNK), F32), jax.ShapeDtypeStruct((CHUNK, CHUNK), F32),
                   acc_vec, acc_vec, acc_vec, acc_vec),
        in_specs=[_row_spec(ts, DIN), _row_spec(ts, D), _row_spec(ts, D), vec, vec, sq, vec, vec,
                  _const_spec((NG, CHUNK, CHUNK)), _const_spec((NG, CHUNK, CHUNK)), _const_spec((CHUNK, D)),
                  sq, sq],
        out_specs=(_row_spec(ts, D), _row_spec(ts, DIN)) + (_row_spec(ts, D),) * 6
        + (_acc_spec((NG, CHUNK, CHUNK)), _acc_spec((CHUNK, CHUNK))) + (_acc_spec((1, D)),) * 4,
        scratch=[pltpu.VMEM((ts, D), F32), pltpu.VMEM((ts, D), BF), pltpu.VMEM((ts, D), F32),
                 pltpu.VMEM((CHUNK, D), F32)], rider=rider)


def conv_bwd(p, dc, dp, conv_w):
    s = dc.shape[0]
    ts = _tile(s, 256)
    nsteps = s // ts
    per = ts // HALO

    rb = 16

    def body(pm_ref, pp_ref, dcm_ref, dcn_ref, cw_ref, dpin_ref, dp_ref, dw_ref, db_ref, ext_ref, dext_ref,
             dw8_ref, sh_ref, dsh_ref, dglu_ref):
        del dpin_ref
        step = pl.program_id(0)

        @pl.when(step == 0)
        def _():
            dw8_ref[...] = jnp.zeros_like(dw8_ref)
            db_ref[...] = jnp.zeros_like(db_ref)

        a_val = pm_ref[:, 0:D].astype(F32)
        sg = _sigmoid(pm_ref[:, D:2 * D].astype(F32))
        prev = pp_ref[:, 0:D].astype(F32) * _sigmoid(pp_ref[:, D:2 * D].astype(F32))
        ext_ref[0:HALO, :] = jnp.where(step > 0, prev, 0.0)
        ext_ref[HALO:HALO + ts, :] = a_val * sg
        dcm = dcm_ref[...]
        dext_ref[0:ts, :] = dcm
        dext_ref[ts:ts + HALO, :] = jnp.where(step < nsteps - 1, dcn_ref[...], 0.0)
        db_ref[...] += _colsum(dcm)
        for l0 in range(0, D, CB):
            lanes = slice(l0, l0 + CB)
            _shifted_copies(dext_ref, dsh_ref, lanes, ts)
            for r0 in range(0, ts, CB):
                acc = jnp.zeros((CB, CB), F32)
                for k in range(CW):
                    acc = acc + cw_ref[k:k + 1, lanes] * _window(dext_ref, dsh_ref, lanes, CW - 1 - k, r0, CB)
                dglu_ref[r0:r0 + CB, lanes] = acc
            _shifted_copies(ext_ref, sh_ref, lanes, ts)
            accs = [jnp.zeros((SUB, CB), F32) for _ in range(CW)]
            for r0 in range(0, ts, rb):
                dcb = dext_ref[r0:r0 + rb, lanes]
                for k in range(CW):
                    prod = dcb * _window(ext_ref, sh_ref, lanes, HALO - (CW - 1) + k, r0, rb)
                    accs[k] = accs[k] + jnp.sum(prod.reshape(rb // SUB, SUB, CB), axis=0)
            for k in range(CW):
                dw8_ref[k, :, lanes] += accs[k]
        dglu = dglu_ref[...]
        dp_ref[:, 0:D] = (dglu * sg).astype(BF)
        dp_ref[:, D:2 * D] = (dglu * a_val * sg * (1.0 - sg)).astype(BF)

        @pl.when(step == nsteps - 1)
        def _():
            dw_ref[...] = jnp.zeros_like(dw_ref)
            for k in range(CW):
                dw_ref[k:k + 1, :] = _colsum(dw8_ref[k])

    return pl.pallas_call(
        body, grid=(nsteps,), name="conv_bwd",
        out_shape=(jax.ShapeDtypeStruct((s, DIN), BF), jax.ShapeDtypeStruct((HALO, D), F32),
                   jax.ShapeDtypeStruct((1, D), F32)),
        in_specs=[pl.BlockSpec((ts, 2 * D), lambda i: (i, 0)),
                  pl.BlockSpec((HALO, 2 * D), lambda i: (jnp.maximum(i * per - 1, 0), 0)),
                  _row_spec(ts, D),
                  pl.BlockSpec((HALO, D), lambda i: (jnp.minimum((i + 1) * per, s // HALO - 1), 0)),
                  _const_spec((HALO, D)),
                  pl.BlockSpec(memory_space=pl.ANY)],
        out_specs=(pl.BlockSpec((ts, 2 * D), lambda i: (i, 0)), _acc_spec((HALO, D)), _acc_spec((1, D))),
        scratch_shapes=[pltpu.VMEM((ts + HALO, D), F32), pltpu.VMEM((ts + HALO, D), F32),
                        pltpu.VMEM((HALO, SUB, D), F32),
                        pltpu.VMEM((SUB - 1, ts + SH_ROWS_EXTRA, CB), F32),
                        pltpu.VMEM((SUB - 1, ts + SH_ROWS_EXTRA, CB), F32),
                        pltpu.VMEM((ts, D), F32)],
        input_output_aliases={5: 0},
        compiler_params=_cparams(("arbitrary",)),
    )(p, p, dc, dc, conv_w, dp)


def mix_bwd_in(dp, h, dh, gain, win, rider=None):
    s = h.shape[0]
    ts = _tile(s, 512)

    def body(dp_ref, h_ref, dh_ref, g_ref, win_ref, dx_ref, dg_ref, db_ref):
        @pl.when(pl.program_id(0) == 0)
        def _():
            dg_ref[...] = jnp.zeros_like(dg_ref)
            db_ref[...] = jnp.zeros_like(db_ref)

        gain_v = g_ref[...]
        xh, r = _rms(h_ref[...])
        dn = jnp.zeros((ts, D), F32)
        for k in range(NSH):
            dpk = dp_ref[:, k * INB:(k + 1) * INB]
            dn = dn + _dot_nt(dpk, win_ref[k])
            db_ref[:, k * INB:(k + 1) * INB] += _colsum(dpk.astype(F32))
        dg_ref[...] += _colsum(dn * xh)
        dx_ref[...] = dh_ref[...] + _rms_bwd(dn, xh, r, gain_v)

    return _pcall(
        body, grid=(s // ts,), name="mix_bwd_in", args=(dp, h, dh, gain, win),
        out_shape=(jax.ShapeDtypeStruct((s, D), F32), jax.ShapeDtypeStruct((1, D), F32),
                   jax.ShapeDtypeStruct((1, DIN), F32)),
        in_specs=[_row_spec(ts, DIN), _row_spec(ts, D), _row_spec(ts, D), _const_spec((1, D)),
                  _const_spec((NSH, D, INB))],
        out_specs=(_row_spec(ts, D), _acc_spec((1, D)), _acc_spec((1, DIN))), rider=rider)


def kv_proj(mem, gain, wkv):
    def body(m_ref, g_ref, w_ref, k_ref, v_ref, n_ref):
        xh, _ = _rms(m_ref[...])
        nb = (xh * g_ref[...]).astype(BF)
        n_ref[...] = nb
        half = D // 2
        for j in range(2):
            k_ref[:, j * half:(j + 1) * half] = _dot(nb, w_ref[j]).astype(BF)
            v_ref[:, j * half:(j + 1) * half] = _dot(nb, w_ref[2 + j]).astype(BF)

    o = jax.ShapeDtypeStruct((NMEM, D), BF)
    return pl.pallas_call(body, name="kv_proj", out_shape=(o, o, o), compiler_params=_cparams())(mem, gain, wkv)


def kv_bwd(mem, gain, memn, wkv, dk, dv):
    def body(m_ref, g_ref, n_ref, w_ref, dk_ref, dv_ref, dw_ref, dwb_ref, dg_ref):
        xh, _ = _rms(m_ref[...])
        nb = n_ref[...]
        half = D // 2
        dn = jnp.zeros((NMEM, D), F32)
        for j in range(2):
            dkb = dk_ref[:, j * half:(j + 1) * half].astype(BF)
            dvb = dv_ref[:, j * half:(j + 1) * half].astype(BF)
            for slot, dyb in ((j, dkb), (2 + j, dvb)):
                dw = _dot_tn(nb, dyb)
                dw_ref[slot] = dw
                dwb_ref[slot] = dw.astype(BF)
            dn = dn + _dot_nt(dkb, w_ref[j]) + _dot_nt(dvb, w_ref[2 + j])
        dg_ref[...] = _colsum(dn * xh)

    return pl.pallas_call(
        body, name="kv_bwd",
        out_shape=(jax.ShapeDtypeStruct((NSH, D, D // 2), F32), jax.ShapeDtypeStruct((NSH, D, D // 2), BF),
                   jax.ShapeDtypeStruct((1, D), F32)),
        compiler_params=_cparams())(mem, gain, memn, wkv, dk, dv)


def _attend(qb, k_ref, v_ref, h):
    cols = slice(h * HD, (h + 1) * HD)
    sc = _dot_nt(qb[:, cols], k_ref[:, cols]) * ATT_SCALE
    e = jnp.exp(sc - jnp.max(sc, axis=-1, keepdims=True))
    pr = e / jnp.sum(e, axis=-1, keepdims=True)
    return pr, _dot(pr.astype(BF), v_ref[:, cols])


def xattn_fwd(h, gain, wq, k, v, wo):
    s = h.shape[0]
    ts = _tile(s, 1024)

    def body(h_ref, g_ref, wq_ref, k_ref, v_ref, wo_ref, o_ref, att_ref):
        x = h_ref[...]
        xh, _ = _rms(x)
        nb = (xh * g_ref[...]).astype(BF)
        qb = _dot(nb, wq_ref[...]).astype(BF)
        for hd in range(NH):
            _, oh = _attend(qb, k_ref, v_ref, hd)
            att_ref[:, hd * HD:(hd + 1) * HD] = oh.astype(BF)
        o_ref[...] = x + _dot(att_ref[...], wo_ref[...])

    sq = _const_spec((D, D))
    kvs = _const_spec((NMEM, D))
    return pl.pallas_call(
        body, grid=(s // ts,), name="xattn_fwd",
        out_shape=jax.ShapeDtypeStruct((s, D), F32),
        in_specs=[_row_spec(ts, D), _const_spec((1, D)), sq, kvs, kvs, sq],
        out_specs=_row_spec(ts, D),
        scratch_shapes=[pltpu.VMEM((ts, D), BF)],
        compiler_params=_cparams(("arbitrary",)),
    )(h, gain, wq, k, v, wo)


def xattn_bwd(h, dh, gain, wq, k, v, wo, rider=None):
    s = h.shape[0]
    ts = _tile(s, 512)

    def body(h_ref, dh_ref, g_ref, wq_ref, k_ref, v_ref, wo_ref,
             dx_ref, n_ref, dq_ref, att_ref, dhb_ref, dk_ref, dv_ref, dg_ref):
        @pl.when(pl.program_id(0) == 0)
        def _():
            for ref in (dk_ref, dv_ref, dg_ref):
                ref[...] = jnp.zeros_like(ref)

        x = h_ref[...]
        dh = dh_ref[...]
        gain_v = g_ref[...]
        xh, r = _rms(x)
        nb = (xh * gain_v).astype(BF)
        n_ref[...] = nb
        qb = _dot(nb, wq_ref[...]).astype(BF)
        dhb = dh.astype(BF)
        dhb_ref[...] = dhb
        dob = _dot_nt(dhb, wo_ref[...]).astype(BF)
        for hd in range(NH):
            cols = slice(hd * HD, (hd + 1) * HD)
            pr, oh = _attend(qb, k_ref, v_ref, hd)
            att_ref[:, cols] = oh.astype(BF)
            doh = dob[:, cols]
            dpr = _dot_nt(doh, v_ref[:, cols])
            dv_ref[:, cols] += _dot_tn(pr.astype(BF), doh)
            dsc = (pr * (dpr - jnp.sum(dpr * pr, axis=-1, keepdims=True)) * ATT_SCALE).astype(BF)
            dq_ref[:, cols] = _dot(dsc, k_ref[:, cols]).astype(BF)
            dk_ref[:, cols] += _dot_tn(dsc, qb[:, cols])
        dn = _dot_nt(dq_ref[...], wq_ref[...])
        dg_ref[...] += _colsum(dn * xh)
        dx_ref[...] = dh + _rms_bwd(dn, xh, r, gain_v)

    sq = _const_spec((D, D))
    kvs = _const_spec((NMEM, D))
    bf_rows = jax.ShapeDtypeStruct((s, D), BF)
    kv_acc = jax.ShapeDtypeStruct((NMEM, D), F32)
    return _pcall(
        body, grid=(s // ts,), name="xattn_bwd", args=(h, dh, gain, wq, k, v, wo),
        out_shape=(jax.ShapeDtypeStruct((s, D), F32), bf_rows, bf_rows, bf_rows, bf_rows, kv_acc, kv_acc,
                   jax.ShapeDtypeStruct((1, D), F32)),
        in_specs=[_row_spec(ts, D), _row_spec(ts, D), _const_spec((1, D)), sq, kvs, kvs, sq],
        out_specs=(_row_spec(ts, D),) * 5 + (_acc_spec((NMEM, D)), _acc_spec((NMEM, D)), _acc_spec((1, D))),
        rider=rider)


BLOCK_BYTES = 3 << 19


def _row_block(rows, cols):
    rb = rows
    while rb * cols * 4 > BLOCK_BYTES and rb % 32 == 0:
        rb //= 2
    return rb


def cast_bf16(w, chip, name):
    r, c = w.shape
    rb = _row_block(r, c)

    def body(chip_ref, w_ref, o_ref):
        del chip_ref
        o_ref[0] = w_ref[...].astype(BF)

    return pl.pallas_call(
        body, name=name, out_shape=jax.ShapeDtypeStruct((NSH, r, c), BF),
        grid_spec=pltpu.PrefetchScalarGridSpec(
            num_scalar_prefetch=1, grid=(r // rb,),
            in_specs=[pl.BlockSpec((rb, c), lambda i, chip_ref: (i, 0))],
            out_specs=pl.BlockSpec((1, rb, c), lambda i, chip_ref: (chip_ref[0], i, 0))),
        compiler_params=_cparams(("arbitrary",)))(chip, w)


NDEV = 8


def device_sum(g4, recv, place, name):
    _, _, rh, c = g4.shape
    rb = _row_block(rh, c)

    def body(place_ref, g_ref, r_ref, o_ref):
        del place_ref
        acc = g_ref[0, 0]
        for j in range(NDEV - 1):
            acc = acc + r_ref[j].astype(F32)
        o_ref[0] = acc

    return pl.pallas_call(
        body, name=name, out_shape=jax.ShapeDtypeStruct((2, rh, c), F32),
        grid_spec=pltpu.PrefetchScalarGridSpec(
            num_scalar_prefetch=1, grid=(rh // rb,),
            in_specs=[pl.BlockSpec((1, 1, rb, c), lambda i, place_ref: (place_ref[0], place_ref[1], i, 0)),
                      pl.BlockSpec((NDEV - 1, rb, c), lambda i, place_ref: (0, i, 0))],
            out_specs=pl.BlockSpec((1, rb, c), lambda i, place_ref: (place_ref[1], i, 0))),
        compiler_params=_cparams(("arbitrary",)))(place, g4, recv)


def _adamw_math(w, g, m, v):
    m = ADAM_B1 * m + (1.0 - ADAM_B1) * g
    v = ADAM_B2 * v + (1.0 - ADAM_B2) * (g * g)
    m_hat = m / (1.0 - ADAM_B1 ** ADAM_STEP)
    v_hat = v / (1.0 - ADAM_B2 ** ADAM_STEP)
    delta = -ADAM_LR * (m_hat / (jnp.sqrt(v_hat) + ADAM_EPS) + ADAM_WD * w)
    return delta, m, v


def adamw(w, g, m, v, name):
    r, c = w.shape
    rb = _row_block(r, c)

    def body(w_ref, g_ref, m_ref, v_ref, d_ref, mo_ref, vo_ref):
        d, mn, vn = _adamw_math(w_ref[...], g_ref[...], m_ref[...], v_ref[...])
        d_ref[...] = d
        mo_ref[...] = mn
        vo_ref[...] = vn

    o = jax.ShapeDtypeStruct((r, c), F32)
    spec = _row_spec(rb, c)
    return pl.pallas_call(
        body, grid=(r // rb,), name=name, out_shape=(o, o, o),
        in_specs=[spec] * 4, out_specs=(spec,) * 3,
        compiler_params=_cparams(("arbitrary",)))(w, g, m, v)


def _place():
    return lax.axis_index("x"), lax.axis_index("y"), lax.axis_index("c")


def _other_chips(x, y):
    return [(1 - x, y), (x, 1 - y), (1 - x, 1 - y)]


NOTHER = NSH - 1


def gather_rider(arrays):
    nw = len(arrays)
    nici = nw * NOTHER

    def copies(refs, send_sems, recv_sems):
        x, y, c = _place()
        ici, d2d = [], []
        for w in range(nw):
            for j, (px, py) in enumerate(_other_chips(x, y)):
                n = w * NOTHER + j
                sems = dict(send_sem=send_sems.at[n], recv_sem=recv_sems.at[n],
                            device_id=(px, py, c), device_id_type=MESH)
                mine = refs[w].at[2 * x + y, c]
                theirs = refs[w].at[2 * px + py, c]
                ici.append((pltpu.make_async_remote_copy(src_ref=mine, dst_ref=mine, **sems),
                            pltpu.make_async_remote_copy(src_ref=mine, dst_ref=theirs, **sems)))
                sems = dict(send_sem=send_sems.at[nici + n], recv_sem=recv_sems.at[nici + n],
                            device_id=(x, y, 1 - c), device_id_type=MESH)
                d2d.append((pltpu.make_async_remote_copy(src_ref=theirs, dst_ref=theirs, **sems),
                            pltpu.make_async_remote_copy(src_ref=theirs, dst_ref=refs[w].at[2 * px + py, 1 - c],
                                                         **sems)))
        return ici, d2d

    def start(ins, outs, send_sems, recv_sems):
        ici, _ = copies(outs, send_sems, recv_sems)
        for send, _ in ici:
            send.start()

    def relay(ins, outs, send_sems, recv_sems):
        ici, d2d = copies(outs, send_sems, recv_sems)
        for (_, landed), (forward, _) in zip(ici, d2d):
            landed.wait_recv()
            forward.start()

    def finish(ins, outs, send_sems, recv_sems):
        ici, d2d = copies(outs, send_sems, recv_sems)
        for _, landed in d2d:
            landed.wait_recv()
        for send, _ in ici + d2d:
            send.wait_send()

    return Rider(arrays, [jax.ShapeDtypeStruct(a.shape, a.dtype) for a in arrays], {i: i for i in range(nw)},
                 2 * nici, start, finish, relay)


def _peers(x, y, c):
    return [(x ^ (rel >> 2), y ^ ((rel >> 1) & 1), c ^ (rel & 1)) for rel in range(1, NDEV)]


def reduce_rider(grads):
    nw = len(grads)
    npeer = NDEV - 1

    def copies(ins, outs, send_sems, recv_sems):
        x, y, c = _place()
        return [pltpu.make_async_remote_copy(
            src_ref=ins[w].at[2 * px + py, pc], dst_ref=outs[w].at[r],
            send_sem=send_sems.at[w * npeer + r], recv_sem=recv_sems.at[w * npeer + r],
            device_id=(px, py, pc), device_id_type=MESH)
            for w in range(nw) for r, (px, py, pc) in enumerate(_peers(x, y, c))]

    def start(ins, outs, send_sems, recv_sems):
        for cp in copies(ins, outs, send_sems, recv_sems):
            cp.start()

    def finish(ins, outs, send_sems, recv_sems):
        for cp in copies(ins, outs, send_sems, recv_sems):
            cp.wait()

    return Rider(grads, [jax.ShapeDtypeStruct((npeer,) + g.shape[2:], g.dtype) for g in grads], {},
                 nw * npeer, start, finish)


class _Offset:
    def __init__(self, ref, base):
        self.ref, self.base = ref, base

    @property
    def at(self):
        return self

    def __getitem__(self, i):
        return self.ref.at[self.base + i]


def merge_riders(riders):
    ins, outs, aliases, spans, nsem = [], [], {}, [], 0
    for r in riders:
        spans.append((len(ins), len(outs), nsem))
        aliases.update({len(ins) + i: len(outs) + j for i, j in r.aliases.items()})
        ins, outs, nsem = ins + r.ins, outs + r.outs, nsem + r.nsem

    def each(step):
        def run(in_refs, out_refs, send_sems, recv_sems):
            for r, (i0, o0, s0) in zip(riders, spans):
                if getattr(r, step) is not None:
                    getattr(r, step)(in_refs[i0:i0 + len(r.ins)], out_refs[o0:o0 + len(r.outs)],
                                     _Offset(send_sems, s0), _Offset(recv_sems, s0))
        return run

    relay = each("relay") if any(r.relay is not None for r in riders) else None
    return Rider(ins, outs, aliases, nsem, each("start"), each("finish"), relay)


def split_results(riders, results):
    out, o0 = [], 0
    for r in riders:
        out.append(tuple(results[o0:o0 + len(r.outs)]))
        o0 += len(r.outs)
    return out


def swap_rider(halves):
    nw = len(halves)

    def copies(refs, send_sems, recv_sems):
        x, y, c = _place()
        out = []
        for w in range(nw):
            sems = dict(send_sem=send_sems.at[w], recv_sem=recv_sems.at[w],
                        device_id=(x, y, 1 - c), device_id_type=MESH)
            mine = refs[w].at[c]
            out.append((pltpu.make_async_remote_copy(src_ref=mine, dst_ref=mine, **sems),
                        pltpu.make_async_remote_copy(src_ref=mine, dst_ref=refs[w].at[1 - c], **sems)))
        return out

    def start(ins, outs, send_sems, recv_sems):
        for send, _ in copies(outs, send_sems, recv_sems):
            send.start()

    def finish(ins, outs, send_sems, recv_sems):
        cps = copies(outs, send_sems, recv_sems)
        for _, recv in cps:
            recv.wait_recv()
        for send, _ in cps:
            send.wait_send()

    return Rider(halves, [jax.ShapeDtypeStruct(h.shape, h.dtype) for h in halves], {i: i for i in range(nw)},
                 nw, start, finish)


def allgather_rider(slots):
    def copies(ref, send_sems, recv_sems):
        x, y, c = _place()
        mine = ref.at[4 * x + 2 * y + c]
        out = []
        for r, peer in enumerate(_peers(x, y, c)):
            sems = dict(send_sem=send_sems.at[r], recv_sem=recv_sems.at[r], device_id=peer, device_id_type=MESH)
            out.append((pltpu.make_async_remote_copy(src_ref=mine, dst_ref=mine, **sems),
                        pltpu.make_async_remote_copy(
                            src_ref=mine, dst_ref=ref.at[4 * peer[0] + 2 * peer[1] + peer[2]], **sems)))
        return out

    def start(ins, outs, send_sems, recv_sems):
        for send, _ in copies(outs[0], send_sems, recv_sems):
            send.start()

    def finish(ins, outs, send_sems, recv_sems):
        cps = copies(outs[0], send_sems, recv_sems)
        for _, recv in cps:
            recv.wait_recv()
        for send, _ in cps:
            send.wait_send()

    return Rider([slots], [jax.ShapeDtypeStruct(slots.shape, slots.dtype)], {0: 0}, NDEV - 1, start, finish)


def sum_slots(slots):
    def body(s_ref, o_ref):
        acc = s_ref[0]
        for dev in range(1, NDEV):
            acc = acc + s_ref[dev]
        o_ref[...] = acc

    return pl.pallas_call(body, name="sum_slots", out_shape=jax.ShapeDtypeStruct(slots.shape[1:], F32),
                          compiler_params=_cparams())(slots)


BIG = ("ffn1_w_gu", "ffn1_w_down", "w_in", "w_a_out", "w_b_out", "w_out", "w_q", "w_kv", "w_o",
       "ffn2_w_gu", "ffn2_w_down")
SMALL = {"ffn1_norm": (0, 1), "mix_norm": (8, 1), "xattn_norm": (16, 1), "mem_norm": (24, 1),
         "ffn2_norm": (32, 1), "final_norm": (40, 1), "conv_b": (48, 1), "conv_ln_g": (56, 1),
         "conv_ln_b": (64, 1), "sgu_ln_g": (72, 1), "sgu_ln_b": (80, 1), "b_in": (88, 6),
         "conv_w": (96, CW), "sgu_w": (128, 64), "sgu_b": (192, 1)}
LOSS_ROW = 200
SMALL_ROWS = 208


def _pad_rows(a, rows):
    return jnp.pad(a, ((0, rows - a.shape[0]), (0, D - a.shape[1])))


def _pack_small(parts):
    names = sorted(parts, key=lambda n: SMALL[n][0] if n in SMALL else LOSS_ROW)
    rows = []
    for i, n in enumerate(names):
        start = SMALL[n][0] if n in SMALL else LOSS_ROW
        end = SMALL_ROWS if i + 1 == len(names) else (SMALL[names[i + 1]][0] if names[i + 1] in SMALL else LOSS_ROW)
        rows.append(_pad_rows(parts[n], end - start))
    return jnp.concatenate(rows, axis=0)


def _small_views(w):
    return {
        "ffn1_norm": w["ffn1_norm"], "mix_norm": w["mix_norm"], "xattn_norm": w["xattn_norm"],
        "mem_norm": w["mem_norm"], "ffn2_norm": w["ffn2_norm"], "final_norm": w["final_norm"].reshape(1, D),
        "conv_b": w["conv_b"], "conv_ln_g": w["conv_ln_g"], "conv_ln_b": w["conv_ln_b"],
        "sgu_ln_g": w["sgu_ln_g"], "sgu_ln_b": w["sgu_ln_b"], "b_in": w["b_in"].reshape(6, D),
        "conv_w": w["conv_w"][0], "sgu_w": w["sgu_w"].reshape(64, D), "sgu_b": w["sgu_b"].reshape(1, NG * CHUNK),
    }


def _unpack_small(buf, like, chip):
    out = {}
    for n, (start, rows) in SMALL.items():
        blk = buf[start:start + rows]
        if n == "conv_w":
            blk = blk[:, :like[n].shape[-1]] if chip is None else lax.dynamic_slice_in_dim(
                blk, chip * like[n].shape[-1], like[n].shape[-1], axis=1)
        elif n == "sgu_b":
            blk = blk[:, :NG * CHUNK]
        out[n] = blk.reshape(like[n].shape)
    return out


def kernel(x, mem, ffn1_norm, ffn1_w_gu, ffn1_w_down, mix_norm, w_in, b_in, conv_w, conv_b, conv_ln_g, conv_ln_b, w_a_out, sgu_ln_g, sgu_ln_b, sgu_w, sgu_b, w_b_out, w_out, xattn_norm, mem_norm, w_q, w_kv, w_o, ffn2_norm, ffn2_w_gu, ffn2_w_down, final_norm, loss_target, m_ffn1_norm, m_ffn1_w_gu, m_ffn1_w_down, m_mix_norm, m_w_in, m_b_in, m_conv_w, m_conv_b, m_conv_ln_g, m_conv_ln_b, m_w_a_out, m_sgu_ln_g, m_sgu_ln_b, m_sgu_w, m_sgu_b, m_w_b_out, m_w_out, m_xattn_norm, m_mem_norm, m_w_q, m_w_kv, m_w_o, m_ffn2_norm, m_ffn2_w_gu, m_ffn2_w_down, m_final_norm, v_ffn1_norm, v_ffn1_w_gu, v_ffn1_w_down, v_mix_norm, v_w_in, v_b_in, v_conv_w, v_conv_b, v_conv_ln_g, v_conv_ln_b, v_w_a_out, v_sgu_ln_g, v_sgu_ln_b, v_sgu_w, v_sgu_b, v_w_b_out, v_w_out, v_xattn_norm, v_mem_norm, v_w_q, v_w_kv, v_w_o, v_ffn2_norm, v_ffn2_w_gu, v_ffn2_w_down, v_final_norm):
    names = ("ffn1_norm", "ffn1_w_gu", "ffn1_w_down", "mix_norm", "w_in", "b_in", "conv_w", "conv_b",
             "conv_ln_g", "conv_ln_b", "w_a_out", "sgu_ln_g", "sgu_ln_b", "sgu_w", "sgu_b", "w_b_out", "w_out",
             "xattn_norm", "mem_norm", "w_q", "w_kv", "w_o", "ffn2_norm", "ffn2_w_gu", "ffn2_w_down",
             "final_norm")
    wts = dict(zip(names, (ffn1_norm, ffn1_w_gu, ffn1_w_down, mix_norm, w_in, b_in, conv_w, conv_b, conv_ln_g,
                           conv_ln_b, w_a_out, sgu_ln_g, sgu_ln_b, sgu_w, sgu_b, w_b_out, w_out, xattn_norm,
                           mem_norm, w_q, w_kv, w_o, ffn2_norm, ffn2_w_gu, ffn2_w_down, final_norm)))
    mom1 = dict(zip(names, (m_ffn1_norm, m_ffn1_w_gu, m_ffn1_w_down, m_mix_norm, m_w_in, m_b_in, m_conv_w,
                            m_conv_b, m_conv_ln_g, m_conv_ln_b, m_w_a_out, m_sgu_ln_g, m_sgu_ln_b, m_sgu_w,
                            m_sgu_b, m_w_b_out, m_w_out, m_xattn_norm, m_mem_norm, m_w_q, m_w_kv, m_w_o,
                            m_ffn2_norm, m_ffn2_w_gu, m_ffn2_w_down, m_final_norm)))
    mom2 = dict(zip(names, (v_ffn1_norm, v_ffn1_w_gu, v_ffn1_w_down, v_mix_norm, v_w_in, v_b_in, v_conv_w,
                            v_conv_b, v_conv_ln_g, v_conv_ln_b, v_w_a_out, v_sgu_ln_g, v_sgu_ln_b, v_sgu_w,
                            v_sgu_b, v_w_b_out, v_w_out, v_xattn_norm, v_mem_norm, v_w_q, v_w_kv, v_w_o,
                            v_ffn2_norm, v_ffn2_w_gu, v_ffn2_w_down, v_final_norm)))
    xi, yi, ci = _place()
    chip = (2 * xi + yi).astype(jnp.int32)
    core = ci.astype(jnp.int32)
    chip_arr = chip.reshape(1)
    place_arr = jnp.stack([chip, core])
    x2, mem2, tgt = x[0], mem[0], loss_target[0]

    slot = {n: cast_bf16(wts[n][0], chip_arr, "cast_" + n) for n in BIG}
    cw_pad = jnp.pad(conv_w[0], ((0, HALO - CW), (0, 0)))
    slot["conv_w"] = lax.dynamic_update_slice(jnp.zeros((NSH,) + cw_pad.shape, F32), cw_pad[None], (chip, 0, 0))
    g_first = ("ffn1_w_gu",)
    g_mix = ("ffn1_w_down", "w_in", "w_a_out", "w_b_out", "w_out", "conv_w")
    g_rest = ("w_q", "w_kv", "w_o", "ffn2_w_gu", "ffn2_w_down")

    def gather(group):
        return gather_rider([slot[n].reshape(NSH, 2, slot[n].shape[1] // 2, slot[n].shape[2]) for n in group])

    def gathered(group, res):
        return {n: r.reshape(slot[n].shape) for n, r in zip(group, res)}

    full = gathered(g_first, run_rider(gather(g_first), "gather_ffn1"))
    wgu1 = full["ffn1_w_gu"]
    tril = jnp.tril(jnp.ones((CHUNK, CHUNK), dtype=bool))
    ws = jnp.where(tril[None], sgu_w[0], 0.0).astype(BF)
    wst = jnp.transpose(ws, (0, 2, 1))
    sbias = jnp.repeat(jnp.transpose(sgu_b[0]), GD, axis=1)
    gfin = final_norm.reshape(1, D)

    (gu1, act1), rode = ffn_hidden(x2, ffn1_norm, wgu1, rider=gather(g_mix))
    full.update(gathered(g_mix, rode))
    wd1 = full["ffn1_w_down"].reshape(FF, D)
    h1 = ffn_down(x2, act1, wd1)
    win = full["w_in"]
    wa, wb, wout = (full[n].reshape(D, D) for n in ("w_a_out", "w_b_out", "w_out"))
    cw_full = jnp.transpose(full["conv_w"], (1, 0, 2)).reshape(HALO, D)
    (h2, proj, n2b, conv_out), rode = mix_fwd(
        h1, mix_norm, win, b_in, cw_full, conv_b, conv_ln_g, conv_ln_b, wa, sgu_ln_g, sgu_ln_b, ws, sbias, wb,
        wout, rider=gather(g_rest))
    full.update(gathered(g_rest, rode))
    wgu2, wd2, wkv = full["ffn2_w_gu"], full["ffn2_w_down"].reshape(FF, D), full["w_kv"]
    wq, wo = full["w_q"].reshape(D, D), full["w_o"].reshape(D, D)
    kb, vb, memn = kv_proj(mem2, mem_norm, wkv)
    h3 = xattn_fwd(h2, xattn_norm, wq, kb, vb, wo)
    dh4, gu2, loss_lanes, d_final = ffn_fwd_loss(h3, ffn2_norm, wgu2, wd2, gfin, tgt)

    own, halves = {}, {}

    def exchange(group, grads):
        views = []
        for n, (g, gb) in zip(group, grads):
            rs, cs = wts[n].shape[1:]
            own[n] = g.reshape(NSH, 2, rs // 2, cs)
            views.append(gb.reshape(NSH, 2, rs // 2, cs))
        return reduce_rider(views)

    def reduce(group, recv):
        for n, r in zip(group, recv):
            halves[n] = device_sum(own[n], r, place_arr, "device_sum_" + n)

    dh3, n4, a4, dgu4, dhb4, d_ffn2n = ffn_bwd(h3, gu2, dh4, ffn2_norm, wgu2, wd2, "ffn2_bwd")
    g_ffn2 = ("ffn2_w_gu", "ffn2_w_down")
    ride = exchange(g_ffn2, [dw_matmul(n4, dgu4, NSH, "dw_ffn2_gu")[:2], dw_matmul(a4, dhb4, 1, "dw_ffn2_down")[:2]])
    (dh2, n3, dq, att, dhb3, dk, dv, d_xn), rode = xattn_bwd(h2, dh3, xattn_norm, wq, kb, vb, wo, rider=ride)
    reduce(g_ffn2, rode)
    g_att = ("w_q", "w_o", "w_kv")
    d_wkv, d_wkv_b, d_memn = kv_bwd(mem2, mem_norm, memn, wkv, dk, dv)
    ride = exchange(g_att, [dw_matmul(n3, dq, 1, "dw_q")[:2], dw_matmul(att, dhb3, 1, "dw_o")[:2],
                            (d_wkv, d_wkv_b)])
    ((dconv, dproj, sa, dya, ob, dyb, mg, dhb2, d_sgu_w, d_sgu_b, d_lna_g, d_lna_b, d_lnb_g, d_lnb_b),
     rode) = mix_bwd_branches(proj, conv_out, dh2, conv_ln_g, conv_ln_b, wa, sgu_ln_g, sgu_ln_b, ws, wst,
                              sbias, wb, wout, rider=ride)
    reduce(g_att, rode)
    dproj, d_conv_w, d_conv_b = conv_bwd(proj, dconv, dproj, cw_full)
    g_mixw = ("w_in", "w_a_out", "w_b_out")
    ride = exchange(g_mixw, [dw_matmul(n2b, dproj, NSH, "dw_in")[:2], dw_matmul(sa, dya, 1, "dw_a_out")[:2],
                             dw_matmul(ob, dyb, 1, "dw_b_out")[:2]])
    (dh1, d_mixn, d_b_in), rode = mix_bwd_in(dproj, h1, dh2, mix_norm, win, rider=ride)
    reduce(g_mixw, rode)
    dx, n1, a1, dgu1, dhb1, d_ffn1n = ffn_bwd(x2, gu1, dh1, ffn1_norm, wgu1, wd1, "ffn1_bwd")
    small_grads = {
        "ffn1_norm": d_ffn1n, "mix_norm": d_mixn, "xattn_norm": d_xn, "mem_norm": d_memn, "ffn2_norm": d_ffn2n,
        "final_norm": d_final, "conv_b": d_conv_b, "conv_ln_g": d_lna_g, "conv_ln_b": d_lna_b,
        "sgu_ln_g": d_lnb_g, "sgu_ln_b": d_lnb_b, "b_in": d_b_in.reshape(6, D), "conv_w": d_conv_w[:CW],
        "sgu_w": d_sgu_w.reshape(64, D), "sgu_b": jnp.transpose(d_sgu_b[:, :NG]).reshape(1, NG * CHUNK),
        "loss": loss_lanes}
    slots = lax.dynamic_update_slice(jnp.zeros((NDEV, SMALL_ROWS, D), F32), _pack_small(small_grads)[None],
                                     (2 * chip + core, 0, 0))
    d_wgu1, d_wgu1_b, all_slots = dw_matmul(n1, dgu1, NSH, "dw_ffn1_gu", rider=allgather_rider(slots))
    ride = exchange(("ffn1_w_gu",), [(d_wgu1, d_wgu1_b)])
    d_down, d_down_b, rode = dw_matmul(a1, dhb1, 1, "dw_ffn1_down", rider=ride)
    reduce(("ffn1_w_gu",), rode)
    ride = exchange(("ffn1_w_down",), [(d_down, d_down_b)])
    d_out, d_out_b, rode = dw_matmul(mg, dhb2, 1, "dw_out", rider=ride)
    reduce(("ffn1_w_down",), rode)
    others = tuple(n for n in BIG if n != "w_out")
    riders = [exchange(("w_out",), [(d_out, d_out_b)]), swap_rider([halves[n] for n in others])]
    recv_out, swapped = split_results(riders, run_rider(merge_riders(riders), "exchange_w_out"))
    reduce(("w_out",), recv_out)
    gshard = {n: g.reshape(wts[n].shape[1:]) for n, g in zip(others, swapped)}
    gshard["w_out"] = run_rider(swap_rider([halves["w_out"]]), "pair_swap")[0].reshape(wts["w_out"].shape[1:])

    small = sum_slots(all_slots[0])
    loss = (0.5 / D) * jnp.sum(small[LOSS_ROW])
    gsmall = _unpack_small(small, wts, chip)

    out_g, out_d, out_m, out_v = dict(gsmall), {}, {}, {}
    sw, sm, sv = (_pack_small(_small_views(t))[:LOSS_ROW] for t in (wts, mom1, mom2))
    sg = _pack_small(_small_views({n: gsmall[n] for n in SMALL}))[:LOSS_ROW]
    for dst, packed in zip((out_d, out_m, out_v), adamw(sw, sg, sm, sv, "adamw_small")):
        dst.update(_unpack_small(packed, wts, None))
    for n in BIG:
        shape = wts[n].shape
        out_g[n] = gshard[n].reshape(shape)
        d, mn, vn = adamw(wts[n][0], gshard[n], mom1[n][0], mom2[n][0], "adamw_" + n)
        out_d[n], out_m[n], out_v[n] = d.reshape(shape), mn.reshape(shape), vn.reshape(shape)
    return (loss, dx[None], *[out_g[n] for n in names], *[out_d[n] for n in names],
            *[out_m[n] for n in names], *[out_v[n] for n in names])
```
